```python
import jax, jax.numpy as jnp
from jax import lax
import numpy as np


D_MODEL = 1024
BATCH = 32
SEQ = 2048
DEPTH = 2

CHUNK = 64
N_META = 16
EPS = 1e-6
N_EVEN = (DEPTH + 1) // 2
N_ODD = DEPTH // 2
CONV_W = 4

LRU_WIDTH = D_MODEL
LRU_BLOCKS = 4
LRU_BLOCK = LRU_WIDTH // LRU_BLOCKS
RG_LRU_C = 8.0

SSD_WIDTH = D_MODEL
SSD_HEAD_DIM = 64
SSD_HEADS = SSD_WIDTH // SSD_HEAD_DIM
SSD_GROUPS = 2
SSD_HPG = SSD_HEADS // SSD_GROUPS
SSD_STATE = 128
SSD_CHUNK = CHUNK
SSD_CONV_DIM = SSD_WIDTH + 2 * SSD_GROUPS * SSD_STATE

EVEN_SPLITS = (LRU_WIDTH, 2 * LRU_WIDTH, 2 * LRU_WIDTH + SSD_WIDTH,
               2 * LRU_WIDTH + SSD_WIDTH + SSD_CONV_DIM)
EVEN_IN = 2 * LRU_WIDTH + SSD_WIDTH + SSD_CONV_DIM + SSD_HEADS
EVEN_MIX = LRU_WIDTH + SSD_WIDTH

SB_HEADS = 16
SB_HEAD_DIM = D_MODEL // SB_HEADS
SB_WIDTH = SB_HEADS * SB_HEAD_DIM
SB_BLOCK = 128
ODD_IN = 4 * SB_WIDTH

kernel_name = 'hybrid_rglru_ssd_stickbreaking_meta'


def rmsnorm(x, w):
    xf = x.astype(jnp.float32)
    y = xf * lax.rsqrt(jnp.mean(xf * xf, axis=-1, keepdims=True) + EPS)
    return (y * w.astype(jnp.float32)).astype(x.dtype)


def causal_dwconv(u, w, b):
    out = lax.conv_general_dilated(
        u, w[:, None, :].astype(u.dtype), window_strides=(1,),
        padding=[(CONV_W - 1, 0)], dimension_numbers=('NWC', 'WIO', 'NWC'),
        feature_group_count=u.shape[-1])
    return out + b


def linear_scan(a, b):
    def combine(left, right):
        al, bl = left
        ar, br = right
        return al * ar, ar * bl + br
    _, h = lax.associative_scan(combine, (a, b), axis=1)
    return h


def rg_lru(u, w_a, b_a, w_x, b_x, lam):
    bsz, L, _ = u.shape
    uf = u.astype(jnp.float32)
    ub = uf.reshape(bsz, L, LRU_BLOCKS, LRU_BLOCK)
    r = jax.nn.sigmoid(jnp.einsum('blgi,gij->blgj', ub, w_a).reshape(bsz, L, LRU_WIDTH) + b_a)
    i = jax.nn.sigmoid(jnp.einsum('blgi,gij->blgj', ub, w_x).reshape(bsz, L, LRU_WIDTH) + b_x)
    log_a = -RG_LRU_C * r * jax.nn.softplus(-lam)
    a = jnp.exp(log_a)
    mult = jnp.sqrt(-jnp.expm1(2.0 * log_a))
    return linear_scan(a, mult * i * uf)


def ssd_scan(xh, dt, a, bmat, cmat):
    bsz, L, _, _ = xh.shape
    pad = (-L) % SSD_CHUNK
    def padf(t):
        return jnp.pad(t, [(0, 0), (pad, 0)] + [(0, 0)] * (t.ndim - 2))
    f32 = jnp.float32
    xdt = padf((xh * dt[..., None]).astype(f32))
    adt = padf((dt * a).astype(f32))
    bm = padf(bmat.astype(f32))
    cm = padf(cmat.astype(f32))
    nc = (L + pad) // SSD_CHUNK
    X = xdt.reshape(bsz, nc, SSD_CHUNK, SSD_GROUPS, SSD_HPG, SSD_HEAD_DIM)
    A = adt.reshape(bsz, nc, SSD_CHUNK, SSD_GROUPS, SSD_HPG).transpose(0, 3, 4, 1, 2)
    Bc = bm.reshape(bsz, nc, SSD_CHUNK, SSD_GROUPS, SSD_STATE)
    Cc = cm.reshape(bsz, nc, SSD_CHUNK, SSD_GROUPS, SSD_STATE)
    a_cum = jnp.cumsum(A, axis=-1)
    tri = jnp.tril(jnp.ones((SSD_CHUNK, SSD_CHUNK), bool))
    seg = a_cum[..., :, None] - a_cum[..., None, :]
    decay = jnp.exp(jnp.where(tri, seg, -jnp.inf))
    cb = jnp.einsum('bclgn,bcsgn->bcgls', Cc, Bc)
    y_diag = jnp.einsum('bcgls,bgecls,bcsgep->bclgep', cb, decay, X)
    decay_states = jnp.exp(a_cum[..., -1:] - a_cum)
    states = jnp.einsum('bclgn,bgecl,bclgep->bcgepn', Bc, decay_states, X)
    chunk_tot = jnp.pad(a_cum[..., -1], [(0, 0), (0, 0), (0, 0), (1, 0)])
    cs = jnp.cumsum(chunk_tot, axis=-1)
    tri_c = jnp.tril(jnp.ones((nc + 1, nc + 1), bool))
    decay_chunk = jnp.exp(jnp.where(tri_c, cs[..., :, None] - cs[..., None, :], -jnp.inf))
    states = jnp.concatenate([jnp.zeros_like(states[:, :1]), states], axis=1)
    states = jnp.einsum('bgezc,bcgepn->bzgepn', decay_chunk, states)[:, :-1]
    y_off = jnp.einsum('bclgn,bcgepn,bgecl->bclgep', Cc, states, jnp.exp(a_cum))
    y = (y_diag + y_off).reshape(bsz, L + pad, SSD_HEADS, SSD_HEAD_DIM)
    return y[:, pad:]


def gated_group_rmsnorm(y, z, w):
    bsz, L, W = y.shape
    g = (y * jax.nn.silu(z.astype(jnp.float32))).reshape(bsz, L, SSD_GROUPS, W // SSD_GROUPS)
    g = g * lax.rsqrt(jnp.mean(g * g, axis=-1, keepdims=True) + EPS)
    return g.reshape(bsz, L, W) * w.astype(jnp.float32)


def rglru_ssd_layer(h, norm_w, w_in, lru_conv_w, lru_conv_b, lru_w_a, lru_b_a,
                    lru_w_x, lru_b_x, lru_lambda, ssd_conv_w, ssd_conv_b,
                    ssd_dt_bias, ssd_a_log, ssd_d, ssd_norm, w_out):
    bsz, L, _ = h.shape
    u = rmsnorm(h, norm_w)
    proj = u @ w_in
    lru_x, lru_g, ssd_z, ssd_xbc, ssd_dt = jnp.split(proj, EVEN_SPLITS, axis=-1)
    lx = causal_dwconv(lru_x, lru_conv_w, lru_conv_b)
    y_a = rg_lru(lx, lru_w_a, lru_b_a, lru_w_x, lru_b_x, lru_lambda) * jax.nn.silu(lru_g.astype(jnp.float32))
    xbc = jax.nn.silu(causal_dwconv(ssd_xbc, ssd_conv_w, ssd_conv_b))
    xs, bm, cm = jnp.split(xbc, (SSD_WIDTH, SSD_WIDTH + SSD_GROUPS * SSD_STATE), axis=-1)
    dt = jax.nn.softplus(ssd_dt.astype(jnp.float32) + ssd_dt_bias)
    a = -jnp.exp(ssd_a_log.astype(jnp.float32))
    xh = xs.reshape(bsz, L, SSD_HEADS, SSD_HEAD_DIM)
    y = ssd_scan(xh, dt, a,
                 bm.reshape(bsz, L, SSD_GROUPS, SSD_STATE),
                 cm.reshape(bsz, L, SSD_GROUPS, SSD_STATE))
    y = y + xh.astype(jnp.float32) * ssd_d[:, None]
    y_b = gated_group_rmsnorm(y.reshape(bsz, L, SSD_WIDTH), ssd_z, ssd_norm)
    mixed = jnp.concatenate([y_a, y_b], axis=-1).astype(h.dtype)
    return h + mixed @ w_out


def stick_breaking_block(q_blk, k_ctx, v_ctx, q0):
    tq = q_blk.shape[1]
    s_len = k_ctx.shape[1]
    z = jnp.einsum('bthd,bshd->bhts', q_blk.astype(jnp.float32),
                   k_ctx.astype(jnp.float32)) * (SB_HEAD_DIM ** -0.5)
    before = jnp.arange(s_len)[None, :] < (q0 + jnp.arange(tq))[:, None]
    log_keep = jnp.where(before, jax.nn.log_sigmoid(-z), 0.0)
    csum = jnp.cumsum(log_keep, axis=-1)
    weights = jnp.where(before, jnp.exp(jax.nn.log_sigmoid(z) + csum[..., -1:] - csum), 0.0)
    return jnp.einsum('bhts,bshd->bthd', weights, v_ctx.astype(jnp.float32))


def stick_breaking_layer(h, norm_w, w_in, w_out):
    bsz, L, _ = h.shape
    u = rmsnorm(h, norm_w)
    q, k, v, g = jnp.split(u @ w_in, 4, axis=-1)
    q = q.reshape(bsz, L, SB_HEADS, SB_HEAD_DIM)
    k = k.reshape(bsz, L, SB_HEADS, SB_HEAD_DIM)
    v = v.reshape(bsz, L, SB_HEADS, SB_HEAD_DIM)
    bounds = [0] + list(range(N_META, L, SB_BLOCK)) + [L]
    outs = [stick_breaking_block(q[:, s:e], k[:, :e], v[:, :e], s)
            for s, e in zip(bounds[:-1], bounds[1:])]
    o = jnp.concatenate(outs, axis=1).reshape(bsz, L, SB_WIDTH)
    o = (o * jax.nn.silu(g.astype(jnp.float32))).astype(h.dtype)
    return h + o @ w_out


def _fwd_setup_inputs(seed: int = 0) -> dict:
    key = jax.random.key(seed)
    ks = jax.random.split(key, 24)
    f32 = jnp.float32

    def nrm(k, shape, fan_in):
        return jax.random.normal(k, shape, f32) * (fan_in ** -0.5)

    def gain(k, shape):
        return 1.0 + 0.05 * jax.random.normal(k, shape, f32)

    def bias(k, shape, s=0.05):
        return s * jax.random.normal(k, shape, f32)

    x = jax.random.normal(ks[0], (BATCH, SEQ, D_MODEL), f32)
    meta = jax.random.normal(ks[1], (N_META, D_MODEL), f32)
    even_norm = gain(ks[2], (N_EVEN, D_MODEL))
    even_w_in = nrm(ks[3], (N_EVEN, D_MODEL, EVEN_IN), D_MODEL)
    lru_conv_w = nrm(ks[4], (N_EVEN, CONV_W, LRU_WIDTH), CONV_W)
    lru_conv_b = bias(ks[5], (N_EVEN, LRU_WIDTH))
    lru_w_a = nrm(ks[6], (N_EVEN, LRU_BLOCKS, LRU_BLOCK, LRU_BLOCK), LRU_BLOCK)
    lru_b_a = bias(ks[7], (N_EVEN, LRU_WIDTH), 0.1)
    lru_w_x = nrm(ks[8], (N_EVEN, LRU_BLOCKS, LRU_BLOCK, LRU_BLOCK), LRU_BLOCK)
    lru_b_x = bias(ks[9], (N_EVEN, LRU_WIDTH), 0.1)
    a_c = jax.random.uniform(ks[10], (N_EVEN, LRU_WIDTH), f32, minval=0.9, maxval=0.999)
    a0 = a_c ** (1.0 / RG_LRU_C)
    lru_lambda = jnp.log(a0) - jnp.log1p(-a0)
    ssd_conv_w = nrm(ks[11], (N_EVEN, CONV_W, SSD_CONV_DIM), CONV_W)
    ssd_conv_b = bias(ks[12], (N_EVEN, SSD_CONV_DIM))
    dt0 = jnp.exp(jax.random.uniform(ks[13], (N_EVEN, SSD_HEADS), f32,
                                     minval=float(np.log(1e-3)), maxval=float(np.log(1e-1))))
    ssd_dt_bias = dt0 + jnp.log(-jnp.expm1(-dt0))
    ssd_a_log = jnp.log(jax.random.uniform(ks[14], (N_EVEN, SSD_HEADS), f32, minval=1.0, maxval=16.0))
    ssd_d = gain(ks[15], (N_EVEN, SSD_HEADS))
    ssd_norm = gain(ks[16], (N_EVEN, SSD_WIDTH))
    even_w_out = nrm(ks[17], (N_EVEN, EVEN_MIX, D_MODEL), EVEN_MIX)
    odd_norm = gain(ks[18], (N_ODD, D_MODEL))
    odd_w_in = nrm(ks[19], (N_ODD, D_MODEL, ODD_IN), D_MODEL)
    odd_w_out = nrm(ks[20], (N_ODD, SB_WIDTH, D_MODEL), SB_WIDTH)
    final_norm = gain(ks[21], (D_MODEL,))
    return {'x': x, 'meta': meta, 'even_norm': even_norm, 'even_w_in': even_w_in,
            'lru_conv_w': lru_conv_w, 'lru_conv_b': lru_conv_b,
            'lru_w_a': lru_w_a, 'lru_b_a': lru_b_a, 'lru_w_x': lru_w_x, 'lru_b_x': lru_b_x,
            'lru_lambda': lru_lambda, 'ssd_conv_w': ssd_conv_w, 'ssd_conv_b': ssd_conv_b,
            'ssd_dt_bias': ssd_dt_bias, 'ssd_a_log': ssd_a_log, 'ssd_d': ssd_d,
            'ssd_norm': ssd_norm, 'even_w_out': even_w_out, 'odd_norm': odd_norm,
            'odd_w_in': odd_w_in, 'odd_w_out': odd_w_out, 'final_norm': final_norm}


def _fwd_reference(x, meta, even_norm, even_w_in, lru_conv_w, lru_conv_b, lru_w_a, lru_b_a,
              lru_w_x, lru_b_x, lru_lambda, ssd_conv_w, ssd_conv_b, ssd_dt_bias,
              ssd_a_log, ssd_d, ssd_norm, even_w_out, odd_norm, odd_w_in, odd_w_out,
              final_norm):
    bsz = x.shape[0]
    meta_b = jnp.broadcast_to(meta[None].astype(x.dtype), (bsz, N_META, D_MODEL))
    h = jnp.concatenate([meta_b, x], axis=1)
    for layer in range(DEPTH):
        j = layer // 2
        if layer % 2 == 0:
            h = rglru_ssd_layer(h, even_norm[j], even_w_in[j], lru_conv_w[j], lru_conv_b[j],
                                lru_w_a[j], lru_b_a[j], lru_w_x[j], lru_b_x[j], lru_lambda[j],
                                ssd_conv_w[j], ssd_conv_b[j], ssd_dt_bias[j], ssd_a_log[j],
                                ssd_d[j], ssd_norm[j], even_w_out[j])
        else:
            h = stick_breaking_layer(h, odd_norm[j], odd_w_in[j], odd_w_out[j])
    return rmsnorm(h, final_norm)[:, N_META:].astype(x.dtype)


import jax as _jax
import jax.numpy as _jnp

TWIN_FORMAT = 'train_step'
FWD_PARAMS = ['x', 'meta', 'even_norm', 'even_w_in', 'lru_conv_w', 'lru_conv_b', 'lru_w_a', 'lru_b_a', 'lru_w_x', 'lru_b_x', 'lru_lambda', 'ssd_conv_w', 'ssd_conv_b', 'ssd_dt_bias', 'ssd_a_log', 'ssd_d', 'ssd_norm', 'even_w_out', 'odd_norm', 'odd_w_in', 'odd_w_out', 'final_norm']
TWIN_WEIGHTS = ['meta', 'even_norm', 'even_w_in', 'lru_conv_w', 'lru_conv_b', 'lru_w_a', 'lru_b_a', 'lru_w_x', 'lru_b_x', 'lru_lambda', 'ssd_conv_w', 'ssd_conv_b', 'ssd_dt_bias', 'ssd_a_log', 'ssd_d', 'ssd_norm', 'even_w_out', 'odd_norm', 'odd_w_in', 'odd_w_out', 'final_norm']
TWIN_DIFF_INPUT = 'x'
TWIN_INPUTS = ['x', 'meta', 'even_norm', 'even_w_in', 'lru_conv_w', 'lru_conv_b', 'lru_w_a', 'lru_b_a', 'lru_w_x', 'lru_b_x', 'lru_lambda', 'ssd_conv_w', 'ssd_conv_b', 'ssd_dt_bias', 'ssd_a_log', 'ssd_d', 'ssd_norm', 'even_w_out', 'odd_norm', 'odd_w_in', 'odd_w_out', 'final_norm', 'loss_target', 'm_meta', 'm_even_norm', 'm_even_w_in', 'm_lru_conv_w', 'm_lru_conv_b', 'm_lru_w_a', 'm_lru_b_a', 'm_lru_w_x', 'm_lru_b_x', 'm_lru_lambda', 'm_ssd_conv_w', 'm_ssd_conv_b', 'm_ssd_dt_bias', 'm_ssd_a_log', 'm_ssd_d', 'm_ssd_norm', 'm_even_w_out', 'm_odd_norm', 'm_odd_w_in', 'm_odd_w_out', 'm_final_norm', 'v_meta', 'v_even_norm', 'v_even_w_in', 'v_lru_conv_w', 'v_lru_conv_b', 'v_lru_w_a', 'v_lru_b_a', 'v_lru_w_x', 'v_lru_b_x', 'v_lru_lambda', 'v_ssd_conv_w', 'v_ssd_conv_b', 'v_ssd_dt_bias', 'v_ssd_a_log', 'v_ssd_d', 'v_ssd_norm', 'v_even_w_out', 'v_odd_norm', 'v_odd_w_in', 'v_odd_w_out', 'v_final_norm']
TWIN_OUTPUTS = ['loss', 'grad_x', 'grad_meta', 'grad_even_norm', 'grad_even_w_in', 'grad_lru_conv_w', 'grad_lru_conv_b', 'grad_lru_w_a', 'grad_lru_b_a', 'grad_lru_w_x', 'grad_lru_b_x', 'grad_lru_lambda', 'grad_ssd_conv_w', 'grad_ssd_conv_b', 'grad_ssd_dt_bias', 'grad_ssd_a_log', 'grad_ssd_d', 'grad_ssd_norm', 'grad_even_w_out', 'grad_odd_norm', 'grad_odd_w_in', 'grad_odd_w_out', 'grad_final_norm', 'delta_meta', 'delta_even_norm', 'delta_even_w_in', 'delta_lru_conv_w', 'delta_lru_conv_b', 'delta_lru_w_a', 'delta_lru_b_a', 'delta_lru_w_x', 'delta_lru_b_x', 'delta_lru_lambda', 'delta_ssd_conv_w', 'delta_ssd_conv_b', 'delta_ssd_dt_bias', 'delta_ssd_a_log', 'delta_ssd_d', 'delta_ssd_norm', 'delta_even_w_out', 'delta_odd_norm', 'delta_odd_w_in', 'delta_odd_w_out', 'delta_final_norm', 'new_m_meta', 'new_m_even_norm', 'new_m_even_w_in', 'new_m_lru_conv_w', 'new_m_lru_conv_b', 'new_m_lru_w_a', 'new_m_lru_b_a', 'new_m_lru_w_x', 'new_m_lru_b_x', 'new_m_lru_lambda', 'new_m_ssd_conv_w', 'new_m_ssd_conv_b', 'new_m_ssd_dt_bias', 'new_m_ssd_a_log', 'new_m_ssd_d', 'new_m_ssd_norm', 'new_m_even_w_out', 'new_m_odd_norm', 'new_m_odd_w_in', 'new_m_odd_w_out', 'new_m_final_norm', 'new_v_meta', 'new_v_even_norm', 'new_v_even_w_in', 'new_v_lru_conv_w', 'new_v_lru_conv_b', 'new_v_lru_w_a', 'new_v_lru_b_a', 'new_v_lru_w_x', 'new_v_lru_b_x', 'new_v_lru_lambda', 'new_v_ssd_conv_w', 'new_v_ssd_conv_b', 'new_v_ssd_dt_bias', 'new_v_ssd_a_log', 'new_v_ssd_d', 'new_v_ssd_norm', 'new_v_even_w_out', 'new_v_odd_norm', 'new_v_odd_w_in', 'new_v_odd_w_out', 'new_v_final_norm']
TWIN_LEAF_KINDS = {'loss': 'loss', 'grad_x': 'grad_x', 'grad_meta': 'grad_w', 'grad_even_norm': 'grad_w', 'grad_even_w_in': 'grad_w', 'grad_lru_conv_w': 'grad_w', 'grad_lru_conv_b': 'grad_w', 'grad_lru_w_a': 'grad_w', 'grad_lru_b_a': 'grad_w', 'grad_lru_w_x': 'grad_w', 'grad_lru_b_x': 'grad_w', 'grad_lru_lambda': 'grad_w', 'grad_ssd_conv_w': 'grad_w', 'grad_ssd_conv_b': 'grad_w', 'grad_ssd_dt_bias': 'grad_w', 'grad_ssd_a_log': 'grad_w', 'grad_ssd_d': 'grad_w', 'grad_ssd_norm': 'grad_w', 'grad_even_w_out': 'grad_w', 'grad_odd_norm': 'grad_w', 'grad_odd_w_in': 'grad_w', 'grad_odd_w_out': 'grad_w', 'grad_final_norm': 'grad_w', 'delta_meta': 'delta_w', 'delta_even_norm': 'delta_w', 'delta_even_w_in': 'delta_w', 'delta_lru_conv_w': 'delta_w', 'delta_lru_conv_b': 'delta_w', 'delta_lru_w_a': 'delta_w', 'delta_lru_b_a': 'delta_w', 'delta_lru_w_x': 'delta_w', 'delta_lru_b_x': 'delta_w', 'delta_lru_lambda': 'delta_w', 'delta_ssd_conv_w': 'delta_w', 'delta_ssd_conv_b': 'delta_w', 'delta_ssd_dt_bias': 'delta_w', 'delta_ssd_a_log': 'delta_w', 'delta_ssd_d': 'delta_w', 'delta_ssd_norm': 'delta_w', 'delta_even_w_out': 'delta_w', 'delta_odd_norm': 'delta_w', 'delta_odd_w_in': 'delta_w', 'delta_odd_w_out': 'delta_w', 'delta_final_norm': 'delta_w', 'new_m_meta': 'new_m', 'new_m_even_norm': 'new_m', 'new_m_even_w_in': 'new_m', 'new_m_lru_conv_w': 'new_m', 'new_m_lru_conv_b': 'new_m', 'new_m_lru_w_a': 'new_m', 'new_m_lru_b_a': 'new_m', 'new_m_lru_w_x': 'new_m', 'new_m_lru_b_x': 'new_m', 'new_m_lru_lambda': 'new_m', 'new_m_ssd_conv_w': 'new_m', 'new_m_ssd_conv_b': 'new_m', 'new_m_ssd_dt_bias': 'new_m', 'new_m_ssd_a_log': 'new_m', 'new_m_ssd_d': 'new_m', 'new_m_ssd_norm': 'new_m', 'new_m_even_w_out': 'new_m', 'new_m_odd_norm': 'new_m', 'new_m_odd_w_in': 'new_m', 'new_m_odd_w_out': 'new_m', 'new_m_final_norm': 'new_m', 'new_v_meta': 'new_v', 'new_v_even_norm': 'new_v', 'new_v_even_w_in': 'new_v', 'new_v_lru_conv_w': 'new_v', 'new_v_lru_conv_b': 'new_v', 'new_v_lru_w_a': 'new_v', 'new_v_lru_b_a': 'new_v', 'new_v_lru_w_x': 'new_v', 'new_v_lru_b_x': 'new_v', 'new_v_lru_lambda': 'new_v', 'new_v_ssd_conv_w': 'new_v', 'new_v_ssd_conv_b': 'new_v', 'new_v_ssd_dt_bias': 'new_v', 'new_v_ssd_a_log': 'new_v', 'new_v_ssd_d': 'new_v', 'new_v_ssd_norm': 'new_v', 'new_v_even_w_out': 'new_v', 'new_v_odd_norm': 'new_v', 'new_v_odd_w_in': 'new_v', 'new_v_odd_w_out': 'new_v', 'new_v_final_norm': 'new_v'}


def _forward(args):
    return _fwd_reference(*[args[k] for k in FWD_PARAMS])


def _output_shape():
    out = _jax.eval_shape(lambda: _forward(_fwd_setup_inputs(0)))
    return out.shape, out.dtype

N_MICROBATCH = 1
ADAM_LR = 0.001
ADAM_B1 = 0.9
ADAM_B2 = 0.999
ADAM_EPS = 1e-08
ADAM_WD = 0.01
ADAM_STEP = 10
PER_EXAMPLE_BATCH_AXIS = {'x': 0, 'loss_target': 0}
SHARED_INPUTS = []
_WEIGHT_DTYPES = {'meta': _jnp.float32, 'even_norm': _jnp.float32, 'even_w_in': _jnp.float32, 'lru_conv_w': _jnp.float32, 'lru_conv_b': _jnp.float32, 'lru_w_a': _jnp.float32, 'lru_b_a': _jnp.float32, 'lru_w_x': _jnp.float32, 'lru_b_x': _jnp.float32, 'lru_lambda': _jnp.float32, 'ssd_conv_w': _jnp.float32, 'ssd_conv_b': _jnp.float32, 'ssd_dt_bias': _jnp.float32, 'ssd_a_log': _jnp.float32, 'ssd_d': _jnp.float32, 'ssd_norm': _jnp.float32, 'even_w_out': _jnp.float32, 'odd_norm': _jnp.float32, 'odd_w_in': _jnp.float32, 'odd_w_out': _jnp.float32, 'final_norm': _jnp.float32}
MOMENT_SCALE = {'meta': 5.717471e-03, 'even_norm': 2.529965e-01, 'even_w_in': 1.161728e-01, 'lru_conv_w': 6.450784e-02, 'lru_conv_b': 9.481270e-01, 'lru_w_a': 1.939526e-02, 'lru_b_a': 1.291687e-02, 'lru_w_x': 3.516693e-02, 'lru_b_x': 3.321649e-02, 'lru_lambda': 2.669825e-02, 'ssd_conv_w': 1.376995e-01, 'ssd_conv_b': 1.844764e-01, 'ssd_dt_bias': 1.155667e+00, 'ssd_a_log': 1.156599e+00, 'ssd_d': 9.586369e-01, 'ssd_norm': 1.562311e-01, 'even_w_out': 1.730571e-01, 'odd_norm': 1.158318e-01, 'odd_w_in': 5.962752e-02, 'odd_w_out': 7.587001e-02, 'final_norm': 6.410381e+01}


def _to_microbatches(a, axis):
    t = _jnp.moveaxis(a, axis, 0)
    t = t.reshape((N_MICROBATCH, t.shape[0] // N_MICROBATCH) + t.shape[1:])
    return _jnp.moveaxis(t, 1, axis + 1)


def setup_inputs(seed: int = 0) -> dict:
    inp = _fwd_setup_inputs(seed)
    key = _jax.random.fold_in(_jax.random.key(seed), 7919)
    shape, _ = _output_shape()
    out = dict(inp)
    out["loss_target"] = _jax.random.normal(_jax.random.fold_in(key, 0), shape, _jnp.float32)
    for i, name in enumerate(TWIN_WEIGHTS):
        w = inp[name].astype(_jnp.float32)
        if MOMENT_SCALE is None:
            s = _jnp.sqrt(_jnp.mean(_jnp.square(w)) + 1e-30)
        else:
            s = MOMENT_SCALE[name]
        km, kv = _jax.random.split(_jax.random.fold_in(key, i + 1))
        out[name] = w
        out["m_" + name] = s * _jax.random.normal(km, w.shape, _jnp.float32)
        out["v_" + name] = (s * s) * _jax.random.uniform(kv, w.shape, _jnp.float32, 0.5, 1.5)
    if N_MICROBATCH > 1:
        for name, axis in PER_EXAMPLE_BATCH_AXIS.items():
            out[name] = _to_microbatches(out[name], axis)
    return {'x': out['x'], 'meta': out['meta'], 'even_norm': out['even_norm'], 'even_w_in': out['even_w_in'], 'lru_conv_w': out['lru_conv_w'], 'lru_conv_b': out['lru_conv_b'], 'lru_w_a': out['lru_w_a'], 'lru_b_a': out['lru_b_a'], 'lru_w_x': out['lru_w_x'], 'lru_b_x': out['lru_b_x'], 'lru_lambda': out['lru_lambda'], 'ssd_conv_w': out['ssd_conv_w'], 'ssd_conv_b': out['ssd_conv_b'], 'ssd_dt_bias': out['ssd_dt_bias'], 'ssd_a_log': out['ssd_a_log'], 'ssd_d': out['ssd_d'], 'ssd_norm': out['ssd_norm'], 'even_w_out': out['even_w_out'], 'odd_norm': out['odd_norm'], 'odd_w_in': out['odd_w_in'], 'odd_w_out': out['odd_w_out'], 'final_norm': out['final_norm'], 'loss_target': out['loss_target'], 'm_meta': out['m_meta'], 'm_even_norm': out['m_even_norm'], 'm_even_w_in': out['m_even_w_in'], 'm_lru_conv_w': out['m_lru_conv_w'], 'm_lru_conv_b': out['m_lru_conv_b'], 'm_lru_w_a': out['m_lru_w_a'], 'm_lru_b_a': out['m_lru_b_a'], 'm_lru_w_x': out['m_lru_w_x'], 'm_lru_b_x': out['m_lru_b_x'], 'm_lru_lambda': out['m_lru_lambda'], 'm_ssd_conv_w': out['m_ssd_conv_w'], 'm_ssd_conv_b': out['m_ssd_conv_b'], 'm_ssd_dt_bias': out['m_ssd_dt_bias'], 'm_ssd_a_log': out['m_ssd_a_log'], 'm_ssd_d': out['m_ssd_d'], 'm_ssd_norm': out['m_ssd_norm'], 'm_even_w_out': out['m_even_w_out'], 'm_odd_norm': out['m_odd_norm'], 'm_odd_w_in': out['m_odd_w_in'], 'm_odd_w_out': out['m_odd_w_out'], 'm_final_norm': out['m_final_norm'], 'v_meta': out['v_meta'], 'v_even_norm': out['v_even_norm'], 'v_even_w_in': out['v_even_w_in'], 'v_lru_conv_w': out['v_lru_conv_w'], 'v_lru_conv_b': out['v_lru_conv_b'], 'v_lru_w_a': out['v_lru_w_a'], 'v_lru_b_a': out['v_lru_b_a'], 'v_lru_w_x': out['v_lru_w_x'], 'v_lru_b_x': out['v_lru_b_x'], 'v_lru_lambda': out['v_lru_lambda'], 'v_ssd_conv_w': out['v_ssd_conv_w'], 'v_ssd_conv_b': out['v_ssd_conv_b'], 'v_ssd_dt_bias': out['v_ssd_dt_bias'], 'v_ssd_a_log': out['v_ssd_a_log'], 'v_ssd_d': out['v_ssd_d'], 'v_ssd_norm': out['v_ssd_norm'], 'v_even_w_out': out['v_even_w_out'], 'v_odd_norm': out['v_odd_norm'], 'v_odd_w_in': out['v_odd_w_in'], 'v_odd_w_out': out['v_odd_w_out'], 'v_final_norm': out['v_final_norm']}


def _loss(weights, diff, rest, loss_target):
    with _jax.named_scope("forward"):
        args = {**rest, TWIN_DIFF_INPUT: diff, **{k: w.astype(_WEIGHT_DTYPES[k]) for k, w in weights.items()}}
        y = _forward(args)
    with _jax.named_scope("loss_head"):
        err = _jnp.square(y.astype(_jnp.float32) - loss_target)
        return 0.5 * _jnp.sum(_jnp.mean(err, axis=-1)) if err.ndim else 0.5 * err


def _adamw(w, g, m, v):
    m = ADAM_B1 * m + (1.0 - ADAM_B1) * g
    v = ADAM_B2 * v + (1.0 - ADAM_B2) * _jnp.square(g)
    m_hat = m / (1.0 - ADAM_B1 ** ADAM_STEP)
    v_hat = v / (1.0 - ADAM_B2 ** ADAM_STEP)
    delta = -ADAM_LR * (m_hat / (_jnp.sqrt(v_hat) + ADAM_EPS) + ADAM_WD * w)
    return delta, m, v


def reference(x, meta, even_norm, even_w_in, lru_conv_w, lru_conv_b, lru_w_a, lru_b_a, lru_w_x, lru_b_x, lru_lambda, ssd_conv_w, ssd_conv_b, ssd_dt_bias, ssd_a_log, ssd_d, ssd_norm, even_w_out, odd_norm, odd_w_in, odd_w_out, final_norm, loss_target, m_meta, m_even_norm, m_even_w_in, m_lru_conv_w, m_lru_conv_b, m_lru_w_a, m_lru_b_a, m_lru_w_x, m_lru_b_x, m_lru_lambda, m_ssd_conv_w, m_ssd_conv_b, m_ssd_dt_bias, m_ssd_a_log, m_ssd_d, m_ssd_norm, m_even_w_out, m_odd_norm, m_odd_w_in, m_odd_w_out, m_final_norm, v_meta, v_even_norm, v_even_w_in, v_lru_conv_w, v_lru_conv_b, v_lru_w_a, v_lru_b_a, v_lru_w_x, v_lru_b_x, v_lru_lambda, v_ssd_conv_w, v_ssd_conv_b, v_ssd_dt_bias, v_ssd_a_log, v_ssd_d, v_ssd_norm, v_even_w_out, v_odd_norm, v_odd_w_in, v_odd_w_out, v_final_norm):
    given = dict(x=x, meta=meta, even_norm=even_norm, even_w_in=even_w_in, lru_conv_w=lru_conv_w, lru_conv_b=lru_conv_b, lru_w_a=lru_w_a, lru_b_a=lru_b_a, lru_w_x=lru_w_x, lru_b_x=lru_b_x, lru_lambda=lru_lambda, ssd_conv_w=ssd_conv_w, ssd_conv_b=ssd_conv_b, ssd_dt_bias=ssd_dt_bias, ssd_a_log=ssd_a_log, ssd_d=ssd_d, ssd_norm=ssd_norm, even_w_out=even_w_out, odd_norm=odd_norm, odd_w_in=odd_w_in, odd_w_out=odd_w_out, final_norm=final_norm, loss_target=loss_target, m_meta=m_meta, m_even_norm=m_even_norm, m_even_w_in=m_even_w_in, m_lru_conv_w=m_lru_conv_w, m_lru_conv_b=m_lru_conv_b, m_lru_w_a=m_lru_w_a, m_lru_b_a=m_lru_b_a, m_lru_w_x=m_lru_w_x, m_lru_b_x=m_lru_b_x, m_lru_lambda=m_lru_lambda, m_ssd_conv_w=m_ssd_conv_w, m_ssd_conv_b=m_ssd_conv_b, m_ssd_dt_bias=m_ssd_dt_bias, m_ssd_a_log=m_ssd_a_log, m_ssd_d=m_ssd_d, m_ssd_norm=m_ssd_norm, m_even_w_out=m_even_w_out, m_odd_norm=m_odd_norm, m_odd_w_in=m_odd_w_in, m_odd_w_out=m_odd_w_out, m_final_norm=m_final_norm, v_meta=v_meta, v_even_norm=v_even_norm, v_even_w_in=v_even_w_in, v_lru_conv_w=v_lru_conv_w, v_lru_conv_b=v_lru_conv_b, v_lru_w_a=v_lru_w_a, v_lru_b_a=v_lru_b_a, v_lru_w_x=v_lru_w_x, v_lru_b_x=v_lru_b_x, v_lru_lambda=v_lru_lambda, v_ssd_conv_w=v_ssd_conv_w, v_ssd_conv_b=v_ssd_conv_b, v_ssd_dt_bias=v_ssd_dt_bias, v_ssd_a_log=v_ssd_a_log, v_ssd_d=v_ssd_d, v_ssd_norm=v_ssd_norm, v_even_w_out=v_even_w_out, v_odd_norm=v_odd_norm, v_odd_w_in=v_odd_w_in, v_odd_w_out=v_odd_w_out, v_final_norm=v_final_norm)
    weights = {n: given[n] for n in TWIN_WEIGHTS}
    shared = {n: given[n] for n in SHARED_INPUTS}
    per_example = {n: given[n] for n in ['x']}
    grad_fn = _jax.value_and_grad(_loss, argnums=(0, 1))

    def one_microbatch(ex, loss_target):
        ex = dict(ex)
        diff = ex.pop(TWIN_DIFF_INPUT)
        return grad_fn(weights, diff, {**shared, **ex}, loss_target)

    if N_MICROBATCH == 1:
        loss, (grad_w, grad_x) = one_microbatch(per_example, given["loss_target"])
    else:
        def body(carry, xs):
            loss_sum, grad_sum = carry
            l_k, (gw_k, gx_k) = one_microbatch(xs[0], xs[1])
            with _jax.named_scope("update"):
                return (loss_sum + l_k, _jax.tree.map(_jnp.add, grad_sum, gw_k)), gx_k

        init = (_jnp.zeros((), _jnp.float32), _jax.tree.map(_jnp.zeros_like, weights))
        (loss, grad_w), grad_x = _jax.lax.scan(body, init, (per_example, given["loss_target"]))
    with _jax.named_scope("update"):
        delta_w, new_m, new_v = {}, {}, {}
        for n in TWIN_WEIGHTS:
            delta_w[n], new_m[n], new_v[n] = _adamw(weights[n], grad_w[n], given["m_" + n], given["v_" + n])
    return (loss, grad_x, *[grad_w[n] for n in TWIN_WEIGHTS], *[delta_w[n] for n in TWIN_WEIGHTS],
            *[new_m[n] for n in TWIN_WEIGHTS], *[new_v[n] for n in TWIN_WEIGHTS])
```

```python
import functools

import jax
import jax.numpy as jnp
from jax import lax
from jax.experimental import pallas as pl
from jax.experimental.pallas import tpu as pltpu

F32 = jnp.float32
BF16 = jnp.bfloat16

D_MODEL = 1024
N_META = 16
LEAD = 128
PAD = LEAD - N_META
EPS = 1e-6
CONV_W = 4
LRU_BLOCKS = 4
LRU_BLOCK = 256
RG_LRU_C = 8.0
SSD_HEADS = 16
SSD_HEAD_DIM = 64
SSD_GROUPS = 2
SSD_HPG = 8
SSD_STATE = 128
CHUNK = 64
SSD_CONV_DIM = 1536
EVEN_IN = 4624
EVEN_IN_P = 4736
COL_LRU_X, COL_LRU_G, COL_Z, COL_XBC, COL_DT = 0, 1024, 2048, 3072, 4608
SB_HEADS = 16
SB_HEAD_DIM = 64
SB_BLOCK = 128
ADAM_LR, ADAM_B1, ADAM_B2, ADAM_EPS, ADAM_WD, ADAM_STEP = 0.001, 0.9, 0.999, 1e-08, 0.01, 10
VMEM_LIMIT_V7X = 56 * 1024 * 1024
MESH = pl.DeviceIdType.MESH
S = jax.ShapeDtypeStruct


def _tile(n, prefs):
    for p in prefs:
        if n % p == 0:
            return p
    raise ValueError(f"no tile of {prefs} divides {n}")


def _call(body, *, name, out_shape, grid=(), in_specs=None, out_specs=None, scratch=(), sem=None):
    kw = {}
    if in_specs is not None:
        kw["in_specs"] = in_specs
    if out_specs is not None:
        kw["out_specs"] = out_specs
    return pl.pallas_call(
        body, out_shape=out_shape, grid=grid, scratch_shapes=tuple(scratch), name=name,
        compiler_params=pltpu.CompilerParams(dimension_semantics=sem, vmem_limit_bytes=VMEM_LIMIT_V7X), **kw)


def _sigmoid(x):
    return 0.5 * (jnp.tanh(0.5 * x) + 1.0)


def _silu(x):
    return x * _sigmoid(x)


def _softplus(x):
    return jnp.maximum(x, 0.0) + jnp.log(1.0 + jnp.exp(-jnp.abs(x)))


def _dot(a, b):
    return jnp.dot(a, b, preferred_element_type=F32)


def _dot_nt(a, b):
    return lax.dot_general(a, b, (((1,), (1,)), ((), ())), preferred_element_type=F32)


def _dot_tn(a, b):
    return lax.dot_general(a, b, (((0,), (0,)), ((), ())), preferred_element_type=F32)


def mm_nn(a_list, w, *, name, resid=None):
    m = a_list[0].shape[0]
    k_tot, n = w.shape
    ks = [a.shape[1] for a in a_list]
    assert sum(ks) == k_tot
    tm = _tile(m, (256, 128))
    n_a = len(a_list)
    offs = [sum(ks[:i]) for i in range(n_a)]
    n_chunks = [(c0, min(512, n - c0)) for c0 in range(0, n, 512)]

    def body(*refs):
        a_refs, w_ref = refs[:n_a], refs[n_a]
        r_ref = refs[n_a + 1] if resid is not None else None
        o_ref = refs[-1]
        for c0, cw in n_chunks:
            acc = None
            for a_ref, k0, k in zip(a_refs, offs, ks):
                p = _dot(a_ref[...], w_ref[k0:k0 + k, c0:c0 + cw])
                acc = p if acc is None else acc + p
            if r_ref is not None:
                acc = acc + r_ref[:, c0:c0 + cw]
            o_ref[:, c0:c0 + cw] = acc

    in_specs = [pl.BlockSpec((tm, k), lambda i: (i, 0)) for k in ks]
    in_specs.append(pl.BlockSpec((k_tot, n), lambda i: (0, 0)))
    args = list(a_list) + [w]
    if resid is not None:
        in_specs.append(pl.BlockSpec((tm, n), lambda i: (i, 0)))
        args.append(resid)
    return _call(body, name=name, out_shape=S((m, n), F32), grid=(m // tm,), in_specs=in_specs,
                 out_specs=pl.BlockSpec((tm, n), lambda i: (i, 0)), sem=("parallel",))(*args)


def mm_tn(a, g, *, name):
    t, m = a.shape
    n = g.shape[1]
    tk = _tile(t, (512, 256, 128))
    tn = _tile(n, (512, 256, 128))

    def body(a_ref, g_ref, o_ref):
        @pl.when(pl.program_id(1) == 0)
        def _():
            o_ref[...] = jnp.zeros_like(o_ref)
        o_ref[...] += _dot_tn(a_ref[...], g_ref[...])

    return _call(body, name=name, out_shape=S((m, n), F32), grid=(n // tn, t // tk),
                 in_specs=[pl.BlockSpec((tk, m), lambda j, k: (k, 0)), pl.BlockSpec((tk, tn), lambda j, k: (k, j))],
                 out_specs=pl.BlockSpec((m, tn), lambda j, k: (0, j)), sem=("parallel", "arbitrary"))(a, g)


def rmsnorm_fwd(h, w, *, name):
    m, d = h.shape
    tm = _tile(m, (512, 256, 128))

    def body(h_ref, w_ref, o_ref):
        x = h_ref[...]
        r = lax.rsqrt(jnp.mean(x * x, axis=-1, keepdims=True) + EPS)
        o_ref[...] = (x * r * w_ref[...]).astype(BF16)

    return _call(body, name=name, out_shape=S((m, d), BF16), grid=(m // tm,),
                 in_specs=[pl.BlockSpec((tm, d), lambda i: (i, 0)), pl.BlockSpec((1, d), lambda i: (0, 0))],
                 out_specs=pl.BlockSpec((tm, d), lambda i: (i, 0)), sem=("parallel",))(h, w)


def rmsnorm_bwd(du, h, w, dres, *, name):
    m, d = h.shape
    tm = _tile(m, (256, 128))

    def body(du_ref, h_ref, w_ref, dr_ref, dh_ref, dhb_ref, dw_ref):
        @pl.when(pl.program_id(0) == 0)
        def _():
            dw_ref[...] = jnp.zeros_like(dw_ref)
        x = h_ref[...]
        r = lax.rsqrt(jnp.mean(x * x, axis=-1, keepdims=True) + EPS)
        du_ = du_ref[...]
        g = du_ * w_ref[...]
        c = jnp.mean(g * x, axis=-1, keepdims=True)
        dh = dr_ref[...] + r * g - x * (r * r * r) * c
        dh_ref[...] = dh
        dhb_ref[...] = dh.astype(BF16)
        dw_ref[...] += jnp.sum(du_ * x * r, axis=0, keepdims=True)

    row = pl.BlockSpec((tm, d), lambda i: (i, 0))
    vec = pl.BlockSpec((1, d), lambda i: (0, 0))
    return _call(body, name=name, out_shape=(S((m, d), F32), S((m, d), BF16), S((1, d), F32)), grid=(m // tm,),
                 in_specs=[row, row, vec, row], out_specs=(row, row, vec), sem=("arbitrary",))(du, h, w, dres)


def gate_fwd(o, o_cb, g, g_cb, *, name):
    m = o.shape[0]
    d = D_MODEL
    tm = _tile(m, (512, 256, 128))

    def body(o_ref, g_ref, y_ref):
        y_ref[...] = (o_ref[...] * _silu(g_ref[...])).astype(BF16)

    return _call(body, name=name, out_shape=S((m, d), BF16), grid=(m // tm,),
                 in_specs=[pl.BlockSpec((tm, d), lambda i: (i, o_cb)), pl.BlockSpec((tm, d), lambda i: (i, g_cb))],
                 out_specs=pl.BlockSpec((tm, d), lambda i: (i, 0)), sem=("parallel",))(o, g)


def gate_bwd(dy, dy_cb, o, o_cb, g, g_cb, *, name):
    m = o.shape[0]
    d = D_MODEL
    tm = _tile(m, (512, 256, 128))

    def body(dy_ref, o_ref, g_ref, do_ref, dg_ref):
        gv = g_ref[...]
        s = _sigmoid(gv)
        dyv = dy_ref[...]
        do_ref[...] = dyv * gv * s
        dg_ref[...] = (dyv * o_ref[...] * (s + gv * s * (1.0 - s))).astype(BF16)

    return _call(body, name=name, out_shape=(S((m, d), F32), S((m, d), BF16)), grid=(m // tm,),
                 in_specs=[pl.BlockSpec((tm, d), lambda i: (i, dy_cb)), pl.BlockSpec((tm, d), lambda i: (i, o_cb)),
                           pl.BlockSpec((tm, d), lambda i: (i, g_cb))],
                 out_specs=(pl.BlockSpec((tm, d), lambda i: (i, 0)),) * 2, sem=("parallel",))(dy, o, g)


def _group_mean(x):
    half = x.shape[1] // SSD_GROUPS
    parts = [jnp.broadcast_to(jnp.mean(x[:, k * half:(k + 1) * half], axis=-1, keepdims=True), (x.shape[0], half))
             for k in range(SSD_GROUPS)]
    return jnp.concatenate(parts, axis=1)


def gnorm_fwd(y, z, z_cb, w, *, name):
    m = y.shape[0]
    d = D_MODEL
    tm = _tile(m, (256, 128))

    def body(y_ref, z_ref, w_ref, o_ref):
        g = y_ref[...] * _silu(z_ref[...])
        r = lax.rsqrt(_group_mean(g * g) + EPS)
        o_ref[...] = (g * r * w_ref[...]).astype(BF16)

    return _call(body, name=name, out_shape=S((m, d), BF16), grid=(m // tm,),
                 in_specs=[pl.BlockSpec((tm, d), lambda i: (i, 0)), pl.BlockSpec((tm, d), lambda i: (i, z_cb)),
                           pl.BlockSpec((1, d), lambda i: (0, 0))],
                 out_specs=pl.BlockSpec((tm, d), lambda i: (i, 0)), sem=("parallel",))(y, z, w)


def gnorm_bwd(do, do_cb, y, z, z_cb, w, *, name):
    m = y.shape[0]
    d = D_MODEL
    tm = _tile(m, (256, 128))

    def body(do_ref, y_ref, z_ref, w_ref, dy_ref, dz_ref, dw_ref):
        @pl.when(pl.program_id(0) == 0)
        def _():
            dw_ref[...] = jnp.zeros_like(dw_ref)
        yv, zv, dov = y_ref[...], z_ref[...], do_ref[...]
        s = _sigmoid(zv)
        sz = zv * s
        g = yv * sz
        r = lax.rsqrt(_group_mean(g * g) + EPS)
        dw_ref[...] += jnp.sum(dov * g * r, axis=0, keepdims=True)
        dn = dov * w_ref[...]
        dg = r * dn - g * (r * r * r) * _group_mean(dn * g)
        dy_ref[...] = dg * sz
        dz_ref[...] = (dg * yv * (s + zv * s * (1.0 - s))).astype(BF16)

    row = pl.BlockSpec((tm, d), lambda i: (i, 0))
    vec = pl.BlockSpec((1, d), lambda i: (0, 0))
    return _call(body, name=name, out_shape=(S((m, d), F32), S((m, d), BF16), S((1, d), F32)), grid=(m // tm,),
                 in_specs=[pl.BlockSpec((tm, d), lambda i: (i, do_cb)), row, pl.BlockSpec((tm, d), lambda i: (i, z_cb)), vec],
                 out_specs=(row, row, vec), sem=("arbitrary",))(do, y, z, w)


def loss_head(h, target, w, lp, *, name):
    m, d = h.shape
    bsz = m // lp
    tm = SB_BLOCK
    nblk = lp // tm
    lead_blk = LEAD // tm

    def body(h_ref, t_ref, w_ref, dh_ref, dhb_ref, l_ref, dw_ref):
        i = pl.program_id(1)

        @pl.when(jnp.logical_and(pl.program_id(0) == 0, i == 0))
        def _():
            l_ref[...] = jnp.zeros_like(l_ref)
            dw_ref[...] = jnp.zeros_like(dw_ref)

        @pl.when(i < lead_blk)
        def _():
            dh_ref[...] = jnp.zeros_like(dh_ref)
            dhb_ref[...] = jnp.zeros_like(dhb_ref)

        @pl.when(i >= lead_blk)
        def _():
            x = h_ref[...]
            r = lax.rsqrt(jnp.mean(x * x, axis=-1, keepdims=True) + EPS)
            wv = w_ref[...]
            e = x * r * wv - t_ref[0]
            l_ref[...] += 0.5 * jnp.sum(jnp.mean(e * e, axis=-1, keepdims=True))
            dy = e * (1.0 / d)
            g = dy * wv
            c = jnp.mean(g * x, axis=-1, keepdims=True)
            dh = r * g - x * (r * r * r) * c
            dh_ref[...] = dh
            dhb_ref[...] = dh.astype(BF16)
            dw_ref[...] += jnp.sum(dy * x * r, axis=0, keepdims=True)

    row = pl.BlockSpec((tm, d), lambda b, i: (b * nblk + i, 0))
    return _call(body, name=name, out_shape=(S((m, d), F32), S((m, d), BF16), S((8, 128), F32), S((1, d), F32)),
                 grid=(bsz, nblk),
                 in_specs=[row, pl.BlockSpec((1, tm, d), lambda b, i: (b, jnp.maximum(i - lead_blk, 0), 0)),
                           pl.BlockSpec((1, d), lambda b, i: (0, 0))],
                 out_specs=(row, row, pl.BlockSpec((8, 128), lambda b, i: (0, 0)),
                            pl.BlockSpec((1, d), lambda b, i: (0, 0))),
                 sem=("arbitrary", "arbitrary"))(h, target, w)


def _conv_tiles(m, c):
    return _tile(m, (256, 128)), _tile(c, (512, 256, 128))


def conv_fwd(x, col0, c, w, b, *, name):
    m = x.shape[0]
    tm, tc = _conv_tiles(m, c)
    assert col0 % tc == 0
    cb0 = col0 // tc
    hb = tm // 8

    def body(x_ref, halo_ref, w_ref, b_ref, o_ref, ext):
        ext[0:8, :] = halo_ref[...]
        ext[8:8 + tm, :] = x_ref[...]
        acc = b_ref[...] + w_ref[3:4, :] * ext[8:8 + tm, :]
        for k in range(CONV_W - 1):
            acc = acc + w_ref[k:k + 1, :] * ext[pl.ds(5 + k, tm), :]
        o_ref[...] = acc

    return _call(body, name=name, out_shape=S((m, c), F32), grid=(m // tm, c // tc),
                 in_specs=[pl.BlockSpec((tm, tc), lambda i, j: (i, cb0 + j)),
                           pl.BlockSpec((8, tc), lambda i, j: (jnp.maximum(i * hb - 1, 0), cb0 + j)),
                           pl.BlockSpec((CONV_W, tc), lambda i, j: (0, j)), pl.BlockSpec((1, tc), lambda i, j: (0, j))],
                 out_specs=pl.BlockSpec((tm, tc), lambda i, j: (i, j)), scratch=[pltpu.VMEM((tm + 8, tc), F32)],
                 sem=("parallel", "parallel"))(x, x, w, b)


def conv_bwd(x, col0, c, dy, w, *, name):
    m = x.shape[0]
    tm, tc = _conv_tiles(m, c)
    cb0 = col0 // tc
    n_i = m // tm
    hb = tm // 8

    def body(x_ref, xh_ref, dy_ref, dyn_ref, w_ref, dx_ref, dw_ref, db_ref, ext, dext):
        i = pl.program_id(1)

        @pl.when(i == 0)
        def _():
            dw_ref[...] = jnp.zeros_like(dw_ref)
            db_ref[...] = jnp.zeros_like(db_ref)

        ext[0:8, :] = xh_ref[...]
        ext[8:8 + tm, :] = x_ref[...]
        d_cur = dy_ref[...]
        dext[0:tm, :] = d_cur
        dext[tm:tm + 8, :] = jnp.where(i == n_i - 1, 0.0, dyn_ref[...])
        dx = w_ref[3:4, :] * d_cur
        for k in range(CONV_W - 1):
            dx = dx + w_ref[k:k + 1, :] * dext[pl.ds(3 - k, tm), :]
        dx_ref[...] = dx.astype(BF16)
        for k in range(CONV_W - 1):
            dw_ref[k:k + 1, :] += jnp.sum(ext[pl.ds(5 + k, tm), :] * d_cur, axis=0, keepdims=True)
        dw_ref[3:4, :] += jnp.sum(ext[8:8 + tm, :] * d_cur, axis=0, keepdims=True)
        db_ref[...] += jnp.sum(d_cur, axis=0, keepdims=True)

    return _call(body, name=name, out_shape=(S((m, c), BF16), S((CONV_W, c), F32), S((1, c), F32)),
                 grid=(c // tc, n_i),
                 in_specs=[pl.BlockSpec((tm, tc), lambda j, i: (i, cb0 + j)),
                           pl.BlockSpec((8, tc), lambda j, i: (jnp.maximum(i * hb - 1, 0), cb0 + j)),
                           pl.BlockSpec((tm, tc), lambda j, i: (i, j)),
                           pl.BlockSpec((8, tc), lambda j, i: (jnp.minimum((i + 1) * hb, n_i * hb - 1), j)),
                           pl.BlockSpec((CONV_W, tc), lambda j, i: (0, j))],
                 out_specs=(pl.BlockSpec((tm, tc), lambda j, i: (i, j)), pl.BlockSpec((CONV_W, tc), lambda j, i: (0, j)),
                            pl.BlockSpec((1, tc), lambda j, i: (0, j))),
                 scratch=[pltpu.VMEM((tm + 8, tc), F32), pltpu.VMEM((tm + 8, tc), F32)],
                 sem=("parallel", "arbitrary"))(x, x, dy, dy, w)


def _row_valid(tile_idx, tiles_per_seq, tm):
    pos = lax.rem(tile_idx, tiles_per_seq) * tm + lax.broadcasted_iota(jnp.int32, (tm, 1), 0)
    return (pos >= PAD).astype(F32)


def _neg_expm1(x):
    small = -(x * (1.0 + x * (0.5 + x * (1.0 / 6.0))))
    return jnp.where(x > -0.01, small, 1.0 - jnp.exp(x))


def _gates_core(lx, wa, ba, wx, bx, lam, valid):
    lxb = lx.astype(BF16)
    r = _sigmoid(_dot(lxb, wa.astype(BF16)) + ba)
    i = _sigmoid(_dot(lxb, wx.astype(BF16)) + bx)
    log_a = (-RG_LRU_C) * r * _softplus(-lam)
    a = jnp.exp(log_a)
    mult = jnp.sqrt(_neg_expm1(2.0 * log_a))
    return a, valid * (mult * i * lx)


def gates_fwd(lx, wa, ba, wx, bx, lam, lp, *, name):
    m = lx.shape[0]
    tm = SB_BLOCK
    tps = lp // tm
    cb = LRU_BLOCK

    def body(lx_ref, wa_ref, ba_ref, wx_ref, bx_ref, lam_ref, a_ref, b_ref):
        valid = _row_valid(pl.program_id(1), tps, tm)
        a, b = _gates_core(lx_ref[...], wa_ref[0], ba_ref[...], wx_ref[0], bx_ref[...], lam_ref[...], valid)
        a_ref[...] = a
        b_ref[...] = b

    tok = pl.BlockSpec((tm, cb), lambda g, i: (i, g))
    wsp = pl.BlockSpec((1, cb, cb), lambda g, i: (g, 0, 0))
    vec = pl.BlockSpec((1, cb), lambda g, i: (0, g))
    return _call(body, name=name, out_shape=(S((m, D_MODEL), F32),) * 2, grid=(LRU_BLOCKS, m // tm),
                 in_specs=[tok, wsp, vec, wsp, vec, vec], out_specs=(tok, tok),
                 sem=("parallel", "parallel"))(lx, wa, ba, wx, bx, lam)


def gates_bwd(lx, da, db, wa, ba, wx, bx, lam, lp, *, name):
    m = lx.shape[0]
    tm = SB_BLOCK
    tps = lp // tm
    cb = LRU_BLOCK

    def body(lx_ref, da_ref, db_ref, wa_ref, ba_ref, wx_ref, bx_ref, lam_ref,
             dlx_ref, dwa_ref, dwx_ref, dba_ref, dbx_ref, dlam_ref):
        @pl.when(pl.program_id(1) == 0)
        def _():
            for r in (dwa_ref, dwx_ref, dba_ref, dbx_ref, dlam_ref):
                r[...] = jnp.zeros_like(r)
        valid = _row_valid(pl.program_id(1), tps, tm)
        core = functools.partial(_gates_core, valid=valid)
        _, vjp = jax.vjp(core, lx_ref[...], wa_ref[0].astype(F32), ba_ref[...], wx_ref[0].astype(F32), bx_ref[...],
                         lam_ref[...])
        dlx, dwa, dba, dwx, dbx, dlam = vjp((da_ref[...], db_ref[...]))
        dlx_ref[...] = dlx
        dwa_ref[0] += dwa
        dwx_ref[0] += dwx
        dba_ref[...] += dba
        dbx_ref[...] += dbx
        dlam_ref[...] += dlam

    tok = pl.BlockSpec((tm, cb), lambda g, i: (i, g))
    wsp = pl.BlockSpec((1, cb, cb), lambda g, i: (g, 0, 0))
    vec = pl.BlockSpec((1, cb), lambda g, i: (0, g))
    wshape = S((LRU_BLOCKS, cb, cb), F32)
    vshape = S((1, D_MODEL), F32)
    return _call(body, name=name, out_shape=(S((m, D_MODEL), F32), wshape, wshape, vshape, vshape, vshape),
                 grid=(LRU_BLOCKS, m // tm), in_specs=[tok, tok, tok, wsp, vec, wsp, vec, vec],
                 out_specs=(tok, wsp, wsp, vec, vec, vec),
                 sem=("parallel", "arbitrary"))(lx, da, db, wa, ba, wx, bx, lam)


SCAN_TOK = 128


def scan_fwd(a, b, lp, *, name):
    m = a.shape[0]
    bsz = m // lp
    nch = lp // SCAN_TOK
    rows = SCAN_TOK * 8

    def body(a_ref, b_ref, h_ref, carry):
        @pl.when(pl.program_id(1) == 0)
        def _():
            carry[...] = jnp.zeros_like(carry)

        def step(t, h):
            r = pl.ds(pl.multiple_of(t * 8, 8), 8)
            h = a_ref[r, :] * h + b_ref[r, :]
            h_ref[r, :] = h
            return h

        carry[...] = lax.fori_loop(0, SCAN_TOK, step, carry[...], unroll=8)

    blk = pl.BlockSpec((rows, 128), lambda s, c: (s * nch + c, 0))
    out = _call(body, name=name, out_shape=S((m * 8, 128), F32), grid=(bsz, nch), in_specs=[blk, blk], out_specs=blk,
                scratch=[pltpu.VMEM((8, 128), F32)], sem=("parallel", "arbitrary"))(
                    a.reshape(m * 8, 128), b.reshape(m * 8, 128))
    return out.reshape(m, D_MODEL)


def scan_bwd(a, h, dh, lp, *, name):
    m = a.shape[0]
    bsz = m // lp
    nch = lp // SCAN_TOK
    rows = SCAN_TOK * 8

    def body(a_ref, h_ref, hprev_ref, dh_ref, da_ref, db_ref, carry):
        c = pl.program_id(1)

        @pl.when(c == 0)
        def _():
            carry[...] = jnp.zeros_like(carry)

        h_before = jnp.where(c == nch - 1, 0.0, hprev_ref[...])

        def step(k, ag):
            t = SCAN_TOK - 1 - k
            r = pl.ds(pl.multiple_of(t * 8, 8), 8)
            g = dh_ref[r, :] + ag
            db_ref[r, :] = g
            tp = jnp.maximum(t - 1, 0)
            hp = jnp.where(t == 0, h_before, h_ref[pl.ds(pl.multiple_of(tp * 8, 8), 8), :])
            da_ref[r, :] = g * hp
            return a_ref[r, :] * g

        carry[...] = lax.fori_loop(0, SCAN_TOK, step, carry[...], unroll=8)

    blk = pl.BlockSpec((rows, 128), lambda s, c: (s * nch + (nch - 1 - c), 0))
    prev = pl.BlockSpec((8, 128), lambda s, c: (jnp.maximum((s * nch + (nch - 1 - c)) * SCAN_TOK - 1, 0), 0))
    a2, h2, dh2 = (v.reshape(m * 8, 128) for v in (a, h, dh))
    da, db = _call(body, name=name, out_shape=(S((m * 8, 128), F32),) * 2, grid=(bsz, nch),
                   in_specs=[blk, blk, prev, blk], out_specs=(blk, blk), scratch=[pltpu.VMEM((8, 128), F32)],
                   sem=("parallel", "arbitrary"))(a2, h2, h2, dh2)
    return da.reshape(m, D_MODEL), db.reshape(m, D_MODEL)


def _ssd_chunk(xs, bs, cs, dtr, ss, dtb, alog, dpar, valid):
    row = lax.broadcasted_iota(jnp.int32, (CHUNK, CHUNK), 0)
    col = lax.broadcasted_iota(jnp.int32, (CHUNK, CHUNK), 1)
    tri = row >= col
    dt = _softplus(dtr + dtb) * valid
    adt = dt * (-jnp.exp(alog))
    acum = jnp.dot(tri.astype(F32), adt, precision=lax.Precision.HIGHEST, preferred_element_type=F32)
    acum_t = acum.T
    b16 = [(_silu(b) * valid).astype(BF16) for b in bs]
    c16 = [(_silu(c) * valid).astype(BF16) for c in cs]
    cbs = [_dot_nt(c, b) for c, b in zip(c16, b16)]
    ys, s_new = [], []
    for h in range(SSD_HEADS):
        g = h // SSD_HPG
        x = _silu(xs[h])
        ac = acum[:, h:h + 1]
        lm = jnp.exp(jnp.where(tri, ac - acum_t[h:h + 1, :], -1e30))
        xd = x * dt[:, h:h + 1]
        alast = acum[CHUNK - 1:CHUNK, h:h + 1]
        y = _dot((cbs[g] * lm).astype(BF16), xd.astype(BF16))
        y = y + _dot_nt(c16[g], ss[h].astype(BF16)) * jnp.exp(ac)
        ys.append(y + x * dpar[:, h:h + 1])
        st = _dot_tn((xd * jnp.exp(alast - ac)).astype(BF16), b16[g])
        s_new.append(jnp.exp(alast) * ss[h] + st)
    return ys, s_new


def _ssd_load(pre_ref, s_ref):
    xs = [pre_ref[:, h * SSD_HEAD_DIM:(h + 1) * SSD_HEAD_DIM] for h in range(SSD_HEADS)]
    b0 = SSD_HEADS * SSD_HEAD_DIM
    bs = [pre_ref[:, b0 + g * SSD_STATE:b0 + (g + 1) * SSD_STATE] for g in range(SSD_GROUPS)]
    c0 = b0 + SSD_GROUPS * SSD_STATE
    cs = [pre_ref[:, c0 + g * SSD_STATE:c0 + (g + 1) * SSD_STATE] for g in range(SSD_GROUPS)]
    ss = [s_ref[h * SSD_HEAD_DIM:(h + 1) * SSD_HEAD_DIM, :] for h in range(SSD_HEADS)]
    return xs, bs, cs, ss


def ssd_fwd(pre, proj, dtb, alog, dpar, lp, *, name):
    m = pre.shape[0]
    bsz = m // lp
    nc = lp // CHUNK
    srows = SSD_HEADS * SSD_HEAD_DIM

    def body(pre_ref, dt_ref, dtb_ref, alog_ref, d_ref, y_ref, sin_ref, state):
        c = pl.program_id(1)

        @pl.when(c == 0)
        def _():
            state[...] = jnp.zeros_like(state)

        sin_ref[...] = state[...]
        xs, bs, cs, ss = _ssd_load(pre_ref, state)
        valid = _row_valid(c, nc, CHUNK)
        ys, s_new = _ssd_chunk(xs, bs, cs, dt_ref[:, 0:SSD_HEADS], ss, dtb_ref[...], alog_ref[...], d_ref[...], valid)
        for h in range(SSD_HEADS):
            y_ref[:, h * SSD_HEAD_DIM:(h + 1) * SSD_HEAD_DIM] = ys[h]
            state[h * SSD_HEAD_DIM:(h + 1) * SSD_HEAD_DIM, :] = s_new[h]

    par = pl.BlockSpec((1, SSD_HEADS), lambda s, c: (0, 0))
    return _call(body, name=name, out_shape=(S((m, D_MODEL), F32), S((m // CHUNK * srows, SSD_STATE), F32)),
                 grid=(bsz, nc),
                 in_specs=[pl.BlockSpec((CHUNK, SSD_CONV_DIM), lambda s, c: (s * nc + c, 0)),
                           pl.BlockSpec((CHUNK, 128), lambda s, c: (s * nc + c, COL_DT // 128)), par, par, par],
                 out_specs=(pl.BlockSpec((CHUNK, D_MODEL), lambda s, c: (s * nc + c, 0)),
                            pl.BlockSpec((srows, SSD_STATE), lambda s, c: (s * nc + c, 0))),
                 scratch=[pltpu.VMEM((srows, SSD_STATE), F32)],
                 sem=("parallel", "arbitrary"))(pre, proj, dtb, alog, dpar)


def ssd_bwd(pre, proj, s_in, dy, dtb, alog, dpar, lp, *, name):
    m = pre.shape[0]
    bsz = m // lp
    nc = lp // CHUNK
    srows = SSD_HEADS * SSD_HEAD_DIM

    def body(pre_ref, dt_ref, sin_ref, dy_ref, dtb_ref, alog_ref, d_ref,
             dpre_ref, ddt_ref, ddtb_ref, dalog_ref, dd_ref, dstate):
        c = pl.program_id(1)

        @pl.when(jnp.logical_and(pl.program_id(0) == 0, c == 0))
        def _():
            for r in (ddtb_ref, dalog_ref, dd_ref):
                r[...] = jnp.zeros_like(r)

        @pl.when(c == 0)
        def _():
            dstate[...] = jnp.zeros_like(dstate)

        xs, bs, cs, ss = _ssd_load(pre_ref, sin_ref)
        valid = _row_valid(nc - 1 - c, nc, CHUNK)
        core = functools.partial(_ssd_chunk, valid=valid)
        _, vjp = jax.vjp(core, xs, bs, cs, dt_ref[:, 0:SSD_HEADS], ss, dtb_ref[...], alog_ref[...], d_ref[...])
        dys = [dy_ref[:, h * SSD_HEAD_DIM:(h + 1) * SSD_HEAD_DIM] for h in range(SSD_HEADS)]
        dsn = [dstate[h * SSD_HEAD_DIM:(h + 1) * SSD_HEAD_DIM, :] for h in range(SSD_HEADS)]
        dxs, dbs, dcs, ddtr, dss, ddtb, dalog, dd = vjp((dys, dsn))
        b0 = SSD_HEADS * SSD_HEAD_DIM
        c0 = b0 + SSD_GROUPS * SSD_STATE
        for h in range(SSD_HEADS):
            dpre_ref[:, h * SSD_HEAD_DIM:(h + 1) * SSD_HEAD_DIM] = dxs[h]
            dstate[h * SSD_HEAD_DIM:(h + 1) * SSD_HEAD_DIM, :] = dss[h]
        for g in range(SSD_GROUPS):
            dpre_ref[:, b0 + g * SSD_STATE:b0 + (g + 1) * SSD_STATE] = dbs[g]
            dpre_ref[:, c0 + g * SSD_STATE:c0 + (g + 1) * SSD_STATE] = dcs[g]
        ddt_ref[...] = jnp.zeros_like(ddt_ref)
        ddt_ref[:, 0:SSD_HEADS] = ddtr.astype(BF16)
        ddtb_ref[...] += ddtb
        dalog_ref[...] += dalog
        dd_ref[...] += dd

    par = pl.BlockSpec((1, SSD_HEADS), lambda s, c: (0, 0))
    rev = lambda s, c: (s * nc + (nc - 1 - c), 0)
    pshape = S((1, SSD_HEADS), F32)
    return _call(body, name=name,
                 out_shape=(S((m, SSD_CONV_DIM), F32), S((m, 128), BF16), pshape, pshape, pshape), grid=(bsz, nc),
                 in_specs=[pl.BlockSpec((CHUNK, SSD_CONV_DIM), rev),
                           pl.BlockSpec((CHUNK, 128), lambda s, c: (s * nc + (nc - 1 - c), COL_DT // 128)),
                           pl.BlockSpec((srows, SSD_STATE), rev), pl.BlockSpec((CHUNK, D_MODEL), rev), par, par, par],
                 out_specs=(pl.BlockSpec((CHUNK, SSD_CONV_DIM), rev), pl.BlockSpec((CHUNK, 128), rev), par, par, par),
                 scratch=[pltpu.VMEM((srows, SSD_STATE), F32)],
                 sem=("arbitrary", "arbitrary"))(pre, proj, s_in, dy, dtb, alog, dpar)


def _suffix_mats():
    r = lax.broadcasted_iota(jnp.int32, (SB_BLOCK, 2 * SB_BLOCK), 0)
    c = lax.broadcasted_iota(jnp.int32, (SB_BLOCK, 2 * SB_BLOCK), 1)
    later = jnp.logical_or(c >= SB_BLOCK, r > c).astype(BF16)
    earlier = jnp.logical_or(c >= SB_BLOCK, r < c).astype(BF16)
    return later, earlier


def _split_dot(x, mat):
    hi = x.astype(BF16)
    lo = (x - hi.astype(F32)).astype(BF16)
    return _dot(hi, mat) + _dot(lo, mat)


def _sb_tile(q_i, k_j, i, j, m_strict):
    z = _dot_nt(q_i, k_j)
    t_pos = i * SB_BLOCK + lax.broadcasted_iota(jnp.int32, (SB_BLOCK, SB_BLOCK), 0)
    s_pos = j * SB_BLOCK + lax.broadcasted_iota(jnp.int32, (SB_BLOCK, SB_BLOCK), 1)
    valid = jnp.logical_and(s_pos < t_pos, s_pos >= PAD)
    lk = jnp.where(valid, -_softplus(z), 0.0)
    tsum = _split_dot(lk, m_strict)
    return valid, z, lk, tsum


def attn_fwd(qkvg, lp, *, name):
    m = qkvg.shape[0]
    bsz = m // lp
    nb = lp // SB_BLOCK
    hd = SB_HEAD_DIM

    def body(q_ref, k_ref, v_ref, o_ref, qs, ks, vs):
        m_strict, _ = _suffix_mats()
        for hh in range(2):
            ln = slice(hh * hd, (hh + 1) * hd)
            qs[...] = (q_ref[:, ln] * (hd ** -0.5)).astype(BF16)
            ks[...] = k_ref[:, ln].astype(BF16)
            vs[...] = v_ref[:, ln].astype(BF16)

            def q_block(i, _):
                rows = pl.ds(pl.multiple_of(i * SB_BLOCK, SB_BLOCK), SB_BLOCK)
                q_i = qs[rows, :]

                def k_block(jj, carry):
                    acc, run = carry
                    j = i - jj
                    cols = pl.ds(pl.multiple_of(j * SB_BLOCK, SB_BLOCK), SB_BLOCK)
                    valid, z, lk, tsum = _sb_tile(q_i, ks[cols, :], i, j, m_strict)
                    w = jnp.where(valid, jnp.exp(z + lk + tsum[:, :SB_BLOCK] + run), 0.0)
                    acc = acc + _dot(w.astype(BF16), vs[cols, :])
                    return acc, run + tsum[:, SB_BLOCK:]

                acc, _ = lax.fori_loop(0, i + 1, k_block,
                                       (jnp.zeros((SB_BLOCK, hd), F32), jnp.zeros((SB_BLOCK, SB_BLOCK), F32)))
                o_ref[rows, ln] = acc
                return 0

            lax.fori_loop(0, nb, q_block, 0)

    blk = lambda cb: pl.BlockSpec((lp, 128), lambda s, p: (s, cb * 8 + p))
    return _call(body, name=name, out_shape=S((m, D_MODEL), F32), grid=(bsz, 8),
                 in_specs=[blk(0), blk(1), blk(2)], out_specs=pl.BlockSpec((lp, 128), lambda s, p: (s, p)),
                 scratch=[pltpu.VMEM((lp, hd), BF16)] * 3, sem=("parallel", "parallel"))(qkvg, qkvg, qkvg)


def attn_bwd(qkvg, do, lp, *, name):
    m = qkvg.shape[0]
    bsz = m // lp
    nb = lp // SB_BLOCK
    hd = SB_HEAD_DIM
    scale = hd ** -0.5

    def body(q_ref, k_ref, v_ref, do_ref, dq_ref, dk_ref, dv_ref, qs, ks, vs, dos, dka, dva, g_keep, s_keep):
        m_later, m_earlier = _suffix_mats()
        for hh in range(2):
            ln = slice(hh * hd, (hh + 1) * hd)
            qs[...] = (q_ref[:, ln] * scale).astype(BF16)
            ks[...] = k_ref[:, ln].astype(BF16)
            vs[...] = v_ref[:, ln].astype(BF16)
            dos[...] = do_ref[:, ln].astype(BF16)
            dka[...] = jnp.zeros_like(dka)
            dva[...] = jnp.zeros_like(dva)

            def q_block(i, _):
                rows = pl.ds(pl.multiple_of(i * SB_BLOCK, SB_BLOCK), SB_BLOCK)
                q_i = qs[rows, :]
                do_i = dos[rows, :]

                def sweep_left(jj, run):
                    j = i - jj
                    cols = pl.ds(pl.multiple_of(j * SB_BLOCK, SB_BLOCK), SB_BLOCK)
                    valid, z, lk, tsum = _sb_tile(q_i, ks[cols, :], i, j, m_later)
                    sig = jnp.where(valid, jnp.exp(z + lk), 0.0)
                    w = sig * jnp.exp(tsum[:, :SB_BLOCK] + run)
                    g_keep[j] = _dot_nt(do_i, vs[cols, :]) * w
                    s_keep[j] = sig
                    dva[cols, :] += _dot_tn(w.astype(BF16), do_i)
                    return run + tsum[:, SB_BLOCK:]

                lax.fori_loop(0, i + 1, sweep_left, jnp.zeros((SB_BLOCK, SB_BLOCK), F32))

                def sweep_right(j, carry):
                    dq, grun = carry
                    cols = pl.ds(pl.multiple_of(j * SB_BLOCK, SB_BLOCK), SB_BLOCK)
                    gmat = g_keep[j]
                    sig = s_keep[j]
                    gsum = _split_dot(gmat, m_earlier)
                    dz = (gmat - sig * (gmat + gsum[:, :SB_BLOCK] + grun)).astype(BF16)
                    dq = dq + _dot(dz, ks[cols, :])
                    dka[cols, :] += _dot_tn(dz, q_i)
                    return dq, grun + gsum[:, SB_BLOCK:]

                dq, _ = lax.fori_loop(0, i + 1, sweep_right,
                                      (jnp.zeros((SB_BLOCK, hd), F32), jnp.zeros((SB_BLOCK, SB_BLOCK), F32)))
                dq_ref[rows, ln] = (dq * scale).astype(BF16)
                return 0

            lax.fori_loop(0, nb, q_block, 0)
            dk_ref[:, ln] = dka[...].astype(BF16)
            dv_ref[:, ln] = dva[...].astype(BF16)

    blk = lambda cb: pl.BlockSpec((lp, 128), lambda s, p: (s, cb * 8 + p))
    one = pl.BlockSpec((lp, 128), lambda s, p: (s, p))
    return _call(body, name=name, out_shape=(S((m, D_MODEL), BF16),) * 3, grid=(bsz, 8),
                 in_specs=[blk(0), blk(1), blk(2), one], out_specs=(one, one, one),
                 scratch=[pltpu.VMEM((lp, hd), BF16)] * 4 + [pltpu.VMEM((lp, hd), F32)] * 2
                 + [pltpu.VMEM((nb, SB_BLOCK, SB_BLOCK), F32)] * 2,
                 sem=("parallel", "parallel"))(qkvg, qkvg, qkvg, do)


def meta_grad(dh, lp, *, name):
    m, d = dh.shape
    bsz = m // lp
    per = lp // N_META

    def body(dh_ref, o_ref):
        @pl.when(pl.program_id(0) == 0)
        def _():
            o_ref[...] = jnp.zeros_like(o_ref)
        o_ref[...] += dh_ref[...]

    return _call(body, name=name, out_shape=S((N_META, d), F32), grid=(bsz,),
                 in_specs=[pl.BlockSpec((N_META, d), lambda b: (b * per + PAD // N_META, 0))],
                 out_specs=pl.BlockSpec((N_META, d), lambda b: (0, 0)), sem=("arbitrary",))(dh)


def sum_lead(arr, *, name):
    n, r, c = arr.shape
    tr = _tile(r, (128, 64, 32, 16, 8))

    def body(a_ref, o_ref):
        acc = a_ref[0]
        for k in range(1, n):
            acc = acc + a_ref[k]
        o_ref[...] = acc

    return _call(body, name=name, out_shape=S((r, c), F32), grid=(r // tr,),
                 in_specs=[pl.BlockSpec((n, tr, c), lambda i: (0, i, 0))],
                 out_specs=pl.BlockSpec((tr, c), lambda i: (i, 0)), sem=("parallel",))(arr)


def adamw(w, g_parts, mom, var, *, name):
    r, c = w.shape
    tr = _tile(r, (128, 64, 32, 16, 8))
    n_g = len(g_parts)
    c1 = 1.0 - ADAM_B1 ** ADAM_STEP
    c2 = 1.0 - ADAM_B2 ** ADAM_STEP

    def body(*refs):
        w_ref, g_refs, m_ref, v_ref = refs[0], refs[1:1 + n_g], refs[1 + n_g], refs[2 + n_g]
        g_out, d_out, m_out, v_out = refs[3 + n_g:]
        g = g_refs[0][...]
        for gr in g_refs[1:]:
            g = g + gr[...]
        mn = ADAM_B1 * m_ref[...] + (1.0 - ADAM_B1) * g
        vn = ADAM_B2 * v_ref[...] + (1.0 - ADAM_B2) * (g * g)
        g_out[...] = g
        m_out[...] = mn
        v_out[...] = vn
        d_out[...] = -ADAM_LR * ((mn / c1) / (jnp.sqrt(vn / c2) + ADAM_EPS) + ADAM_WD * w_ref[...])

    blk = pl.BlockSpec((tr, c), lambda i: (i, 0))
    return _call(body, name=name, out_shape=(S((r, c), F32),) * 4, grid=(r // tr,), in_specs=[blk] * (3 + n_g),
                 out_specs=(blk,) * 4, sem=("parallel",))(w, *g_parts, mom, var)


_ANY = pl.BlockSpec(memory_space=pl.ANY)


def _position():
    return lax.axis_index("x"), lax.axis_index("y"), lax.axis_index("c")


def _other_chips(x, y):
    return [(1 - x, y), (x, 1 - y), (1 - x, 1 - y)]


def _comm_call(body, arrs, out_shapes, n_sem, *, name):
    return pl.pallas_call(
        body, out_shape=tuple(out_shapes), in_specs=[_ANY] * len(arrs), out_specs=tuple([_ANY] * len(out_shapes)),
        scratch_shapes=(pltpu.SemaphoreType.DMA((n_sem,)), pltpu.SemaphoreType.DMA((n_sem,)),
                        pltpu.SemaphoreType.DMA((len(arrs),))),
        name=name)(*arrs)


def allgather_chips(arrs, *, name):
    n = len(arrs)

    def body(*refs):
        ins, outs = refs[:n], refs[n:2 * n]
        send_sems, recv_sems, loc_sems = refs[2 * n:]
        x, y, c = _position()
        me = 2 * x + y
        copies = []
        for a in range(n):
            loc = pltpu.make_async_copy(ins[a], outs[a].at[me], loc_sems.at[a])
            loc.start()
            copies.append(loc)
            for k, (px, py) in enumerate(_other_chips(x, y)):
                cp = pltpu.make_async_remote_copy(
                    src_ref=ins[a], dst_ref=outs[a].at[me], send_sem=send_sems.at[3 * a + k],
                    recv_sem=recv_sems.at[3 * a + k], device_id=(px, py, c), device_id_type=MESH)
                cp.start()
                copies.append(cp)
        for cp in copies:
            cp.wait()

    outs = [S((4,) + a.shape, a.dtype) for a in arrs]
    return _comm_call(body, arrs, outs, 3 * n, name=name)


def exchange_chips(arrs, small, *, name):
    n = len(arrs)

    def body(*refs):
        ins, small_in = refs[:n], refs[n]
        outs, small_out = refs[n + 1:2 * n + 1], refs[2 * n + 1]
        send_sems, recv_sems, loc_sems = refs[2 * n + 2:]
        x, y, c = _position()
        me = 2 * x + y
        copies = []
        for a in range(n):
            loc = pltpu.make_async_copy(ins[a].at[me], outs[a].at[me], loc_sems.at[a])
            loc.start()
            copies.append(loc)
            for k, (px, py) in enumerate(_other_chips(x, y)):
                cp = pltpu.make_async_remote_copy(
                    src_ref=ins[a].at[2 * px + py], dst_ref=outs[a].at[me], send_sem=send_sems.at[3 * a + k],
                    recv_sem=recv_sems.at[3 * a + k], device_id=(px, py, c), device_id_type=MESH)
                cp.start()
                copies.append(cp)
        me8 = 4 * x + 2 * y + c
        loc = pltpu.make_async_copy(small_in, small_out.at[me8], loc_sems.at[n])
        loc.start()
        copies.append(loc)
        k = 3 * n
        for fx in (0, 1):
            for fy in (0, 1):
                for fc in (0, 1):
                    if fx + fy + fc == 0:
                        continue
                    peer = (1 - x if fx else x, 1 - y if fy else y, 1 - c if fc else c)
                    cp = pltpu.make_async_remote_copy(
                        src_ref=small_in, dst_ref=small_out.at[me8], send_sem=send_sems.at[k],
                        recv_sem=recv_sems.at[k], device_id=peer, device_id_type=MESH)
                    cp.start()
                    copies.append(cp)
                    k += 1
        for cp in copies:
            cp.wait()

    outs = [S(a.shape, a.dtype) for a in arrs] + [S((8,) + small.shape, small.dtype)]
    return _comm_call(body, list(arrs) + [small], outs, 3 * n + 7, name=name)


def swap_cores(arrs, *, name):
    n = len(arrs)

    def body(*refs):
        ins, outs = refs[:n], refs[n:2 * n]
        send_sems, recv_sems, _ = refs[2 * n:]
        x, y, c = _position()
        copies = []
        for a in range(n):
            cp = pltpu.make_async_remote_copy(
                src_ref=ins[a], dst_ref=outs[a], send_sem=send_sems.at[a], recv_sem=recv_sems.at[a],
                device_id=(x, y, 1 - c), device_id_type=MESH)
            cp.start()
            copies.append(cp)
        for cp in copies:
            cp.wait()

    return _comm_call(body, arrs, [S(a.shape, a.dtype) for a in arrs], n, name=name)


def _local_step(p, x, target):
    bsz, seq, d = x.shape
    lp = LEAD + seq
    m = bsz * lp
    h0 = jnp.concatenate([jnp.zeros((bsz, PAD, d), F32), jnp.broadcast_to(p["meta"][None], (bsz, N_META, d)), x],
                         axis=1).reshape(m, d)
    u0 = rmsnorm_fwd(h0, p["even_norm"], name="norm0")
    proj = mm_nn([u0], p["win_e"], name="proj0")
    lx = conv_fwd(proj, COL_LRU_X, D_MODEL, p["lru_conv_w"], p["lru_conv_b"], name="lru_conv")
    a, b = gates_fwd(lx, p["lru_w_a"], p["lru_b_a"], p["lru_w_x"], p["lru_b_x"], p["lru_lambda"], lp, name="lru_gates")
    hs = scan_fwd(a, b, lp, name="lru_scan")
    ya = gate_fwd(hs, 0, proj, COL_LRU_G // D_MODEL, name="lru_out_gate")
    pre = conv_fwd(proj, COL_XBC, SSD_CONV_DIM, p["ssd_conv_w"], p["ssd_conv_b"], name="ssd_conv")
    y, s_in = ssd_fwd(pre, proj, p["ssd_dt_bias"], p["ssd_a_log"], p["ssd_d"], lp, name="ssd")
    yb = gnorm_fwd(y, proj, COL_Z // D_MODEL, p["ssd_norm"], name="ssd_norm")
    h1 = mm_nn([ya, yb], p["wout_e"], resid=h0, name="out0")
    u1 = rmsnorm_fwd(h1, p["odd_norm"], name="norm1")
    qkvg = mm_nn([u1], p["win_o"], name="proj1")
    o = attn_fwd(qkvg, lp, name="attn")
    og = gate_fwd(o, 0, qkvg, 3, name="attn_gate")
    h2 = mm_nn([og], p["wout_o"], resid=h1, name="out1")
    dh2, dh2b, loss, d_final = loss_head(h2, target, p["final_norm"], lp, name="loss_head")
    g = {"final_norm": d_final}
    g["odd_w_out"] = mm_tn(og, dh2b, name="dw_out1")
    d_og = mm_nn([dh2b], p["wout_o_t"], name="d_out1")
    do, dgate = gate_bwd(d_og, 0, o, 0, qkvg, 3, name="attn_gate_bwd")
    dq, dk, dv = attn_bwd(qkvg, do, lp, name="attn_bwd")
    segs1 = [dq, dk, dv, dgate]
    du1 = mm_nn(segs1, p["win_o_t"], name="d_proj1")
    g["odd_w_in"] = jnp.concatenate([mm_tn(u1, t, name=f"dw_proj1_{k}") for k, t in enumerate(segs1)], axis=1)
    dh1, dh1b, g["odd_norm"] = rmsnorm_bwd(du1, h1, p["odd_norm"], dh2, name="norm1_bwd")
    g["even_w_out"] = jnp.concatenate([mm_tn(ya, dh1b, name="dw_out0_a"), mm_tn(yb, dh1b, name="dw_out0_b")], axis=0)
    d_mixed = mm_nn([dh1b], p["wout_e_t"], name="d_out0")
    dhs, dlg = gate_bwd(d_mixed, 0, hs, 0, proj, COL_LRU_G // D_MODEL, name="lru_out_gate_bwd")
    dy, dz, g["ssd_norm"] = gnorm_bwd(d_mixed, 1, y, proj, COL_Z // D_MODEL, p["ssd_norm"], name="ssd_norm_bwd")
    dpre, ddt, g["ssd_dt_bias"], g["ssd_a_log"], g["ssd_d"] = ssd_bwd(
        pre, proj, s_in, dy, p["ssd_dt_bias"], p["ssd_a_log"], p["ssd_d"], lp, name="ssd_bwd")
    dxbc, g["ssd_conv_w"], g["ssd_conv_b"] = conv_bwd(proj, COL_XBC, SSD_CONV_DIM, dpre, p["ssd_conv_w"],
                                                      name="ssd_conv_bwd")
    da, db = scan_bwd(a, hs, dhs, lp, name="lru_scan_bwd")
    dlx, g["lru_w_a"], g["lru_w_x"], g["lru_b_a"], g["lru_b_x"], g["lru_lambda"] = gates_bwd(
        lx, da, db, p["lru_w_a"], p["lru_b_a"], p["lru_w_x"], p["lru_b_x"], p["lru_lambda"], lp, name="lru_gates_bwd")
    dlrux, g["lru_conv_w"], g["lru_conv_b"] = conv_bwd(proj, COL_LRU_X, D_MODEL, dlx, p["lru_conv_w"],
                                                       name="lru_conv_bwd")
    segs0 = [dlrux, dlg, dz, dxbc, ddt]
    du0 = mm_nn(segs0, p["win_e_t"], name="d_proj0")
    g["even_w_in"] = jnp.concatenate([mm_tn(u0, t, name=f"dw_proj0_{k}") for k, t in enumerate(segs0)],
                                     axis=1)[:, :EVEN_IN]
    dh0, _, g["even_norm"] = rmsnorm_bwd(du0, h0, p["even_norm"], dh1, name="norm0_bwd")
    g["meta"] = meta_grad(dh0, lp, name="meta_grad")
    grad_x = dh0.reshape(bsz, lp, d)[:, LEAD:]
    return loss, grad_x, g


WEIGHTS = ("meta", "even_norm", "even_w_in", "lru_conv_w", "lru_conv_b", "lru_w_a", "lru_b_a", "lru_w_x", "lru_b_x",
           "lru_lambda", "ssd_conv_w", "ssd_conv_b", "ssd_dt_bias", "ssd_a_log", "ssd_d", "ssd_norm", "even_w_out",
           "odd_norm", "odd_w_in", "odd_w_out", "final_norm")
BIG = ("even_w_in", "even_w_out", "odd_w_in", "odd_w_out", "lru_w_a", "lru_w_x")
SHARDED_SMALL = {"meta": 256, "lru_conv_w": 256, "ssd_conv_w": 384, "odd_norm": 256}
SMALL_SHAPES = {"meta": (16, 1024), "even_norm": (1, 1024), "lru_conv_w": (4, 1024), "lru_conv_b": (1, 1024),
                "lru_b_a": (1, 1024), "lru_b_x": (1, 1024), "lru_lambda": (1, 1024), "ssd_conv_w": (4, 1536),
                "ssd_conv_b": (1, 1536), "ssd_dt_bias": (1, 16), "ssd_a_log": (1, 16), "ssd_d": (1, 16),
                "ssd_norm": (1, 1024), "odd_norm": (1, 1024), "final_norm": (1, 1024)}
PACK_UNIT = 1024


def _pack(parts):
    flat = []
    for part in parts:
        v = part.reshape(-1)
        flat.append(jnp.pad(v, (0, -v.shape[0] % PACK_UNIT)))
    return jnp.concatenate(flat).reshape(-1, 128)


def _unpack(buf, shapes):
    v = buf.reshape(-1)
    out, off = [], 0
    for shp in shapes:
        n = 1
        for s_ in shp:
            n *= s_
        out.append(v[off:off + n].reshape(shp))
        off += n + (-n % PACK_UNIT)
    return out


def _chip_cols(a4):
    return jnp.transpose(a4, (1, 0, 2)).reshape(a4.shape[1], -1)


def _to_chip_cols(a, cols):
    return jnp.transpose(a.reshape(a.shape[0], 4, cols), (1, 0, 2))


def kernel(x, meta, even_norm, even_w_in, lru_conv_w, lru_conv_b, lru_w_a, lru_b_a, lru_w_x, lru_b_x, lru_lambda, ssd_conv_w, ssd_conv_b, ssd_dt_bias, ssd_a_log, ssd_d, ssd_norm, even_w_out, odd_norm, odd_w_in, odd_w_out, final_norm, loss_target, m_meta, m_even_norm, m_even_w_in, m_lru_conv_w, m_lru_conv_b, m_lru_w_a, m_lru_b_a, m_lru_w_x, m_lru_b_x, m_lru_lambda, m_ssd_conv_w, m_ssd_conv_b, m_ssd_dt_bias, m_ssd_a_log, m_ssd_d, m_ssd_norm, m_even_w_out, m_odd_norm, m_odd_w_in, m_odd_w_out, m_final_norm, v_meta, v_even_norm, v_even_w_in, v_lru_conv_w, v_lru_conv_b, v_lru_w_a, v_lru_b_a, v_lru_w_x, v_lru_b_x, v_lru_lambda, v_ssd_conv_w, v_ssd_conv_b, v_ssd_dt_bias, v_ssd_a_log, v_ssd_d, v_ssd_norm, v_even_w_out, v_odd_norm, v_odd_w_in, v_odd_w_out, v_final_norm):
    given = dict(locals())
    w = {n: given[n] for n in WEIGHTS}
    mom = {n: given["m_" + n] for n in WEIGHTS}
    var = {n: given["v_" + n] for n in WEIGHTS}
    chip = 2 * lax.axis_index("x") + lax.axis_index("y")

    big_local = {"even_w_in": even_w_in[0], "even_w_out": even_w_out[0], "odd_w_in": odd_w_in[0],
                 "odd_w_out": odd_w_out[0], "lru_w_a": lru_w_a[0].reshape(256, 256),
                 "lru_w_x": lru_w_x[0].reshape(256, 256)}
    sharded_local = [meta, lru_conv_w[0], ssd_conv_w[0], odd_norm]
    gathered = allgather_chips([big_local[n].astype(BF16) for n in BIG] + [_pack(sharded_local)], name="gather_weights")
    gb = dict(zip(BIG, gathered[:-1]))
    per_chip = [_unpack(gathered[-1][k], [a.shape for a in sharded_local]) for k in range(4)]
    full_small = [jnp.concatenate([per_chip[k][j] for k in range(4)], axis=-1) for j in range(len(sharded_local))]

    def lru_full(a4):
        return jnp.transpose(a4.reshape(4, LRU_BLOCKS, 64, LRU_BLOCK), (1, 0, 2, 3)).reshape(LRU_BLOCKS, LRU_BLOCK, LRU_BLOCK)

    p = {"meta": full_small[0], "lru_conv_w": full_small[1], "ssd_conv_w": full_small[2], "odd_norm": full_small[3],
         "even_norm": even_norm, "lru_conv_b": lru_conv_b, "lru_b_a": lru_b_a, "lru_b_x": lru_b_x,
         "lru_lambda": lru_lambda, "ssd_conv_b": ssd_conv_b, "ssd_dt_bias": ssd_dt_bias, "ssd_a_log": ssd_a_log,
         "ssd_d": ssd_d, "ssd_norm": ssd_norm, "final_norm": final_norm.reshape(1, D_MODEL)}
    p["win_e"] = jnp.pad(_chip_cols(gb["even_w_in"]), ((0, 0), (0, EVEN_IN_P - EVEN_IN)))
    p["wout_e"] = gb["even_w_out"].reshape(2 * D_MODEL, D_MODEL)
    p["win_o"] = _chip_cols(gb["odd_w_in"])
    p["wout_o"] = gb["odd_w_out"].reshape(D_MODEL, D_MODEL)
    for n in ("win_e", "wout_e", "win_o", "wout_o"):
        p[n + "_t"] = p[n].T
    p["lru_w_a"] = lru_full(gb["lru_w_a"])
    p["lru_w_x"] = lru_full(gb["lru_w_x"])

    loss_part, grad_x, g = _local_step(p, x, loss_target)

    def lru_slabs(a):
        return jnp.transpose(a.reshape(LRU_BLOCKS, 4, 64, LRU_BLOCK), (1, 0, 2, 3)).reshape(4, 256, LRU_BLOCK)

    slabs = [_to_chip_cols(g["even_w_in"], EVEN_IN // 4), g["even_w_out"].reshape(4, 512, D_MODEL),
             _to_chip_cols(g["odd_w_in"], D_MODEL), g["odd_w_out"].reshape(4, 256, D_MODEL),
             lru_slabs(g["lru_w_a"]), lru_slabs(g["lru_w_x"])]
    small_names = list(SMALL_SHAPES)
    small_part = _pack([loss_part[0:1, 0:1]] + [g[n] for n in small_names])
    *recv, small_all = exchange_chips(slabs, small_part, name="exchange_grads")
    core_sums = [sum_lead(r, name=f"sum_chips_{n}") for n, r in zip(BIG, recv)]
    other_sums = swap_cores(core_sums, name="swap_cores")
    small_sum = sum_lead(small_all, name="sum_small")
    small_g = dict(zip(["loss"] + small_names, _unpack(small_sum, [(1, 1)] + [SMALL_SHAPES[n] for n in small_names])))
    loss = small_g["loss"].reshape(())

    grads, delta, new_m, new_v = {}, {}, {}, {}
    for n, mine, other in zip(BIG, core_sums, other_sums):
        shp = w[n].shape
        two_d = lambda t: t.reshape(mine.shape)
        res = adamw(two_d(w[n]), [mine, other], two_d(mom[n]), two_d(var[n]), name=f"adamw_{n}")
        grads[n], delta[n], new_m[n], new_v[n] = (r.reshape(shp) for r in res)
    local_g = []
    for n in small_names:
        gn = small_g[n]
        if n in SHARDED_SMALL:
            gn = lax.dynamic_slice_in_dim(gn, chip * SHARDED_SMALL[n], SHARDED_SMALL[n], axis=1)
        local_g.append(gn)
    res = adamw(_pack([w[n] for n in small_names]), [_pack(local_g)], _pack([mom[n] for n in small_names]),
                _pack([var[n] for n in small_names]), name="adamw_small")
    shapes = [w[n].shape for n in small_names]
    for out, r in zip((grads, delta, new_m, new_v), res):
        out.update(dict(zip(small_names, _unpack(r, shapes))))
    return (loss, grad_x, *[grads[n] for n in WEIGHTS], *[delta[n] for n in WEIGHTS], *[new_m[n] for n in WEIGHTS],
            *[new_v[n] for n in WEIGHTS])
```

```python
import functools

import jax
import jax.numpy as jnp
from jax import lax
from jax.experimental import pallas as pl
from jax.experimental.pallas import tpu as pltpu

F32 = jnp.float32
BF16 = jnp.bfloat16

D_MODEL = 1024
N_META = 16
LEAD = 128
PAD = LEAD - N_META
EPS = 1e-6
CONV_W = 4
LRU_BLOCKS = 4
LRU_BLOCK = 256
RG_LRU_C = 8.0
SSD_HEADS = 16
SSD_HEAD_DIM = 64
SSD_GROUPS = 2
SSD_HPG = 8
SSD_STATE = 128
CHUNK = 64
SSD_CONV_DIM = 1536
EVEN_IN = 4624
EVEN_IN_P = 4736
COL_LRU_X, COL_LRU_G, COL_Z, COL_XBC, COL_DT = 0, 1024, 2048, 3072, 4608
SB_HEADS = 16
SB_HEAD_DIM = 64
SB_BLOCK = 128
ADAM_LR, ADAM_B1, ADAM_B2, ADAM_EPS, ADAM_WD, ADAM_STEP = 0.001, 0.9, 0.999, 1e-08, 0.01, 10
VMEM_LIMIT_V7X = 56 * 1024 * 1024
MESH = pl.DeviceIdType.MESH
S = jax.ShapeDtypeStruct


def _tile(n, prefs):
    for p in prefs:
        if n % p == 0:
            return p
    raise ValueError(f"no tile of {prefs} divides {n}")


def _call(body, *, name, out_shape, grid=(), in_specs=None, out_specs=None, scratch=(), sem=None):
    kw = {}
    if in_specs is not None:
        kw["in_specs"] = in_specs
    if out_specs is not None:
        kw["out_specs"] = out_specs
    return pl.pallas_call(
        body, out_shape=out_shape, grid=grid, scratch_shapes=tuple(scratch), name=name,
        compiler_params=pltpu.CompilerParams(dimension_semantics=sem, vmem_limit_bytes=VMEM_LIMIT_V7X), **kw)


def _sigmoid(x):
    return 0.5 * (jnp.tanh(0.5 * x) + 1.0)


def _silu(x):
    return x * _sigmoid(x)


def _softplus(x):
    return jnp.maximum(x, 0.0) + jnp.log(1.0 + jnp.exp(-jnp.abs(x)))


def _dot(a, b):
    return jnp.dot(a, b, preferred_element_type=F32)


def _dot_nt(a, b):
    return lax.dot_general(a, b, (((1,), (1,)), ((), ())), preferred_element_type=F32)


def _dot_tn(a, b):
    return lax.dot_general(a, b, (((0,), (0,)), ((), ())), preferred_element_type=F32)


def mm_nn(a_list, w, *, name, resid=None):
    m = a_list[0].shape[0]
    k_tot, n = w.shape
    ks = [a.shape[1] for a in a_list]
    assert sum(ks) == k_tot
    tm = _tile(m, (256, 128))
    n_a = len(a_list)
    offs = [sum(ks[:i]) for i in range(n_a)]
    n_chunks = [(c0, min(512, n - c0)) for c0 in range(0, n, 512)]

    def body(*refs):
        a_refs, w_ref = refs[:n_a], refs[n_a]
        r_ref = refs[n_a + 1] if resid is not None else None
        o_ref = refs[-1]
        for c0, cw in n_chunks:
            acc = None
            for a_ref, k0, k in zip(a_refs, offs, ks):
                p = _dot(a_ref[...], w_ref[k0:k0 + k, c0:c0 + cw])
                acc = p if acc is None else acc + p
            if r_ref is not None:
                acc = acc + r_ref[:, c0:c0 + cw]
            o_ref[:, c0:c0 + cw] = acc

    in_specs = [pl.BlockSpec((tm, k), lambda i: (i, 0)) for k in ks]
    in_specs.append(pl.BlockSpec((k_tot, n), lambda i: (0, 0)))
    args = list(a_list) + [w]
    if resid is not None:
        in_specs.append(pl.BlockSpec((tm, n), lambda i: (i, 0)))
        args.append(resid)
    return _call(body, name=name, out_shape=S((m, n), F32), grid=(m // tm,), in_specs=in_specs,
                 out_specs=pl.BlockSpec((tm, n), lambda i: (i, 0)), sem=("parallel",))(*args)


def mm_tn(a, g, *, name):
    t, m = a.shape
    n = g.shape[1]
    tk = _tile(t, (512, 256, 128))
    tn = _tile(n, (512, 256, 128))

    def body(a_ref, g_ref, o_ref):
        @pl.when(pl.program_id(1) == 0)
        def _():
            o_ref[...] = jnp.zeros_like(o_ref)
        o_ref[...] += _dot_tn(a_ref[...], g_ref[...])

    return _call(body, name=name, out_shape=S((m, n), F32), grid=(n // tn, t // tk),
                 in_specs=[pl.BlockSpec((tk, m), lambda j, k: (k, 0)), pl.BlockSpec((tk, tn), lambda j, k: (k, j))],
                 out_specs=pl.BlockSpec((m, tn), lambda j, k: (0, j)), sem=("parallel", "arbitrary"))(a, g)


def rmsnorm_fwd(h, w, *, name):
    m, d = h.shape
    tm = _tile(m, (512, 256, 128))

    def body(h_ref, w_ref, o_ref):
        x = h_ref[...]
        r = lax.rsqrt(jnp.mean(x * x, axis=-1, keepdims=True) + EPS)
        o_ref[...] = (x * r * w_ref[...]).astype(BF16)

    return _call(body, name=name, out_shape=S((m, d), BF16), grid=(m // tm,),
                 in_specs=[pl.BlockSpec((tm, d), lambda i: (i, 0)), pl.BlockSpec((1, d), lambda i: (0, 0))],
                 out_specs=pl.BlockSpec((tm, d), lambda i: (i, 0)), sem=("parallel",))(h, w)


def rmsnorm_bwd(du, h, w, dres, *, name):
    m, d = h.shape
    tm = _tile(m, (256, 128))

    def body(du_ref, h_ref, w_ref, dr_ref, dh_ref, dhb_ref, dw_ref):
        @pl.when(pl.program_id(0) == 0)
        def _():
            dw_ref[...] = jnp.zeros_like(dw_ref)
        x = h_ref[...]
        r = lax.rsqrt(jnp.mean(x * x, axis=-1, keepdims=True) + EPS)
        du_ = du_ref[...]
        g = du_ * w_ref[...]
        c = jnp.mean(g * x, axis=-1, keepdims=True)
        dh = dr_ref[...] + r * g - x * (r * r * r) * c
        dh_ref[...] = dh
        dhb_ref[...] = dh.astype(BF16)
        dw_ref[...] += jnp.sum(du_ * x * r, axis=0, keepdims=True)

    row = pl.BlockSpec((tm, d), lambda i: (i, 0))
    vec = pl.BlockSpec((1, d), lambda i: (0, 0))
    return _call(body, name=name, out_shape=(S((m, d), F32), S((m, d), BF16), S((1, d), F32)), grid=(m // tm,),
                 in_specs=[row, row, vec, row], out_specs=(row, row, vec), sem=("arbitrary",))(du, h, w, dres)


def gate_fwd(o, o_cb, g, g_cb, *, name):
    m = o.shape[0]
    d = D_MODEL
    tm = _tile(m, (512, 256, 128))

    def body(o_ref, g_ref, y_ref):
        y_ref[...] = (o_ref[...] * _silu(g_ref[...])).astype(BF16)

    return _call(body, name=name, out_shape=S((m, d), BF16), grid=(m // tm,),
                 in_specs=[pl.BlockSpec((tm, d), lambda i: (i, o_cb)), pl.BlockSpec((tm, d), lambda i: (i, g_cb))],
                 out_specs=pl.BlockSpec((tm, d), lambda i: (i, 0)), sem=("parallel",))(o, g)


def gate_bwd(dy, dy_cb, o, o_cb, g, g_cb, *, name):
    m = o.shape[0]
    d = D_MODEL
    tm = _tile(m, (512, 256, 128))

    def body(dy_ref, o_ref, g_ref, do_ref, dg_ref):
        gv = g_ref[...]
        s = _sigmoid(gv)
        dyv = dy_ref[...]
        do_ref[...] = dyv * gv * s
        dg_ref[...] = (dyv * o_ref[...] * (s + gv * s * (1.0 - s))).astype(BF16)

    return _call(body, name=name, out_shape=(S((m, d), F32), S((m, d), BF16)), grid=(m // tm,),
                 in_specs=[pl.BlockSpec((tm, d), lambda i: (i, dy_cb)), pl.BlockSpec((tm, d), lambda i: (i, o_cb)),
                           pl.BlockSpec((tm, d), lambda i: (i, g_cb))],
                 out_specs=(pl.BlockSpec((tm, d), lambda i: (i, 0)),) * 2, sem=("parallel",))(dy, o, g)


def _group_mean(x):
    half = x.shape[1] // SSD_GROUPS
    parts = [jnp.broadcast_to(jnp.mean(x[:, k * half:(k + 1) * half], axis=-1, keepdims=True), (x.shape[0], half))
             for k in range(SSD_GROUPS)]
    return jnp.concatenate(parts, axis=1)


def gnorm_fwd(y, z, z_cb, w, *, name):
    m = y.shape[0]
    d = D_MODEL
    tm = _tile(m, (256, 128))

    def body(y_ref, z_ref, w_ref, o_ref):
        g = y_ref[...] * _silu(z_ref[...])
        r = lax.rsqrt(_group_mean(g * g) + EPS)
        o_ref[...] = (g * r * w_ref[...]).astype(BF16)

    return _call(body, name=name, out_shape=S((m, d), BF16), grid=(m // tm,),
                 in_specs=[pl.BlockSpec((tm, d), lambda i: (i, 0)), pl.BlockSpec((tm, d), lambda i: (i, z_cb)),
                           pl.BlockSpec((1, d), lambda i: (0, 0))],
                 out_specs=pl.BlockSpec((tm, d), lambda i: (i, 0)), sem=("parallel",))(y, z, w)


def gnorm_bwd(do, do_cb, y, z, z_cb, w, *, name):
    m = y.shape[0]
    d = D_MODEL
    tm = _tile(m, (256, 128))

    def body(do_ref, y_ref, z_ref, w_ref, dy_ref, dz_ref, dw_ref):
        @pl.when(pl.program_id(0) == 0)
        def _():
            dw_ref[...] = jnp.zeros_like(dw_ref)
        yv, zv, dov = y_ref[...], z_ref[...], do_ref[...]
        s = _sigmoid(zv)
        sz = zv * s
        g = yv * sz
        r = lax.rsqrt(_group_mean(g * g) + EPS)
        dw_ref[...] += jnp.sum(dov * g * r, axis=0, keepdims=True)
        dn = dov * w_ref[...]
        dg = r * dn - g * (r * r * r) * _group_mean(dn * g)
        dy_ref[...] = dg * sz
        dz_ref[...] = (dg * yv * (s + zv * s * (1.0 - s))).astype(BF16)

    row = pl.BlockSpec((tm, d), lambda i: (i, 0))
    vec = pl.BlockSpec((1, d), lambda i: (0, 0))
    return _call(body, name=name, out_shape=(S((m, d), F32), S((m, d), BF16), S((1, d), F32)), grid=(m // tm,),
                 in_specs=[pl.BlockSpec((tm, d), lambda i: (i, do_cb)), row, pl.BlockSpec((tm, d), lambda i: (i, z_cb)), vec],
                 out_specs=(row, row, vec), sem=("arbitrary",))(do, y, z, w)


def loss_head(h, target, w, lp, *, name):
    m, d = h.shape
    bsz = m // lp
    tm = SB_BLOCK
    nblk = lp // tm
    lead_blk = LEAD // tm

    def body(h_ref, t_ref, w_ref, dh_ref, dhb_ref, l_ref, dw_ref):
        i = pl.program_id(1)

        @pl.when(jnp.logical_and(pl.program_id(0) == 0, i == 0))
        def _():
            l_ref[...] = jnp.zeros_like(l_ref)
            dw_ref[...] = jnp.zeros_like(dw_ref)

        @pl.when(i < lead_blk)
        def _():
            dh_ref[...] = jnp.zeros_like(dh_ref)
            dhb_ref[...] = jnp.zeros_like(dhb_ref)

        @pl.when(i >= lead_blk)
        def _():
            x = h_ref[...]
            r = lax.rsqrt(jnp.mean(x * x, axis=-1, keepdims=True) + EPS)
            wv = w_ref[...]
            e = x * r * wv - t_ref[0]
            l_ref[...] += 0.5 * jnp.sum(jnp.mean(e * e, axis=-1, keepdims=True))
            dy = e * (1.0 / d)
            g = dy * wv
            c = jnp.mean(g * x, axis=-1, keepdims=True)
            dh = r * g - x * (r * r * r) * c
            dh_ref[...] = dh
            dhb_ref[...] = dh.astype(BF16)
            dw_ref[...] += jnp.sum(dy * x * r, axis=0, keepdims=True)

    row = pl.BlockSpec((tm, d), lambda b, i: (b * nblk + i, 0))
    return _call(body, name=name, out_shape=(S((m, d), F32), S((m, d), BF16), S((8, 128), F32), S((1, d), F32)),
                 grid=(bsz, nblk),
                 in_specs=[row, pl.BlockSpec((1, tm, d), lambda b, i: (b, jnp.maximum(i - lead_blk, 0), 0)),
                           pl.BlockSpec((1, d), lambda b, i: (0, 0))],
                 out_specs=(row, row, pl.BlockSpec((8, 128), lambda b, i: (0, 0)),
                            pl.BlockSpec((1, d), lambda b, i: (0, 0))),
                 sem=("arbitrary", "arbitrary"))(h, target, w)


def _conv_tiles(m, c):
    return _tile(m, (256, 128)), _tile(c, (512, 256, 128))


def conv_fwd(x, col0, c, w, b, *, name):
    m = x.shape[0]
    tm, tc = _conv_tiles(m, c)
    assert col0 % tc == 0
    cb0 = col0 // tc
    hb = tm // 8

    def body(x_ref, halo_ref, w_ref, b_ref, o_ref, ext):
        ext[0:8, :] = halo_ref[...]
        ext[8:8 + tm, :] = x_ref[...]
        acc = b_ref[...] + w_ref[3:4, :] * ext[8:8 + tm, :]
        for k in range(CONV_W - 1):
            acc = acc + w_ref[k:k + 1, :] * ext[pl.ds(5 + k, tm), :]
        o_ref[...] = acc

    return _call(body, name=name, out_shape=S((m, c), F32), grid=(m // tm, c // tc),
                 in_specs=[pl.BlockSpec((tm, tc), lambda i, j: (i, cb0 + j)),
                           pl.BlockSpec((8, tc), lambda i, j: (jnp.maximum(i * hb - 1, 0), cb0 + j)),
                           pl.BlockSpec((CONV_W, tc), lambda i, j: (0, j)), pl.BlockSpec((1, tc), lambda i, j: (0, j))],
                 out_specs=pl.BlockSpec((tm, tc), lambda i, j: (i, j)), scratch=[pltpu.VMEM((tm + 8, tc), F32)],
                 sem=("parallel", "parallel"))(x, x, w, b)


def conv_bwd(x, col0, c, dy, w, *, name):
    m = x.shape[0]
    tm, tc = _conv_tiles(m, c)
    cb0 = col0 // tc
    n_i = m // tm
    hb = tm // 8

    def body(x_ref, xh_ref, dy_ref, dyn_ref, w_ref, dx_ref, dw_ref, db_ref, ext, dext):
        i = pl.program_id(1)

        @pl.when(i == 0)
        def _():
            dw_ref[...] = jnp.zeros_like(dw_ref)
            db_ref[...] = jnp.zeros_like(db_ref)

        ext[0:8, :] = xh_ref[...]
        ext[8:8 + tm, :] = x_ref[...]
        d_cur = dy_ref[...]
        dext[0:tm, :] = d_cur
        dext[tm:tm + 8, :] = jnp.where(i == n_i - 1, 0.0, dyn_ref[...])
        dx = w_ref[3:4, :] * d_cur
        for k in range(CONV_W - 1):
            dx = dx + w_ref[k:k + 1, :] * dext[pl.ds(3 - k, tm), :]
        dx_ref[...] = dx.astype(BF16)
        for k in range(CONV_W - 1):
            dw_ref[k:k + 1, :] += jnp.sum(ext[pl.ds(5 + k, tm), :] * d_cur, axis=0, keepdims=True)
        dw_ref[3:4, :] += jnp.sum(ext[8:8 + tm, :] * d_cur, axis=0, keepdims=True)
        db_ref[...] += jnp.sum(d_cur, axis=0, keepdims=True)

    return _call(body, name=name, out_shape=(S((m, c), BF16), S((CONV_W, c), F32), S((1, c), F32)),
                 grid=(c // tc, n_i),
                 in_specs=[pl.BlockSpec((tm, tc), lambda j, i: (i, cb0 + j)),
                           pl.BlockSpec((8, tc), lambda j, i: (jnp.maximum(i * hb - 1, 0), cb0 + j)),
                           pl.BlockSpec((tm, tc), lambda j, i: (i, j)),
                           pl.BlockSpec((8, tc), lambda j, i: (jnp.minimum((i + 1) * hb, n_i * hb - 1), j)),
                           pl.BlockSpec((CONV_W, tc), lambda j, i: (0, j))],
                 out_specs=(pl.BlockSpec((tm, tc), lambda j, i: (i, j)), pl.BlockSpec((CONV_W, tc), lambda j, i: (0, j)),
                            pl.BlockSpec((1, tc), lambda j, i: (0, j))),
                 scratch=[pltpu.VMEM((tm + 8, tc), F32), pltpu.VMEM((tm + 8, tc), F32)],
                 sem=("parallel", "arbitrary"))(x, x, dy, dy, w)


def _row_valid(tile_idx, tiles_per_seq, tm):
    pos = lax.rem(tile_idx, tiles_per_seq) * tm + lax.broadcasted_iota(jnp.int32, (tm, 1), 0)
    return (pos >= PAD).astype(F32)


def _neg_expm1(x):
    small = -(x * (1.0 + x * (0.5 + x * (1.0 / 6.0))))
    return jnp.where(x > -0.01, small, 1.0 - jnp.exp(x))


def _gates_core(lx, wa, ba, wx, bx, lam, valid):
    lxb = lx.astype(BF16)
    r = _sigmoid(_dot(lxb, wa.astype(BF16)) + ba)
    i = _sigmoid(_dot(lxb, wx.astype(BF16)) + bx)
    log_a = (-RG_LRU_C) * r * _softplus(-lam)
    a = jnp.exp(log_a)
    mult = jnp.sqrt(_neg_expm1(2.0 * log_a))
    return a, valid * (mult * i * lx)


def gates_fwd(lx, wa, ba, wx, bx, lam, lp, *, name):
    m = lx.shape[0]
    tm = SB_BLOCK
    tps = lp // tm
    cb = LRU_BLOCK

    def body(lx_ref, wa_ref, ba_ref, wx_ref, bx_ref, lam_ref, a_ref, b_ref):
        valid = _row_valid(pl.program_id(1), tps, tm)
        a, b = _gates_core(lx_ref[...], wa_ref[0], ba_ref[...], wx_ref[0], bx_ref[...], lam_ref[...], valid)
        a_ref[...] = a
        b_ref[...] = b

    tok = pl.BlockSpec((tm, cb), lambda g, i: (i, g))
    wsp = pl.BlockSpec((1, cb, cb), lambda g, i: (g, 0, 0))
    vec = pl.BlockSpec((1, cb), lambda g, i: (0, g))
    return _call(body, name=name, out_shape=(S((m, D_MODEL), F32),) * 2, grid=(LRU_BLOCKS, m // tm),
                 in_specs=[tok, wsp, vec, wsp, vec, vec], out_specs=(tok, tok),
                 sem=("parallel", "parallel"))(lx, wa, ba, wx, bx, lam)


def gates_bwd(lx, da, db, wa, ba, wx, bx, lam, lp, *, name):
    m = lx.shape[0]
    tm = SB_BLOCK
    tps = lp // tm
    cb = LRU_BLOCK

    def body(lx_ref, da_ref, db_ref, wa_ref, ba_ref, wx_ref, bx_ref, lam_ref,
             dlx_ref, dwa_ref, dwx_ref, dba_ref, dbx_ref, dlam_ref):
        @pl.when(pl.program_id(1) == 0)
        def _():
            for r in (dwa_ref, dwx_ref, dba_ref, dbx_ref, dlam_ref):
                r[...] = jnp.zeros_like(r)
        valid = _row_valid(pl.program_id(1), tps, tm)
        core = functools.partial(_gates_core, valid=valid)
        _, vjp = jax.vjp(core, lx_ref[...], wa_ref[0].astype(F32), ba_ref[...], wx_ref[0].astype(F32), bx_ref[...],
                         lam_ref[...])
        dlx, dwa, dba, dwx, dbx, dlam = vjp((da_ref[...], db_ref[...]))
        dlx_ref[...] = dlx
        dwa_ref[0] += dwa
        dwx_ref[0] += dwx
        dba_ref[...] += dba
        dbx_ref[...] += dbx
        dlam_ref[...] += dlam

    tok = pl.BlockSpec((tm, cb), lambda g, i: (i, g))
    wsp = pl.BlockSpec((1, cb, cb), lambda g, i: (g, 0, 0))
    vec = pl.BlockSpec((1, cb), lambda g, i: (0, g))
    wshape = S((LRU_BLOCKS, cb, cb), F32)
    vshape = S((1, D_MODEL), F32)
    return _call(body, name=name, out_shape=(S((m, D_MODEL), F32), wshape, wshape, vshape, vshape, vshape),
                 grid=(LRU_BLOCKS, m // tm), in_specs=[tok, tok, tok, wsp, vec, wsp, vec, vec],
                 out_specs=(tok, wsp, wsp, vec, vec, vec),
                 sem=("parallel", "arbitrary"))(lx, da, db, wa, ba, wx, bx, lam)


SCAN_TOK = 128


def scan_fwd(a, b, lp, *, name):
    m = a.shape[0]
    bsz = m // lp
    nch = lp // SCAN_TOK
    rows = SCAN_TOK * 8

    def body(a_ref, b_ref, h_ref, carry):
        @pl.when(pl.program_id(1) == 0)
        def _():
            carry[...] = jnp.zeros_like(carry)

        def step(t, h):
            r = pl.ds(pl.multiple_of(t * 8, 8), 8)
            h = a_ref[r, :] * h + b_ref[r, :]
            h_ref[r, :] = h
            return h

        carry[...] = lax.fori_loop(0, SCAN_TOK, step, carry[...], unroll=8)

    blk = pl.BlockSpec((rows, 128), lambda s, c: (s * nch + c, 0))
    out = _call(body, name=name, out_shape=S((m * 8, 128), F32), grid=(bsz, nch), in_specs=[blk, blk], out_specs=blk,
                scratch=[pltpu.VMEM((8, 128), F32)], sem=("parallel", "arbitrary"))(
                    a.reshape(m * 8, 128), b.reshape(m * 8, 128))
    return out.reshape(m, D_MODEL)


def scan_bwd(a, h, dh, lp, *, name):
    m = a.shape[0]
    bsz = m // lp
    nch = lp // SCAN_TOK
    rows = SCAN_TOK * 8

    def body(a_ref, h_ref, hprev_ref, dh_ref, da_ref, db_ref, carry):
        c = pl.program_id(1)

        @pl.when(c == 0)
        def _():
            carry[...] = jnp.zeros_like(carry)

        h_before = jnp.where(c == nch - 1, 0.0, hprev_ref[...])

        def step(k, ag):
            t = SCAN_TOK - 1 - k
            r = pl.ds(pl.multiple_of(t * 8, 8), 8)
            g = dh_ref[r, :] + ag
            db_ref[r, :] = g
            tp = jnp.maximum(t - 1, 0)
            hp = jnp.where(t == 0, h_before, h_ref[pl.ds(pl.multiple_of(tp * 8, 8), 8), :])
            da_ref[r, :] = g * hp
            return a_ref[r, :] * g

        carry[...] = lax.fori_loop(0, SCAN_TOK, step, carry[...], unroll=8)

    blk = pl.BlockSpec((rows, 128), lambda s, c: (s * nch + (nch - 1 - c), 0))
    prev = pl.BlockSpec((8, 128), lambda s, c: (jnp.maximum((s * nch + (nch - 1 - c)) * SCAN_TOK - 1, 0), 0))
    a2, h2, dh2 = (v.reshape(m * 8, 128) for v in (a, h, dh))
    da, db = _call(body, name=name, out_shape=(S((m * 8, 128), F32),) * 2, grid=(bsz, nch),
                   in_specs=[blk, blk, prev, blk], out_specs=(blk, blk), scratch=[pltpu.VMEM((8, 128), F32)],
                   sem=("parallel", "arbitrary"))(a2, h2, h2, dh2)
    return da.reshape(m, D_MODEL), db.reshape(m, D_MODEL)


def _ssd_chunk(xs, bs, cs, dtr, ss, dtb, alog, dpar, valid):
    row = lax.broadcasted_iota(jnp.int32, (CHUNK, CHUNK), 0)
    col = lax.broadcasted_iota(jnp.int32, (CHUNK, CHUNK), 1)
    tri = row >= col
    dt = _softplus(dtr + dtb) * valid
    adt = dt * (-jnp.exp(alog))
    acum = jnp.dot(tri.astype(F32), adt, precision=lax.Precision.HIGHEST, preferred_element_type=F32)
    acum_t = acum.T
    b16 = [(_silu(b) * valid).astype(BF16) for b in bs]
    c16 = [(_silu(c) * valid).astype(BF16) for c in cs]
    cbs = [_dot_nt(c, b) for c, b in zip(c16, b16)]
    ys, s_new = [], []
    for h in range(SSD_HEADS):
        g = h // SSD_HPG
        x = _silu(xs[h])
        ac = acum[:, h:h + 1]
        lm = jnp.exp(jnp.where(tri, ac - acum_t[h:h + 1, :], -1e30))
        xd = x * dt[:, h:h + 1]
        alast = acum[CHUNK - 1:CHUNK, h:h + 1]
        y = _dot((cbs[g] * lm).astype(BF16), xd.astype(BF16))
        y = y + _dot_nt(c16[g], ss[h].astype(BF16)) * jnp.exp(ac)
        ys.append(y + x * dpar[:, h:h + 1])
        st = _dot_tn((xd * jnp.exp(alast - ac)).astype(BF16), b16[g])
        s_new.append(jnp.exp(alast) * ss[h] + st)
    return ys, s_new


def _ssd_load(pre_ref, s_ref):
    xs = [pre_ref[:, h * SSD_HEAD_DIM:(h + 1) * SSD_HEAD_DIM] for h in range(SSD_HEADS)]
    b0 = SSD_HEADS * SSD_HEAD_DIM
    bs = [pre_ref[:, b0 + g * SSD_STATE:b0 + (g + 1) * SSD_STATE] for g in range(SSD_GROUPS)]
    c0 = b0 + SSD_GROUPS * SSD_STATE
    cs = [pre_ref[:, c0 + g * SSD_STATE:c0 + (g + 1) * SSD_STATE] for g in range(SSD_GROUPS)]
    ss = [s_ref[h * SSD_HEAD_DIM:(h + 1) * SSD_HEAD_DIM, :] for h in range(SSD_HEADS)]
    return xs, bs, cs, ss


def ssd_fwd(pre, proj, dtb, alog, dpar, lp, *, name):
    m = pre.shape[0]
    bsz = m // lp
    nc = lp // CHUNK
    srows = SSD_HEADS * SSD_HEAD_DIM

    def body(pre_ref, dt_ref, dtb_ref, alog_ref, d_ref, y_ref, sin_ref, state):
        c = pl.program_id(1)

        @pl.when(c == 0)
        def _():
            state[...] = jnp.zeros_like(state)

        sin_ref[...] = state[...]
        xs, bs, cs, ss = _ssd_load(pre_ref, state)
        valid = _row_valid(c, nc, CHUNK)
        ys, s_new = _ssd_chunk(xs, bs, cs, dt_ref[:, 0:SSD_HEADS], ss, dtb_ref[...], alog_ref[...], d_ref[...], valid)
        for h in range(SSD_HEADS):
            y_ref[:, h * SSD_HEAD_DIM:(h + 1) * SSD_HEAD_DIM] = ys[h]
            state[h * SSD_HEAD_DIM:(h + 1) * SSD_HEAD_DIM, :] = s_new[h]

    par = pl.BlockSpec((1, SSD_HEADS), lambda s, c: (0, 0))
    return _call(body, name=name, out_shape=(S((m, D_MODEL), F32), S((m // CHUNK * srows, SSD_STATE), F32)),
                 grid=(bsz, nc),
                 in_specs=[pl.BlockSpec((CHUNK, SSD_CONV_DIM), lambda s, c: (s * nc + c, 0)),
                           pl.BlockSpec((CHUNK, 128), lambda s, c: (s * nc + c, COL_DT // 128)), par, par, par],
                 out_specs=(pl.BlockSpec((CHUNK, D_MODEL), lambda s, c: (s * nc + c, 0)),
                            pl.BlockSpec((srows, SSD_STATE), lambda s, c: (s * nc + c, 0))),
                 scratch=[pltpu.VMEM((srows, SSD_STATE), F32)],
                 sem=("parallel", "arbitrary"))(pre, proj, dtb, alog, dpar)


def ssd_bwd(pre, proj, s_in, dy, dtb, alog, dpar, lp, *, name):
    m = pre.shape[0]
    bsz = m // lp
    nc = lp // CHUNK
    srows = SSD_HEADS * SSD_HEAD_DIM

    def body(pre_ref, dt_ref, sin_ref, dy_ref, dtb_ref, alog_ref, d_ref,
             dpre_ref, ddt_ref, ddtb_ref, dalog_ref, dd_ref, dstate):
        c = pl.program_id(1)

        @pl.when(jnp.logical_and(pl.program_id(0) == 0, c == 0))
        def _():
            for r in (ddtb_ref, dalog_ref, dd_ref):
                r[...] = jnp.zeros_like(r)

        @pl.when(c == 0)
        def _():
            dstate[...] = jnp.zeros_like(dstate)

        xs, bs, cs, ss = _ssd_load(pre_ref, sin_ref)
        valid = _row_valid(nc - 1 - c, nc, CHUNK)
        core = functools.partial(_ssd_chunk, valid=valid)
        _, vjp = jax.vjp(core, xs, bs, cs, dt_ref[:, 0:SSD_HEADS], ss, dtb_ref[...], alog_ref[...], d_ref[...])
        dys = [dy_ref[:, h * SSD_HEAD_DIM:(h + 1) * SSD_HEAD_DIM] for h in range(SSD_HEADS)]
        dsn = [dstate[h * SSD_HEAD_DIM:(h + 1) * SSD_HEAD_DIM, :] for h in range(SSD_HEADS)]
        dxs, dbs, dcs, ddtr, dss, ddtb, dalog, dd = vjp((dys, dsn))
        b0 = SSD_HEADS * SSD_HEAD_DIM
        c0 = b0 + SSD_GROUPS * SSD_STATE
        for h in range(SSD_HEADS):
            dpre_ref[:, h * SSD_HEAD_DIM:(h + 1) * SSD_HEAD_DIM] = dxs[h]
            dstate[h * SSD_HEAD_DIM:(h + 1) * SSD_HEAD_DIM, :] = dss[h]
        for g in range(SSD_GROUPS):
            dpre_ref[:, b0 + g * SSD_STATE:b0 + (g + 1) * SSD_STATE] = dbs[g]
            dpre_ref[:, c0 + g * SSD_STATE:c0 + (g + 1) * SSD_STATE] = dcs[g]
        ddt_ref[...] = jnp.zeros_like(ddt_ref)
        ddt_ref[:, 0:SSD_HEADS] = ddtr.astype(BF16)
        ddtb_ref[...] += ddtb
        dalog_ref[...] += dalog
        dd_ref[...] += dd

    par = pl.BlockSpec((1, SSD_HEADS), lambda s, c: (0, 0))
    rev = lambda s, c: (s * nc + (nc - 1 - c), 0)
    pshape = S((1, SSD_HEADS), F32)
    return _call(body, name=name,
                 out_shape=(S((m, SSD_CONV_DIM), F32), S((m, 128), BF16), pshape, pshape, pshape), grid=(bsz, nc),
                 in_specs=[pl.BlockSpec((CHUNK, SSD_CONV_DIM), rev),
                           pl.BlockSpec((CHUNK, 128), lambda s, c: (s * nc + (nc - 1 - c), COL_DT // 128)),
                           pl.BlockSpec((srows, SSD_STATE), rev), pl.BlockSpec((CHUNK, D_MODEL), rev), par, par, par],
                 out_specs=(pl.BlockSpec((CHUNK, SSD_CONV_DIM), rev), pl.BlockSpec((CHUNK, 128), rev), par, par, par),
                 scratch=[pltpu.VMEM((srows, SSD_STATE), F32)],
                 sem=("arbitrary", "arbitrary"))(pre, proj, s_in, dy, dtb, alog, dpar)


def _suffix_mats():
    r = lax.broadcasted_iota(jnp.int32, (SB_BLOCK, 2 * SB_BLOCK), 0)
    c = lax.broadcasted_iota(jnp.int32, (SB_BLOCK, 2 * SB_BLOCK), 1)
    later = jnp.logical_or(c >= SB_BLOCK, r > c).astype(BF16)
    earlier = jnp.logical_or(c >= SB_BLOCK, r < c).astype(BF16)
    return later, earlier


def _split_dot(x, mat):
    hi = x.astype(BF16)
    lo = (x - hi.astype(F32)).astype(BF16)
    return _dot(hi, mat) + _dot(lo, mat)


SB_STRIP = 2


def _sb_tile(q_i, k_j, i, ju, diff, col, m_later):
    z = _dot_nt(q_i, k_j)
    off = (ju - i) * SB_BLOCK
    valid = jnp.logical_and(diff > off, col >= PAD - ju * SB_BLOCK)
    valid = jnp.logical_and(valid, ju >= 0)
    lk = jnp.where(valid, -_softplus(z), 0.0)
    tsum = _split_dot(lk, m_later)
    return valid, z, lk, tsum


def _sb_iotas():
    row = lax.broadcasted_iota(jnp.int32, (SB_BLOCK, SB_BLOCK), 0)
    col = lax.broadcasted_iota(jnp.int32, (SB_BLOCK, SB_BLOCK), 1)
    return row - col, col


def _strips(i):
    return lax.div(i + SB_STRIP, SB_STRIP)


def _key_block(i, jj, k):
    ju = i - jj * SB_STRIP - k
    return ju, pl.ds(pl.multiple_of(jnp.maximum(ju, 0) * SB_BLOCK, SB_BLOCK), SB_BLOCK)


def attn_fwd(qkvg, lp, *, name):
    m = qkvg.shape[0]
    bsz = m // lp
    nb = lp // SB_BLOCK
    hd = SB_HEAD_DIM

    def body(q_ref, k_ref, v_ref, o_ref, qs, ks, vs):
        m_later, _ = _suffix_mats()
        diff, col = _sb_iotas()
        for hh in range(2):
            ln = slice(hh * hd, (hh + 1) * hd)
            qs[hh] = (q_ref[:, ln] * (hd ** -0.5)).astype(BF16)
            ks[hh] = k_ref[:, ln].astype(BF16)
            vs[hh] = v_ref[:, ln].astype(BF16)

        def q_block(i, _):
            rows = pl.ds(pl.multiple_of(i * SB_BLOCK, SB_BLOCK), SB_BLOCK)
            q_i = [qs[hh, rows, :] for hh in range(2)]

            def strip(jj, carry):
                out = []
                for hh in range(2):
                    acc, run = carry[hh]
                    tiles = []
                    for k in range(SB_STRIP):
                        ju, cols = _key_block(i, jj, k)
                        tiles.append((cols,) + _sb_tile(q_i[hh], ks[hh, cols, :], i, ju, diff, col, m_later))
                    for cols, valid, z, lk, tsum in tiles:
                        w = jnp.where(valid, jnp.exp(z + lk + tsum[:, :SB_BLOCK] + run), 0.0)
                        acc = acc + _dot(w.astype(BF16), vs[hh, cols, :])
                        run = run + tsum[:, SB_BLOCK:]
                    out.append((acc, run))
                return tuple(out)

            zero = (jnp.zeros((SB_BLOCK, hd), F32), jnp.zeros((SB_BLOCK, SB_BLOCK), F32))
            res = lax.fori_loop(0, _strips(i), strip, (zero, zero))
            for hh in range(2):
                o_ref[rows, hh * hd:(hh + 1) * hd] = res[hh][0]
            return 0

        lax.fori_loop(0, nb, q_block, 0)

    blk = lambda cb: pl.BlockSpec((lp, 128), lambda s, p: (s, cb * 8 + p))
    return _call(body, name=name, out_shape=S((m, D_MODEL), F32), grid=(bsz, 8),
                 in_specs=[blk(0), blk(1), blk(2)], out_specs=pl.BlockSpec((lp, 128), lambda s, p: (s, p)),
                 scratch=[pltpu.VMEM((2, lp, hd), BF16)] * 3, sem=("parallel", "parallel"))(qkvg, qkvg, qkvg)


def attn_bwd(qkvg, do, lp, *, name):
    m = qkvg.shape[0]
    bsz = m // lp
    nb = lp // SB_BLOCK
    hd = SB_HEAD_DIM
    scale = hd ** -0.5

    def body(q_ref, k_ref, v_ref, do_ref, dq_ref, dk_ref, dv_ref, qs, ks, vs, dos, dka, dva, g_keep, s_keep):
        m_later, m_earlier = _suffix_mats()
        diff, col = _sb_iotas()
        zero_tile = jnp.zeros((SB_BLOCK, SB_BLOCK), F32)
        for hh in range(2):
            ln = slice(hh * hd, (hh + 1) * hd)
            qs[hh] = (q_ref[:, ln] * scale).astype(BF16)
            ks[hh] = k_ref[:, ln].astype(BF16)
            vs[hh] = v_ref[:, ln].astype(BF16)
            dos[hh] = do_ref[:, ln].astype(BF16)
            g_keep[hh, nb] = zero_tile
            s_keep[hh, nb] = zero_tile
        dka[...] = jnp.zeros_like(dka)
        dva[...] = jnp.zeros_like(dva)

        def q_block(i, _):
            rows = pl.ds(pl.multiple_of(i * SB_BLOCK, SB_BLOCK), SB_BLOCK)
            q_i = [qs[hh, rows, :] for hh in range(2)]
            do_i = [dos[hh, rows, :] for hh in range(2)]

            def sweep_left(jj, carry):
                out = []
                for hh in range(2):
                    run = carry[hh]
                    tiles = []
                    for k in range(SB_STRIP):
                        ju, cols = _key_block(i, jj, k)
                        tiles.append((ju, cols) + _sb_tile(q_i[hh], ks[hh, cols, :], i, ju, diff, col, m_later))
                    for ju, cols, valid, z, lk, tsum in tiles:
                        sig = jnp.where(valid, jnp.exp(z + lk), 0.0)
                        w = sig * jnp.exp(tsum[:, :SB_BLOCK] + run)
                        slot = jnp.where(ju >= 0, ju, nb)
                        g_keep[hh, slot] = _dot_nt(do_i[hh], vs[hh, cols, :]) * w
                        s_keep[hh, slot] = sig
                        dva[hh, cols, :] += _dot_tn(w.astype(BF16), do_i[hh])
                        run = run + tsum[:, SB_BLOCK:]
                    out.append(run)
                return tuple(out)

            lax.fori_loop(0, _strips(i), sweep_left, (zero_tile, zero_tile))

            def sweep_right(jj, carry):
                out = []
                for hh in range(2):
                    dq, grun = carry[hh]
                    for k in range(SB_STRIP):
                        j = jj * SB_STRIP + k
                        slot = jnp.where(j <= i, j, nb)
                        cols = pl.ds(pl.multiple_of(jnp.minimum(j, i) * SB_BLOCK, SB_BLOCK), SB_BLOCK)
                        gmat = g_keep[hh, slot]
                        sig = s_keep[hh, slot]
                        gsum = _split_dot(gmat, m_earlier)
                        dz = (gmat - sig * (gmat + gsum[:, :SB_BLOCK] + grun)).astype(BF16)
                        dq = dq + _dot(dz, ks[hh, cols, :])
                        dka[hh, cols, :] += _dot_tn(dz, q_i[hh])
                        grun = grun + gsum[:, SB_BLOCK:]
                    out.append((dq, grun))
                return tuple(out)

            zero = (jnp.zeros((SB_BLOCK, hd), F32), zero_tile)
            res = lax.fori_loop(0, _strips(i), sweep_right, (zero, zero))
            for hh in range(2):
                dq_ref[rows, hh * hd:(hh + 1) * hd] = (res[hh][0] * scale).astype(BF16)
            return 0

        lax.fori_loop(0, nb, q_block, 0)
        for hh in range(2):
            dk_ref[:, hh * hd:(hh + 1) * hd] = dka[hh].astype(BF16)
            dv_ref[:, hh * hd:(hh + 1) * hd] = dva[hh].astype(BF16)

    blk = lambda cb: pl.BlockSpec((lp, 128), lambda s, p: (s, cb * 8 + p))
    one = pl.BlockSpec((lp, 128), lambda s, p: (s, p))
    return _call(body, name=name, out_shape=(S((m, D_MODEL), BF16),) * 3, grid=(bsz, 8),
                 in_specs=[blk(0), blk(1), blk(2), one], out_specs=(one, one, one),
                 scratch=[pltpu.VMEM((2, lp, hd), BF16)] * 4 + [pltpu.VMEM((2, lp, hd), F32)] * 2
                 + [pltpu.VMEM((2, nb + 1, SB_BLOCK, SB_BLOCK), F32)] * 2,
                 sem=("parallel", "parallel"))(qkvg, qkvg, qkvg, do)


def meta_grad(dh, lp, *, name):
    m, d = dh.shape
    bsz = m // lp
    per = lp // N_META

    def body(dh_ref, o_ref):
        @pl.when(pl.program_id(0) == 0)
        def _():
            o_ref[...] = jnp.zeros_like(o_ref)
        o_ref[...] += dh_ref[...]

    return _call(body, name=name, out_shape=S((N_META, d), F32), grid=(bsz,),
                 in_specs=[pl.BlockSpec((N_META, d), lambda b: (b * per + PAD // N_META, 0))],
                 out_specs=pl.BlockSpec((N_META, d), lambda b: (0, 0)), sem=("arbitrary",))(dh)


def sum_lead(arr, *, name):
    n, r, c = arr.shape
    tr = _tile(r, (128, 64, 32, 16, 8))

    def body(a_ref, o_ref):
        acc = a_ref[0]
        for k in range(1, n):
            acc = acc + a_ref[k]
        o_ref[...] = acc

    return _call(body, name=name, out_shape=S((r, c), F32), grid=(r // tr,),
                 in_specs=[pl.BlockSpec((n, tr, c), lambda i: (0, i, 0))],
                 out_specs=pl.BlockSpec((tr, c), lambda i: (i, 0)), sem=("parallel",))(arr)


def adamw(w, g_parts, mom, var, *, name):
    r, c = w.shape
    tr = _tile(r, (128, 64, 32, 16, 8))
    n_g = len(g_parts)
    c1 = 1.0 - ADAM_B1 ** ADAM_STEP
    c2 = 1.0 - ADAM_B2 ** ADAM_STEP

    def body(*refs):
        w_ref, g_refs, m_ref, v_ref = refs[0], refs[1:1 + n_g], refs[1 + n_g], refs[2 + n_g]
        g_out, d_out, m_out, v_out = refs[3 + n_g:]
        g = g_refs[0][...]
        for gr in g_refs[1:]:
            g = g + gr[...]
        mn = ADAM_B1 * m_ref[...] + (1.0 - ADAM_B1) * g
        vn = ADAM_B2 * v_ref[...] + (1.0 - ADAM_B2) * (g * g)
        g_out[...] = g
        m_out[...] = mn
        v_out[...] = vn
        d_out[...] = -ADAM_LR * ((mn / c1) / (jnp.sqrt(vn / c2) + ADAM_EPS) + ADAM_WD * w_ref[...])

    blk = pl.BlockSpec((tr, c), lambda i: (i, 0))
    return _call(body, name=name, out_shape=(S((r, c), F32),) * 4, grid=(r // tr,), in_specs=[blk] * (3 + n_g),
                 out_specs=(blk,) * 4, sem=("parallel",))(w, *g_parts, mom, var)


_ANY = pl.BlockSpec(memory_space=pl.ANY)


def _position():
    return lax.axis_index("x"), lax.axis_index("y"), lax.axis_index("c")


def _other_chips(x, y):
    return [(1 - x, y), (x, 1 - y), (1 - x, 1 - y)]


def _comm_call(body, arrs, out_shapes, n_sem, *, name):
    return pl.pallas_call(
        body, out_shape=tuple(out_shapes), in_specs=[_ANY] * len(arrs), out_specs=tuple([_ANY] * len(out_shapes)),
        scratch_shapes=(pltpu.SemaphoreType.DMA((n_sem,)), pltpu.SemaphoreType.DMA((n_sem,)),
                        pltpu.SemaphoreType.DMA((len(arrs),))),
        name=name)(*arrs)


def allgather_chips(arrs, *, name):
    n = len(arrs)

    def body(*refs):
        ins, outs = refs[:n], refs[n:2 * n]
        send_sems, recv_sems, loc_sems = refs[2 * n:]
        x, y, c = _position()
        me = 2 * x + y
        copies = []
        for a in range(n):
            loc = pltpu.make_async_copy(ins[a], outs[a].at[me], loc_sems.at[a])
            loc.start()
            copies.append(loc)
            for k, (px, py) in enumerate(_other_chips(x, y)):
                cp = pltpu.make_async_remote_copy(
                    src_ref=ins[a], dst_ref=outs[a].at[me], send_sem=send_sems.at[3 * a + k],
                    recv_sem=recv_sems.at[3 * a + k], device_id=(px, py, c), device_id_type=MESH)
                cp.start()
                copies.append(cp)
        for cp in copies:
            cp.wait()

    outs = [S((4,) + a.shape, a.dtype) for a in arrs]
    return _comm_call(body, arrs, outs, 3 * n, name=name)


def exchange_chips(arrs, small, *, name):
    n = len(arrs)

    def body(*refs):
        ins, small_in = refs[:n], refs[n]
        outs, small_out = refs[n + 1:2 * n + 1], refs[2 * n + 1]
        send_sems, recv_sems, loc_sems = refs[2 * n + 2:]
        x, y, c = _position()
        me = 2 * x + y
        copies = []
        for a in range(n):
            loc = pltpu.make_async_copy(ins[a].at[me], outs[a].at[me], loc_sems.at[a])
            loc.start()
            copies.append(loc)
            for k, (px, py) in enumerate(_other_chips(x, y)):
                cp = pltpu.make_async_remote_copy(
                    src_ref=ins[a].at[2 * px + py], dst_ref=outs[a].at[me], send_sem=send_sems.at[3 * a + k],
                    recv_sem=recv_sems.at[3 * a + k], device_id=(px, py, c), device_id_type=MESH)
                cp.start()
                copies.append(cp)
        me8 = 4 * x + 2 * y + c
        loc = pltpu.make_async_copy(small_in, small_out.at[me8], loc_sems.at[n])
        loc.start()
        copies.append(loc)
        k = 3 * n
        for fx in (0, 1):
            for fy in (0, 1):
                for fc in (0, 1):
                    if fx + fy + fc == 0:
                        continue
                    peer = (1 - x if fx else x, 1 - y if fy else y, 1 - c if fc else c)
                    cp = pltpu.make_async_remote_copy(
                        src_ref=small_in, dst_ref=small_out.at[me8], send_sem=send_sems.at[k],
                        recv_sem=recv_sems.at[k], device_id=peer, device_id_type=MESH)
                    cp.start()
                    copies.append(cp)
                    k += 1
        for cp in copies:
            cp.wait()

    outs = [S(a.shape, a.dtype) for a in arrs] + [S((8,) + small.shape, small.dtype)]
    return _comm_call(body, list(arrs) + [small], outs, 3 * n + 7, name=name)


def swap_cores(arrs, *, name):
    n = len(arrs)

    def body(*refs):
        ins, outs = refs[:n], refs[n:2 * n]
        send_sems, recv_sems, _ = refs[2 * n:]
        x, y, c = _position()
        copies = []
        for a in range(n):
            cp = pltpu.make_async_remote_copy(
                src_ref=ins[a], dst_ref=outs[a], send_sem=send_sems.at[a], recv_sem=recv_sems.at[a],
                device_id=(x, y, 1 - c), device_id_type=MESH)
            cp.start()
            copies.append(cp)
        for cp in copies:
            cp.wait()

    return _comm_call(body, arrs, [S(a.shape, a.dtype) for a in arrs], n, name=name)


def _local_step(p, x, target):
    bsz, seq, d = x.shape
    lp = LEAD + seq
    m = bsz * lp
    h0 = jnp.concatenate([jnp.zeros((bsz, PAD, d), F32), jnp.broadcast_to(p["meta"][None], (bsz, N_META, d)), x],
                         axis=1).reshape(m, d)
    u0 = rmsnorm_fwd(h0, p["even_norm"], name="norm0")
    proj = mm_nn([u0], p["win_e"], name="proj0")
    lx = conv_fwd(proj, COL_LRU_X, D_MODEL, p["lru_conv_w"], p["lru_conv_b"], name="lru_conv")
    a, b = gates_fwd(lx, p["lru_w_a"], p["lru_b_a"], p["lru_w_x"], p["lru_b_x"], p["lru_lambda"], lp, name="lru_gates")
    hs = scan_fwd(a, b, lp, name="lru_scan")
    ya = gate_fwd(hs, 0, proj, COL_LRU_G // D_MODEL, name="lru_out_gate")
    pre = conv_fwd(proj, COL_XBC, SSD_CONV_DIM, p["ssd_conv_w"], p["ssd_conv_b"], name="ssd_conv")
    y, s_in = ssd_fwd(pre, proj, p["ssd_dt_bias"], p["ssd_a_log"], p["ssd_d"], lp, name="ssd")
    yb = gnorm_fwd(y, proj, COL_Z // D_MODEL, p["ssd_norm"], name="ssd_norm")
    h1 = mm_nn([ya, yb], p["wout_e"], resid=h0, name="out0")
    u1 = rmsnorm_fwd(h1, p["odd_norm"], name="norm1")
    qkvg = mm_nn([u1], p["win_o"], name="proj1")
    o = attn_fwd(qkvg, lp, name="attn")
    og = gate_fwd(o, 0, qkvg, 3, name="attn_gate")
    h2 = mm_nn([og], p["wout_o"], resid=h1, name="out1")
    dh2, dh2b, loss, d_final = loss_head(h2, target, p["final_norm"], lp, name="loss_head")
    g = {"final_norm": d_final}
    g["odd_w_out"] = mm_tn(og, dh2b, name="dw_out1")
    d_og = mm_nn([dh2b], p["wout_o_t"], name="d_out1")
    do, dgate = gate_bwd(d_og, 0, o, 0, qkvg, 3, name="attn_gate_bwd")
    dq, dk, dv = attn_bwd(qkvg, do, lp, name="attn_bwd")
    segs1 = [dq, dk, dv, dgate]
    du1 = mm_nn(segs1, p["win_o_t"], name="d_proj1")
    g["odd_w_in"] = jnp.concatenate([mm_tn(u1, t, name=f"dw_proj1_{k}") for k, t in enumerate(segs1)], axis=1)
    dh1, dh1b, g["odd_norm"] = rmsnorm_bwd(du1, h1, p["odd_norm"], dh2, name="norm1_bwd")
    g["even_w_out"] = jnp.concatenate([mm_tn(ya, dh1b, name="dw_out0_a"), mm_tn(yb, dh1b, name="dw_out0_b")], axis=0)
    d_mixed = mm_nn([dh1b], p["wout_e_t"], name="d_out0")
    dhs, dlg = gate_bwd(d_mixed, 0, hs, 0, proj, COL_LRU_G // D_MODEL, name="lru_out_gate_bwd")
    dy, dz, g["ssd_norm"] = gnorm_bwd(d_mixed, 1, y, proj, COL_Z // D_MODEL, p["ssd_norm"], name="ssd_norm_bwd")
    dpre, ddt, g["ssd_dt_bias"], g["ssd_a_log"], g["ssd_d"] = ssd_bwd(
        pre, proj, s_in, dy, p["ssd_dt_bias"], p["ssd_a_log"], p["ssd_d"], lp, name="ssd_bwd")
    dxbc, g["ssd_conv_w"], g["ssd_conv_b"] = conv_bwd(proj, COL_XBC, SSD_CONV_DIM, dpre, p["ssd_conv_w"],
                                                      name="ssd_conv_bwd")
    da, db = scan_bwd(a, hs, dhs, lp, name="lru_scan_bwd")
    dlx, g["lru_w_a"], g["lru_w_x"], g["lru_b_a"], g["lru_b_x"], g["lru_lambda"] = gates_bwd(
        lx, da, db, p["lru_w_a"], p["lru_b_a"], p["lru_w_x"], p["lru_b_x"], p["lru_lambda"], lp, name="lru_gates_bwd")
    dlrux, g["lru_conv_w"], g["lru_conv_b"] = conv_bwd(proj, COL_LRU_X, D_MODEL, dlx, p["lru_conv_w"],
                                                       name="lru_conv_bwd")
    segs0 = [dlrux, dlg, dz, dxbc, ddt]
    du0 = mm_nn(segs0, p["win_e_t"], name="d_proj0")
    g["even_w_in"] = jnp.concatenate([mm_tn(u0, t, name=f"dw_proj0_{k}") for k, t in enumerate(segs0)],
                                     axis=1)[:, :EVEN_IN]
    dh0, _, g["even_norm"] = rmsnorm_bwd(du0, h0, p["even_norm"], dh1, name="norm0_bwd")
    g["meta"] = meta_grad(dh0, lp, name="meta_grad")
    grad_x = dh0.reshape(bsz, lp, d)[:, LEAD:]
    return loss, grad_x, g


WEIGHTS = ("meta", "even_norm", "even_w_in", "lru_conv_w", "lru_conv_b", "lru_w_a", "lru_b_a", "lru_w_x", "lru_b_x",
           "lru_lambda", "ssd_conv_w", "ssd_conv_b", "ssd_dt_bias", "ssd_a_log", "ssd_d", "ssd_norm", "even_w_out",
           "odd_norm", "odd_w_in", "odd_w_out", "final_norm")
BIG = ("even_w_in", "even_w_out", "odd_w_in", "odd_w_out", "lru_w_a", "lru_w_x")
SHARDED_SMALL = {"meta": 256, "lru_conv_w": 256, "ssd_conv_w": 384, "odd_norm": 256}
SMALL_SHAPES = {"meta": (16, 1024), "even_norm": (1, 1024), "lru_conv_w": (4, 1024), "lru_conv_b": (1, 1024),
                "lru_b_a": (1, 1024), "lru_b_x": (1, 1024), "lru_lambda": (1, 1024), "ssd_conv_w": (4, 1536),
                "ssd_conv_b": (1, 1536), "ssd_dt_bias": (1, 16), "ssd_a_log": (1, 16), "ssd_d": (1, 16),
                "ssd_norm": (1, 1024), "odd_norm": (1, 1024), "final_norm": (1, 1024)}
PACK_UNIT = 1024


def _pack(parts):
    flat = []
    for part in parts:
        v = part.reshape(-1)
        flat.append(jnp.pad(v, (0, -v.shape[0] % PACK_UNIT)))
    return jnp.concatenate(flat).reshape(-1, 128)


def _unpack(buf, shapes):
    v = buf.reshape(-1)
    out, off = [], 0
    for shp in shapes:
        n = 1
        for s_ in shp:
            n *= s_
        out.append(v[off:off + n].reshape(shp))
        off += n + (-n % PACK_UNIT)
    return out


def _chip_cols(a4):
    return jnp.transpose(a4, (1, 0, 2)).reshape(a4.shape[1], -1)


def _to_chip_cols(a, cols):
    return jnp.transpose(a.reshape(a.shape[0], 4, cols), (1, 0, 2))


def kernel(x, meta, even_norm, even_w_in, lru_conv_w, lru_conv_b, lru_w_a, lru_b_a, lru_w_x, lru_b_x, lru_lambda, ssd_conv_w, ssd_conv_b, ssd_dt_bias, ssd_a_log, ssd_d, ssd_norm, even_w_out, odd_norm, odd_w_in, odd_w_out, final_norm, loss_target, m_meta, m_even_norm, m_even_w_in, m_lru_conv_w, m_lru_conv_b, m_lru_w_a, m_lru_b_a, m_lru_w_x, m_lru_b_x, m_lru_lambda, m_ssd_conv_w, m_ssd_conv_b, m_ssd_dt_bias, m_ssd_a_log, m_ssd_d, m_ssd_norm, m_even_w_out, m_odd_norm, m_odd_w_in, m_odd_w_out, m_final_norm, v_meta, v_even_norm, v_even_w_in, v_lru_conv_w, v_lru_conv_b, v_lru_w_a, v_lru_b_a, v_lru_w_x, v_lru_b_x, v_lru_lambda, v_ssd_conv_w, v_ssd_conv_b, v_ssd_dt_bias, v_ssd_a_log, v_ssd_d, v_ssd_norm, v_even_w_out, v_odd_norm, v_odd_w_in, v_odd_w_out, v_final_norm):
    given = dict(locals())
    w = {n: given[n] for n in WEIGHTS}
    mom = {n: given["m_" + n] for n in WEIGHTS}
    var = {n: given["v_" + n] for n in WEIGHTS}
    chip = 2 * lax.axis_index("x") + lax.axis_index("y")

    big_local = {"even_w_in": even_w_in[0], "even_w_out": even_w_out[0], "odd_w_in": odd_w_in[0],
                 "odd_w_out": odd_w_out[0], "lru_w_a": lru_w_a[0].reshape(256, 256),
                 "lru_w_x": lru_w_x[0].reshape(256, 256)}
    sharded_local = [meta, lru_conv_w[0], ssd_conv_w[0], odd_norm]
    gathered = allgather_chips([big_local[n].astype(BF16) for n in BIG] + [_pack(sharded_local)], name="gather_weights")
    gb = dict(zip(BIG, gathered[:-1]))
    per_chip = [_unpack(gathered[-1][k], [a.shape for a in sharded_local]) for k in range(4)]
    full_small = [jnp.concatenate([per_chip[k][j] for k in range(4)], axis=-1) for j in range(len(sharded_local))]

    def lru_full(a4):
        return jnp.transpose(a4.reshape(4, LRU_BLOCKS, 64, LRU_BLOCK), (1, 0, 2, 3)).reshape(LRU_BLOCKS, LRU_BLOCK, LRU_BLOCK)

    p = {"meta": full_small[0], "lru_conv_w": full_small[1], "ssd_conv_w": full_small[2], "odd_norm": full_small[3],
         "even_norm": even_norm, "lru_conv_b": lru_conv_b, "lru_b_a": lru_b_a, "lru_b_x": lru_b_x,
         "lru_lambda": lru_lambda, "ssd_conv_b": ssd_conv_b, "ssd_dt_bias": ssd_dt_bias, "ssd_a_log": ssd_a_log,
         "ssd_d": ssd_d, "ssd_norm": ssd_norm, "final_norm": final_norm.reshape(1, D_MODEL)}
    p["win_e"] = jnp.pad(_chip_cols(gb["even_w_in"]), ((0, 0), (0, EVEN_IN_P - EVEN_IN)))
    p["wout_e"] = gb["even_w_out"].reshape(2 * D_MODEL, D_MODEL)
    p["win_o"] = _chip_cols(gb["odd_w_in"])
    p["wout_o"] = gb["odd_w_out"].reshape(D_MODEL, D_MODEL)
    for n in ("win_e", "wout_e", "win_o", "wout_o"):
        p[n + "_t"] = p[n].T
    p["lru_w_a"] = lru_full(gb["lru_w_a"])
    p["lru_w_x"] = lru_full(gb["lru_w_x"])

    loss_part, grad_x, g = _local_step(p, x, loss_target)

    def lru_slabs(a):
        return jnp.transpose(a.reshape(LRU_BLOCKS, 4, 64, LRU_BLOCK), (1, 0, 2, 3)).reshape(4, 256, LRU_BLOCK)

    slabs = [_to_chip_cols(g["even_w_in"], EVEN_IN // 4), g["even_w_out"].reshape(4, 512, D_MODEL),
             _to_chip_cols(g["odd_w_in"], D_MODEL), g["odd_w_out"].reshape(4, 256, D_MODEL),
             lru_slabs(g["lru_w_a"]), lru_slabs(g["lru_w_x"])]
    small_names = list(SMALL_SHAPES)
    small_part = _pack([loss_part[0:1, 0:1]] + [g[n] for n in small_names])
    *recv, small_all = exchange_chips(slabs, small_part, name="exchange_grads")
    core_sums = [sum_lead(r, name=f"sum_chips_{n}") for n, r in zip(BIG, recv)]
    other_sums = swap_cores(core_sums, name="swap_cores")
    small_sum = sum_lead(small_all, name="sum_small")
    small_g = dict(zip(["loss"] + small_names, _unpack(small_sum, [(1, 1)] + [SMALL_SHAPES[n] for n in small_names])))
    loss = small_g["loss"].reshape(())

    grads, delta, new_m, new_v = {}, {}, {}, {}
    for n, mine, other in zip(BIG, core_sums, other_sums):
        shp = w[n].shape
        two_d = lambda t: t.reshape(mine.shape)
        res = adamw(two_d(w[n]), [mine, other], two_d(mom[n]), two_d(var[n]), name=f"adamw_{n}")
        grads[n], delta[n], new_m[n], new_v[n] = (r.reshape(shp) for r in res)
    local_g = []
    for n in small_names:
        gn = small_g[n]
        if n in SHARDED_SMALL:
            gn = lax.dynamic_slice_in_dim(gn, chip * SHARDED_SMALL[n], SHARDED_SMALL[n], axis=1)
        local_g.append(gn)
    res = adamw(_pack([w[n] for n in small_names]), [_pack(local_g)], _pack([mom[n] for n in small_names]),
                _pack([var[n] for n in small_names]), name="adamw_small")
    shapes = [w[n].shape for n in small_names]
    for out, r in zip((grads, delta, new_m, new_v), res):
        out.update(dict(zip(small_names, _unpack(r, shapes))))
    return (loss, grad_x, *[grads[n] for n in WEIGHTS], *[delta[n] for n in WEIGHTS], *[new_m[n] for n in WEIGHTS],
            *[new_v[n] for n in WEIGHTS])
```

```python
import functools

import jax
import jax.numpy as jnp
from jax import lax
from jax.experimental import pallas as pl
from jax.experimental.pallas import tpu as pltpu

F32 = jnp.float32
BF16 = jnp.bfloat16

D_MODEL = 1024
N_META = 16
LEAD = 128
PAD = LEAD - N_META
EPS = 1e-6
CONV_W = 4
LRU_BLOCKS = 4
LRU_BLOCK = 256
RG_LRU_C = 8.0
SSD_HEADS = 16
SSD_HEAD_DIM = 64
SSD_GROUPS = 2
SSD_HPG = 8
SSD_STATE = 128
CHUNK = 64
SSD_CONV_DIM = 1536
EVEN_IN = 4624
EVEN_IN_P = 4736
COL_LRU_X, COL_LRU_G, COL_Z, COL_XBC, COL_DT = 0, 1024, 2048, 3072, 4608
SB_HEADS = 16
SB_HEAD_DIM = 64
SB_BLOCK = 128
ADAM_LR, ADAM_B1, ADAM_B2, ADAM_EPS, ADAM_WD, ADAM_STEP = 0.001, 0.9, 0.999, 1e-08, 0.01, 10
VMEM_LIMIT_V7X = 56 * 1024 * 1024
MESH = pl.DeviceIdType.MESH
S = jax.ShapeDtypeStruct


def _tile(n, prefs):
    for p in prefs:
        if n % p == 0:
            return p
    raise ValueError(f"no tile of {prefs} divides {n}")


def _call(body, *, name, out_shape, grid=(), in_specs=None, out_specs=None, scratch=(), sem=None):
    kw = {}
    if in_specs is not None:
        kw["in_specs"] = in_specs
    if out_specs is not None:
        kw["out_specs"] = out_specs
    return pl.pallas_call(
        body, out_shape=out_shape, grid=grid, scratch_shapes=tuple(scratch), name=name,
        compiler_params=pltpu.CompilerParams(dimension_semantics=sem, vmem_limit_bytes=VMEM_LIMIT_V7X), **kw)


def _sigmoid(x):
    return 0.5 * (jnp.tanh(0.5 * x) + 1.0)


def _silu(x):
    return x * _sigmoid(x)


def _softplus(x):
    return jnp.maximum(x, 0.0) + jnp.log(1.0 + jnp.exp(-jnp.abs(x)))


def _dot(a, b):
    return jnp.dot(a, b, preferred_element_type=F32)


def _dot_nt(a, b):
    return lax.dot_general(a, b, (((1,), (1,)), ((), ())), preferred_element_type=F32)


def _dot_tn(a, b):
    return lax.dot_general(a, b, (((0,), (0,)), ((), ())), preferred_element_type=F32)


def mm_nn(a_list, w, *, name, resid=None):
    m = a_list[0].shape[0]
    k_tot, n = w.shape
    ks = [a.shape[1] for a in a_list]
    assert sum(ks) == k_tot
    tm = _tile(m, (256, 128))
    n_a = len(a_list)
    offs = [sum(ks[:i]) for i in range(n_a)]
    n_chunks = [(c0, min(512, n - c0)) for c0 in range(0, n, 512)]

    def body(*refs):
        a_refs, w_ref = refs[:n_a], refs[n_a]
        r_ref = refs[n_a + 1] if resid is not None else None
        o_ref = refs[-1]
        for c0, cw in n_chunks:
            acc = None
            for a_ref, k0, k in zip(a_refs, offs, ks):
                p = _dot(a_ref[...], w_ref[k0:k0 + k, c0:c0 + cw])
                acc = p if acc is None else acc + p
            if r_ref is not None:
                acc = acc + r_ref[:, c0:c0 + cw]
            o_ref[:, c0:c0 + cw] = acc

    in_specs = [pl.BlockSpec((tm, k), lambda i: (i, 0)) for k in ks]
    in_specs.append(pl.BlockSpec((k_tot, n), lambda i: (0, 0)))
    args = list(a_list) + [w]
    if resid is not None:
        in_specs.append(pl.BlockSpec((tm, n), lambda i: (i, 0)))
        args.append(resid)
    return _call(body, name=name, out_shape=S((m, n), F32), grid=(m // tm,), in_specs=in_specs,
                 out_specs=pl.BlockSpec((tm, n), lambda i: (i, 0)), sem=("parallel",))(*args)


def mm_tn(a, g, *, name):
    t, m = a.shape
    n = g.shape[1]
    tk = _tile(t, (512, 256, 128))
    tn = _tile(n, (512, 256, 128))

    def body(a_ref, g_ref, o_ref):
        @pl.when(pl.program_id(1) == 0)
        def _():
            o_ref[...] = jnp.zeros_like(o_ref)
        o_ref[...] += _dot_tn(a_ref[...], g_ref[...])

    return _call(body, name=name, out_shape=S((m, n), F32), grid=(n // tn, t // tk),
                 in_specs=[pl.BlockSpec((tk, m), lambda j, k: (k, 0)), pl.BlockSpec((tk, tn), lambda j, k: (k, j))],
                 out_specs=pl.BlockSpec((m, tn), lambda j, k: (0, j)), sem=("parallel", "arbitrary"))(a, g)


def rmsnorm_fwd(h, w, *, name):
    m, d = h.shape
    tm = _tile(m, (512, 256, 128))

    def body(h_ref, w_ref, o_ref):
        x = h_ref[...]
        r = lax.rsqrt(jnp.mean(x * x, axis=-1, keepdims=True) + EPS)
        o_ref[...] = (x * r * w_ref[...]).astype(BF16)

    return _call(body, name=name, out_shape=S((m, d), BF16), grid=(m // tm,),
                 in_specs=[pl.BlockSpec((tm, d), lambda i: (i, 0)), pl.BlockSpec((1, d), lambda i: (0, 0))],
                 out_specs=pl.BlockSpec((tm, d), lambda i: (i, 0)), sem=("parallel",))(h, w)


def rmsnorm_bwd(du, h, w, dres, *, name):
    m, d = h.shape
    tm = _tile(m, (256, 128))

    def body(du_ref, h_ref, w_ref, dr_ref, dh_ref, dhb_ref, dw_ref):
        @pl.when(pl.program_id(0) == 0)
        def _():
            dw_ref[...] = jnp.zeros_like(dw_ref)
        x = h_ref[...]
        r = lax.rsqrt(jnp.mean(x * x, axis=-1, keepdims=True) + EPS)
        du_ = du_ref[...]
        g = du_ * w_ref[...]
        c = jnp.mean(g * x, axis=-1, keepdims=True)
        dh = dr_ref[...] + r * g - x * (r * r * r) * c
        dh_ref[...] = dh
        dhb_ref[...] = dh.astype(BF16)
        dw_ref[...] += jnp.sum(du_ * x * r, axis=0, keepdims=True)

    row = pl.BlockSpec((tm, d), lambda i: (i, 0))
    vec = pl.BlockSpec((1, d), lambda i: (0, 0))
    return _call(body, name=name, out_shape=(S((m, d), F32), S((m, d), BF16), S((1, d), F32)), grid=(m // tm,),
                 in_specs=[row, row, vec, row], out_specs=(row, row, vec), sem=("arbitrary",))(du, h, w, dres)


def gate_fwd(o, o_cb, g, g_cb, *, name):
    m = o.shape[0]
    d = D_MODEL
    tm = _tile(m, (512, 256, 128))

    def body(o_ref, g_ref, y_ref):
        y_ref[...] = (o_ref[...] * _silu(g_ref[...])).astype(BF16)

    return _call(body, name=name, out_shape=S((m, d), BF16), grid=(m // tm,),
                 in_specs=[pl.BlockSpec((tm, d), lambda i: (i, o_cb)), pl.BlockSpec((tm, d), lambda i: (i, g_cb))],
                 out_specs=pl.BlockSpec((tm, d), lambda i: (i, 0)), sem=("parallel",))(o, g)


def gate_bwd(dy, dy_cb, o, o_cb, g, g_cb, *, name):
    m = o.shape[0]
    d = D_MODEL
    tm = _tile(m, (512, 256, 128))

    def body(dy_ref, o_ref, g_ref, do_ref, dg_ref):
        gv = g_ref[...]
        s = _sigmoid(gv)
        dyv = dy_ref[...]
        do_ref[...] = dyv * gv * s
        dg_ref[...] = (dyv * o_ref[...] * (s + gv * s * (1.0 - s))).astype(BF16)

    return _call(body, name=name, out_shape=(S((m, d), F32), S((m, d), BF16)), grid=(m // tm,),
                 in_specs=[pl.BlockSpec((tm, d), lambda i: (i, dy_cb)), pl.BlockSpec((tm, d), lambda i: (i, o_cb)),
                           pl.BlockSpec((tm, d), lambda i: (i, g_cb))],
                 out_specs=(pl.BlockSpec((tm, d), lambda i: (i, 0)),) * 2, sem=("parallel",))(dy, o, g)


def _group_mean(x):
    half = x.shape[1] // SSD_GROUPS
    parts = [jnp.broadcast_to(jnp.mean(x[:, k * half:(k + 1) * half], axis=-1, keepdims=True), (x.shape[0], half))
             for k in range(SSD_GROUPS)]
    return jnp.concatenate(parts, axis=1)


def gnorm_fwd(y, z, z_cb, w, *, name):
    m = y.shape[0]
    d = D_MODEL
    tm = _tile(m, (256, 128))

    def body(y_ref, z_ref, w_ref, o_ref):
        g = y_ref[...] * _silu(z_ref[...])
        r = lax.rsqrt(_group_mean(g * g) + EPS)
        o_ref[...] = (g * r * w_ref[...]).astype(BF16)

    return _call(body, name=name, out_shape=S((m, d), BF16), grid=(m // tm,),
                 in_specs=[pl.BlockSpec((tm, d), lambda i: (i, 0)), pl.BlockSpec((tm, d), lambda i: (i, z_cb)),
                           pl.BlockSpec((1, d), lambda i: (0, 0))],
                 out_specs=pl.BlockSpec((tm, d), lambda i: (i, 0)), sem=("parallel",))(y, z, w)


def gnorm_bwd(do, do_cb, y, z, z_cb, w, *, name):
    m = y.shape[0]
    d = D_MODEL
    tm = _tile(m, (256, 128))

    def body(do_ref, y_ref, z_ref, w_ref, dy_ref, dz_ref, dw_ref):
        @pl.when(pl.program_id(0) == 0)
        def _():
            dw_ref[...] = jnp.zeros_like(dw_ref)
        yv, zv, dov = y_ref[...], z_ref[...], do_ref[...]
        s = _sigmoid(zv)
        sz = zv * s
        g = yv * sz
        r = lax.rsqrt(_group_mean(g * g) + EPS)
        dw_ref[...] += jnp.sum(dov * g * r, axis=0, keepdims=True)
        dn = dov * w_ref[...]
        dg = r * dn - g * (r * r * r) * _group_mean(dn * g)
        dy_ref[...] = dg * sz
        dz_ref[...] = (dg * yv * (s + zv * s * (1.0 - s))).astype(BF16)

    row = pl.BlockSpec((tm, d), lambda i: (i, 0))
    vec = pl.BlockSpec((1, d), lambda i: (0, 0))
    return _call(body, name=name, out_shape=(S((m, d), F32), S((m, d), BF16), S((1, d), F32)), grid=(m // tm,),
                 in_specs=[pl.BlockSpec((tm, d), lambda i: (i, do_cb)), row, pl.BlockSpec((tm, d), lambda i: (i, z_cb)), vec],
                 out_specs=(row, row, vec), sem=("arbitrary",))(do, y, z, w)


def loss_head(h, target, w, lp, *, name):
    m, d = h.shape
    bsz = m // lp
    tm = SB_BLOCK
    nblk = lp // tm
    lead_blk = LEAD // tm

    def body(h_ref, t_ref, w_ref, dh_ref, dhb_ref, l_ref, dw_ref):
        i = pl.program_id(1)

        @pl.when(jnp.logical_and(pl.program_id(0) == 0, i == 0))
        def _():
            l_ref[...] = jnp.zeros_like(l_ref)
            dw_ref[...] = jnp.zeros_like(dw_ref)

        @pl.when(i < lead_blk)
        def _():
            dh_ref[...] = jnp.zeros_like(dh_ref)
            dhb_ref[...] = jnp.zeros_like(dhb_ref)

        @pl.when(i >= lead_blk)
        def _():
            x = h_ref[...]
            r = lax.rsqrt(jnp.mean(x * x, axis=-1, keepdims=True) + EPS)
            wv = w_ref[...]
            e = x * r * wv - t_ref[0]
            l_ref[...] += 0.5 * jnp.sum(jnp.mean(e * e, axis=-1, keepdims=True))
            dy = e * (1.0 / d)
            g = dy * wv
            c = jnp.mean(g * x, axis=-1, keepdims=True)
            dh = r * g - x * (r * r * r) * c
            dh_ref[...] = dh
            dhb_ref[...] = dh.astype(BF16)
            dw_ref[...] += jnp.sum(dy * x * r, axis=0, keepdims=True)

    row = pl.BlockSpec((tm, d), lambda b, i: (b * nblk + i, 0))
    return _call(body, name=name, out_shape=(S((m, d), F32), S((m, d), BF16), S((8, 128), F32), S((1, d), F32)),
                 grid=(bsz, nblk),
                 in_specs=[row, pl.BlockSpec((1, tm, d), lambda b, i: (b, jnp.maximum(i - lead_blk, 0), 0)),
                           pl.BlockSpec((1, d), lambda b, i: (0, 0))],
                 out_specs=(row, row, pl.BlockSpec((8, 128), lambda b, i: (0, 0)),
                            pl.BlockSpec((1, d), lambda b, i: (0, 0))),
                 sem=("arbitrary", "arbitrary"))(h, target, w)


def _conv_tiles(m, c):
    return _tile(m, (256, 128)), _tile(c, (512, 256, 128))


def conv_fwd(x, col0, c, w, b, *, name):
    m = x.shape[0]
    tm, tc = _conv_tiles(m, c)
    assert col0 % tc == 0
    cb0 = col0 // tc
    hb = tm // 8

    def body(x_ref, halo_ref, w_ref, b_ref, o_ref, ext):
        ext[0:8, :] = halo_ref[...]
        ext[8:8 + tm, :] = x_ref[...]
        acc = b_ref[...] + w_ref[3:4, :] * ext[8:8 + tm, :]
        for k in range(CONV_W - 1):
            acc = acc + w_ref[k:k + 1, :] * ext[pl.ds(5 + k, tm), :]
        o_ref[...] = acc

    return _call(body, name=name, out_shape=S((m, c), F32), grid=(m // tm, c // tc),
                 in_specs=[pl.BlockSpec((tm, tc), lambda i, j: (i, cb0 + j)),
                           pl.BlockSpec((8, tc), lambda i, j: (jnp.maximum(i * hb - 1, 0), cb0 + j)),
                           pl.BlockSpec((CONV_W, tc), lambda i, j: (0, j)), pl.BlockSpec((1, tc), lambda i, j: (0, j))],
                 out_specs=pl.BlockSpec((tm, tc), lambda i, j: (i, j)), scratch=[pltpu.VMEM((tm + 8, tc), F32)],
                 sem=("parallel", "parallel"))(x, x, w, b)


def conv_bwd(x, col0, c, dy, w, *, name):
    m = x.shape[0]
    tm, tc = _conv_tiles(m, c)
    cb0 = col0 // tc
    n_i = m // tm
    hb = tm // 8

    def body(x_ref, xh_ref, dy_ref, dyn_ref, w_ref, dx_ref, dw_ref, db_ref, ext, dext):
        i = pl.program_id(1)

        @pl.when(i == 0)
        def _():
            dw_ref[...] = jnp.zeros_like(dw_ref)
            db_ref[...] = jnp.zeros_like(db_ref)

        ext[0:8, :] = xh_ref[...]
        ext[8:8 + tm, :] = x_ref[...]
        d_cur = dy_ref[...]
        dext[0:tm, :] = d_cur
        dext[tm:tm + 8, :] = jnp.where(i == n_i - 1, 0.0, dyn_ref[...])
        dx = w_ref[3:4, :] * d_cur
        for k in range(CONV_W - 1):
            dx = dx + w_ref[k:k + 1, :] * dext[pl.ds(3 - k, tm), :]
        dx_ref[...] = dx.astype(BF16)
        for k in range(CONV_W - 1):
            dw_ref[k:k + 1, :] += jnp.sum(ext[pl.ds(5 + k, tm), :] * d_cur, axis=0, keepdims=True)
        dw_ref[3:4, :] += jnp.sum(ext[8:8 + tm, :] * d_cur, axis=0, keepdims=True)
        db_ref[...] += jnp.sum(d_cur, axis=0, keepdims=True)

    return _call(body, name=name, out_shape=(S((m, c), BF16), S((CONV_W, c), F32), S((1, c), F32)),
                 grid=(c // tc, n_i),
                 in_specs=[pl.BlockSpec((tm, tc), lambda j, i: (i, cb0 + j)),
                           pl.BlockSpec((8, tc), lambda j, i: (jnp.maximum(i * hb - 1, 0), cb0 + j)),
                           pl.BlockSpec((tm, tc), lambda j, i: (i, j)),
                           pl.BlockSpec((8, tc), lambda j, i: (jnp.minimum((i + 1) * hb, n_i * hb - 1), j)),
                           pl.BlockSpec((CONV_W, tc), lambda j, i: (0, j))],
                 out_specs=(pl.BlockSpec((tm, tc), lambda j, i: (i, j)), pl.BlockSpec((CONV_W, tc), lambda j, i: (0, j)),
                            pl.BlockSpec((1, tc), lambda j, i: (0, j))),
                 scratch=[pltpu.VMEM((tm + 8, tc), F32), pltpu.VMEM((tm + 8, tc), F32)],
                 sem=("parallel", "arbitrary"))(x, x, dy, dy, w)


def _row_valid(tile_idx, tiles_per_seq, tm):
    pos = lax.rem(tile_idx, tiles_per_seq) * tm + lax.broadcasted_iota(jnp.int32, (tm, 1), 0)
    return (pos >= PAD).astype(F32)


def _neg_expm1(x):
    small = -(x * (1.0 + x * (0.5 + x * (1.0 / 6.0))))
    return jnp.where(x > -0.01, small, 1.0 - jnp.exp(x))


def _gates_core(lx, wa, ba, wx, bx, lam, valid):
    lxb = lx.astype(BF16)
    r = _sigmoid(_dot(lxb, wa.astype(BF16)) + ba)
    i = _sigmoid(_dot(lxb, wx.astype(BF16)) + bx)
    log_a = (-RG_LRU_C) * r * _softplus(-lam)
    a = jnp.exp(log_a)
    mult = jnp.sqrt(_neg_expm1(2.0 * log_a))
    return a, valid * (mult * i * lx)


def gates_fwd(lx, wa, ba, wx, bx, lam, lp, *, name):
    m = lx.shape[0]
    tm = SB_BLOCK
    tps = lp // tm
    cb = LRU_BLOCK

    def body(lx_ref, wa_ref, ba_ref, wx_ref, bx_ref, lam_ref, a_ref, b_ref):
        valid = _row_valid(pl.program_id(1), tps, tm)
        a, b = _gates_core(lx_ref[...], wa_ref[0], ba_ref[...], wx_ref[0], bx_ref[...], lam_ref[...], valid)
        a_ref[...] = a
        b_ref[...] = b

    tok = pl.BlockSpec((tm, cb), lambda g, i: (i, g))
    wsp = pl.BlockSpec((1, cb, cb), lambda g, i: (g, 0, 0))
    vec = pl.BlockSpec((1, cb), lambda g, i: (0, g))
    return _call(body, name=name, out_shape=(S((m, D_MODEL), F32),) * 2, grid=(LRU_BLOCKS, m // tm),
                 in_specs=[tok, wsp, vec, wsp, vec, vec], out_specs=(tok, tok),
                 sem=("parallel", "parallel"))(lx, wa, ba, wx, bx, lam)


def gates_bwd(lx, da, db, wa, ba, wx, bx, lam, lp, *, name):
    m = lx.shape[0]
    tm = SB_BLOCK
    tps = lp // tm
    cb = LRU_BLOCK

    def body(lx_ref, da_ref, db_ref, wa_ref, ba_ref, wx_ref, bx_ref, lam_ref,
             dlx_ref, dwa_ref, dwx_ref, dba_ref, dbx_ref, dlam_ref):
        @pl.when(pl.program_id(1) == 0)
        def _():
            for r in (dwa_ref, dwx_ref, dba_ref, dbx_ref, dlam_ref):
                r[...] = jnp.zeros_like(r)
        valid = _row_valid(pl.program_id(1), tps, tm)
        core = functools.partial(_gates_core, valid=valid)
        _, vjp = jax.vjp(core, lx_ref[...], wa_ref[0].astype(F32), ba_ref[...], wx_ref[0].astype(F32), bx_ref[...],
                         lam_ref[...])
        dlx, dwa, dba, dwx, dbx, dlam = vjp((da_ref[...], db_ref[...]))
        dlx_ref[...] = dlx
        dwa_ref[0] += dwa
        dwx_ref[0] += dwx
        dba_ref[...] += dba
        dbx_ref[...] += dbx
        dlam_ref[...] += dlam

    tok = pl.BlockSpec((tm, cb), lambda g, i: (i, g))
    wsp = pl.BlockSpec((1, cb, cb), lambda g, i: (g, 0, 0))
    vec = pl.BlockSpec((1, cb), lambda g, i: (0, g))
    wshape = S((LRU_BLOCKS, cb, cb), F32)
    vshape = S((1, D_MODEL), F32)
    return _call(body, name=name, out_shape=(S((m, D_MODEL), F32), wshape, wshape, vshape, vshape, vshape),
                 grid=(LRU_BLOCKS, m // tm), in_specs=[tok, tok, tok, wsp, vec, wsp, vec, vec],
                 out_specs=(tok, wsp, wsp, vec, vec, vec),
                 sem=("parallel", "arbitrary"))(lx, da, db, wa, ba, wx, bx, lam)


SCAN_TOK = 128


def scan_fwd(a, b, lp, *, name):
    m = a.shape[0]
    bsz = m // lp
    nch = lp // SCAN_TOK
    rows = SCAN_TOK * 8

    def body(a_ref, b_ref, h_ref, carry):
        @pl.when(pl.program_id(1) == 0)
        def _():
            carry[...] = jnp.zeros_like(carry)

        def step(t, h):
            r = pl.ds(pl.multiple_of(t * 8, 8), 8)
            h = a_ref[r, :] * h + b_ref[r, :]
            h_ref[r, :] = h
            return h

        carry[...] = lax.fori_loop(0, SCAN_TOK, step, carry[...], unroll=8)

    blk = pl.BlockSpec((rows, 128), lambda s, c: (s * nch + c, 0))
    out = _call(body, name=name, out_shape=S((m * 8, 128), F32), grid=(bsz, nch), in_specs=[blk, blk], out_specs=blk,
                scratch=[pltpu.VMEM((8, 128), F32)], sem=("parallel", "arbitrary"))(
                    a.reshape(m * 8, 128), b.reshape(m * 8, 128))
    return out.reshape(m, D_MODEL)


def scan_bwd(a, h, dh, lp, *, name):
    m = a.shape[0]
    bsz = m // lp
    nch = lp // SCAN_TOK
    rows = SCAN_TOK * 8

    def body(a_ref, h_ref, hprev_ref, dh_ref, da_ref, db_ref, carry):
        c = pl.program_id(1)

        @pl.when(c == 0)
        def _():
            carry[...] = jnp.zeros_like(carry)

        h_before = jnp.where(c == nch - 1, 0.0, hprev_ref[...])

        def step(k, ag):
            t = SCAN_TOK - 1 - k
            r = pl.ds(pl.multiple_of(t * 8, 8), 8)
            g = dh_ref[r, :] + ag
            db_ref[r, :] = g
            tp = jnp.maximum(t - 1, 0)
            hp = jnp.where(t == 0, h_before, h_ref[pl.ds(pl.multiple_of(tp * 8, 8), 8), :])
            da_ref[r, :] = g * hp
            return a_ref[r, :] * g

        carry[...] = lax.fori_loop(0, SCAN_TOK, step, carry[...], unroll=8)

    blk = pl.BlockSpec((rows, 128), lambda s, c: (s * nch + (nch - 1 - c), 0))
    prev = pl.BlockSpec((8, 128), lambda s, c: (jnp.maximum((s * nch + (nch - 1 - c)) * SCAN_TOK - 1, 0), 0))
    a2, h2, dh2 = (v.reshape(m * 8, 128) for v in (a, h, dh))
    da, db = _call(body, name=name, out_shape=(S((m * 8, 128), F32),) * 2, grid=(bsz, nch),
                   in_specs=[blk, blk, prev, blk], out_specs=(blk, blk), scratch=[pltpu.VMEM((8, 128), F32)],
                   sem=("parallel", "arbitrary"))(a2, h2, h2, dh2)
    return da.reshape(m, D_MODEL), db.reshape(m, D_MODEL)


def _ssd_chunk(xs, bs, cs, dtr, ss, dtb, alog, dpar, valid):
    row = lax.broadcasted_iota(jnp.int32, (CHUNK, CHUNK), 0)
    col = lax.broadcasted_iota(jnp.int32, (CHUNK, CHUNK), 1)
    tri = row >= col
    dt = _softplus(dtr + dtb) * valid
    adt = dt * (-jnp.exp(alog))
    acum = jnp.dot(tri.astype(F32), adt, precision=lax.Precision.HIGHEST, preferred_element_type=F32)
    acum_t = acum.T
    b16 = [(_silu(b) * valid).astype(BF16) for b in bs]
    c16 = [(_silu(c) * valid).astype(BF16) for c in cs]
    cbs = [_dot_nt(c, b) for c, b in zip(c16, b16)]
    heads = range(SSD_HEADS)
    grp = [h // SSD_HPG for h in heads]
    x = [_silu(xs[h]) for h in heads]
    ac = [acum[:, h:h + 1] for h in heads]
    alast = [acum[CHUNK - 1:CHUNK, h:h + 1] for h in heads]
    xd = [x[h] * dt[:, h:h + 1] for h in heads]
    lhs = [(cbs[grp[h]] * jnp.exp(jnp.where(tri, ac[h] - acum_t[h:h + 1, :], -1e30))).astype(BF16) for h in heads]
    xd16 = [xd[h].astype(BF16) for h in heads]
    xdec16 = [(xd[h] * jnp.exp(alast[h] - ac[h])).astype(BF16) for h in heads]
    s16 = [ss[h].astype(BF16) for h in heads]
    y_diag = [_dot(lhs[h], xd16[h]) for h in heads]
    y_off = [_dot_nt(c16[grp[h]], s16[h]) for h in heads]
    st = [_dot_tn(xdec16[h], b16[grp[h]]) for h in heads]
    ys = [y_diag[h] + y_off[h] * jnp.exp(ac[h]) + x[h] * dpar[:, h:h + 1] for h in heads]
    s_new = [jnp.exp(alast[h]) * ss[h] + st[h] for h in heads]
    return ys, s_new


def _ssd_load(pre_ref, s_ref):
    xs = [pre_ref[:, h * SSD_HEAD_DIM:(h + 1) * SSD_HEAD_DIM] for h in range(SSD_HEADS)]
    b0 = SSD_HEADS * SSD_HEAD_DIM
    bs = [pre_ref[:, b0 + g * SSD_STATE:b0 + (g + 1) * SSD_STATE] for g in range(SSD_GROUPS)]
    c0 = b0 + SSD_GROUPS * SSD_STATE
    cs = [pre_ref[:, c0 + g * SSD_STATE:c0 + (g + 1) * SSD_STATE] for g in range(SSD_GROUPS)]
    ss = [s_ref[h * SSD_HEAD_DIM:(h + 1) * SSD_HEAD_DIM, :] for h in range(SSD_HEADS)]
    return xs, bs, cs, ss


def ssd_fwd(pre, proj, dtb, alog, dpar, lp, *, name):
    m = pre.shape[0]
    bsz = m // lp
    nc = lp // CHUNK
    srows = SSD_HEADS * SSD_HEAD_DIM

    def body(pre_ref, dt_ref, dtb_ref, alog_ref, d_ref, y_ref, sin_ref, state):
        c = pl.program_id(1)

        @pl.when(c == 0)
        def _():
            state[...] = jnp.zeros_like(state)

        sin_ref[...] = state[...]
        xs, bs, cs, ss = _ssd_load(pre_ref, state)
        valid = _row_valid(c, nc, CHUNK)
        ys, s_new = _ssd_chunk(xs, bs, cs, dt_ref[:, 0:SSD_HEADS], ss, dtb_ref[...], alog_ref[...], d_ref[...], valid)
        for h in range(SSD_HEADS):
            y_ref[:, h * SSD_HEAD_DIM:(h + 1) * SSD_HEAD_DIM] = ys[h]
            state[h * SSD_HEAD_DIM:(h + 1) * SSD_HEAD_DIM, :] = s_new[h]

    par = pl.BlockSpec((1, SSD_HEADS), lambda s, c: (0, 0))
    return _call(body, name=name, out_shape=(S((m, D_MODEL), F32), S((m // CHUNK * srows, SSD_STATE), F32)),
                 grid=(bsz, nc),
                 in_specs=[pl.BlockSpec((CHUNK, SSD_CONV_DIM), lambda s, c: (s * nc + c, 0)),
                           pl.BlockSpec((CHUNK, 128), lambda s, c: (s * nc + c, COL_DT // 128)), par, par, par],
                 out_specs=(pl.BlockSpec((CHUNK, D_MODEL), lambda s, c: (s * nc + c, 0)),
                            pl.BlockSpec((srows, SSD_STATE), lambda s, c: (s * nc + c, 0))),
                 scratch=[pltpu.VMEM((srows, SSD_STATE), F32)],
                 sem=("parallel", "arbitrary"))(pre, proj, dtb, alog, dpar)


def ssd_bwd(pre, proj, s_in, dy, dtb, alog, dpar, lp, *, name):
    m = pre.shape[0]
    bsz = m // lp
    nc = lp // CHUNK
    srows = SSD_HEADS * SSD_HEAD_DIM

    def body(pre_ref, dt_ref, sin_ref, dy_ref, dtb_ref, alog_ref, d_ref,
             dpre_ref, ddt_ref, ddtb_ref, dalog_ref, dd_ref, dstate):
        c = pl.program_id(1)

        @pl.when(jnp.logical_and(pl.program_id(0) == 0, c == 0))
        def _():
            for r in (ddtb_ref, dalog_ref, dd_ref):
                r[...] = jnp.zeros_like(r)

        @pl.when(c == 0)
        def _():
            dstate[...] = jnp.zeros_like(dstate)

        xs, bs, cs, ss = _ssd_load(pre_ref, sin_ref)
        valid = _row_valid(nc - 1 - c, nc, CHUNK)
        core = functools.partial(_ssd_chunk, valid=valid)
        _, vjp = jax.vjp(core, xs, bs, cs, dt_ref[:, 0:SSD_HEADS], ss, dtb_ref[...], alog_ref[...], d_ref[...])
        dys = [dy_ref[:, h * SSD_HEAD_DIM:(h + 1) * SSD_HEAD_DIM] for h in range(SSD_HEADS)]
        dsn = [dstate[h * SSD_HEAD_DIM:(h + 1) * SSD_HEAD_DIM, :] for h in range(SSD_HEADS)]
        dxs, dbs, dcs, ddtr, dss, ddtb, dalog, dd = vjp((dys, dsn))
        b0 = SSD_HEADS * SSD_HEAD_DIM
        c0 = b0 + SSD_GROUPS * SSD_STATE
        for h in range(SSD_HEADS):
            dpre_ref[:, h * SSD_HEAD_DIM:(h + 1) * SSD_HEAD_DIM] = dxs[h]
            dstate[h * SSD_HEAD_DIM:(h + 1) * SSD_HEAD_DIM, :] = dss[h]
        for g in range(SSD_GROUPS):
            dpre_ref[:, b0 + g * SSD_STATE:b0 + (g + 1) * SSD_STATE] = dbs[g]
            dpre_ref[:, c0 + g * SSD_STATE:c0 + (g + 1) * SSD_STATE] = dcs[g]
        ddt_ref[...] = jnp.zeros_like(ddt_ref)
        ddt_ref[:, 0:SSD_HEADS] = ddtr.astype(BF16)
        ddtb_ref[...] += ddtb
        dalog_ref[...] += dalog
        dd_ref[...] += dd

    par = pl.BlockSpec((1, SSD_HEADS), lambda s, c: (0, 0))
    rev = lambda s, c: (s * nc + (nc - 1 - c), 0)
    pshape = S((1, SSD_HEADS), F32)
    return _call(body, name=name,
                 out_shape=(S((m, SSD_CONV_DIM), F32), S((m, 128), BF16), pshape, pshape, pshape), grid=(bsz, nc),
                 in_specs=[pl.BlockSpec((CHUNK, SSD_CONV_DIM), rev),
                           pl.BlockSpec((CHUNK, 128), lambda s, c: (s * nc + (nc - 1 - c), COL_DT // 128)),
                           pl.BlockSpec((srows, SSD_STATE), rev), pl.BlockSpec((CHUNK, D_MODEL), rev), par, par, par],
                 out_specs=(pl.BlockSpec((CHUNK, SSD_CONV_DIM), rev), pl.BlockSpec((CHUNK, 128), rev), par, par, par),
                 scratch=[pltpu.VMEM((srows, SSD_STATE), F32)],
                 sem=("arbitrary", "arbitrary"))(pre, proj, s_in, dy, dtb, alog, dpar)


SB_KEYS = 256


def _order_mats():
    r = lax.broadcasted_iota(jnp.int32, (SB_KEYS, SB_KEYS), 0)
    c = lax.broadcasted_iota(jnp.int32, (SB_KEYS, SB_KEYS), 1)
    return (r > c).astype(BF16), (r < c).astype(BF16)


def _split_dot(x, mat):
    hi = x.astype(BF16)
    lo = (x - hi.astype(F32)).astype(BF16)
    return _dot(hi, mat) + _dot(lo, mat)


def _sb_tiles(qs_, ks_, blocks_, jt, diff, col, m_later):
    zs = [_dot_nt(q_i, k_t) for q_i, k_t in zip(qs_, ks_)]
    out = []
    for z, i in zip(zs, blocks_):
        valid = jnp.logical_and(diff > jt * SB_KEYS - i * SB_BLOCK, col >= PAD - jt * SB_KEYS)
        lk = jnp.where(valid, -_softplus(z), 0.0)
        out.append((valid, z, lk))
    sums = [_split_dot(lk, m_later) for _, _, lk in out]
    return [(valid, z, lk, tsum, tsum[:, 0:1] + lk[:, 0:1]) for (valid, z, lk), tsum in zip(out, sums)]


def _sb_iotas():
    row = lax.broadcasted_iota(jnp.int32, (SB_BLOCK, SB_KEYS), 0)
    col = lax.broadcasted_iota(jnp.int32, (SB_BLOCK, SB_KEYS), 1)
    return row - col, col


def _sb_rows(i, size):
    start = i * size
    return pl.ds(start if isinstance(start, int) else pl.multiple_of(start, size), size)


def _sb_fill(dst, src_ref, ln, lp, scale=None):
    v = src_ref[:, ln]
    dst[0:lp, :] = (v if scale is None else v * scale).astype(BF16)
    if dst.shape[0] > lp:
        dst[lp:, :] = jnp.zeros((dst.shape[0] - lp, dst.shape[1]), BF16)


def _sb_schedule(nb, run_blocks):
    def pair(a, _):
        run_blocks([2 * a, 2 * a + 1], a + 1)
        return 0
    lax.fori_loop(0, nb // 2, pair, 0)
    if nb % 2:
        run_blocks([nb - 1], (nb + 1) // 2)


def attn_fwd(qkvg, lp, *, name):
    m = qkvg.shape[0]
    bsz = m // lp
    nb = lp // SB_BLOCK
    nkt = (nb + 1) // 2
    hd = SB_HEAD_DIM

    def body(q_ref, k_ref, v_ref, o_ref, qs, ks, vs):
        m_later, _ = _order_mats()
        diff, col = _sb_iotas()
        for hh in range(2):
            ln = slice(hh * hd, (hh + 1) * hd)
            _sb_fill(qs.at[hh], q_ref, ln, lp, hd ** -0.5)
            _sb_fill(ks.at[hh], k_ref, ln, lp)
            _sb_fill(vs.at[hh], v_ref, ln, lp)

        def run_blocks(blocks, ntiles):
            rows = [_sb_rows(i, SB_BLOCK) for i in blocks]
            chains = [(b, hh) for b in range(len(blocks)) for hh in range(2)]
            q = [qs[hh, rows[b], :] for b, hh in chains]
            blk_of = [blocks[b] for b, _ in chains]

            def tile_step(jj, carry):
                jt = ntiles - 1 - jj
                cols = _sb_rows(jt, SB_KEYS)
                k_t = [ks[hh, cols, :] for hh in range(2)]
                v_t = [vs[hh, cols, :] for hh in range(2)]
                tiles = _sb_tiles(q, [k_t[hh] for _, hh in chains], blk_of, jt, diff, col, m_later)
                ws = [jnp.where(valid, jnp.exp(z + lk + tsum), 0.0).astype(BF16) for valid, z, lk, tsum, _ in tiles]
                pvs = [_dot(w, v_t[hh]) for w, (_, hh) in zip(ws, chains)]
                return tuple((acc + jnp.exp(run) * pv, run + tile[4])
                             for (acc, run), pv, tile in zip(carry, pvs, tiles))

            zero = (jnp.zeros((SB_BLOCK, hd), F32), jnp.zeros((SB_BLOCK, 1), F32))
            res = lax.fori_loop(0, ntiles, tile_step, (zero,) * len(chains))
            for n, (b, hh) in enumerate(chains):
                o_ref[rows[b], hh * hd:(hh + 1) * hd] = res[n][0]

        _sb_schedule(nb, run_blocks)

    blk = lambda cb: pl.BlockSpec((lp, 128), lambda s, p: (s, cb * 8 + p))
    return _call(body, name=name, out_shape=S((m, D_MODEL), F32), grid=(bsz, 8),
                 in_specs=[blk(0), blk(1), blk(2)], out_specs=pl.BlockSpec((lp, 128), lambda s, p: (s, p)),
                 scratch=[pltpu.VMEM((2, lp, hd), BF16)] + [pltpu.VMEM((2, nkt * SB_KEYS, hd), BF16)] * 2,
                 sem=("parallel", "parallel"))(qkvg, qkvg, qkvg)


def attn_bwd(qkvg, do, lp, *, name):
    m = qkvg.shape[0]
    bsz = m // lp
    nb = lp // SB_BLOCK
    hd = SB_HEAD_DIM
    scale = hd ** -0.5

    nkt = (nb + 1) // 2

    def body(q_ref, k_ref, v_ref, do_ref, dq_ref, dk_ref, dv_ref, qs, ks, vs, dka, dva, g_keep, s_keep):
        m_later, m_earlier = _order_mats()
        diff, col = _sb_iotas()
        for hh in range(2):
            ln = slice(hh * hd, (hh + 1) * hd)
            _sb_fill(qs.at[hh], q_ref, ln, lp, scale)
            _sb_fill(ks.at[hh], k_ref, ln, lp)
            _sb_fill(vs.at[hh], v_ref, ln, lp)
        dka[...] = jnp.zeros_like(dka)
        dva[...] = jnp.zeros_like(dva)

        def run_blocks(blocks, ntiles):
            rows = [_sb_rows(i, SB_BLOCK) for i in blocks]
            chains = [(b, hh) for b in range(len(blocks)) for hh in range(2)]
            q = [qs[hh, rows[b], :] for b, hh in chains]
            do = [do_ref[rows[b], hh * hd:(hh + 1) * hd] for b, hh in chains]

            blk_of = [blocks[b] for b, _ in chains]
            heads = [hh for _, hh in chains]

            def sweep_left(jj, carry):
                jt = ntiles - 1 - jj
                cols = _sb_rows(jt, SB_KEYS)
                k_t = [ks[hh, cols, :] for hh in range(2)]
                v_t = [vs[hh, cols, :] for hh in range(2)]
                tiles = _sb_tiles(q, [k_t[hh] for hh in heads], blk_of, jt, diff, col, m_later)
                do_run = [(d * jnp.exp(run)).astype(BF16) for d, run in zip(do, carry)]
                dws = [_dot_nt(d, v_t[hh]) for d, hh in zip(do_run, heads)]
                ws = []
                for n, ((valid, z, lk, tsum, _), dw) in enumerate(zip(tiles, dws)):
                    sig = jnp.where(valid, jnp.exp(z + lk), 0.0)
                    w = sig * jnp.exp(tsum)
                    g_keep[n, jt] = dw * w
                    s_keep[n, jt] = sig
                    ws.append(w.astype(BF16))
                dvs = [_dot_tn(w, d) for w, d in zip(ws, do_run)]
                for n, hh in enumerate(heads):
                    dva[hh, cols, :] += dvs[n]
                return tuple(run + tile[4] for run, tile in zip(carry, tiles))

            lax.fori_loop(0, ntiles, sweep_left, (jnp.zeros((SB_BLOCK, 1), F32),) * len(chains))

            def sweep_right(jt, carry):
                cols = _sb_rows(jt, SB_KEYS)
                k_t = [ks[hh, cols, :] for hh in range(2)]
                gmats = [g_keep[n, jt] for n in range(len(chains))]
                gsums = [_split_dot(gmat, m_earlier) for gmat in gmats]
                dzs = [(gmat - s_keep[n, jt] * (gmat + gsum + grun)).astype(BF16)
                       for n, (gmat, gsum, (_, grun)) in enumerate(zip(gmats, gsums, carry))]
                dqs = [_dot(dz, k_t[hh]) for dz, hh in zip(dzs, heads)]
                dks = [_dot_tn(dz, q_n) for dz, q_n in zip(dzs, q)]
                for n, hh in enumerate(heads):
                    dka[hh, cols, :] += dks[n]
                last = slice(SB_KEYS - 1, SB_KEYS)
                return tuple((dq + dqn, grun + gsum[:, last] + gmat[:, last])
                             for (dq, grun), dqn, gsum, gmat in zip(carry, dqs, gsums, gmats))

            zero = (jnp.zeros((SB_BLOCK, hd), F32), jnp.zeros((SB_BLOCK, 1), F32))
            res = lax.fori_loop(0, ntiles, sweep_right, (zero,) * len(chains))
            for n, (b, hh) in enumerate(chains):
                dq_ref[rows[b], hh * hd:(hh + 1) * hd] = (res[n][0] * scale).astype(BF16)

        _sb_schedule(nb, run_blocks)
        for hh in range(2):
            dk_ref[:, hh * hd:(hh + 1) * hd] = dka[hh, 0:lp, :].astype(BF16)
            dv_ref[:, hh * hd:(hh + 1) * hd] = dva[hh, 0:lp, :].astype(BF16)

    blk = lambda cb: pl.BlockSpec((lp, 128), lambda s, p: (s, cb * 8 + p))
    one = pl.BlockSpec((lp, 128), lambda s, p: (s, p))
    keys = nkt * SB_KEYS
    return _call(body, name=name, out_shape=(S((m, D_MODEL), BF16),) * 3, grid=(bsz, 8),
                 in_specs=[blk(0), blk(1), blk(2), one], out_specs=(one, one, one),
                 scratch=[pltpu.VMEM((2, lp, hd), BF16)] + [pltpu.VMEM((2, keys, hd), BF16)] * 2
                 + [pltpu.VMEM((2, keys, hd), F32)] * 2 + [pltpu.VMEM((4, nkt, SB_BLOCK, SB_KEYS), F32)] * 2,
                 sem=("parallel", "parallel"))(qkvg, qkvg, qkvg, do)


def meta_grad(dh, lp, *, name):
    m, d = dh.shape
    bsz = m // lp
    per = lp // N_META

    def body(dh_ref, o_ref):
        @pl.when(pl.program_id(0) == 0)
        def _():
            o_ref[...] = jnp.zeros_like(o_ref)
        o_ref[...] += dh_ref[...]

    return _call(body, name=name, out_shape=S((N_META, d), F32), grid=(bsz,),
                 in_specs=[pl.BlockSpec((N_META, d), lambda b: (b * per + PAD // N_META, 0))],
                 out_specs=pl.BlockSpec((N_META, d), lambda b: (0, 0)), sem=("arbitrary",))(dh)


def sum_lead(arr, *, name):
    n, r, c = arr.shape
    tr = _tile(r, (128, 64, 32, 16, 8))

    def body(a_ref, o_ref):
        acc = a_ref[0].astype(F32)
        for k in range(1, n):
            acc = acc + a_ref[k].astype(F32)
        o_ref[...] = acc

    return _call(body, name=name, out_shape=S((r, c), F32), grid=(r // tr,),
                 in_specs=[pl.BlockSpec((n, tr, c), lambda i: (0, i, 0))],
                 out_specs=pl.BlockSpec((tr, c), lambda i: (i, 0)), sem=("parallel",))(arr)


def adamw(w, g_parts, mom, var, *, name):
    r, c = w.shape
    tr = _tile(r, (128, 64, 32, 16, 8))
    n_g = len(g_parts)
    c1 = 1.0 - ADAM_B1 ** ADAM_STEP
    c2 = 1.0 - ADAM_B2 ** ADAM_STEP

    def body(*refs):
        w_ref, g_refs, m_ref, v_ref = refs[0], refs[1:1 + n_g], refs[1 + n_g], refs[2 + n_g]
        g_out, d_out, m_out, v_out = refs[3 + n_g:]
        g = g_refs[0][...]
        for gr in g_refs[1:]:
            g = g + gr[...]
        mn = ADAM_B1 * m_ref[...] + (1.0 - ADAM_B1) * g
        vn = ADAM_B2 * v_ref[...] + (1.0 - ADAM_B2) * (g * g)
        g_out[...] = g
        m_out[...] = mn
        v_out[...] = vn
        d_out[...] = -ADAM_LR * ((mn / c1) / (jnp.sqrt(vn / c2) + ADAM_EPS) + ADAM_WD * w_ref[...])

    blk = pl.BlockSpec((tr, c), lambda i: (i, 0))
    return _call(body, name=name, out_shape=(S((r, c), F32),) * 4, grid=(r // tr,), in_specs=[blk] * (3 + n_g),
                 out_specs=(blk,) * 4, sem=("parallel",))(w, *g_parts, mom, var)


_ANY = pl.BlockSpec(memory_space=pl.ANY)


def _position():
    return lax.axis_index("x"), lax.axis_index("y"), lax.axis_index("c")


def _other_chips(x, y):
    return [(1 - x, y), (x, 1 - y), (1 - x, 1 - y)]


def _comm_call(body, arrs, out_shapes, n_sem, *, name):
    return pl.pallas_call(
        body, out_shape=tuple(out_shapes), in_specs=[_ANY] * len(arrs), out_specs=tuple([_ANY] * len(out_shapes)),
        scratch_shapes=(pltpu.SemaphoreType.DMA((n_sem,)), pltpu.SemaphoreType.DMA((n_sem,)),
                        pltpu.SemaphoreType.DMA((len(arrs),))),
        name=name)(*arrs)


def allgather_chips(arrs, *, name):
    n = len(arrs)

    def body(*refs):
        ins, outs = refs[:n], refs[n:2 * n]
        send_sems, recv_sems, loc_sems = refs[2 * n:]
        x, y, c = _position()
        me = 2 * x + y
        copies = []
        for a in range(n):
            loc = pltpu.make_async_copy(ins[a], outs[a].at[me], loc_sems.at[a])
            loc.start()
            copies.append(loc)
            for k, (px, py) in enumerate(_other_chips(x, y)):
                cp = pltpu.make_async_remote_copy(
                    src_ref=ins[a], dst_ref=outs[a].at[me], send_sem=send_sems.at[3 * a + k],
                    recv_sem=recv_sems.at[3 * a + k], device_id=(px, py, c), device_id_type=MESH)
                cp.start()
                copies.append(cp)
        for cp in copies:
            cp.wait()

    outs = [S((4,) + a.shape, a.dtype) for a in arrs]
    return _comm_call(body, arrs, outs, 3 * n, name=name)


def exchange_chips(arrs, small, *, name):
    n = len(arrs)

    def body(*refs):
        ins, small_in = refs[:n], refs[n]
        outs, small_out = refs[n + 1:2 * n + 1], refs[2 * n + 1]
        send_sems, recv_sems, loc_sems = refs[2 * n + 2:]
        x, y, c = _position()
        me = 2 * x + y
        copies = []
        for a in range(n):
            loc = pltpu.make_async_copy(ins[a].at[me], outs[a].at[me], loc_sems.at[a])
            loc.start()
            copies.append(loc)
            for k, (px, py) in enumerate(_other_chips(x, y)):
                cp = pltpu.make_async_remote_copy(
                    src_ref=ins[a].at[2 * px + py], dst_ref=outs[a].at[me], send_sem=send_sems.at[3 * a + k],
                    recv_sem=recv_sems.at[3 * a + k], device_id=(px, py, c), device_id_type=MESH)
                cp.start()
                copies.append(cp)
        me8 = 4 * x + 2 * y + c
        loc = pltpu.make_async_copy(small_in, small_out.at[me8], loc_sems.at[n])
        loc.start()
        copies.append(loc)
        k = 3 * n
        for fx in (0, 1):
            for fy in (0, 1):
                for fc in (0, 1):
                    if fx + fy + fc == 0:
                        continue
                    peer = (1 - x if fx else x, 1 - y if fy else y, 1 - c if fc else c)
                    cp = pltpu.make_async_remote_copy(
                        src_ref=small_in, dst_ref=small_out.at[me8], send_sem=send_sems.at[k],
                        recv_sem=recv_sems.at[k], device_id=peer, device_id_type=MESH)
                    cp.start()
                    copies.append(cp)
                    k += 1
        for cp in copies:
            cp.wait()

    outs = [S(a.shape, a.dtype) for a in arrs] + [S((8,) + small.shape, small.dtype)]
    return _comm_call(body, list(arrs) + [small], outs, 3 * n + 7, name=name)


def swap_cores(arrs, *, name):
    n = len(arrs)

    def body(*refs):
        ins, outs = refs[:n], refs[n:2 * n]
        send_sems, recv_sems, _ = refs[2 * n:]
        x, y, c = _position()
        copies = []
        for a in range(n):
            cp = pltpu.make_async_remote_copy(
                src_ref=ins[a], dst_ref=outs[a], send_sem=send_sems.at[a], recv_sem=recv_sems.at[a],
                device_id=(x, y, 1 - c), device_id_type=MESH)
            cp.start()
            copies.append(cp)
        for cp in copies:
            cp.wait()

    return _comm_call(body, arrs, [S(a.shape, a.dtype) for a in arrs], n, name=name)


def _local_step(p, x, target):
    bsz, seq, d = x.shape
    lp = LEAD + seq
    m = bsz * lp
    h0 = jnp.concatenate([jnp.zeros((bsz, PAD, d), F32), jnp.broadcast_to(p["meta"][None], (bsz, N_META, d)), x],
                         axis=1).reshape(m, d)
    u0 = rmsnorm_fwd(h0, p["even_norm"], name="norm0")
    proj = mm_nn([u0], p["win_e"], name="proj0")
    lx = conv_fwd(proj, COL_LRU_X, D_MODEL, p["lru_conv_w"], p["lru_conv_b"], name="lru_conv")
    a, b = gates_fwd(lx, p["lru_w_a"], p["lru_b_a"], p["lru_w_x"], p["lru_b_x"], p["lru_lambda"], lp, name="lru_gates")
    hs = scan_fwd(a, b, lp, name="lru_scan")
    ya = gate_fwd(hs, 0, proj, COL_LRU_G // D_MODEL, name="lru_out_gate")
    pre = conv_fwd(proj, COL_XBC, SSD_CONV_DIM, p["ssd_conv_w"], p["ssd_conv_b"], name="ssd_conv")
    y, s_in = ssd_fwd(pre, proj, p["ssd_dt_bias"], p["ssd_a_log"], p["ssd_d"], lp, name="ssd")
    yb = gnorm_fwd(y, proj, COL_Z // D_MODEL, p["ssd_norm"], name="ssd_norm")
    h1 = mm_nn([ya, yb], p["wout_e"], resid=h0, name="out0")
    u1 = rmsnorm_fwd(h1, p["odd_norm"], name="norm1")
    qkvg = mm_nn([u1], p["win_o"], name="proj1")
    o = attn_fwd(qkvg, lp, name="attn")
    og = gate_fwd(o, 0, qkvg, 3, name="attn_gate")
    h2 = mm_nn([og], p["wout_o"], resid=h1, name="out1")
    dh2, dh2b, loss, d_final = loss_head(h2, target, p["final_norm"], lp, name="loss_head")
    g = {"final_norm": d_final}
    g["odd_w_out"] = mm_tn(og, dh2b, name="dw_out1")
    d_og = mm_nn([dh2b], p["wout_o_t"], name="d_out1")
    do, dgate = gate_bwd(d_og, 0, o, 0, qkvg, 3, name="attn_gate_bwd")
    dq, dk, dv = attn_bwd(qkvg, do, lp, name="attn_bwd")
    segs1 = [dq, dk, dv, dgate]
    du1 = mm_nn(segs1, p["win_o_t"], name="d_proj1")
    g["odd_w_in"] = jnp.concatenate([mm_tn(u1, t, name=f"dw_proj1_{k}") for k, t in enumerate(segs1)], axis=1)
    dh1, dh1b, g["odd_norm"] = rmsnorm_bwd(du1, h1, p["odd_norm"], dh2, name="norm1_bwd")
    g["even_w_out"] = jnp.concatenate([mm_tn(ya, dh1b, name="dw_out0_a"), mm_tn(yb, dh1b, name="dw_out0_b")], axis=0)
    d_mixed = mm_nn([dh1b], p["wout_e_t"], name="d_out0")
    dhs, dlg = gate_bwd(d_mixed, 0, hs, 0, proj, COL_LRU_G // D_MODEL, name="lru_out_gate_bwd")
    dy, dz, g["ssd_norm"] = gnorm_bwd(d_mixed, 1, y, proj, COL_Z // D_MODEL, p["ssd_norm"], name="ssd_norm_bwd")
    dpre, ddt, g["ssd_dt_bias"], g["ssd_a_log"], g["ssd_d"] = ssd_bwd(
        pre, proj, s_in, dy, p["ssd_dt_bias"], p["ssd_a_log"], p["ssd_d"], lp, name="ssd_bwd")
    dxbc, g["ssd_conv_w"], g["ssd_conv_b"] = conv_bwd(proj, COL_XBC, SSD_CONV_DIM, dpre, p["ssd_conv_w"],
                                                      name="ssd_conv_bwd")
    da, db = scan_bwd(a, hs, dhs, lp, name="lru_scan_bwd")
    dlx, g["lru_w_a"], g["lru_w_x"], g["lru_b_a"], g["lru_b_x"], g["lru_lambda"] = gates_bwd(
        lx, da, db, p["lru_w_a"], p["lru_b_a"], p["lru_w_x"], p["lru_b_x"], p["lru_lambda"], lp, name="lru_gates_bwd")
    dlrux, g["lru_conv_w"], g["lru_conv_b"] = conv_bwd(proj, COL_LRU_X, D_MODEL, dlx, p["lru_conv_w"],
                                                       name="lru_conv_bwd")
    segs0 = [dlrux, dlg, dz, dxbc, ddt]
    du0 = mm_nn(segs0, p["win_e_t"], name="d_proj0")
    g["even_w_in"] = jnp.concatenate([mm_tn(u0, t, name=f"dw_proj0_{k}") for k, t in enumerate(segs0)],
                                     axis=1)[:, :EVEN_IN]
    dh0, _, g["even_norm"] = rmsnorm_bwd(du0, h0, p["even_norm"], dh1, name="norm0_bwd")
    g["meta"] = meta_grad(dh0, lp, name="meta_grad")
    grad_x = dh0.reshape(bsz, lp, d)[:, LEAD:]
    return loss, grad_x, g


WEIGHTS = ("meta", "even_norm", "even_w_in", "lru_conv_w", "lru_conv_b", "lru_w_a", "lru_b_a", "lru_w_x", "lru_b_x",
           "lru_lambda", "ssd_conv_w", "ssd_conv_b", "ssd_dt_bias", "ssd_a_log", "ssd_d", "ssd_norm", "even_w_out",
           "odd_norm", "odd_w_in", "odd_w_out", "final_norm")
BIG = ("even_w_in", "even_w_out", "odd_w_in", "odd_w_out", "lru_w_a", "lru_w_x")
SHARDED_SMALL = {"meta": 256, "lru_conv_w": 256, "ssd_conv_w": 384, "odd_norm": 256}
SMALL_SHAPES = {"meta": (16, 1024), "even_norm": (1, 1024), "lru_conv_w": (4, 1024), "lru_conv_b": (1, 1024),
                "lru_b_a": (1, 1024), "lru_b_x": (1, 1024), "lru_lambda": (1, 1024), "ssd_conv_w": (4, 1536),
                "ssd_conv_b": (1, 1536), "ssd_dt_bias": (1, 16), "ssd_a_log": (1, 16), "ssd_d": (1, 16),
                "ssd_norm": (1, 1024), "odd_norm": (1, 1024), "final_norm": (1, 1024)}
PACK_UNIT = 1024


def _pack(parts):
    flat = []
    for part in parts:
        v = part.reshape(-1)
        flat.append(jnp.pad(v, (0, -v.shape[0] % PACK_UNIT)))
    return jnp.concatenate(flat).reshape(-1, 128)


def _unpack(buf, shapes):
    v = buf.reshape(-1)
    out, off = [], 0
    for shp in shapes:
        n = 1
        for s_ in shp:
            n *= s_
        out.append(v[off:off + n].reshape(shp))
        off += n + (-n % PACK_UNIT)
    return out


def _chip_cols(a4):
    return jnp.transpose(a4, (1, 0, 2)).reshape(a4.shape[1], -1)


def _to_chip_cols(a, cols):
    return jnp.transpose(a.reshape(a.shape[0], 4, cols), (1, 0, 2))


def kernel(x, meta, even_norm, even_w_in, lru_conv_w, lru_conv_b, lru_w_a, lru_b_a, lru_w_x, lru_b_x, lru_lambda, ssd_conv_w, ssd_conv_b, ssd_dt_bias, ssd_a_log, ssd_d, ssd_norm, even_w_out, odd_norm, odd_w_in, odd_w_out, final_norm, loss_target, m_meta, m_even_norm, m_even_w_in, m_lru_conv_w, m_lru_conv_b, m_lru_w_a, m_lru_b_a, m_lru_w_x, m_lru_b_x, m_lru_lambda, m_ssd_conv_w, m_ssd_conv_b, m_ssd_dt_bias, m_ssd_a_log, m_ssd_d, m_ssd_norm, m_even_w_out, m_odd_norm, m_odd_w_in, m_odd_w_out, m_final_norm, v_meta, v_even_norm, v_even_w_in, v_lru_conv_w, v_lru_conv_b, v_lru_w_a, v_lru_b_a, v_lru_w_x, v_lru_b_x, v_lru_lambda, v_ssd_conv_w, v_ssd_conv_b, v_ssd_dt_bias, v_ssd_a_log, v_ssd_d, v_ssd_norm, v_even_w_out, v_odd_norm, v_odd_w_in, v_odd_w_out, v_final_norm):
    given = dict(locals())
    w = {n: given[n] for n in WEIGHTS}
    mom = {n: given["m_" + n] for n in WEIGHTS}
    var = {n: given["v_" + n] for n in WEIGHTS}
    chip = 2 * lax.axis_index("x") + lax.axis_index("y")

    big_local = {"even_w_in": even_w_in[0], "even_w_out": even_w_out[0], "odd_w_in": odd_w_in[0],
                 "odd_w_out": odd_w_out[0], "lru_w_a": lru_w_a[0].reshape(256, 256),
                 "lru_w_x": lru_w_x[0].reshape(256, 256)}
    sharded_local = [meta, lru_conv_w[0], ssd_conv_w[0], odd_norm]
    gathered = allgather_chips([big_local[n].astype(BF16) for n in BIG] + [_pack(sharded_local)], name="gather_weights")
    gb = dict(zip(BIG, gathered[:-1]))
    per_chip = [_unpack(gathered[-1][k], [a.shape for a in sharded_local]) for k in range(4)]
    full_small = [jnp.concatenate([per_chip[k][j] for k in range(4)], axis=-1) for j in range(len(sharded_local))]

    def lru_full(a4):
        return jnp.transpose(a4.reshape(4, LRU_BLOCKS, 64, LRU_BLOCK), (1, 0, 2, 3)).reshape(LRU_BLOCKS, LRU_BLOCK, LRU_BLOCK)

    p = {"meta": full_small[0], "lru_conv_w": full_small[1], "ssd_conv_w": full_small[2], "odd_norm": full_small[3],
         "even_norm": even_norm, "lru_conv_b": lru_conv_b, "lru_b_a": lru_b_a, "lru_b_x": lru_b_x,
         "lru_lambda": lru_lambda, "ssd_conv_b": ssd_conv_b, "ssd_dt_bias": ssd_dt_bias, "ssd_a_log": ssd_a_log,
         "ssd_d": ssd_d, "ssd_norm": ssd_norm, "final_norm": final_norm.reshape(1, D_MODEL)}
    p["win_e"] = jnp.pad(_chip_cols(gb["even_w_in"]), ((0, 0), (0, EVEN_IN_P - EVEN_IN)))
    p["wout_e"] = gb["even_w_out"].reshape(2 * D_MODEL, D_MODEL)
    p["win_o"] = _chip_cols(gb["odd_w_in"])
    p["wout_o"] = gb["odd_w_out"].reshape(D_MODEL, D_MODEL)
    for n in ("win_e", "wout_e", "win_o", "wout_o"):
        p[n + "_t"] = p[n].T
    p["lru_w_a"] = lru_full(gb["lru_w_a"])
    p["lru_w_x"] = lru_full(gb["lru_w_x"])

    loss_part, grad_x, g = _local_step(p, x, loss_target)

    def lru_slabs(a):
        return jnp.transpose(a.reshape(LRU_BLOCKS, 4, 64, LRU_BLOCK), (1, 0, 2, 3)).reshape(4, 256, LRU_BLOCK)

    slabs = [_to_chip_cols(g["even_w_in"], EVEN_IN // 4), g["even_w_out"].reshape(4, 512, D_MODEL),
             _to_chip_cols(g["odd_w_in"], D_MODEL), g["odd_w_out"].reshape(4, 256, D_MODEL),
             lru_slabs(g["lru_w_a"]), lru_slabs(g["lru_w_x"])]
    small_names = list(SMALL_SHAPES)
    small_part = _pack([loss_part[0:1, 0:1]] + [g[n] for n in small_names])
    *recv, small_all = exchange_chips([t.astype(BF16) for t in slabs], small_part, name="exchange_grads")
    core_sums = [sum_lead(r, name=f"sum_chips_{n}") for n, r in zip(BIG, recv)]
    other_sums = swap_cores(core_sums, name="swap_cores")
    small_sum = sum_lead(small_all, name="sum_small")
    small_g = dict(zip(["loss"] + small_names, _unpack(small_sum, [(1, 1)] + [SMALL_SHAPES[n] for n in small_names])))
    loss = small_g["loss"].reshape(())

    grads, delta, new_m, new_v = {}, {}, {}, {}
    for n, mine, other in zip(BIG, core_sums, other_sums):
        shp = w[n].shape
        two_d = lambda t: t.reshape(mine.shape)
        res = adamw(two_d(w[n]), [mine, other], two_d(mom[n]), two_d(var[n]), name=f"adamw_{n}")
        grads[n], delta[n], new_m[n], new_v[n] = (r.reshape(shp) for r in res)
    local_g = []
    for n in small_names:
        gn = small_g[n]
        if n in SHARDED_SMALL:
            gn = lax.dynamic_slice_in_dim(gn, chip * SHARDED_SMALL[n], SHARDED_SMALL[n], axis=1)
        local_g.append(gn)
    res = adamw(_pack([w[n] for n in small_names]), [_pack(local_g)], _pack([mom[n] for n in small_names]),
                _pack([var[n] for n in small_names]), name="adamw_small")
    shapes = [w[n].shape for n in small_names]
    for out, r in zip((grads, delta, new_m, new_v), res):
        out.update(dict(zip(small_names, _unpack(r, shapes))))
    return (loss, grad_x, *[grads[n] for n in WEIGHTS], *[delta[n] for n in WEIGHTS], *[new_m[n] for n in WEIGHTS],
            *[new_v[n] for n in WEIGHTS])
```

```python
import functools

import jax
import jax.numpy as jnp
from jax import lax
from jax.experimental import pallas as pl
from jax.experimental.pallas import tpu as pltpu

F32 = jnp.float32
BF16 = jnp.bfloat16

D_MODEL = 1024
N_META = 16
LEAD = 128
PAD = LEAD - N_META
EPS = 1e-6
CONV_W = 4
LRU_BLOCKS = 4
LRU_BLOCK = 256
RG_LRU_C = 8.0
SSD_HEADS = 16
SSD_HEAD_DIM = 64
SSD_GROUPS = 2
SSD_HPG = 8
SSD_STATE = 128
CHUNK = 64
SSD_CONV_DIM = 1536
EVEN_IN = 4624
EVEN_IN_P = 4736
COL_LRU_X, COL_LRU_G, COL_Z, COL_XBC, COL_DT = 0, 1024, 2048, 3072, 4608
SB_HEADS = 16
SB_HEAD_DIM = 64
SB_BLOCK = 128
ADAM_LR, ADAM_B1, ADAM_B2, ADAM_EPS, ADAM_WD, ADAM_STEP = 0.001, 0.9, 0.999, 1e-08, 0.01, 10
VMEM_LIMIT_V7X = 56 * 1024 * 1024
MESH = pl.DeviceIdType.MESH
S = jax.ShapeDtypeStruct


def _tile(n, prefs):
    for p in prefs:
        if n % p == 0:
            return p
    raise ValueError(f"no tile of {prefs} divides {n}")


def _call(body, *, name, out_shape, grid=(), in_specs=None, out_specs=None, scratch=(), sem=None):
    kw = {}
    if in_specs is not None:
        kw["in_specs"] = in_specs
    if out_specs is not None:
        kw["out_specs"] = out_specs
    return pl.pallas_call(
        body, out_shape=out_shape, grid=grid, scratch_shapes=tuple(scratch), name=name,
        compiler_params=pltpu.CompilerParams(dimension_semantics=sem, vmem_limit_bytes=VMEM_LIMIT_V7X), **kw)


def _sigmoid(x):
    return 0.5 * (jnp.tanh(0.5 * x) + 1.0)


def _silu(x):
    return x * _sigmoid(x)


def _softplus(x):
    return jnp.maximum(x, 0.0) + jnp.log(1.0 + jnp.exp(-jnp.abs(x)))


def _dot(a, b):
    return jnp.dot(a, b, preferred_element_type=F32)


def _dot_nt(a, b):
    return lax.dot_general(a, b, (((1,), (1,)), ((), ())), preferred_element_type=F32)


def _dot_tn(a, b):
    return lax.dot_general(a, b, (((0,), (0,)), ((), ())), preferred_element_type=F32)


def mm_nn(a_list, w, *, name, resid=None):
    m = a_list[0].shape[0]
    k_tot, n = w.shape
    ks = [a.shape[1] for a in a_list]
    assert sum(ks) == k_tot
    tm = _tile(m, (256, 128))
    n_a = len(a_list)
    offs = [sum(ks[:i]) for i in range(n_a)]
    n_chunks = [(c0, min(512, n - c0)) for c0 in range(0, n, 512)]

    def body(*refs):
        a_refs, w_ref = refs[:n_a], refs[n_a]
        r_ref = refs[n_a + 1] if resid is not None else None
        o_ref = refs[-1]
        for c0, cw in n_chunks:
            acc = None
            for a_ref, k0, k in zip(a_refs, offs, ks):
                p = _dot(a_ref[...], w_ref[k0:k0 + k, c0:c0 + cw])
                acc = p if acc is None else acc + p
            if r_ref is not None:
                acc = acc + r_ref[:, c0:c0 + cw]
            o_ref[:, c0:c0 + cw] = acc

    in_specs = [pl.BlockSpec((tm, k), lambda i: (i, 0)) for k in ks]
    in_specs.append(pl.BlockSpec((k_tot, n), lambda i: (0, 0)))
    args = list(a_list) + [w]
    if resid is not None:
        in_specs.append(pl.BlockSpec((tm, n), lambda i: (i, 0)))
        args.append(resid)
    return _call(body, name=name, out_shape=S((m, n), F32), grid=(m // tm,), in_specs=in_specs,
                 out_specs=pl.BlockSpec((tm, n), lambda i: (i, 0)), sem=("parallel",))(*args)


def mm_tn(a, g, *, name):
    t, m = a.shape
    n = g.shape[1]
    tk = _tile(t, (512, 256, 128))
    tn = _tile(n, (512, 256, 128))

    def body(a_ref, g_ref, o_ref):
        @pl.when(pl.program_id(1) == 0)
        def _():
            o_ref[...] = jnp.zeros_like(o_ref)
        o_ref[...] += _dot_tn(a_ref[...], g_ref[...])

    return _call(body, name=name, out_shape=S((m, n), F32), grid=(n // tn, t // tk),
                 in_specs=[pl.BlockSpec((tk, m), lambda j, k: (k, 0)), pl.BlockSpec((tk, tn), lambda j, k: (k, j))],
                 out_specs=pl.BlockSpec((m, tn), lambda j, k: (0, j)), sem=("parallel", "arbitrary"))(a, g)


def rmsnorm_fwd(h, w, *, name):
    m, d = h.shape
    tm = _tile(m, (512, 256, 128))

    def body(h_ref, w_ref, o_ref):
        x = h_ref[...]
        r = lax.rsqrt(jnp.mean(x * x, axis=-1, keepdims=True) + EPS)
        o_ref[...] = (x * r * w_ref[...]).astype(BF16)

    return _call(body, name=name, out_shape=S((m, d), BF16), grid=(m // tm,),
                 in_specs=[pl.BlockSpec((tm, d), lambda i: (i, 0)), pl.BlockSpec((1, d), lambda i: (0, 0))],
                 out_specs=pl.BlockSpec((tm, d), lambda i: (i, 0)), sem=("parallel",))(h, w)


def rmsnorm_bwd(du, h, w, dres, *, name):
    m, d = h.shape
    tm = _tile(m, (256, 128))

    def body(du_ref, h_ref, w_ref, dr_ref, dh_ref, dhb_ref, dw_ref):
        @pl.when(pl.program_id(0) == 0)
        def _():
            dw_ref[...] = jnp.zeros_like(dw_ref)
        x = h_ref[...]
        r = lax.rsqrt(jnp.mean(x * x, axis=-1, keepdims=True) + EPS)
        du_ = du_ref[...]
        g = du_ * w_ref[...]
        c = jnp.mean(g * x, axis=-1, keepdims=True)
        dh = dr_ref[...] + r * g - x * (r * r * r) * c
        dh_ref[...] = dh
        dhb_ref[...] = dh.astype(BF16)
        dw_ref[...] += jnp.sum(du_ * x * r, axis=0, keepdims=True)

    row = pl.BlockSpec((tm, d), lambda i: (i, 0))
    vec = pl.BlockSpec((1, d), lambda i: (0, 0))
    return _call(body, name=name, out_shape=(S((m, d), F32), S((m, d), BF16), S((1, d), F32)), grid=(m // tm,),
                 in_specs=[row, row, vec, row], out_specs=(row, row, vec), sem=("arbitrary",))(du, h, w, dres)


def gate_fwd(o, o_cb, g, g_cb, *, name, o_scan=False):
    m = g.shape[0]
    d = D_MODEL
    tm = _tile(m, (256, 128))

    def body(o_ref, g_ref, y_ref):
        ov = _from_scan_layout(o_ref, 0, d, tm) if o_scan else o_ref[...]
        y_ref[...] = (ov * _silu(g_ref[...])).astype(BF16)

    o_spec = pl.BlockSpec((tm * 8, 128), lambda i: (i, 0)) if o_scan else pl.BlockSpec((tm, d), lambda i: (i, o_cb))
    return _call(body, name=name, out_shape=S((m, d), BF16), grid=(m // tm,),
                 in_specs=[o_spec, pl.BlockSpec((tm, d), lambda i: (i, g_cb))],
                 out_specs=pl.BlockSpec((tm, d), lambda i: (i, 0)), sem=("parallel",))(o, g)


def gate_bwd(dy, dy_cb, o, o_cb, g, g_cb, *, name, o_scan=False):
    m = g.shape[0]
    d = D_MODEL
    tm = _tile(m, (256, 128))

    def body(dy_ref, o_ref, g_ref, do_ref, dg_ref):
        gv = g_ref[...]
        s = _sigmoid(gv)
        dyv = dy_ref[...]
        do = dyv * gv * s
        if o_scan:
            _to_scan_layout(do_ref, do, 0, tm)
            ov = _from_scan_layout(o_ref, 0, d, tm)
        else:
            do_ref[...] = do
            ov = o_ref[...]
        dg_ref[...] = (dyv * ov * (s + gv * s * (1.0 - s))).astype(BF16)

    nat = pl.BlockSpec((tm, d), lambda i: (i, 0))
    scn = pl.BlockSpec((tm * 8, 128), lambda i: (i, 0))
    o_spec = scn if o_scan else pl.BlockSpec((tm, d), lambda i: (i, o_cb))
    do_shape = S((m * 8, 128), F32) if o_scan else S((m, d), F32)
    return _call(body, name=name, out_shape=(do_shape, S((m, d), BF16)), grid=(m // tm,),
                 in_specs=[pl.BlockSpec((tm, d), lambda i: (i, dy_cb)), o_spec, pl.BlockSpec((tm, d), lambda i: (i, g_cb))],
                 out_specs=(scn if o_scan else nat, nat), sem=("parallel",))(dy, o, g)


def _group_mean(x):
    half = x.shape[1] // SSD_GROUPS
    parts = [jnp.broadcast_to(jnp.mean(x[:, k * half:(k + 1) * half], axis=-1, keepdims=True), (x.shape[0], half))
             for k in range(SSD_GROUPS)]
    return jnp.concatenate(parts, axis=1)


def gnorm_fwd(y, z, z_cb, w, *, name):
    m = y.shape[0]
    d = D_MODEL
    tm = _tile(m, (256, 128))

    def body(y_ref, z_ref, w_ref, o_ref):
        g = y_ref[...] * _silu(z_ref[...])
        r = lax.rsqrt(_group_mean(g * g) + EPS)
        o_ref[...] = (g * r * w_ref[...]).astype(BF16)

    return _call(body, name=name, out_shape=S((m, d), BF16), grid=(m // tm,),
                 in_specs=[pl.BlockSpec((tm, d), lambda i: (i, 0)), pl.BlockSpec((tm, d), lambda i: (i, z_cb)),
                           pl.BlockSpec((1, d), lambda i: (0, 0))],
                 out_specs=pl.BlockSpec((tm, d), lambda i: (i, 0)), sem=("parallel",))(y, z, w)


def gnorm_bwd(do, do_cb, y, z, z_cb, w, *, name):
    m = y.shape[0]
    d = D_MODEL
    tm = _tile(m, (256, 128))

    def body(do_ref, y_ref, z_ref, w_ref, dy_ref, dz_ref, dw_ref):
        @pl.when(pl.program_id(0) == 0)
        def _():
            dw_ref[...] = jnp.zeros_like(dw_ref)
        yv, zv, dov = y_ref[...], z_ref[...], do_ref[...]
        s = _sigmoid(zv)
        sz = zv * s
        g = yv * sz
        r = lax.rsqrt(_group_mean(g * g) + EPS)
        dw_ref[...] += jnp.sum(dov * g * r, axis=0, keepdims=True)
        dn = dov * w_ref[...]
        dg = r * dn - g * (r * r * r) * _group_mean(dn * g)
        dy_ref[...] = dg * sz
        dz_ref[...] = (dg * yv * (s + zv * s * (1.0 - s))).astype(BF16)

    row = pl.BlockSpec((tm, d), lambda i: (i, 0))
    vec = pl.BlockSpec((1, d), lambda i: (0, 0))
    return _call(body, name=name, out_shape=(S((m, d), F32), S((m, d), BF16), S((1, d), F32)), grid=(m // tm,),
                 in_specs=[pl.BlockSpec((tm, d), lambda i: (i, do_cb)), row, pl.BlockSpec((tm, d), lambda i: (i, z_cb)), vec],
                 out_specs=(row, row, vec), sem=("arbitrary",))(do, y, z, w)


def loss_head(h, target, w, lp, *, name):
    m, d = h.shape
    bsz = m // lp
    tm = SB_BLOCK
    nblk = lp // tm
    lead_blk = LEAD // tm

    def body(h_ref, t_ref, w_ref, dh_ref, dhb_ref, l_ref, dw_ref):
        i = pl.program_id(1)

        @pl.when(jnp.logical_and(pl.program_id(0) == 0, i == 0))
        def _():
            l_ref[...] = jnp.zeros_like(l_ref)
            dw_ref[...] = jnp.zeros_like(dw_ref)

        @pl.when(i < lead_blk)
        def _():
            dh_ref[...] = jnp.zeros_like(dh_ref)
            dhb_ref[...] = jnp.zeros_like(dhb_ref)

        @pl.when(i >= lead_blk)
        def _():
            x = h_ref[...]
            r = lax.rsqrt(jnp.mean(x * x, axis=-1, keepdims=True) + EPS)
            wv = w_ref[...]
            e = x * r * wv - t_ref[0]
            l_ref[...] += 0.5 * jnp.sum(jnp.mean(e * e, axis=-1, keepdims=True))
            dy = e * (1.0 / d)
            g = dy * wv
            c = jnp.mean(g * x, axis=-1, keepdims=True)
            dh = r * g - x * (r * r * r) * c
            dh_ref[...] = dh
            dhb_ref[...] = dh.astype(BF16)
            dw_ref[...] += jnp.sum(dy * x * r, axis=0, keepdims=True)

    row = pl.BlockSpec((tm, d), lambda b, i: (b * nblk + i, 0))
    return _call(body, name=name, out_shape=(S((m, d), F32), S((m, d), BF16), S((8, 128), F32), S((1, d), F32)),
                 grid=(bsz, nblk),
                 in_specs=[row, pl.BlockSpec((1, tm, d), lambda b, i: (b, jnp.maximum(i - lead_blk, 0), 0)),
                           pl.BlockSpec((1, d), lambda b, i: (0, 0))],
                 out_specs=(row, row, pl.BlockSpec((8, 128), lambda b, i: (0, 0)),
                            pl.BlockSpec((1, d), lambda b, i: (0, 0))),
                 sem=("arbitrary", "arbitrary"))(h, target, w)


def _conv_tiles(m, c):
    return _tile(m, (256, 128)), _tile(c, (512, 256, 128))


def conv_fwd(x, col0, c, w, b, *, name):
    m = x.shape[0]
    tm, tc = _conv_tiles(m, c)
    assert col0 % tc == 0
    cb0 = col0 // tc
    hb = tm // 8

    def body(x_ref, halo_ref, w_ref, b_ref, o_ref, ext):
        ext[0:8, :] = halo_ref[...]
        ext[8:8 + tm, :] = x_ref[...]
        acc = b_ref[...] + w_ref[3:4, :] * ext[8:8 + tm, :]
        for k in range(CONV_W - 1):
            acc = acc + w_ref[k:k + 1, :] * ext[pl.ds(5 + k, tm), :]
        o_ref[...] = acc

    return _call(body, name=name, out_shape=S((m, c), F32), grid=(m // tm, c // tc),
                 in_specs=[pl.BlockSpec((tm, tc), lambda i, j: (i, cb0 + j)),
                           pl.BlockSpec((8, tc), lambda i, j: (jnp.maximum(i * hb - 1, 0), cb0 + j)),
                           pl.BlockSpec((CONV_W, tc), lambda i, j: (0, j)), pl.BlockSpec((1, tc), lambda i, j: (0, j))],
                 out_specs=pl.BlockSpec((tm, tc), lambda i, j: (i, j)), scratch=[pltpu.VMEM((tm + 8, tc), F32)],
                 sem=("parallel", "parallel"))(x, x, w, b)


def conv_bwd(x, col0, c, dy, w, *, name):
    m = x.shape[0]
    tm, tc = _conv_tiles(m, c)
    cb0 = col0 // tc
    n_i = m // tm
    hb = tm // 8

    def body(x_ref, xh_ref, dy_ref, dyn_ref, w_ref, dx_ref, dw_ref, db_ref, ext, dext):
        i = pl.program_id(1)

        @pl.when(i == 0)
        def _():
            dw_ref[...] = jnp.zeros_like(dw_ref)
            db_ref[...] = jnp.zeros_like(db_ref)

        ext[0:8, :] = xh_ref[...]
        ext[8:8 + tm, :] = x_ref[...]
        d_cur = dy_ref[...]
        dext[0:tm, :] = d_cur
        dext[tm:tm + 8, :] = jnp.where(i == n_i - 1, 0.0, dyn_ref[...])
        dx = w_ref[3:4, :] * d_cur
        for k in range(CONV_W - 1):
            dx = dx + w_ref[k:k + 1, :] * dext[pl.ds(3 - k, tm), :]
        dx_ref[...] = dx.astype(BF16)
        for k in range(CONV_W - 1):
            dw_ref[k:k + 1, :] += jnp.sum(ext[pl.ds(5 + k, tm), :] * d_cur, axis=0, keepdims=True)
        dw_ref[3:4, :] += jnp.sum(ext[8:8 + tm, :] * d_cur, axis=0, keepdims=True)
        db_ref[...] += jnp.sum(d_cur, axis=0, keepdims=True)

    return _call(body, name=name, out_shape=(S((m, c), BF16), S((CONV_W, c), F32), S((1, c), F32)),
                 grid=(c // tc, n_i),
                 in_specs=[pl.BlockSpec((tm, tc), lambda j, i: (i, cb0 + j)),
                           pl.BlockSpec((8, tc), lambda j, i: (jnp.maximum(i * hb - 1, 0), cb0 + j)),
                           pl.BlockSpec((tm, tc), lambda j, i: (i, j)),
                           pl.BlockSpec((8, tc), lambda j, i: (jnp.minimum((i + 1) * hb, n_i * hb - 1), j)),
                           pl.BlockSpec((CONV_W, tc), lambda j, i: (0, j))],
                 out_specs=(pl.BlockSpec((tm, tc), lambda j, i: (i, j)), pl.BlockSpec((CONV_W, tc), lambda j, i: (0, j)),
                            pl.BlockSpec((1, tc), lambda j, i: (0, j))),
                 scratch=[pltpu.VMEM((tm + 8, tc), F32), pltpu.VMEM((tm + 8, tc), F32)],
                 sem=("parallel", "arbitrary"))(x, x, dy, dy, w)


def _row_valid(tile_idx, tiles_per_seq, tm):
    pos = lax.rem(tile_idx, tiles_per_seq) * tm + lax.broadcasted_iota(jnp.int32, (tm, 1), 0)
    return (pos >= PAD).astype(F32)


def _neg_expm1(x):
    small = -(x * (1.0 + x * (0.5 + x * (1.0 / 6.0))))
    return jnp.where(x > -0.01, small, 1.0 - jnp.exp(x))


def _to_scan_layout(ref, val, col0, tm):
    for k in range(val.shape[1] // 128):
        ref[pl.ds(col0 // 128 + k, tm, stride=8), :] = val[:, k * 128:(k + 1) * 128]


def _from_scan_layout(ref, col0, width, tm):
    parts = [ref[pl.ds(col0 // 128 + k, tm, stride=8), :] for k in range(width // 128)]
    return parts[0] if len(parts) == 1 else jnp.concatenate(parts, axis=1)


def _gates_core(lx, wa, ba, wx, bx, lam):
    lxb = lx.astype(BF16)
    r = _sigmoid(_dot(lxb, wa) + ba)
    i = _sigmoid(_dot(lxb, wx) + bx)
    sp = _softplus(-lam)
    log_a = (-RG_LRU_C) * r * sp
    a = jnp.exp(log_a)
    mult = jnp.sqrt(_neg_expm1(2.0 * log_a))
    return lxb, r, i, sp, a, mult


def gates_fwd(lx, rowmask, wa, ba, wx, bx, lam, *, name):
    m = lx.shape[0]
    tm = _tile(m, (256, 128))
    cb = LRU_BLOCK

    def body(lx_ref, msk_ref, wa_ref, ba_ref, wx_ref, bx_ref, lam_ref, a_ref, b_ref):
        msk = msk_ref[...]
        for g in range(LRU_BLOCKS):
            ch = slice(g * cb, (g + 1) * cb)
            lxv = lx_ref[:, ch]
            _, _, i, _, a, mult = _gates_core(lxv, wa_ref[g], ba_ref[:, ch], wx_ref[g], bx_ref[:, ch], lam_ref[:, ch])
            _to_scan_layout(a_ref, a, g * cb, tm)
            _to_scan_layout(b_ref, msk * (mult * i * lxv), g * cb, tm)

    tok = pl.BlockSpec((tm, D_MODEL), lambda i: (i, 0))
    msk = pl.BlockSpec((tm, 1), lambda i: (i, 0))
    wsp = pl.BlockSpec((LRU_BLOCKS, cb, cb), lambda i: (0, 0, 0))
    vec = pl.BlockSpec((1, D_MODEL), lambda i: (0, 0))
    scn = pl.BlockSpec((tm * 8, 128), lambda i: (i, 0))
    return _call(body, name=name, out_shape=(S((m * 8, 128), F32),) * 2, grid=(m // tm,),
                 in_specs=[tok, msk, wsp, vec, wsp, vec, vec], out_specs=(scn, scn),
                 sem=("parallel",))(lx, rowmask, wa, ba, wx, bx, lam)


def gates_bwd(lx, rowmask, da, db, wa, ba, wx, bx, lam, *, name):
    m = lx.shape[0]
    tm = _tile(m, (256, 128))
    cb = LRU_BLOCK

    def body(lx_ref, msk_ref, da_ref, db_ref, wa_ref, ba_ref, wx_ref, bx_ref, lam_ref,
             dlx_ref, dwa_ref, dwx_ref, dba_ref, dbx_ref, dlam_ref):
        @pl.when(pl.program_id(0) == 0)
        def _():
            for ref in (dwa_ref, dwx_ref, dba_ref, dbx_ref, dlam_ref):
                ref[...] = jnp.zeros_like(ref)
        msk = msk_ref[...]
        for g in range(LRU_BLOCKS):
            ch = slice(g * cb, (g + 1) * cb)
            lxv = lx_ref[:, ch]
            lamv = lam_ref[:, ch]
            lxb, r, i, sp, a, mult = _gates_core(lxv, wa_ref[g], ba_ref[:, ch], wx_ref[g], bx_ref[:, ch], lamv)
            dbv = msk * _from_scan_layout(db_ref, g * cb, cb, tm)
            d_mult = dbv * (i * lxv)
            d_i = dbv * (mult * lxv)
            d_log_a = _from_scan_layout(da_ref, g * cb, cb, tm) * a - d_mult * (a * a) / mult
            d_pa = (d_log_a * ((-RG_LRU_C) * sp)) * (r * (1.0 - r))
            d_px = d_i * (i * (1.0 - i))
            d_pa16 = d_pa.astype(BF16)
            d_px16 = d_px.astype(BF16)
            dlx_ref[:, ch] = dbv * (mult * i) + _dot_nt(d_pa16, wa_ref[g]) + _dot_nt(d_px16, wx_ref[g])
            dwa_ref[g] += _dot_tn(lxb, d_pa16)
            dwx_ref[g] += _dot_tn(lxb, d_px16)
            dba_ref[:, ch] += jnp.sum(d_pa, axis=0, keepdims=True)
            dbx_ref[:, ch] += jnp.sum(d_px, axis=0, keepdims=True)
            d_sp = jnp.sum(d_log_a * ((-RG_LRU_C) * r), axis=0, keepdims=True)
            dlam_ref[:, ch] += -d_sp * _sigmoid(-lamv)

    tok = pl.BlockSpec((tm, D_MODEL), lambda i: (i, 0))
    msk = pl.BlockSpec((tm, 1), lambda i: (i, 0))
    scn = pl.BlockSpec((tm * 8, 128), lambda i: (i, 0))
    wsp = pl.BlockSpec((LRU_BLOCKS, cb, cb), lambda i: (0, 0, 0))
    vec = pl.BlockSpec((1, D_MODEL), lambda i: (0, 0))
    wshape = S((LRU_BLOCKS, cb, cb), F32)
    vshape = S((1, D_MODEL), F32)
    return _call(body, name=name, out_shape=(S((m, D_MODEL), F32), wshape, wshape, vshape, vshape, vshape),
                 grid=(m // tm,), in_specs=[tok, msk, scn, scn, wsp, vec, wsp, vec, vec],
                 out_specs=(tok, wsp, wsp, vec, vec, vec),
                 sem=("arbitrary",))(lx, rowmask, da, db, wa, ba, wx, bx, lam)


SCAN_TOK = 128


def scan_fwd(a, b, lp, *, name):
    m = a.shape[0] // 8
    bsz = m // lp
    nch = lp // SCAN_TOK
    rows = SCAN_TOK * 8

    def body(a_ref, b_ref, h_ref, carry):
        @pl.when(pl.program_id(1) == 0)
        def _():
            carry[...] = jnp.zeros_like(carry)

        def step(t, h):
            r = pl.ds(pl.multiple_of(t * 8, 8), 8)
            h = a_ref[r, :] * h + b_ref[r, :]
            h_ref[r, :] = h
            return h

        carry[...] = lax.fori_loop(0, SCAN_TOK, step, carry[...], unroll=8)

    blk = pl.BlockSpec((rows, 128), lambda s, c: (s * nch + c, 0))
    return _call(body, name=name, out_shape=S((m * 8, 128), F32), grid=(bsz, nch), in_specs=[blk, blk], out_specs=blk,
                 scratch=[pltpu.VMEM((8, 128), F32)], sem=("parallel", "arbitrary"))(a, b)


def scan_bwd(a, h, dh, lp, *, name):
    m = a.shape[0] // 8
    bsz = m // lp
    nch = lp // SCAN_TOK
    rows = SCAN_TOK * 8

    def body(a_ref, h_ref, hprev_ref, dh_ref, da_ref, db_ref, carry):
        c = pl.program_id(1)

        @pl.when(c == 0)
        def _():
            carry[...] = jnp.zeros_like(carry)

        h_before = jnp.where(c == nch - 1, 0.0, hprev_ref[...])

        def step(k, ag):
            t = SCAN_TOK - 1 - k
            r = pl.ds(pl.multiple_of(t * 8, 8), 8)
            g = dh_ref[r, :] + ag
            db_ref[r, :] = g
            tp = jnp.maximum(t - 1, 0)
            hp = jnp.where(t == 0, h_before, h_ref[pl.ds(pl.multiple_of(tp * 8, 8), 8), :])
            da_ref[r, :] = g * hp
            return a_ref[r, :] * g

        carry[...] = lax.fori_loop(0, SCAN_TOK, step, carry[...], unroll=8)

    blk = pl.BlockSpec((rows, 128), lambda s, c: (s * nch + (nch - 1 - c), 0))
    prev = pl.BlockSpec((8, 128), lambda s, c: (jnp.maximum((s * nch + (nch - 1 - c)) * SCAN_TOK - 1, 0), 0))
    return _call(body, name=name, out_shape=(S((m * 8, 128), F32),) * 2, grid=(bsz, nch),
                 in_specs=[blk, blk, prev, blk], out_specs=(blk, blk), scratch=[pltpu.VMEM((8, 128), F32)],
                 sem=("parallel", "arbitrary"))(a, h, h, dh)


SSD_SEQS = 2
SSD_SEQS_BWD = 1
SSD_SROWS = SSD_HEADS * SSD_HEAD_DIM


def _ssd_chunk(seqs, dtb, alog, dpar, valid):
    row = lax.broadcasted_iota(jnp.int32, (CHUNK, CHUNK), 0)
    col = lax.broadcasted_iota(jnp.int32, (CHUNK, CHUNK), 1)
    tri = row >= col
    neg_a = -jnp.exp(alog)
    dts, acums, acum_ts, b16s, c16s = [], [], [], [], []
    for xs, bs, cs, dtr, ss in seqs:
        dt = _softplus(dtr + dtb) * valid
        acum = jnp.dot(tri.astype(F32), dt * neg_a, precision=lax.Precision.HIGHEST, preferred_element_type=F32)
        dts.append(dt)
        acums.append(acum)
        acum_ts.append(acum.T)
        b16s.append([(_silu(b) * valid).astype(BF16) for b in bs])
        c16s.append([(_silu(c) * valid).astype(BF16) for c in cs])
    cbs = [[_dot_nt(c, b) for c, b in zip(c16, b16)] for c16, b16 in zip(c16s, b16s)]
    idx = [(q, h) for q in range(len(seqs)) for h in range(SSD_HEADS)]
    grp = [h // SSD_HPG for _, h in idx]
    x = [_silu(seqs[q][0][h]) for q, h in idx]
    s_in = [seqs[q][4][h] for q, h in idx]
    ac = [acums[q][:, h:h + 1] for q, h in idx]
    alast = [acums[q][CHUNK - 1:CHUNK, h:h + 1] for q, h in idx]
    xd = [x[n] * dts[q][:, h:h + 1] for n, (q, h) in enumerate(idx)]
    lhs = [(cbs[q][grp[n]] * jnp.exp(jnp.where(tri, ac[n] - acum_ts[q][h:h + 1, :], -1e30))).astype(BF16)
           for n, (q, h) in enumerate(idx)]
    xd16 = [v.astype(BF16) for v in xd]
    xdec16 = [(xd[n] * jnp.exp(alast[n] - ac[n])).astype(BF16) for n in range(len(idx))]
    s16 = [v.astype(BF16) for v in s_in]
    y_diag = [_dot(lhs[n], xd16[n]) for n in range(len(idx))]
    y_off = [_dot_nt(c16s[q][grp[n]], s16[n]) for n, (q, _) in enumerate(idx)]
    st = [_dot_tn(xdec16[n], b16s[q][grp[n]]) for n, (q, _) in enumerate(idx)]
    ys = [y_diag[n] + y_off[n] * jnp.exp(ac[n]) + x[n] * dpar[:, h:h + 1] for n, (_, h) in enumerate(idx)]
    s_new = [jnp.exp(alast[n]) * s_in[n] + st[n] for n in range(len(idx))]
    return [(ys[q * SSD_HEADS:(q + 1) * SSD_HEADS], s_new[q * SSD_HEADS:(q + 1) * SSD_HEADS])
            for q in range(len(seqs))]


def _ssd_load(pre_ref, dt_ref, s_ref, q):
    xs = [pre_ref[:, h * SSD_HEAD_DIM:(h + 1) * SSD_HEAD_DIM] for h in range(SSD_HEADS)]
    b0 = SSD_HEADS * SSD_HEAD_DIM
    bs = [pre_ref[:, b0 + g * SSD_STATE:b0 + (g + 1) * SSD_STATE] for g in range(SSD_GROUPS)]
    c0 = b0 + SSD_GROUPS * SSD_STATE
    cs = [pre_ref[:, c0 + g * SSD_STATE:c0 + (g + 1) * SSD_STATE] for g in range(SSD_GROUPS)]
    r0 = q * SSD_SROWS
    ss = [s_ref[r0 + h * SSD_HEAD_DIM:r0 + (h + 1) * SSD_HEAD_DIM, :] for h in range(SSD_HEADS)]
    return xs, bs, cs, dt_ref[:, 0:SSD_HEADS], ss


def ssd_fwd(pre, proj, dtb, alog, dpar, lp, *, name):
    m = pre.shape[0]
    bsz = m // lp
    nc = lp // CHUNK
    nq = SSD_SEQS
    assert bsz % nq == 0

    def body(pre_ref, dt_ref, dtb_ref, alog_ref, d_ref, y_ref, sin_ref, state):
        c = pl.program_id(1)

        @pl.when(c == 0)
        def _():
            state[...] = jnp.zeros_like(state)

        for q in range(nq):
            sin_ref[q] = state[q * SSD_SROWS:(q + 1) * SSD_SROWS, :]
        seqs = [_ssd_load(pre_ref.at[q], dt_ref.at[q], state, q) for q in range(nq)]
        valid = _row_valid(c, nc, CHUNK)
        res = _ssd_chunk(seqs, dtb_ref[...], alog_ref[...], d_ref[...], valid)
        for q, (ys, s_new) in enumerate(res):
            for h in range(SSD_HEADS):
                y_ref[q, :, h * SSD_HEAD_DIM:(h + 1) * SSD_HEAD_DIM] = ys[h]
                r0 = q * SSD_SROWS + h * SSD_HEAD_DIM
                state[r0:r0 + SSD_HEAD_DIM, :] = s_new[h]

    par = pl.BlockSpec((1, SSD_HEADS), lambda s, c: (0, 0))
    y, s_in = _call(
        body, name=name, out_shape=(S((bsz, lp, D_MODEL), F32), S((bsz, nc * SSD_SROWS, SSD_STATE), F32)),
        grid=(bsz // nq, nc),
        in_specs=[pl.BlockSpec((nq, CHUNK, SSD_CONV_DIM), lambda s, c: (s, c, 0)),
                  pl.BlockSpec((nq, CHUNK, 128), lambda s, c: (s, c, COL_DT // 128)), par, par, par],
        out_specs=(pl.BlockSpec((nq, CHUNK, D_MODEL), lambda s, c: (s, c, 0)),
                   pl.BlockSpec((nq, SSD_SROWS, SSD_STATE), lambda s, c: (s, c, 0))),
        scratch=[pltpu.VMEM((nq * SSD_SROWS, SSD_STATE), F32)],
        sem=("parallel", "arbitrary"))(pre.reshape(bsz, lp, -1), proj.reshape(bsz, lp, -1), dtb, alog, dpar)
    return y.reshape(m, D_MODEL), s_in


def ssd_bwd(pre, proj, s_in, dy, dtb, alog, dpar, lp, *, name):
    m = pre.shape[0]
    bsz = m // lp
    nc = lp // CHUNK
    nq = SSD_SEQS_BWD

    def body(pre_ref, dt_ref, sin_ref, dy_ref, dtb_ref, alog_ref, d_ref,
             dpre_ref, ddt_ref, ddtb_ref, dalog_ref, dd_ref, dstate):
        c = pl.program_id(1)

        @pl.when(jnp.logical_and(pl.program_id(0) == 0, c == 0))
        def _():
            for r in (ddtb_ref, dalog_ref, dd_ref):
                r[...] = jnp.zeros_like(r)

        @pl.when(c == 0)
        def _():
            dstate[...] = jnp.zeros_like(dstate)

        seqs = [_ssd_load(pre_ref.at[q], dt_ref.at[q], sin_ref.at[q], 0) for q in range(nq)]
        valid = _row_valid(nc - 1 - c, nc, CHUNK)
        core = functools.partial(_ssd_chunk, valid=valid)
        _, vjp = jax.vjp(core, seqs, dtb_ref[...], alog_ref[...], d_ref[...])
        cot = []
        for q in range(nq):
            dys = [dy_ref[q, :, h * SSD_HEAD_DIM:(h + 1) * SSD_HEAD_DIM] for h in range(SSD_HEADS)]
            r0 = q * SSD_SROWS
            dsn = [dstate[r0 + h * SSD_HEAD_DIM:r0 + (h + 1) * SSD_HEAD_DIM, :] for h in range(SSD_HEADS)]
            cot.append((dys, dsn))
        dseqs, ddtb, dalog, dd = vjp(cot)
        b0 = SSD_HEADS * SSD_HEAD_DIM
        c0 = b0 + SSD_GROUPS * SSD_STATE
        ddt_ref[...] = jnp.zeros_like(ddt_ref)
        for q, (dxs, dbs, dcs, ddtr, dss) in enumerate(dseqs):
            for h in range(SSD_HEADS):
                dpre_ref[q, :, h * SSD_HEAD_DIM:(h + 1) * SSD_HEAD_DIM] = dxs[h]
                r0 = q * SSD_SROWS + h * SSD_HEAD_DIM
                dstate[r0:r0 + SSD_HEAD_DIM, :] = dss[h]
            for g in range(SSD_GROUPS):
                dpre_ref[q, :, b0 + g * SSD_STATE:b0 + (g + 1) * SSD_STATE] = dbs[g]
                dpre_ref[q, :, c0 + g * SSD_STATE:c0 + (g + 1) * SSD_STATE] = dcs[g]
            ddt_ref[q, :, 0:SSD_HEADS] = ddtr.astype(BF16)
        ddtb_ref[...] += ddtb
        dalog_ref[...] += dalog
        dd_ref[...] += dd

    par = pl.BlockSpec((1, SSD_HEADS), lambda s, c: (0, 0))
    rev = lambda s, c: (s, nc - 1 - c, 0)
    pshape = S((1, SSD_HEADS), F32)
    dpre, ddt, ddtb, dalog, dd = _call(
        body, name=name,
        out_shape=(S((bsz, lp, SSD_CONV_DIM), F32), S((bsz, lp, 128), BF16), pshape, pshape, pshape),
        grid=(bsz // nq, nc),
        in_specs=[pl.BlockSpec((nq, CHUNK, SSD_CONV_DIM), rev),
                  pl.BlockSpec((nq, CHUNK, 128), lambda s, c: (s, nc - 1 - c, COL_DT // 128)),
                  pl.BlockSpec((nq, SSD_SROWS, SSD_STATE), rev), pl.BlockSpec((nq, CHUNK, D_MODEL), rev), par, par, par],
        out_specs=(pl.BlockSpec((nq, CHUNK, SSD_CONV_DIM), rev), pl.BlockSpec((nq, CHUNK, 128), rev), par, par, par),
        scratch=[pltpu.VMEM((nq * SSD_SROWS, SSD_STATE), F32)],
        sem=("arbitrary", "arbitrary"))(pre.reshape(bsz, lp, -1), proj.reshape(bsz, lp, -1), s_in,
                                        dy.reshape(bsz, lp, -1), dtb, alog, dpar)
    return dpre.reshape(m, SSD_CONV_DIM), ddt.reshape(m, 128), ddtb, dalog, dd


SB_KEYS = 256


def _order_mats():
    r = lax.broadcasted_iota(jnp.int32, (SB_KEYS, SB_KEYS), 0)
    c = lax.broadcasted_iota(jnp.int32, (SB_KEYS, SB_KEYS), 1)
    return (r > c).astype(BF16), (r < c).astype(BF16)


def _split_dot(x, mat):
    hi = x.astype(BF16)
    lo = (x - hi.astype(F32)).astype(BF16)
    return _dot(hi, mat) + _dot(lo, mat)


def _sb_tiles(qs_, ks_, blocks_, jt, diff, col, m_later, masked):
    zs = [_dot_nt(q_i, k_t) for q_i, k_t in zip(qs_, ks_)]
    out = []
    for z, i in zip(zs, blocks_):
        if masked:
            valid = jnp.logical_and(diff > jt * SB_KEYS - i * SB_BLOCK, col >= PAD - jt * SB_KEYS)
            lk = jnp.where(valid, -_softplus(z), 0.0)
        else:
            valid, lk = None, -_softplus(z)
        out.append((valid, z, lk))
    sums = [_split_dot(lk, m_later) for _, _, lk in out]
    return [(valid, z, lk, tsum, tsum[:, 0:1] + lk[:, 0:1]) for (valid, z, lk), tsum in zip(out, sums)]


def _sb_iotas():
    row = lax.broadcasted_iota(jnp.int32, (SB_BLOCK, SB_KEYS), 0)
    col = lax.broadcasted_iota(jnp.int32, (SB_BLOCK, SB_KEYS), 1)
    return row - col, col


def _sb_rows(i, size):
    start = i * size
    return pl.ds(start if isinstance(start, int) else pl.multiple_of(start, size), size)


def _sb_fill(dst, src_ref, ln, lp, scale=None):
    v = src_ref[:, ln]
    dst[0:lp, :] = (v if scale is None else v * scale).astype(BF16)
    if dst.shape[0] > lp:
        dst[lp:, :] = jnp.zeros((dst.shape[0] - lp, dst.shape[1]), BF16)


def _sb_schedule(nb, run_blocks):
    def pair(a, _):
        run_blocks([2 * a, 2 * a + 1], a + 1)
        return 0
    if nb >= 2:
        run_blocks([0, 1], 1)
    lax.fori_loop(1, nb // 2, pair, 0)
    if nb % 2:
        run_blocks([nb - 1], (nb + 1) // 2)


def _sb_sweep(ntiles, step, carry):
    carry = step(ntiles - 1, carry, True)
    if isinstance(ntiles, int) and ntiles == 1:
        return carry
    carry = lax.fori_loop(1, ntiles - 1, lambda jj, c: step(ntiles - 1 - jj, c, False), carry)
    return step(0, carry, True)


def attn_fwd(qkvg, lp, *, name):
    m = qkvg.shape[0]
    bsz = m // lp
    nb = lp // SB_BLOCK
    nkt = (nb + 1) // 2
    hd = SB_HEAD_DIM

    def body(q_ref, k_ref, v_ref, o_ref, qs, ks, vs):
        m_later, _ = _order_mats()
        diff, col = _sb_iotas()
        for hh in range(2):
            ln = slice(hh * hd, (hh + 1) * hd)
            _sb_fill(qs.at[hh], q_ref, ln, lp, hd ** -0.5)
            _sb_fill(ks.at[hh], k_ref, ln, lp)
            _sb_fill(vs.at[hh], v_ref, ln, lp)

        def run_blocks(blocks, ntiles):
            rows = [_sb_rows(i, SB_BLOCK) for i in blocks]
            chains = [(b, hh) for b in range(len(blocks)) for hh in range(2)]
            q = [qs[hh, rows[b], :] for b, hh in chains]
            blk_of = [blocks[b] for b, _ in chains]

            def tile_step(jt, carry, masked):
                cols = _sb_rows(jt, SB_KEYS)
                k_t = [ks[hh, cols, :] for hh in range(2)]
                v_t = [vs[hh, cols, :] for hh in range(2)]
                tiles = _sb_tiles(q, [k_t[hh] for _, hh in chains], blk_of, jt, diff, col, m_later, masked)
                ws = []
                for valid, z, lk, tsum, _ in tiles:
                    w = jnp.exp(z + lk + tsum)
                    ws.append((jnp.where(valid, w, 0.0) if masked else w).astype(BF16))
                pvs = [_dot(w, v_t[hh]) for w, (_, hh) in zip(ws, chains)]
                return tuple((acc + jnp.exp(run) * pv, run + tile[4])
                             for (acc, run), pv, tile in zip(carry, pvs, tiles))

            zero = (jnp.zeros((SB_BLOCK, hd), F32), jnp.zeros((SB_BLOCK, 1), F32))
            res = _sb_sweep(ntiles, tile_step, (zero,) * len(chains))
            for n, (b, hh) in enumerate(chains):
                o_ref[rows[b], hh * hd:(hh + 1) * hd] = res[n][0]

        _sb_schedule(nb, run_blocks)

    blk = lambda cb: pl.BlockSpec((lp, 128), lambda s, p: (s, cb * 8 + p))
    return _call(body, name=name, out_shape=S((m, D_MODEL), F32), grid=(bsz, 8),
                 in_specs=[blk(0), blk(1), blk(2)], out_specs=pl.BlockSpec((lp, 128), lambda s, p: (s, p)),
                 scratch=[pltpu.VMEM((2, lp, hd), BF16)] + [pltpu.VMEM((2, nkt * SB_KEYS, hd), BF16)] * 2,
                 sem=("parallel", "parallel"))(qkvg, qkvg, qkvg)


def attn_bwd(qkvg, do, lp, *, name):
    m = qkvg.shape[0]
    bsz = m // lp
    nb = lp // SB_BLOCK
    hd = SB_HEAD_DIM
    scale = hd ** -0.5

    nkt = (nb + 1) // 2

    def body(q_ref, k_ref, v_ref, do_ref, dq_ref, dk_ref, dv_ref, qs, ks, vs, dka, dva, g_keep, s_keep):
        m_later, m_earlier = _order_mats()
        diff, col = _sb_iotas()
        for hh in range(2):
            ln = slice(hh * hd, (hh + 1) * hd)
            _sb_fill(qs.at[hh], q_ref, ln, lp, scale)
            _sb_fill(ks.at[hh], k_ref, ln, lp)
            _sb_fill(vs.at[hh], v_ref, ln, lp)
        dka[...] = jnp.zeros_like(dka)
        dva[...] = jnp.zeros_like(dva)

        def run_blocks(blocks, ntiles):
            rows = [_sb_rows(i, SB_BLOCK) for i in blocks]
            chains = [(b, hh) for b in range(len(blocks)) for hh in range(2)]
            q = [qs[hh, rows[b], :] for b, hh in chains]
            do = [do_ref[rows[b], hh * hd:(hh + 1) * hd] for b, hh in chains]

            blk_of = [blocks[b] for b, _ in chains]
            heads = [hh for _, hh in chains]

            def sweep_left(jt, carry, masked):
                cols = _sb_rows(jt, SB_KEYS)
                k_t = [ks[hh, cols, :] for hh in range(2)]
                v_t = [vs[hh, cols, :] for hh in range(2)]
                tiles = _sb_tiles(q, [k_t[hh] for hh in heads], blk_of, jt, diff, col, m_later, masked)
                do_run = [(d * jnp.exp(run)).astype(BF16) for d, run in zip(do, carry)]
                dws = [_dot_nt(d, v_t[hh]) for d, hh in zip(do_run, heads)]
                ws = []
                for n, ((valid, z, lk, tsum, _), dw) in enumerate(zip(tiles, dws)):
                    sig = jnp.exp(z + lk)
                    if masked:
                        sig = jnp.where(valid, sig, 0.0)
                    w = sig * jnp.exp(tsum)
                    g_keep[n, jt] = dw * w
                    s_keep[n, jt] = sig
                    ws.append(w.astype(BF16))
                dvs = [_dot_tn(w, d) for w, d in zip(ws, do_run)]
                for n, hh in enumerate(heads):
                    dva[hh, cols, :] += dvs[n]
                return tuple(run + tile[4] for run, tile in zip(carry, tiles))

            _sb_sweep(ntiles, sweep_left, (jnp.zeros((SB_BLOCK, 1), F32),) * len(chains))

            def sweep_right(jt, carry):
                cols = _sb_rows(jt, SB_KEYS)
                k_t = [ks[hh, cols, :] for hh in range(2)]
                gmats = [g_keep[n, jt] for n in range(len(chains))]
                gsums = [_split_dot(gmat, m_earlier) for gmat in gmats]
                dzs = [(gmat - s_keep[n, jt] * (gmat + gsum + grun)).astype(BF16)
                       for n, (gmat, gsum, (_, grun)) in enumerate(zip(gmats, gsums, carry))]
                dqs = [_dot(dz, k_t[hh]) for dz, hh in zip(dzs, heads)]
                dks = [_dot_tn(dz, q_n) for dz, q_n in zip(dzs, q)]
                for n, hh in enumerate(heads):
                    dka[hh, cols, :] += dks[n]
                last = slice(SB_KEYS - 1, SB_KEYS)
                return tuple((dq + dqn, grun + gsum[:, last] + gmat[:, last])
                             for (dq, grun), dqn, gsum, gmat in zip(carry, dqs, gsums, gmats))

            zero = (jnp.zeros((SB_BLOCK, hd), F32), jnp.zeros((SB_BLOCK, 1), F32))
            res = lax.fori_loop(0, ntiles, sweep_right, (zero,) * len(chains))
            for n, (b, hh) in enumerate(chains):
                dq_ref[rows[b], hh * hd:(hh + 1) * hd] = (res[n][0] * scale).astype(BF16)

        _sb_schedule(nb, run_blocks)
        for hh in range(2):
            dk_ref[:, hh * hd:(hh + 1) * hd] = dka[hh, 0:lp, :].astype(BF16)
            dv_ref[:, hh * hd:(hh + 1) * hd] = dva[hh, 0:lp, :].astype(BF16)

    blk = lambda cb: pl.BlockSpec((lp, 128), lambda s, p: (s, cb * 8 + p))
    one = pl.BlockSpec((lp, 128), lambda s, p: (s, p))
    keys = nkt * SB_KEYS
    return _call(body, name=name, out_shape=(S((m, D_MODEL), BF16),) * 3, grid=(bsz, 8),
                 in_specs=[blk(0), blk(1), blk(2), one], out_specs=(one, one, one),
                 scratch=[pltpu.VMEM((2, lp, hd), BF16)] + [pltpu.VMEM((2, keys, hd), BF16)] * 2
                 + [pltpu.VMEM((2, keys, hd), F32)] * 2 + [pltpu.VMEM((4, nkt, SB_BLOCK, SB_KEYS), F32)] * 2,
                 sem=("parallel", "parallel"))(qkvg, qkvg, qkvg, do)


def meta_grad(dh, lp, *, name):
    m, d = dh.shape
    bsz = m // lp
    per = lp // N_META

    def body(dh_ref, o_ref):
        @pl.when(pl.program_id(0) == 0)
        def _():
            o_ref[...] = jnp.zeros_like(o_ref)
        o_ref[...] += dh_ref[...]

    return _call(body, name=name, out_shape=S((N_META, d), F32), grid=(bsz,),
                 in_specs=[pl.BlockSpec((N_META, d), lambda b: (b * per + PAD // N_META, 0))],
                 out_specs=pl.BlockSpec((N_META, d), lambda b: (0, 0)), sem=("arbitrary",))(dh)


def sum_lead(arr, *, name):
    n, r, c = arr.shape
    tr = _tile(r, (128, 64, 32, 16, 8))

    def body(a_ref, o_ref):
        acc = a_ref[0].astype(F32)
        for k in range(1, n):
            acc = acc + a_ref[k].astype(F32)
        o_ref[...] = acc

    return _call(body, name=name, out_shape=S((r, c), F32), grid=(r // tr,),
                 in_specs=[pl.BlockSpec((n, tr, c), lambda i: (0, i, 0))],
                 out_specs=pl.BlockSpec((tr, c), lambda i: (i, 0)), sem=("parallel",))(arr)


def adamw(w, g_parts, mom, var, *, name):
    r, c = w.shape
    tr = _tile(r, (128, 64, 32, 16, 8))
    n_g = len(g_parts)
    c1 = 1.0 - ADAM_B1 ** ADAM_STEP
    c2 = 1.0 - ADAM_B2 ** ADAM_STEP

    def body(*refs):
        w_ref, g_refs, m_ref, v_ref = refs[0], refs[1:1 + n_g], refs[1 + n_g], refs[2 + n_g]
        g_out, d_out, m_out, v_out = refs[3 + n_g:]
        g = g_refs[0][...]
        for gr in g_refs[1:]:
            g = g + gr[...]
        mn = ADAM_B1 * m_ref[...] + (1.0 - ADAM_B1) * g
        vn = ADAM_B2 * v_ref[...] + (1.0 - ADAM_B2) * (g * g)
        g_out[...] = g
        m_out[...] = mn
        v_out[...] = vn
        d_out[...] = -ADAM_LR * ((mn / c1) / (jnp.sqrt(vn / c2) + ADAM_EPS) + ADAM_WD * w_ref[...])

    blk = pl.BlockSpec((tr, c), lambda i: (i, 0))
    return _call(body, name=name, out_shape=(S((r, c), F32),) * 4, grid=(r // tr,), in_specs=[blk] * (3 + n_g),
                 out_specs=(blk,) * 4, sem=("parallel",))(w, *g_parts, mom, var)


_ANY = pl.BlockSpec(memory_space=pl.ANY)


def _position():
    return lax.axis_index("x"), lax.axis_index("y"), lax.axis_index("c")


def _other_chips(x, y):
    return [(1 - x, y), (x, 1 - y), (1 - x, 1 - y)]


def _comm_call(body, arrs, out_shapes, n_sem, *, name):
    return pl.pallas_call(
        body, out_shape=tuple(out_shapes), in_specs=[_ANY] * len(arrs), out_specs=tuple([_ANY] * len(out_shapes)),
        scratch_shapes=(pltpu.SemaphoreType.DMA((n_sem,)), pltpu.SemaphoreType.DMA((n_sem,)),
                        pltpu.SemaphoreType.DMA((len(arrs),))),
        name=name)(*arrs)


def allgather_chips(arrs, *, name):
    n = len(arrs)

    def body(*refs):
        ins, outs = refs[:n], refs[n:2 * n]
        send_sems, recv_sems, loc_sems = refs[2 * n:]
        x, y, c = _position()
        me = 2 * x + y
        copies = []
        for a in range(n):
            loc = pltpu.make_async_copy(ins[a], outs[a].at[me], loc_sems.at[a])
            loc.start()
            copies.append(loc)
            for k, (px, py) in enumerate(_other_chips(x, y)):
                cp = pltpu.make_async_remote_copy(
                    src_ref=ins[a], dst_ref=outs[a].at[me], send_sem=send_sems.at[3 * a + k],
                    recv_sem=recv_sems.at[3 * a + k], device_id=(px, py, c), device_id_type=MESH)
                cp.start()
                copies.append(cp)
        for cp in copies:
            cp.wait()

    outs = [S((4,) + a.shape, a.dtype) for a in arrs]
    return _comm_call(body, arrs, outs, 3 * n, name=name)


def exchange_chips(arrs, small, *, name):
    n = len(arrs)

    def body(*refs):
        ins, small_in = refs[:n], refs[n]
        outs, small_out = refs[n + 1:2 * n + 1], refs[2 * n + 1]
        send_sems, recv_sems, loc_sems = refs[2 * n + 2:]
        x, y, c = _position()
        me = 2 * x + y
        copies = []
        for a in range(n):
            loc = pltpu.make_async_copy(ins[a].at[me], outs[a].at[me], loc_sems.at[a])
            loc.start()
            copies.append(loc)
            for k, (px, py) in enumerate(_other_chips(x, y)):
                cp = pltpu.make_async_remote_copy(
                    src_ref=ins[a].at[2 * px + py], dst_ref=outs[a].at[me], send_sem=send_sems.at[3 * a + k],
                    recv_sem=recv_sems.at[3 * a + k], device_id=(px, py, c), device_id_type=MESH)
                cp.start()
                copies.append(cp)
        me8 = 4 * x + 2 * y + c
        loc = pltpu.make_async_copy(small_in, small_out.at[me8], loc_sems.at[n])
        loc.start()
        copies.append(loc)
        k = 3 * n
        for fx in (0, 1):
            for fy in (0, 1):
                for fc in (0, 1):
                    if fx + fy + fc == 0:
                        continue
                    peer = (1 - x if fx else x, 1 - y if fy else y, 1 - c if fc else c)
                    cp = pltpu.make_async_remote_copy(
                        src_ref=small_in, dst_ref=small_out.at[me8], send_sem=send_sems.at[k],
                        recv_sem=recv_sems.at[k], device_id=peer, device_id_type=MESH)
                    cp.start()
                    copies.append(cp)
                    k += 1
        for cp in copies:
            cp.wait()

    outs = [S(a.shape, a.dtype) for a in arrs] + [S((8,) + small.shape, small.dtype)]
    return _comm_call(body, list(arrs) + [small], outs, 3 * n + 7, name=name)


def swap_cores(arrs, *, name):
    n = len(arrs)

    def body(*refs):
        ins, outs = refs[:n], refs[n:2 * n]
        send_sems, recv_sems, _ = refs[2 * n:]
        x, y, c = _position()
        copies = []
        for a in range(n):
            cp = pltpu.make_async_remote_copy(
                src_ref=ins[a], dst_ref=outs[a], send_sem=send_sems.at[a], recv_sem=recv_sems.at[a],
                device_id=(x, y, 1 - c), device_id_type=MESH)
            cp.start()
            copies.append(cp)
        for cp in copies:
            cp.wait()

    return _comm_call(body, arrs, [S(a.shape, a.dtype) for a in arrs], n, name=name)


def _local_step(p, x, target):
    bsz, seq, d = x.shape
    lp = LEAD + seq
    m = bsz * lp
    h0 = jnp.concatenate([jnp.zeros((bsz, PAD, d), F32), jnp.broadcast_to(p["meta"][None], (bsz, N_META, d)), x],
                         axis=1).reshape(m, d)
    u0 = rmsnorm_fwd(h0, p["even_norm"], name="norm0")
    proj = mm_nn([u0], p["win_e"], name="proj0")
    lx = conv_fwd(proj, COL_LRU_X, D_MODEL, p["lru_conv_w"], p["lru_conv_b"], name="lru_conv")
    rowmask = jnp.tile((jnp.arange(lp) >= PAD).astype(F32), bsz).reshape(m, 1)
    a, b = gates_fwd(lx, rowmask, p["lru_w_a"], p["lru_b_a"], p["lru_w_x"], p["lru_b_x"], p["lru_lambda"],
                     name="lru_gates")
    hs = scan_fwd(a, b, lp, name="lru_scan")
    ya = gate_fwd(hs, 0, proj, COL_LRU_G // D_MODEL, name="lru_out_gate", o_scan=True)
    pre = conv_fwd(proj, COL_XBC, SSD_CONV_DIM, p["ssd_conv_w"], p["ssd_conv_b"], name="ssd_conv")
    y, s_in = ssd_fwd(pre, proj, p["ssd_dt_bias"], p["ssd_a_log"], p["ssd_d"], lp, name="ssd")
    yb = gnorm_fwd(y, proj, COL_Z // D_MODEL, p["ssd_norm"], name="ssd_norm")
    h1 = mm_nn([ya, yb], p["wout_e"], resid=h0, name="out0")
    u1 = rmsnorm_fwd(h1, p["odd_norm"], name="norm1")
    qkvg = mm_nn([u1], p["win_o"], name="proj1")
    o = attn_fwd(qkvg, lp, name="attn")
    og = gate_fwd(o, 0, qkvg, 3, name="attn_gate")
    h2 = mm_nn([og], p["wout_o"], resid=h1, name="out1")
    dh2, dh2b, loss, d_final = loss_head(h2, target, p["final_norm"], lp, name="loss_head")
    g = {"final_norm": d_final}
    g["odd_w_out"] = mm_tn(og, dh2b, name="dw_out1")
    d_og = mm_nn([dh2b], p["wout_o_t"], name="d_out1")
    do, dgate = gate_bwd(d_og, 0, o, 0, qkvg, 3, name="attn_gate_bwd")
    dq, dk, dv = attn_bwd(qkvg, do, lp, name="attn_bwd")
    segs1 = [dq, dk, dv, dgate]
    du1 = mm_nn(segs1, p["win_o_t"], name="d_proj1")
    g["odd_w_in"] = jnp.concatenate([mm_tn(u1, t, name=f"dw_proj1_{k}") for k, t in enumerate(segs1)], axis=1)
    dh1, dh1b, g["odd_norm"] = rmsnorm_bwd(du1, h1, p["odd_norm"], dh2, name="norm1_bwd")
    g["even_w_out"] = jnp.concatenate([mm_tn(ya, dh1b, name="dw_out0_a"), mm_tn(yb, dh1b, name="dw_out0_b")], axis=0)
    d_mixed = mm_nn([dh1b], p["wout_e_t"], name="d_out0")
    dhs, dlg = gate_bwd(d_mixed, 0, hs, 0, proj, COL_LRU_G // D_MODEL, name="lru_out_gate_bwd", o_scan=True)
    dy, dz, g["ssd_norm"] = gnorm_bwd(d_mixed, 1, y, proj, COL_Z // D_MODEL, p["ssd_norm"], name="ssd_norm_bwd")
    dpre, ddt, g["ssd_dt_bias"], g["ssd_a_log"], g["ssd_d"] = ssd_bwd(
        pre, proj, s_in, dy, p["ssd_dt_bias"], p["ssd_a_log"], p["ssd_d"], lp, name="ssd_bwd")
    dxbc, g["ssd_conv_w"], g["ssd_conv_b"] = conv_bwd(proj, COL_XBC, SSD_CONV_DIM, dpre, p["ssd_conv_w"],
                                                      name="ssd_conv_bwd")
    da, db = scan_bwd(a, hs, dhs, lp, name="lru_scan_bwd")
    dlx, g["lru_w_a"], g["lru_w_x"], g["lru_b_a"], g["lru_b_x"], g["lru_lambda"] = gates_bwd(
        lx, rowmask, da, db, p["lru_w_a"], p["lru_b_a"], p["lru_w_x"], p["lru_b_x"], p["lru_lambda"],
        name="lru_gates_bwd")
    dlrux, g["lru_conv_w"], g["lru_conv_b"] = conv_bwd(proj, COL_LRU_X, D_MODEL, dlx, p["lru_conv_w"],
                                                       name="lru_conv_bwd")
    segs0 = [dlrux, dlg, dz, dxbc, ddt]
    du0 = mm_nn(segs0, p["win_e_t"], name="d_proj0")
    g["even_w_in"] = jnp.concatenate([mm_tn(u0, t, name=f"dw_proj0_{k}") for k, t in enumerate(segs0)],
                                     axis=1)[:, :EVEN_IN]
    dh0, _, g["even_norm"] = rmsnorm_bwd(du0, h0, p["even_norm"], dh1, name="norm0_bwd")
    g["meta"] = meta_grad(dh0, lp, name="meta_grad")
    grad_x = dh0.reshape(bsz, lp, d)[:, LEAD:]
    return loss, grad_x, g


WEIGHTS = ("meta", "even_norm", "even_w_in", "lru_conv_w", "lru_conv_b", "lru_w_a", "lru_b_a", "lru_w_x", "lru_b_x",
           "lru_lambda", "ssd_conv_w", "ssd_conv_b", "ssd_dt_bias", "ssd_a_log", "ssd_d", "ssd_norm", "even_w_out",
           "odd_norm", "odd_w_in", "odd_w_out", "final_norm")
BIG = ("even_w_in", "even_w_out", "odd_w_in", "odd_w_out", "lru_w_a", "lru_w_x")
SHARDED_SMALL = {"meta": 256, "lru_conv_w": 256, "ssd_conv_w": 384, "odd_norm": 256}
SMALL_SHAPES = {"meta": (16, 1024), "even_norm": (1, 1024), "lru_conv_w": (4, 1024), "lru_conv_b": (1, 1024),
                "lru_b_a": (1, 1024), "lru_b_x": (1, 1024), "lru_lambda": (1, 1024), "ssd_conv_w": (4, 1536),
                "ssd_conv_b": (1, 1536), "ssd_dt_bias": (1, 16), "ssd_a_log": (1, 16), "ssd_d": (1, 16),
                "ssd_norm": (1, 1024), "odd_norm": (1, 1024), "final_norm": (1, 1024)}
PACK_UNIT = 1024


def _pack(parts):
    flat = []
    for part in parts:
        v = part.reshape(-1)
        flat.append(jnp.pad(v, (0, -v.shape[0] % PACK_UNIT)))
    return jnp.concatenate(flat).reshape(-1, 128)


def _unpack(buf, shapes):
    v = buf.reshape(-1)
    out, off = [], 0
    for shp in shapes:
        n = 1
        for s_ in shp:
            n *= s_
        out.append(v[off:off + n].reshape(shp))
        off += n + (-n % PACK_UNIT)
    return out


def _chip_cols(a4):
    return jnp.transpose(a4, (1, 0, 2)).reshape(a4.shape[1], -1)


def _to_chip_cols(a, cols):
    return jnp.transpose(a.reshape(a.shape[0], 4, cols), (1, 0, 2))


def kernel(x, meta, even_norm, even_w_in, lru_conv_w, lru_conv_b, lru_w_a, lru_b_a, lru_w_x, lru_b_x, lru_lambda, ssd_conv_w, ssd_conv_b, ssd_dt_bias, ssd_a_log, ssd_d, ssd_norm, even_w_out, odd_norm, odd_w_in, odd_w_out, final_norm, loss_target, m_meta, m_even_norm, m_even_w_in, m_lru_conv_w, m_lru_conv_b, m_lru_w_a, m_lru_b_a, m_lru_w_x, m_lru_b_x, m_lru_lambda, m_ssd_conv_w, m_ssd_conv_b, m_ssd_dt_bias, m_ssd_a_log, m_ssd_d, m_ssd_norm, m_even_w_out, m_odd_norm, m_odd_w_in, m_odd_w_out, m_final_norm, v_meta, v_even_norm, v_even_w_in, v_lru_conv_w, v_lru_conv_b, v_lru_w_a, v_lru_b_a, v_lru_w_x, v_lru_b_x, v_lru_lambda, v_ssd_conv_w, v_ssd_conv_b, v_ssd_dt_bias, v_ssd_a_log, v_ssd_d, v_ssd_norm, v_even_w_out, v_odd_norm, v_odd_w_in, v_odd_w_out, v_final_norm):
    given = dict(locals())
    w = {n: given[n] for n in WEIGHTS}
    mom = {n: given["m_" + n] for n in WEIGHTS}
    var = {n: given["v_" + n] for n in WEIGHTS}
    chip = 2 * lax.axis_index("x") + lax.axis_index("y")

    big_local = {"even_w_in": even_w_in[0], "even_w_out": even_w_out[0], "odd_w_in": odd_w_in[0],
                 "odd_w_out": odd_w_out[0], "lru_w_a": lru_w_a[0].reshape(256, 256),
                 "lru_w_x": lru_w_x[0].reshape(256, 256)}
    sharded_local = [meta, lru_conv_w[0], ssd_conv_w[0], odd_norm]
    gathered = allgather_chips([big_local[n].astype(BF16) for n in BIG] + [_pack(sharded_local)], name="gather_weights")
    gb = dict(zip(BIG, gathered[:-1]))
    per_chip = [_unpack(gathered[-1][k], [a.shape for a in sharded_local]) for k in range(4)]
    full_small = [jnp.concatenate([per_chip[k][j] for k in range(4)], axis=-1) for j in range(len(sharded_local))]

    def lru_full(a4):
        return jnp.transpose(a4.reshape(4, LRU_BLOCKS, 64, LRU_BLOCK), (1, 0, 2, 3)).reshape(LRU_BLOCKS, LRU_BLOCK, LRU_BLOCK)

    p = {"meta": full_small[0], "lru_conv_w": full_small[1], "ssd_conv_w": full_small[2], "odd_norm": full_small[3],
         "even_norm": even_norm, "lru_conv_b": lru_conv_b, "lru_b_a": lru_b_a, "lru_b_x": lru_b_x,
         "lru_lambda": lru_lambda, "ssd_conv_b": ssd_conv_b, "ssd_dt_bias": ssd_dt_bias, "ssd_a_log": ssd_a_log,
         "ssd_d": ssd_d, "ssd_norm": ssd_norm, "final_norm": final_norm.reshape(1, D_MODEL)}
    p["win_e"] = jnp.pad(_chip_cols(gb["even_w_in"]), ((0, 0), (0, EVEN_IN_P - EVEN_IN)))
    p["wout_e"] = gb["even_w_out"].reshape(2 * D_MODEL, D_MODEL)
    p["win_o"] = _chip_cols(gb["odd_w_in"])
    p["wout_o"] = gb["odd_w_out"].reshape(D_MODEL, D_MODEL)
    for n in ("win_e", "wout_e", "win_o", "wout_o"):
        p[n + "_t"] = p[n].T
    p["lru_w_a"] = lru_full(gb["lru_w_a"])
    p["lru_w_x"] = lru_full(gb["lru_w_x"])

    loss_part, grad_x, g = _local_step(p, x, loss_target)

    def lru_slabs(a):
        return jnp.transpose(a.reshape(LRU_BLOCKS, 4, 64, LRU_BLOCK), (1, 0, 2, 3)).reshape(4, 256, LRU_BLOCK)

    slabs = [_to_chip_cols(g["even_w_in"], EVEN_IN // 4), g["even_w_out"].reshape(4, 512, D_MODEL),
             _to_chip_cols(g["odd_w_in"], D_MODEL), g["odd_w_out"].reshape(4, 256, D_MODEL),
             lru_slabs(g["lru_w_a"]), lru_slabs(g["lru_w_x"])]
    small_names = list(SMALL_SHAPES)
    small_part = _pack([loss_part[0:1, 0:1]] + [g[n] for n in small_names])
    *recv, small_all = exchange_chips([t.astype(BF16) for t in slabs], small_part, name="exchange_grads")
    core_sums = [sum_lead(r, name=f"sum_chips_{n}") for n, r in zip(BIG, recv)]
    other_sums = swap_cores(core_sums, name="swap_cores")
    small_sum = sum_lead(small_all, name="sum_small")
    small_g = dict(zip(["loss"] + small_names, _unpack(small_sum, [(1, 1)] + [SMALL_SHAPES[n] for n in small_names])))
    loss = small_g["loss"].reshape(())

    grads, delta, new_m, new_v = {}, {}, {}, {}
    for n, mine, other in zip(BIG, core_sums, other_sums):
        shp = w[n].shape
        two_d = lambda t: t.reshape(mine.shape)
        res = adamw(two_d(w[n]), [mine, other], two_d(mom[n]), two_d(var[n]), name=f"adamw_{n}")
        grads[n], delta[n], new_m[n], new_v[n] = (r.reshape(shp) for r in res)
    local_g = []
    for n in small_names:
        gn = small_g[n]
        if n in SHARDED_SMALL:
            gn = lax.dynamic_slice_in_dim(gn, chip * SHARDED_SMALL[n], SHARDED_SMALL[n], axis=1)
        local_g.append(gn)
    res = adamw(_pack([w[n] for n in small_names]), [_pack(local_g)], _pack([mom[n] for n in small_names]),
                _pack([var[n] for n in small_names]), name="adamw_small")
    shapes = [w[n].shape for n in small_names]
    for out, r in zip((grads, delta, new_m, new_v), res):
        out.update(dict(zip(small_names, _unpack(r, shapes))))
    return (loss, grad_x, *[grads[n] for n in WEIGHTS], *[delta[n] for n in WEIGHTS], *[new_m[n] for n in WEIGHTS],
            *[new_v[n] for n in WEIGHTS])
```

```python
import functools

import jax
import jax.numpy as jnp
from jax import lax
from jax.experimental import pallas as pl
from jax.experimental.pallas import tpu as pltpu

F32 = jnp.float32
BF16 = jnp.bfloat16

D_MODEL = 1024
N_META = 16
LEAD = 128
PAD = LEAD - N_META
EPS = 1e-6
CONV_W = 4
LRU_BLOCKS = 4
LRU_BLOCK = 256
RG_LRU_C = 8.0
SSD_HEADS = 16
SSD_HEAD_DIM = 64
SSD_GROUPS = 2
SSD_HPG = 8
SSD_STATE = 128
CHUNK = 64
SSD_CONV_DIM = 1536
EVEN_IN = 4624
EVEN_IN_P = 4736
COL_LRU_X, COL_LRU_G, COL_Z, COL_XBC, COL_DT = 0, 1024, 2048, 3072, 4608
SB_HEADS = 16
SB_HEAD_DIM = 64
SB_BLOCK = 128
ADAM_LR, ADAM_B1, ADAM_B2, ADAM_EPS, ADAM_WD, ADAM_STEP = 0.001, 0.9, 0.999, 1e-08, 0.01, 10
VMEM_LIMIT_V7X = 56 * 1024 * 1024
MESH = pl.DeviceIdType.MESH
S = jax.ShapeDtypeStruct


def _tile(n, prefs):
    for p in prefs:
        if n % p == 0:
            return p
    raise ValueError(f"no tile of {prefs} divides {n}")


def _call(body, *, name, out_shape, grid=(), in_specs=None, out_specs=None, scratch=(), sem=None):
    kw = {}
    if in_specs is not None:
        kw["in_specs"] = in_specs
    if out_specs is not None:
        kw["out_specs"] = out_specs
    return pl.pallas_call(
        body, out_shape=out_shape, grid=grid, scratch_shapes=tuple(scratch), name=name,
        compiler_params=pltpu.CompilerParams(dimension_semantics=sem, vmem_limit_bytes=VMEM_LIMIT_V7X), **kw)


def _sigmoid(x):
    return 0.5 * (jnp.tanh(0.5 * x) + 1.0)


def _silu(x):
    return x * _sigmoid(x)


def _softplus(x):
    return jnp.maximum(x, 0.0) + jnp.log(1.0 + jnp.exp(-jnp.abs(x)))


def _dot(a, b):
    return jnp.dot(a, b, preferred_element_type=F32)


def _dot_nt(a, b):
    return lax.dot_general(a, b, (((1,), (1,)), ((), ())), preferred_element_type=F32)


def _dot_tn(a, b):
    return lax.dot_general(a, b, (((0,), (0,)), ((), ())), preferred_element_type=F32)


def mm_nn(a_list, w, *, name, resid=None):
    m = a_list[0].shape[0]
    k_tot, n = w.shape
    ks = [a.shape[1] for a in a_list]
    assert sum(ks) == k_tot
    tm = _tile(m, (256, 128))
    n_a = len(a_list)
    offs = [sum(ks[:i]) for i in range(n_a)]
    n_chunks = [(c0, min(512, n - c0)) for c0 in range(0, n, 512)]

    def body(*refs):
        a_refs, w_ref = refs[:n_a], refs[n_a]
        r_ref = refs[n_a + 1] if resid is not None else None
        o_ref = refs[-1]
        for c0, cw in n_chunks:
            acc = None
            for a_ref, k0, k in zip(a_refs, offs, ks):
                p = _dot(a_ref[...], w_ref[k0:k0 + k, c0:c0 + cw])
                acc = p if acc is None else acc + p
            if r_ref is not None:
                acc = acc + r_ref[:, c0:c0 + cw]
            o_ref[:, c0:c0 + cw] = acc

    in_specs = [pl.BlockSpec((tm, k), lambda i: (i, 0)) for k in ks]
    in_specs.append(pl.BlockSpec((k_tot, n), lambda i: (0, 0)))
    args = list(a_list) + [w]
    if resid is not None:
        in_specs.append(pl.BlockSpec((tm, n), lambda i: (i, 0)))
        args.append(resid)
    return _call(body, name=name, out_shape=S((m, n), F32), grid=(m // tm,), in_specs=in_specs,
                 out_specs=pl.BlockSpec((tm, n), lambda i: (i, 0)), sem=("parallel",))(*args)


def mm_tn(a, g, *, name):
    t, m = a.shape
    n = g.shape[1]
    tk = _tile(t, (512, 256, 128))
    tn = _tile(n, (1024, 512, 256, 128))

    def body(a_ref, g_ref, o_ref):
        @pl.when(pl.program_id(1) == 0)
        def _():
            o_ref[...] = jnp.zeros_like(o_ref)
        o_ref[...] += _dot_tn(a_ref[...], g_ref[...])

    return _call(body, name=name, out_shape=S((m, n), F32), grid=(n // tn, t // tk),
                 in_specs=[pl.BlockSpec((tk, m), lambda j, k: (k, 0)), pl.BlockSpec((tk, tn), lambda j, k: (k, j))],
                 out_specs=pl.BlockSpec((m, tn), lambda j, k: (0, j)), sem=("parallel", "arbitrary"))(a, g)


def rmsnorm_fwd(h, w, *, name):
    m, d = h.shape
    tm = _tile(m, (512, 256, 128))

    def body(h_ref, w_ref, o_ref):
        x = h_ref[...]
        r = lax.rsqrt(jnp.mean(x * x, axis=-1, keepdims=True) + EPS)
        o_ref[...] = (x * r * w_ref[...]).astype(BF16)

    return _call(body, name=name, out_shape=S((m, d), BF16), grid=(m // tm,),
                 in_specs=[pl.BlockSpec((tm, d), lambda i: (i, 0)), pl.BlockSpec((1, d), lambda i: (0, 0))],
                 out_specs=pl.BlockSpec((tm, d), lambda i: (i, 0)), sem=("parallel",))(h, w)


def rmsnorm_bwd(du, h, w, dres, *, name):
    m, d = h.shape
    tm = _tile(m, (256, 128))

    def body(du_ref, h_ref, w_ref, dr_ref, dh_ref, dhb_ref, dw_ref):
        @pl.when(pl.program_id(0) == 0)
        def _():
            dw_ref[...] = jnp.zeros_like(dw_ref)
        x = h_ref[...]
        r = lax.rsqrt(jnp.mean(x * x, axis=-1, keepdims=True) + EPS)
        du_ = du_ref[...]
        g = du_ * w_ref[...]
        c = jnp.mean(g * x, axis=-1, keepdims=True)
        dh = dr_ref[...] + r * g - x * (r * r * r) * c
        dh_ref[...] = dh
        dhb_ref[...] = dh.astype(BF16)
        dw_ref[...] += jnp.sum(du_ * x * r, axis=0, keepdims=True)

    row = pl.BlockSpec((tm, d), lambda i: (i, 0))
    vec = pl.BlockSpec((1, d), lambda i: (0, 0))
    return _call(body, name=name, out_shape=(S((m, d), F32), S((m, d), BF16), S((1, d), F32)), grid=(m // tm,),
                 in_specs=[row, row, vec, row], out_specs=(row, row, vec), sem=("arbitrary",))(du, h, w, dres)


def gate_fwd(o, o_cb, g, g_cb, *, name, o_scan=False):
    m = g.shape[0]
    d = D_MODEL
    tm = _tile(m, (256, 128))

    def body(o_ref, g_ref, y_ref):
        ov = _from_scan_layout(o_ref, 0, d, tm) if o_scan else o_ref[...]
        y_ref[...] = (ov * _silu(g_ref[...])).astype(BF16)

    o_spec = pl.BlockSpec((tm * 8, 128), lambda i: (i, 0)) if o_scan else pl.BlockSpec((tm, d), lambda i: (i, o_cb))
    return _call(body, name=name, out_shape=S((m, d), BF16), grid=(m // tm,),
                 in_specs=[o_spec, pl.BlockSpec((tm, d), lambda i: (i, g_cb))],
                 out_specs=pl.BlockSpec((tm, d), lambda i: (i, 0)), sem=("parallel",))(o, g)


def gate_bwd(dy, dy_cb, o, o_cb, g, g_cb, *, name, o_scan=False):
    m = g.shape[0]
    d = D_MODEL
    tm = _tile(m, (256, 128))

    def body(dy_ref, o_ref, g_ref, do_ref, dg_ref):
        gv = g_ref[...]
        s = _sigmoid(gv)
        dyv = dy_ref[...]
        do = dyv * gv * s
        if o_scan:
            _to_scan_layout(do_ref, do, 0, tm)
            ov = _from_scan_layout(o_ref, 0, d, tm)
        else:
            do_ref[...] = do
            ov = o_ref[...]
        dg_ref[...] = (dyv * ov * (s + gv * s * (1.0 - s))).astype(BF16)

    nat = pl.BlockSpec((tm, d), lambda i: (i, 0))
    scn = pl.BlockSpec((tm * 8, 128), lambda i: (i, 0))
    o_spec = scn if o_scan else pl.BlockSpec((tm, d), lambda i: (i, o_cb))
    do_shape = S((m * 8, 128), F32) if o_scan else S((m, d), F32)
    return _call(body, name=name, out_shape=(do_shape, S((m, d), BF16)), grid=(m // tm,),
                 in_specs=[pl.BlockSpec((tm, d), lambda i: (i, dy_cb)), o_spec, pl.BlockSpec((tm, d), lambda i: (i, g_cb))],
                 out_specs=(scn if o_scan else nat, nat), sem=("parallel",))(dy, o, g)


def _group_mean(x):
    half = x.shape[1] // SSD_GROUPS
    parts = [jnp.broadcast_to(jnp.mean(x[:, k * half:(k + 1) * half], axis=-1, keepdims=True), (x.shape[0], half))
             for k in range(SSD_GROUPS)]
    return jnp.concatenate(parts, axis=1)


def gnorm_fwd(y, z, z_cb, w, *, name):
    m = y.shape[0]
    d = D_MODEL
    tm = _tile(m, (256, 128))

    def body(y_ref, z_ref, w_ref, o_ref):
        g = y_ref[...] * _silu(z_ref[...])
        r = lax.rsqrt(_group_mean(g * g) + EPS)
        o_ref[...] = (g * r * w_ref[...]).astype(BF16)

    return _call(body, name=name, out_shape=S((m, d), BF16), grid=(m // tm,),
                 in_specs=[pl.BlockSpec((tm, d), lambda i: (i, 0)), pl.BlockSpec((tm, d), lambda i: (i, z_cb)),
                           pl.BlockSpec((1, d), lambda i: (0, 0))],
                 out_specs=pl.BlockSpec((tm, d), lambda i: (i, 0)), sem=("parallel",))(y, z, w)


def gnorm_bwd(do, do_cb, y, z, z_cb, w, *, name):
    m = y.shape[0]
    d = D_MODEL
    tm = _tile(m, (256, 128))

    def body(do_ref, y_ref, z_ref, w_ref, dy_ref, dz_ref, dw_ref):
        @pl.when(pl.program_id(0) == 0)
        def _():
            dw_ref[...] = jnp.zeros_like(dw_ref)
        yv, zv, dov = y_ref[...], z_ref[...], do_ref[...]
        s = _sigmoid(zv)
        sz = zv * s
        g = yv * sz
        r = lax.rsqrt(_group_mean(g * g) + EPS)
        dw_ref[...] += jnp.sum(dov * g * r, axis=0, keepdims=True)
        dn = dov * w_ref[...]
        dg = r * dn - g * (r * r * r) * _group_mean(dn * g)
        dy_ref[...] = dg * sz
        dz_ref[...] = (dg * yv * (s + zv * s * (1.0 - s))).astype(BF16)

    row = pl.BlockSpec((tm, d), lambda i: (i, 0))
    vec = pl.BlockSpec((1, d), lambda i: (0, 0))
    return _call(body, name=name, out_shape=(S((m, d), F32), S((m, d), BF16), S((1, d), F32)), grid=(m // tm,),
                 in_specs=[pl.BlockSpec((tm, d), lambda i: (i, do_cb)), row, pl.BlockSpec((tm, d), lambda i: (i, z_cb)), vec],
                 out_specs=(row, row, vec), sem=("arbitrary",))(do, y, z, w)


def loss_head(h, target, w, lp, *, name):
    m, d = h.shape
    bsz = m // lp
    tm = SB_BLOCK
    nblk = lp // tm
    lead_blk = LEAD // tm

    def body(h_ref, t_ref, w_ref, dh_ref, dhb_ref, l_ref, dw_ref):
        i = pl.program_id(1)

        @pl.when(jnp.logical_and(pl.program_id(0) == 0, i == 0))
        def _():
            l_ref[...] = jnp.zeros_like(l_ref)
            dw_ref[...] = jnp.zeros_like(dw_ref)

        @pl.when(i < lead_blk)
        def _():
            dh_ref[...] = jnp.zeros_like(dh_ref)
            dhb_ref[...] = jnp.zeros_like(dhb_ref)

        @pl.when(i >= lead_blk)
        def _():
            x = h_ref[...]
            r = lax.rsqrt(jnp.mean(x * x, axis=-1, keepdims=True) + EPS)
            wv = w_ref[...]
            e = x * r * wv - t_ref[0]
            l_ref[...] += 0.5 * jnp.sum(jnp.mean(e * e, axis=-1, keepdims=True))
            dy = e * (1.0 / d)
            g = dy * wv
            c = jnp.mean(g * x, axis=-1, keepdims=True)
            dh = r * g - x * (r * r * r) * c
            dh_ref[...] = dh
            dhb_ref[...] = dh.astype(BF16)
            dw_ref[...] += jnp.sum(dy * x * r, axis=0, keepdims=True)

    row = pl.BlockSpec((tm, d), lambda b, i: (b * nblk + i, 0))
    return _call(body, name=name, out_shape=(S((m, d), F32), S((m, d), BF16), S((8, 128), F32), S((1, d), F32)),
                 grid=(bsz, nblk),
                 in_specs=[row, pl.BlockSpec((1, tm, d), lambda b, i: (b, jnp.maximum(i - lead_blk, 0), 0)),
                           pl.BlockSpec((1, d), lambda b, i: (0, 0))],
                 out_specs=(row, row, pl.BlockSpec((8, 128), lambda b, i: (0, 0)),
                            pl.BlockSpec((1, d), lambda b, i: (0, 0))),
                 sem=("arbitrary", "arbitrary"))(h, target, w)


def _conv_tiles(m, c):
    return _tile(m, (256, 128)), _tile(c, (512, 256, 128))


def conv_fwd(x, col0, c, w, b, *, name):
    m = x.shape[0]
    tm, tc = _conv_tiles(m, c)
    assert col0 % tc == 0
    cb0 = col0 // tc
    hb = tm // 8

    def body(x_ref, halo_ref, w_ref, b_ref, o_ref, ext):
        ext[0:8, :] = halo_ref[...]
        ext[8:8 + tm, :] = x_ref[...]
        acc = b_ref[...] + w_ref[3:4, :] * ext[8:8 + tm, :]
        for k in range(CONV_W - 1):
            acc = acc + w_ref[k:k + 1, :] * ext[pl.ds(5 + k, tm), :]
        o_ref[...] = acc

    return _call(body, name=name, out_shape=S((m, c), F32), grid=(m // tm, c // tc),
                 in_specs=[pl.BlockSpec((tm, tc), lambda i, j: (i, cb0 + j)),
                           pl.BlockSpec((8, tc), lambda i, j: (jnp.maximum(i * hb - 1, 0), cb0 + j)),
                           pl.BlockSpec((CONV_W, tc), lambda i, j: (0, j)), pl.BlockSpec((1, tc), lambda i, j: (0, j))],
                 out_specs=pl.BlockSpec((tm, tc), lambda i, j: (i, j)), scratch=[pltpu.VMEM((tm + 8, tc), F32)],
                 sem=("parallel", "parallel"))(x, x, w, b)


def conv_bwd(x, col0, c, dy, w, *, name):
    m = x.shape[0]
    tm, tc = _conv_tiles(m, c)
    cb0 = col0 // tc
    n_i = m // tm
    hb = tm // 8

    def body(x_ref, xh_ref, dy_ref, dyn_ref, w_ref, dx_ref, dw_ref, db_ref, ext, dext):
        i = pl.program_id(1)

        @pl.when(i == 0)
        def _():
            dw_ref[...] = jnp.zeros_like(dw_ref)
            db_ref[...] = jnp.zeros_like(db_ref)

        ext[0:8, :] = xh_ref[...]
        ext[8:8 + tm, :] = x_ref[...]
        d_cur = dy_ref[...]
        dext[0:tm, :] = d_cur
        dext[tm:tm + 8, :] = jnp.where(i == n_i - 1, 0.0, dyn_ref[...])
        dx = w_ref[3:4, :] * d_cur
        for k in range(CONV_W - 1):
            dx = dx + w_ref[k:k + 1, :] * dext[pl.ds(3 - k, tm), :]
        dx_ref[...] = dx.astype(BF16)
        for k in range(CONV_W - 1):
            dw_ref[k:k + 1, :] += jnp.sum(ext[pl.ds(5 + k, tm), :] * d_cur, axis=0, keepdims=True)
        dw_ref[3:4, :] += jnp.sum(ext[8:8 + tm, :] * d_cur, axis=0, keepdims=True)
        db_ref[...] += jnp.sum(d_cur, axis=0, keepdims=True)

    return _call(body, name=name, out_shape=(S((m, c), BF16), S((CONV_W, c), F32), S((1, c), F32)),
                 grid=(c // tc, n_i),
                 in_specs=[pl.BlockSpec((tm, tc), lambda j, i: (i, cb0 + j)),
                           pl.BlockSpec((8, tc), lambda j, i: (jnp.maximum(i * hb - 1, 0), cb0 + j)),
                           pl.BlockSpec((tm, tc), lambda j, i: (i, j)),
                           pl.BlockSpec((8, tc), lambda j, i: (jnp.minimum((i + 1) * hb, n_i * hb - 1), j)),
                           pl.BlockSpec((CONV_W, tc), lambda j, i: (0, j))],
                 out_specs=(pl.BlockSpec((tm, tc), lambda j, i: (i, j)), pl.BlockSpec((CONV_W, tc), lambda j, i: (0, j)),
                            pl.BlockSpec((1, tc), lambda j, i: (0, j))),
                 scratch=[pltpu.VMEM((tm + 8, tc), F32), pltpu.VMEM((tm + 8, tc), F32)],
                 sem=("parallel", "arbitrary"))(x, x, dy, dy, w)


def _row_valid(tile_idx, tiles_per_seq, tm):
    pos = lax.rem(tile_idx, tiles_per_seq) * tm + lax.broadcasted_iota(jnp.int32, (tm, 1), 0)
    return (pos >= PAD).astype(F32)


def _neg_expm1(x):
    small = -(x * (1.0 + x * (0.5 + x * (1.0 / 6.0))))
    return jnp.where(x > -0.01, small, 1.0 - jnp.exp(x))


def _to_scan_layout(ref, val, col0, tm):
    for k in range(val.shape[1] // 128):
        ref[pl.ds(col0 // 128 + k, tm, stride=8), :] = val[:, k * 128:(k + 1) * 128]


def _from_scan_layout(ref, col0, width, tm):
    parts = [ref[pl.ds(col0 // 128 + k, tm, stride=8), :] for k in range(width // 128)]
    return parts[0] if len(parts) == 1 else jnp.concatenate(parts, axis=1)


def _gates_core(lx, wa, ba, wx, bx, lam):
    lxb = lx.astype(BF16)
    r = _sigmoid(_dot(lxb, wa) + ba)
    i = _sigmoid(_dot(lxb, wx) + bx)
    sp = _softplus(-lam)
    log_a = (-RG_LRU_C) * r * sp
    a = jnp.exp(log_a)
    mult = jnp.sqrt(_neg_expm1(2.0 * log_a))
    return lxb, r, i, sp, a, mult


def gates_fwd(lx, rowmask, wa, ba, wx, bx, lam, *, name):
    m = lx.shape[0]
    tm = _tile(m, (256, 128))
    cb = LRU_BLOCK

    def body(lx_ref, msk_ref, wa_ref, ba_ref, wx_ref, bx_ref, lam_ref, a_ref, b_ref):
        msk = msk_ref[...]
        for g in range(LRU_BLOCKS):
            ch = slice(g * cb, (g + 1) * cb)
            lxv = lx_ref[:, ch]
            _, _, i, _, a, mult = _gates_core(lxv, wa_ref[g], ba_ref[:, ch], wx_ref[g], bx_ref[:, ch], lam_ref[:, ch])
            _to_scan_layout(a_ref, a, g * cb, tm)
            _to_scan_layout(b_ref, msk * (mult * i * lxv), g * cb, tm)

    tok = pl.BlockSpec((tm, D_MODEL), lambda i: (i, 0))
    msk = pl.BlockSpec((tm, 1), lambda i: (i, 0))
    wsp = pl.BlockSpec((LRU_BLOCKS, cb, cb), lambda i: (0, 0, 0))
    vec = pl.BlockSpec((1, D_MODEL), lambda i: (0, 0))
    scn = pl.BlockSpec((tm * 8, 128), lambda i: (i, 0))
    return _call(body, name=name, out_shape=(S((m * 8, 128), F32),) * 2, grid=(m // tm,),
                 in_specs=[tok, msk, wsp, vec, wsp, vec, vec], out_specs=(scn, scn),
                 sem=("parallel",))(lx, rowmask, wa, ba, wx, bx, lam)


def gates_bwd(lx, rowmask, da, db, wa, ba, wx, bx, lam, *, name):
    m = lx.shape[0]
    tm = _tile(m, (256, 128))
    cb = LRU_BLOCK

    def body(lx_ref, msk_ref, da_ref, db_ref, wa_ref, ba_ref, wx_ref, bx_ref, lam_ref,
             dlx_ref, dwa_ref, dwx_ref, dba_ref, dbx_ref, dlam_ref):
        @pl.when(pl.program_id(0) == 0)
        def _():
            for ref in (dwa_ref, dwx_ref, dba_ref, dbx_ref, dlam_ref):
                ref[...] = jnp.zeros_like(ref)
        msk = msk_ref[...]
        for g in range(LRU_BLOCKS):
            ch = slice(g * cb, (g + 1) * cb)
            lxv = lx_ref[:, ch]
            lamv = lam_ref[:, ch]
            lxb, r, i, sp, a, mult = _gates_core(lxv, wa_ref[g], ba_ref[:, ch], wx_ref[g], bx_ref[:, ch], lamv)
            dbv = msk * _from_scan_layout(db_ref, g * cb, cb, tm)
            d_mult = dbv * (i * lxv)
            d_i = dbv * (mult * lxv)
            d_log_a = _from_scan_layout(da_ref, g * cb, cb, tm) * a - d_mult * (a * a) / mult
            d_pa = (d_log_a * ((-RG_LRU_C) * sp)) * (r * (1.0 - r))
            d_px = d_i * (i * (1.0 - i))
            d_pa16 = d_pa.astype(BF16)
            d_px16 = d_px.astype(BF16)
            dlx_ref[:, ch] = dbv * (mult * i) + _dot_nt(d_pa16, wa_ref[g]) + _dot_nt(d_px16, wx_ref[g])
            dwa_ref[g] += _dot_tn(lxb, d_pa16)
            dwx_ref[g] += _dot_tn(lxb, d_px16)
            dba_ref[:, ch] += jnp.sum(d_pa, axis=0, keepdims=True)
            dbx_ref[:, ch] += jnp.sum(d_px, axis=0, keepdims=True)
            d_sp = jnp.sum(d_log_a * ((-RG_LRU_C) * r), axis=0, keepdims=True)
            dlam_ref[:, ch] += -d_sp * _sigmoid(-lamv)

    tok = pl.BlockSpec((tm, D_MODEL), lambda i: (i, 0))
    msk = pl.BlockSpec((tm, 1), lambda i: (i, 0))
    scn = pl.BlockSpec((tm * 8, 128), lambda i: (i, 0))
    wsp = pl.BlockSpec((LRU_BLOCKS, cb, cb), lambda i: (0, 0, 0))
    vec = pl.BlockSpec((1, D_MODEL), lambda i: (0, 0))
    wshape = S((LRU_BLOCKS, cb, cb), F32)
    vshape = S((1, D_MODEL), F32)
    return _call(body, name=name, out_shape=(S((m, D_MODEL), F32), wshape, wshape, vshape, vshape, vshape),
                 grid=(m // tm,), in_specs=[tok, msk, scn, scn, wsp, vec, wsp, vec, vec],
                 out_specs=(tok, wsp, wsp, vec, vec, vec),
                 sem=("arbitrary",))(lx, rowmask, da, db, wa, ba, wx, bx, lam)


SCAN_TOK = 128


def scan_fwd(a, b, lp, *, name):
    m = a.shape[0] // 8
    bsz = m // lp
    nch = lp // SCAN_TOK
    rows = SCAN_TOK * 8

    def body(a_ref, b_ref, h_ref, carry):
        @pl.when(pl.program_id(1) == 0)
        def _():
            carry[...] = jnp.zeros_like(carry)

        def step(t, h):
            r = pl.ds(pl.multiple_of(t * 8, 8), 8)
            h = a_ref[r, :] * h + b_ref[r, :]
            h_ref[r, :] = h
            return h

        carry[...] = lax.fori_loop(0, SCAN_TOK, step, carry[...], unroll=8)

    blk = pl.BlockSpec((rows, 128), lambda s, c: (s * nch + c, 0))
    return _call(body, name=name, out_shape=S((m * 8, 128), F32), grid=(bsz, nch), in_specs=[blk, blk], out_specs=blk,
                 scratch=[pltpu.VMEM((8, 128), F32)], sem=("parallel", "arbitrary"))(a, b)


def scan_bwd(a, h, dh, lp, *, name):
    m = a.shape[0] // 8
    bsz = m // lp
    nch = lp // SCAN_TOK
    rows = SCAN_TOK * 8

    def body(a_ref, h_ref, hprev_ref, dh_ref, da_ref, db_ref, carry):
        c = pl.program_id(1)

        @pl.when(c == 0)
        def _():
            carry[...] = jnp.zeros_like(carry)

        h_before = jnp.where(c == nch - 1, 0.0, hprev_ref[...])

        def step(k, ag):
            t = SCAN_TOK - 1 - k
            r = pl.ds(pl.multiple_of(t * 8, 8), 8)
            g = dh_ref[r, :] + ag
            db_ref[r, :] = g
            tp = jnp.maximum(t - 1, 0)
            hp = jnp.where(t == 0, h_before, h_ref[pl.ds(pl.multiple_of(tp * 8, 8), 8), :])
            da_ref[r, :] = g * hp
            return a_ref[r, :] * g

        carry[...] = lax.fori_loop(0, SCAN_TOK, step, carry[...], unroll=8)

    blk = pl.BlockSpec((rows, 128), lambda s, c: (s * nch + (nch - 1 - c), 0))
    prev = pl.BlockSpec((8, 128), lambda s, c: (jnp.maximum((s * nch + (nch - 1 - c)) * SCAN_TOK - 1, 0), 0))
    return _call(body, name=name, out_shape=(S((m * 8, 128), F32),) * 2, grid=(bsz, nch),
                 in_specs=[blk, blk, prev, blk], out_specs=(blk, blk), scratch=[pltpu.VMEM((8, 128), F32)],
                 sem=("parallel", "arbitrary"))(a, h, h, dh)


SSD_SEQS = 2
SSD_SEQS_BWD = 1
SSD_SROWS = SSD_HEADS * SSD_HEAD_DIM


def _ssd_chunk(seqs, dtb, alog, dpar, valid):
    row = lax.broadcasted_iota(jnp.int32, (CHUNK, CHUNK), 0)
    col = lax.broadcasted_iota(jnp.int32, (CHUNK, CHUNK), 1)
    tri = row >= col
    neg_a = -jnp.exp(alog)
    dts, acums, acum_ts, b16s, c16s = [], [], [], [], []
    for xs, bs, cs, dtr, ss in seqs:
        dt = _softplus(dtr + dtb) * valid
        acum = jnp.dot(tri.astype(F32), dt * neg_a, precision=lax.Precision.HIGHEST, preferred_element_type=F32)
        dts.append(dt)
        acums.append(acum)
        acum_ts.append(acum.T)
        b16s.append([(_silu(b) * valid).astype(BF16) for b in bs])
        c16s.append([(_silu(c) * valid).astype(BF16) for c in cs])
    cbs = [[_dot_nt(c, b) for c, b in zip(c16, b16)] for c16, b16 in zip(c16s, b16s)]
    idx = [(q, h) for q in range(len(seqs)) for h in range(SSD_HEADS)]
    grp = [h // SSD_HPG for _, h in idx]
    x = [_silu(seqs[q][0][h]) for q, h in idx]
    s_in = [seqs[q][4][h] for q, h in idx]
    ac = [acums[q][:, h:h + 1] for q, h in idx]
    alast = [acums[q][CHUNK - 1:CHUNK, h:h + 1] for q, h in idx]
    xd = [x[n] * dts[q][:, h:h + 1] for n, (q, h) in enumerate(idx)]
    lhs = [(cbs[q][grp[n]] * jnp.exp(jnp.where(tri, ac[n] - acum_ts[q][h:h + 1, :], -1e30))).astype(BF16)
           for n, (q, h) in enumerate(idx)]
    xd16 = [v.astype(BF16) for v in xd]
    xdec16 = [(xd[n] * jnp.exp(alast[n] - ac[n])).astype(BF16) for n in range(len(idx))]
    s16 = [v.astype(BF16) for v in s_in]
    y_diag = [_dot(lhs[n], xd16[n]) for n in range(len(idx))]
    y_off = [_dot_nt(c16s[q][grp[n]], s16[n]) for n, (q, _) in enumerate(idx)]
    st = [_dot_tn(xdec16[n], b16s[q][grp[n]]) for n, (q, _) in enumerate(idx)]
    ys = [y_diag[n] + y_off[n] * jnp.exp(ac[n]) + x[n] * dpar[:, h:h + 1] for n, (_, h) in enumerate(idx)]
    s_new = [jnp.exp(alast[n]) * s_in[n] + st[n] for n in range(len(idx))]
    return [(ys[q * SSD_HEADS:(q + 1) * SSD_HEADS], s_new[q * SSD_HEADS:(q + 1) * SSD_HEADS])
            for q in range(len(seqs))]


def _ssd_load(pre_ref, dt_ref, s_ref, q):
    xs = [pre_ref[:, h * SSD_HEAD_DIM:(h + 1) * SSD_HEAD_DIM] for h in range(SSD_HEADS)]
    b0 = SSD_HEADS * SSD_HEAD_DIM
    bs = [pre_ref[:, b0 + g * SSD_STATE:b0 + (g + 1) * SSD_STATE] for g in range(SSD_GROUPS)]
    c0 = b0 + SSD_GROUPS * SSD_STATE
    cs = [pre_ref[:, c0 + g * SSD_STATE:c0 + (g + 1) * SSD_STATE] for g in range(SSD_GROUPS)]
    r0 = q * SSD_SROWS
    ss = [s_ref[r0 + h * SSD_HEAD_DIM:r0 + (h + 1) * SSD_HEAD_DIM, :] for h in range(SSD_HEADS)]
    return xs, bs, cs, dt_ref[:, 0:SSD_HEADS], ss


def ssd_fwd(pre, proj, dtb, alog, dpar, lp, *, name):
    m = pre.shape[0]
    bsz = m // lp
    nc = lp // CHUNK
    nq = SSD_SEQS
    assert bsz % nq == 0

    def body(pre_ref, dt_ref, dtb_ref, alog_ref, d_ref, y_ref, sin_ref, state):
        c = pl.program_id(1)

        @pl.when(c == 0)
        def _():
            state[...] = jnp.zeros_like(state)

        for q in range(nq):
            sin_ref[q] = state[q * SSD_SROWS:(q + 1) * SSD_SROWS, :]
        seqs = [_ssd_load(pre_ref.at[q], dt_ref.at[q], state, q) for q in range(nq)]
        valid = _row_valid(c, nc, CHUNK)
        res = _ssd_chunk(seqs, dtb_ref[...], alog_ref[...], d_ref[...], valid)
        for q, (ys, s_new) in enumerate(res):
            for h in range(SSD_HEADS):
                y_ref[q, :, h * SSD_HEAD_DIM:(h + 1) * SSD_HEAD_DIM] = ys[h]
                r0 = q * SSD_SROWS + h * SSD_HEAD_DIM
                state[r0:r0 + SSD_HEAD_DIM, :] = s_new[h]

    par = pl.BlockSpec((1, SSD_HEADS), lambda s, c: (0, 0))
    y, s_in = _call(
        body, name=name, out_shape=(S((bsz, lp, D_MODEL), F32), S((bsz, nc * SSD_SROWS, SSD_STATE), F32)),
        grid=(bsz // nq, nc),
        in_specs=[pl.BlockSpec((nq, CHUNK, SSD_CONV_DIM), lambda s, c: (s, c, 0)),
                  pl.BlockSpec((nq, CHUNK, 128), lambda s, c: (s, c, COL_DT // 128)), par, par, par],
        out_specs=(pl.BlockSpec((nq, CHUNK, D_MODEL), lambda s, c: (s, c, 0)),
                   pl.BlockSpec((nq, SSD_SROWS, SSD_STATE), lambda s, c: (s, c, 0))),
        scratch=[pltpu.VMEM((nq * SSD_SROWS, SSD_STATE), F32)],
        sem=("parallel", "arbitrary"))(pre.reshape(bsz, lp, -1), proj.reshape(bsz, lp, -1), dtb, alog, dpar)
    return y.reshape(m, D_MODEL), s_in


def ssd_bwd(pre, proj, s_in, dy, dtb, alog, dpar, lp, *, name):
    m = pre.shape[0]
    bsz = m // lp
    nc = lp // CHUNK
    nq = SSD_SEQS_BWD

    def body(pre_ref, dt_ref, sin_ref, dy_ref, dtb_ref, alog_ref, d_ref,
             dpre_ref, ddt_ref, ddtb_ref, dalog_ref, dd_ref, dstate):
        c = pl.program_id(1)

        @pl.when(jnp.logical_and(pl.program_id(0) == 0, c == 0))
        def _():
            for r in (ddtb_ref, dalog_ref, dd_ref):
                r[...] = jnp.zeros_like(r)

        @pl.when(c == 0)
        def _():
            dstate[...] = jnp.zeros_like(dstate)

        seqs = [_ssd_load(pre_ref.at[q], dt_ref.at[q], sin_ref.at[q], 0) for q in range(nq)]
        valid = _row_valid(nc - 1 - c, nc, CHUNK)
        core = functools.partial(_ssd_chunk, valid=valid)
        _, vjp = jax.vjp(core, seqs, dtb_ref[...], alog_ref[...], d_ref[...])
        cot = []
        for q in range(nq):
            dys = [dy_ref[q, :, h * SSD_HEAD_DIM:(h + 1) * SSD_HEAD_DIM] for h in range(SSD_HEADS)]
            r0 = q * SSD_SROWS
            dsn = [dstate[r0 + h * SSD_HEAD_DIM:r0 + (h + 1) * SSD_HEAD_DIM, :] for h in range(SSD_HEADS)]
            cot.append((dys, dsn))
        dseqs, ddtb, dalog, dd = vjp(cot)
        b0 = SSD_HEADS * SSD_HEAD_DIM
        c0 = b0 + SSD_GROUPS * SSD_STATE
        ddt_ref[...] = jnp.zeros_like(ddt_ref)
        for q, (dxs, dbs, dcs, ddtr, dss) in enumerate(dseqs):
            for h in range(SSD_HEADS):
                dpre_ref[q, :, h * SSD_HEAD_DIM:(h + 1) * SSD_HEAD_DIM] = dxs[h]
                r0 = q * SSD_SROWS + h * SSD_HEAD_DIM
                dstate[r0:r0 + SSD_HEAD_DIM, :] = dss[h]
            for g in range(SSD_GROUPS):
                dpre_ref[q, :, b0 + g * SSD_STATE:b0 + (g + 1) * SSD_STATE] = dbs[g]
                dpre_ref[q, :, c0 + g * SSD_STATE:c0 + (g + 1) * SSD_STATE] = dcs[g]
            ddt_ref[q, :, 0:SSD_HEADS] = ddtr.astype(BF16)
        ddtb_ref[...] += ddtb
        dalog_ref[...] += dalog
        dd_ref[...] += dd

    par = pl.BlockSpec((1, SSD_HEADS), lambda s, c: (0, 0))
    rev = lambda s, c: (s, nc - 1 - c, 0)
    pshape = S((1, SSD_HEADS), F32)
    dpre, ddt, ddtb, dalog, dd = _call(
        body, name=name,
        out_shape=(S((bsz, lp, SSD_CONV_DIM), F32), S((bsz, lp, 128), BF16), pshape, pshape, pshape),
        grid=(bsz // nq, nc),
        in_specs=[pl.BlockSpec((nq, CHUNK, SSD_CONV_DIM), rev),
                  pl.BlockSpec((nq, CHUNK, 128), lambda s, c: (s, nc - 1 - c, COL_DT // 128)),
                  pl.BlockSpec((nq, SSD_SROWS, SSD_STATE), rev), pl.BlockSpec((nq, CHUNK, D_MODEL), rev), par, par, par],
        out_specs=(pl.BlockSpec((nq, CHUNK, SSD_CONV_DIM), rev), pl.BlockSpec((nq, CHUNK, 128), rev), par, par, par),
        scratch=[pltpu.VMEM((nq * SSD_SROWS, SSD_STATE), F32)],
        sem=("arbitrary", "arbitrary"))(pre.reshape(bsz, lp, -1), proj.reshape(bsz, lp, -1), s_in,
                                        dy.reshape(bsz, lp, -1), dtb, alog, dpar)
    return dpre.reshape(m, SSD_CONV_DIM), ddt.reshape(m, 128), ddtb, dalog, dd


SB_KEYS = 256


def _order_mats():
    r = lax.broadcasted_iota(jnp.int32, (SB_KEYS, SB_KEYS), 0)
    c = lax.broadcasted_iota(jnp.int32, (SB_KEYS, SB_KEYS), 1)
    return (r > c).astype(BF16), (r < c).astype(BF16)


def _split_dot(x, mat):
    hi = x.astype(BF16)
    lo = (x - hi.astype(F32)).astype(BF16)
    return _dot(hi, mat) + _dot(lo, mat)


def _sb_tiles(qs_, ks_, blocks_, jt, diff, col, m_later, masked):
    zs = [_dot_nt(q_i, k_t) for q_i, k_t in zip(qs_, ks_)]
    out = []
    for z, i in zip(zs, blocks_):
        if masked:
            valid = jnp.logical_and(diff > jt * SB_KEYS - i * SB_BLOCK, col >= PAD - jt * SB_KEYS)
            lk = jnp.where(valid, -_softplus(z), 0.0)
        else:
            valid, lk = None, -_softplus(z)
        out.append((valid, z, lk))
    sums = [_split_dot(lk, m_later) for _, _, lk in out]
    return [(valid, z, lk, tsum, tsum[:, 0:1] + lk[:, 0:1]) for (valid, z, lk), tsum in zip(out, sums)]


def _sb_iotas():
    row = lax.broadcasted_iota(jnp.int32, (SB_BLOCK, SB_KEYS), 0)
    col = lax.broadcasted_iota(jnp.int32, (SB_BLOCK, SB_KEYS), 1)
    return row - col, col


def _sb_rows(i, size):
    start = i * size
    return pl.ds(start if isinstance(start, int) else pl.multiple_of(start, size), size)


def _sb_fill(dst, src_ref, ln, lp, scale=None):
    v = src_ref[:, ln]
    dst[0:lp, :] = (v if scale is None else v * scale).astype(BF16)
    if dst.shape[0] > lp:
        dst[lp:, :] = jnp.zeros((dst.shape[0] - lp, dst.shape[1]), BF16)


def _sb_schedule(nb, run_blocks):
    def pair(a, _):
        run_blocks([2 * a, 2 * a + 1], a + 1)
        return 0
    if nb >= 2:
        run_blocks([0, 1], 1)
    lax.fori_loop(1, nb // 2, pair, 0)
    if nb % 2:
        run_blocks([nb - 1], (nb + 1) // 2)


def _sb_sweep(ntiles, step, carry):
    carry = step(ntiles - 1, carry, True)
    if isinstance(ntiles, int) and ntiles == 1:
        return carry
    carry = lax.fori_loop(1, ntiles - 1, lambda jj, c: step(ntiles - 1 - jj, c, False), carry)
    return step(0, carry, True)


def attn_fwd(qkvg, lp, *, name):
    m = qkvg.shape[0]
    bsz = m // lp
    nb = lp // SB_BLOCK
    nkt = (nb + 1) // 2
    hd = SB_HEAD_DIM

    def body(q_ref, k_ref, v_ref, o_ref, qs, ks, vs):
        m_later, _ = _order_mats()
        diff, col = _sb_iotas()
        for hh in range(2):
            ln = slice(hh * hd, (hh + 1) * hd)
            _sb_fill(qs.at[hh], q_ref, ln, lp, hd ** -0.5)
            _sb_fill(ks.at[hh], k_ref, ln, lp)
            _sb_fill(vs.at[hh], v_ref, ln, lp)

        def run_blocks(blocks, ntiles):
            rows = [_sb_rows(i, SB_BLOCK) for i in blocks]
            chains = [(b, hh) for b in range(len(blocks)) for hh in range(2)]
            q = [qs[hh, rows[b], :] for b, hh in chains]
            blk_of = [blocks[b] for b, _ in chains]

            def tile_step(jt, carry, masked):
                cols = _sb_rows(jt, SB_KEYS)
                k_t = [ks[hh, cols, :] for hh in range(2)]
                v_t = [vs[hh, cols, :] for hh in range(2)]
                tiles = _sb_tiles(q, [k_t[hh] for _, hh in chains], blk_of, jt, diff, col, m_later, masked)
                ws = []
                for valid, z, lk, tsum, _ in tiles:
                    w = jnp.exp(z + lk + tsum)
                    ws.append((jnp.where(valid, w, 0.0) if masked else w).astype(BF16))
                pvs = [_dot(w, v_t[hh]) for w, (_, hh) in zip(ws, chains)]
                return tuple((acc + jnp.exp(run) * pv, run + tile[4])
                             for (acc, run), pv, tile in zip(carry, pvs, tiles))

            zero = (jnp.zeros((SB_BLOCK, hd), F32), jnp.zeros((SB_BLOCK, 1), F32))
            res = _sb_sweep(ntiles, tile_step, (zero,) * len(chains))
            for n, (b, hh) in enumerate(chains):
                o_ref[rows[b], hh * hd:(hh + 1) * hd] = res[n][0]

        _sb_schedule(nb, run_blocks)

    blk = lambda cb: pl.BlockSpec((lp, 128), lambda s, p: (s, cb * 8 + p))
    return _call(body, name=name, out_shape=S((m, D_MODEL), F32), grid=(bsz, 8),
                 in_specs=[blk(0), blk(1), blk(2)], out_specs=pl.BlockSpec((lp, 128), lambda s, p: (s, p)),
                 scratch=[pltpu.VMEM((2, lp, hd), BF16)] + [pltpu.VMEM((2, nkt * SB_KEYS, hd), BF16)] * 2,
                 sem=("parallel", "parallel"))(qkvg, qkvg, qkvg)


def attn_bwd(qkvg, do, lp, *, name):
    m = qkvg.shape[0]
    bsz = m // lp
    nb = lp // SB_BLOCK
    hd = SB_HEAD_DIM
    scale = hd ** -0.5

    nkt = (nb + 1) // 2

    def body(q_ref, k_ref, v_ref, do_ref, dq_ref, dk_ref, dv_ref, qs, ks, vs, dka, dva, g_keep, s_keep):
        m_later, m_earlier = _order_mats()
        diff, col = _sb_iotas()
        for hh in range(2):
            ln = slice(hh * hd, (hh + 1) * hd)
            _sb_fill(qs.at[hh], q_ref, ln, lp, scale)
            _sb_fill(ks.at[hh], k_ref, ln, lp)
            _sb_fill(vs.at[hh], v_ref, ln, lp)
        dka[...] = jnp.zeros_like(dka)
        dva[...] = jnp.zeros_like(dva)

        def run_blocks(blocks, ntiles):
            rows = [_sb_rows(i, SB_BLOCK) for i in blocks]
            chains = [(b, hh) for b in range(len(blocks)) for hh in range(2)]
            q = [qs[hh, rows[b], :] for b, hh in chains]
            do = [do_ref[rows[b], hh * hd:(hh + 1) * hd] for b, hh in chains]

            blk_of = [blocks[b] for b, _ in chains]
            heads = [hh for _, hh in chains]

            def sweep_left(jt, carry, masked):
                cols = _sb_rows(jt, SB_KEYS)
                k_t = [ks[hh, cols, :] for hh in range(2)]
                v_t = [vs[hh, cols, :] for hh in range(2)]
                tiles = _sb_tiles(q, [k_t[hh] for hh in heads], blk_of, jt, diff, col, m_later, masked)
                do_run = [(d * jnp.exp(run)).astype(BF16) for d, run in zip(do, carry)]
                dws = [_dot_nt(d, v_t[hh]) for d, hh in zip(do_run, heads)]
                ws = []
                for n, ((valid, z, lk, tsum, _), dw) in enumerate(zip(tiles, dws)):
                    sig = jnp.exp(z + lk)
                    if masked:
                        sig = jnp.where(valid, sig, 0.0)
                    w = sig * jnp.exp(tsum)
                    g_keep[n, jt] = dw * w
                    s_keep[n, jt] = sig
                    ws.append(w.astype(BF16))
                dvs = [_dot_tn(w, d) for w, d in zip(ws, do_run)]
                for n, hh in enumerate(heads):
                    dva[hh, cols, :] += dvs[n]
                return tuple(run + tile[4] for run, tile in zip(carry, tiles))

            _sb_sweep(ntiles, sweep_left, (jnp.zeros((SB_BLOCK, 1), F32),) * len(chains))

            def sweep_right(jt, carry):
                cols = _sb_rows(jt, SB_KEYS)
                k_t = [ks[hh, cols, :] for hh in range(2)]
                gmats = [g_keep[n, jt] for n in range(len(chains))]
                gsums = [_split_dot(gmat, m_earlier) for gmat in gmats]
                dzs = [(gmat - s_keep[n, jt] * (gmat + gsum + grun)).astype(BF16)
                       for n, (gmat, gsum, (_, grun)) in enumerate(zip(gmats, gsums, carry))]
                dqs = [_dot(dz, k_t[hh]) for dz, hh in zip(dzs, heads)]
                dks = [_dot_tn(dz, q_n) for dz, q_n in zip(dzs, q)]
                for n, hh in enumerate(heads):
                    dka[hh, cols, :] += dks[n]
                last = slice(SB_KEYS - 1, SB_KEYS)
                return tuple((dq + dqn, grun + gsum[:, last] + gmat[:, last])
                             for (dq, grun), dqn, gsum, gmat in zip(carry, dqs, gsums, gmats))

            zero = (jnp.zeros((SB_BLOCK, hd), F32), jnp.zeros((SB_BLOCK, 1), F32))
            res = lax.fori_loop(0, ntiles, sweep_right, (zero,) * len(chains))
            for n, (b, hh) in enumerate(chains):
                dq_ref[rows[b], hh * hd:(hh + 1) * hd] = (res[n][0] * scale).astype(BF16)

        _sb_schedule(nb, run_blocks)
        for hh in range(2):
            dk_ref[:, hh * hd:(hh + 1) * hd] = dka[hh, 0:lp, :].astype(BF16)
            dv_ref[:, hh * hd:(hh + 1) * hd] = dva[hh, 0:lp, :].astype(BF16)

    blk = lambda cb: pl.BlockSpec((lp, 128), lambda s, p: (s, cb * 8 + p))
    one = pl.BlockSpec((lp, 128), lambda s, p: (s, p))
    keys = nkt * SB_KEYS
    return _call(body, name=name, out_shape=(S((m, D_MODEL), BF16),) * 3, grid=(bsz, 8),
                 in_specs=[blk(0), blk(1), blk(2), one], out_specs=(one, one, one),
                 scratch=[pltpu.VMEM((2, lp, hd), BF16)] + [pltpu.VMEM((2, keys, hd), BF16)] * 2
                 + [pltpu.VMEM((2, keys, hd), F32)] * 2 + [pltpu.VMEM((4, nkt, SB_BLOCK, SB_KEYS), F32)] * 2,
                 sem=("parallel", "parallel"))(qkvg, qkvg, qkvg, do)


def meta_grad(dh, lp, *, name):
    m, d = dh.shape
    bsz = m // lp
    per = lp // N_META

    def body(dh_ref, o_ref):
        @pl.when(pl.program_id(0) == 0)
        def _():
            o_ref[...] = jnp.zeros_like(o_ref)
        o_ref[...] += dh_ref[...]

    return _call(body, name=name, out_shape=S((N_META, d), F32), grid=(bsz,),
                 in_specs=[pl.BlockSpec((N_META, d), lambda b: (b * per + PAD // N_META, 0))],
                 out_specs=pl.BlockSpec((N_META, d), lambda b: (0, 0)), sem=("arbitrary",))(dh)


def sum_lead(arr, *, name):
    n, r, c = arr.shape
    tr = _tile(r, (128, 64, 32, 16, 8))

    def body(a_ref, o_ref):
        acc = a_ref[0].astype(F32)
        for k in range(1, n):
            acc = acc + a_ref[k].astype(F32)
        o_ref[...] = acc

    return _call(body, name=name, out_shape=S((r, c), F32), grid=(r // tr,),
                 in_specs=[pl.BlockSpec((n, tr, c), lambda i: (0, i, 0))],
                 out_specs=pl.BlockSpec((tr, c), lambda i: (i, 0)), sem=("parallel",))(arr)


def adamw(w, g_parts, mom, var, *, name):
    r, c = w.shape
    tr = _tile(r, (128, 64, 32, 16, 8))
    n_g = len(g_parts)
    c1 = 1.0 - ADAM_B1 ** ADAM_STEP
    c2 = 1.0 - ADAM_B2 ** ADAM_STEP

    def body(*refs):
        w_ref, g_refs, m_ref, v_ref = refs[0], refs[1:1 + n_g], refs[1 + n_g], refs[2 + n_g]
        g_out, d_out, m_out, v_out = refs[3 + n_g:]
        g = g_refs[0][...]
        for gr in g_refs[1:]:
            g = g + gr[...]
        mn = ADAM_B1 * m_ref[...] + (1.0 - ADAM_B1) * g
        vn = ADAM_B2 * v_ref[...] + (1.0 - ADAM_B2) * (g * g)
        g_out[...] = g
        m_out[...] = mn
        v_out[...] = vn
        d_out[...] = -ADAM_LR * ((mn / c1) / (jnp.sqrt(vn / c2) + ADAM_EPS) + ADAM_WD * w_ref[...])

    blk = pl.BlockSpec((tr, c), lambda i: (i, 0))
    return _call(body, name=name, out_shape=(S((r, c), F32),) * 4, grid=(r // tr,), in_specs=[blk] * (3 + n_g),
                 out_specs=(blk,) * 4, sem=("parallel",))(w, *g_parts, mom, var)


_ANY = pl.BlockSpec(memory_space=pl.ANY)


def _position():
    return lax.axis_index("x"), lax.axis_index("y"), lax.axis_index("c")


def _other_chips(x, y):
    return [(1 - x, y), (x, 1 - y), (1 - x, 1 - y)]


def _comm_call(body, arrs, out_shapes, n_sem, *, name):
    return pl.pallas_call(
        body, out_shape=tuple(out_shapes), in_specs=[_ANY] * len(arrs), out_specs=tuple([_ANY] * len(out_shapes)),
        scratch_shapes=(pltpu.SemaphoreType.DMA((n_sem,)), pltpu.SemaphoreType.DMA((n_sem,)),
                        pltpu.SemaphoreType.DMA((len(arrs),))),
        name=name)(*arrs)


def allgather_chips(arrs, *, name):
    n = len(arrs)

    def body(*refs):
        ins, outs = refs[:n], refs[n:2 * n]
        send_sems, recv_sems, loc_sems = refs[2 * n:]
        x, y, c = _position()
        me = 2 * x + y
        chips = _other_chips(x, y)

        def half(ref, a, which):
            rows = arrs[a].shape[0] // 2
            return ref.at[pl.ds(which * rows, rows)]

        local, sent, passed = [], [], []
        for a in range(n):
            cp = pltpu.make_async_copy(ins[a], outs[a].at[me], loc_sems.at[a])
            cp.start()
            local.append(cp)
        for a in range(n):
            for k, (px, py) in enumerate(chips):
                cp = pltpu.make_async_remote_copy(
                    src_ref=half(ins[a], a, c), dst_ref=half(outs[a].at[me], a, c), send_sem=send_sems.at[6 * a + k],
                    recv_sem=recv_sems.at[6 * a + k], device_id=(px, py, c), device_id_type=MESH)
                cp.start()
                sent.append(cp)
        for a in range(n):
            for k, (px, py) in enumerate(chips):
                sent[3 * a + k].wait_recv()
                landed = half(outs[a].at[2 * px + py], a, c)
                cp = pltpu.make_async_remote_copy(
                    src_ref=landed, dst_ref=landed, send_sem=send_sems.at[6 * a + 3 + k],
                    recv_sem=recv_sems.at[6 * a + 3 + k], device_id=(x, y, 1 - c), device_id_type=MESH)
                cp.start()
                passed.append(cp)
        for cp in sent:
            cp.wait_send()
        for cp in passed:
            cp.wait()
        for cp in local:
            cp.wait()

    for a in arrs:
        assert a.shape[0] % 32 == 0
    outs = [S((4,) + a.shape, a.dtype) for a in arrs]
    return _comm_call(body, arrs, outs, 6 * n, name=name)


def exchange_chips(arrs, small, *, name):
    n = len(arrs)

    def body(*refs):
        ins, small_in = refs[:n], refs[n]
        outs, small_out = refs[n + 1:2 * n + 1], refs[2 * n + 1]
        send_sems, recv_sems, loc_sems = refs[2 * n + 2:]
        x, y, c = _position()
        me = 2 * x + y
        copies = []
        for a in range(n):
            loc = pltpu.make_async_copy(ins[a].at[me], outs[a].at[me], loc_sems.at[a])
            loc.start()
            copies.append(loc)
            for k, (px, py) in enumerate(_other_chips(x, y)):
                cp = pltpu.make_async_remote_copy(
                    src_ref=ins[a].at[2 * px + py], dst_ref=outs[a].at[me], send_sem=send_sems.at[3 * a + k],
                    recv_sem=recv_sems.at[3 * a + k], device_id=(px, py, c), device_id_type=MESH)
                cp.start()
                copies.append(cp)
        me8 = 4 * x + 2 * y + c
        loc = pltpu.make_async_copy(small_in, small_out.at[me8], loc_sems.at[n])
        loc.start()
        copies.append(loc)
        k = 3 * n
        for fx in (0, 1):
            for fy in (0, 1):
                for fc in (0, 1):
                    if fx + fy + fc == 0:
                        continue
                    peer = (1 - x if fx else x, 1 - y if fy else y, 1 - c if fc else c)
                    cp = pltpu.make_async_remote_copy(
                        src_ref=small_in, dst_ref=small_out.at[me8], send_sem=send_sems.at[k],
                        recv_sem=recv_sems.at[k], device_id=peer, device_id_type=MESH)
                    cp.start()
                    copies.append(cp)
                    k += 1
        for cp in copies:
            cp.wait()

    outs = [S(a.shape, a.dtype) for a in arrs] + [S((8,) + small.shape, small.dtype)]
    return _comm_call(body, list(arrs) + [small], outs, 3 * n + 7, name=name)


def split_cores(arrs, *, name):
    n = len(arrs)

    def body(*refs):
        ins, outs = refs[:n], refs[n:3 * n]
        send_sems, recv_sems, loc_sems = refs[3 * n:]
        x, y, c = _position()
        copies = []
        for a in range(n):
            loc = pltpu.make_async_copy(ins[a].at[c], outs[2 * a], loc_sems.at[a])
            loc.start()
            copies.append(loc)
            cp = pltpu.make_async_remote_copy(
                src_ref=ins[a].at[1 - c], dst_ref=outs[2 * a + 1], send_sem=send_sems.at[a], recv_sem=recv_sems.at[a],
                device_id=(x, y, 1 - c), device_id_type=MESH)
            cp.start()
            copies.append(cp)
        for cp in copies:
            cp.wait()

    outs = [S(a.shape[1:], a.dtype) for a in arrs for _ in range(2)]
    return _comm_call(body, arrs, outs, n, name=name)


def join_cores(arrs, *, name):
    n = len(arrs)

    def body(*refs):
        ins, outs = refs[:n], refs[n:2 * n]
        send_sems, recv_sems, loc_sems = refs[2 * n:]
        x, y, c = _position()
        copies = []
        for a in range(n):
            rows = arrs[a].shape[0]
            mine = outs[a].at[pl.ds(c * rows, rows)]
            loc = pltpu.make_async_copy(ins[a], mine, loc_sems.at[a])
            loc.start()
            copies.append(loc)
            cp = pltpu.make_async_remote_copy(
                src_ref=ins[a], dst_ref=mine, send_sem=send_sems.at[a], recv_sem=recv_sems.at[a],
                device_id=(x, y, 1 - c), device_id_type=MESH)
            cp.start()
            copies.append(cp)
        for cp in copies:
            cp.wait()

    outs = [S((2 * a.shape[0],) + a.shape[1:], a.dtype) for a in arrs]
    return _comm_call(body, arrs, outs, n, name=name)


def pair_add(a, b, *, name):
    k, r, c = a.shape
    tr = _tile(r, (256, 128))

    def body(a_ref, b_ref, o_ref):
        o_ref[...] = (a_ref[...].astype(F32) + b_ref[...].astype(F32)).astype(BF16)

    blk = pl.BlockSpec((1, tr, c), lambda j, i: (j, i, 0))
    return _call(body, name=name, out_shape=S((k, r, c), BF16), grid=(k, r // tr), in_specs=[blk, blk], out_specs=blk,
                 sem=("parallel", "parallel"))(a, b)


def _local_step(p, x, target):
    bsz, seq, d = x.shape
    lp = LEAD + seq
    m = bsz * lp
    h0 = jnp.concatenate([jnp.zeros((bsz, PAD, d), F32), jnp.broadcast_to(p["meta"][None], (bsz, N_META, d)), x],
                         axis=1).reshape(m, d)
    u0 = rmsnorm_fwd(h0, p["even_norm"], name="norm0")
    proj = mm_nn([u0], p["win_e"], name="proj0")
    lx = conv_fwd(proj, COL_LRU_X, D_MODEL, p["lru_conv_w"], p["lru_conv_b"], name="lru_conv")
    rowmask = jnp.tile((jnp.arange(lp) >= PAD).astype(F32), bsz).reshape(m, 1)
    a, b = gates_fwd(lx, rowmask, p["lru_w_a"], p["lru_b_a"], p["lru_w_x"], p["lru_b_x"], p["lru_lambda"],
                     name="lru_gates")
    hs = scan_fwd(a, b, lp, name="lru_scan")
    ya = gate_fwd(hs, 0, proj, COL_LRU_G // D_MODEL, name="lru_out_gate", o_scan=True)
    pre = conv_fwd(proj, COL_XBC, SSD_CONV_DIM, p["ssd_conv_w"], p["ssd_conv_b"], name="ssd_conv")
    y, s_in = ssd_fwd(pre, proj, p["ssd_dt_bias"], p["ssd_a_log"], p["ssd_d"], lp, name="ssd")
    yb = gnorm_fwd(y, proj, COL_Z // D_MODEL, p["ssd_norm"], name="ssd_norm")
    h1 = mm_nn([ya, yb], p["wout_e"], resid=h0, name="out0")
    u1 = rmsnorm_fwd(h1, p["odd_norm"], name="norm1")
    qkvg = mm_nn([u1], p["win_o"], name="proj1")
    o = attn_fwd(qkvg, lp, name="attn")
    og = gate_fwd(o, 0, qkvg, 3, name="attn_gate")
    h2 = mm_nn([og], p["wout_o"], resid=h1, name="out1")
    dh2, dh2b, loss, d_final = loss_head(h2, target, p["final_norm"], lp, name="loss_head")
    g = {"final_norm": d_final}
    g["odd_w_out"] = mm_tn(og, dh2b, name="dw_out1")
    d_og = mm_nn([dh2b], p["wout_o_t"], name="d_out1")
    do, dgate = gate_bwd(d_og, 0, o, 0, qkvg, 3, name="attn_gate_bwd")
    dq, dk, dv = attn_bwd(qkvg, do, lp, name="attn_bwd")
    segs1 = [dq, dk, dv, dgate]
    du1 = mm_nn(segs1, p["win_o_t"], name="d_proj1")
    g["odd_w_in"] = jnp.concatenate([mm_tn(u1, t, name=f"dw_proj1_{k}") for k, t in enumerate(segs1)], axis=1)
    dh1, dh1b, g["odd_norm"] = rmsnorm_bwd(du1, h1, p["odd_norm"], dh2, name="norm1_bwd")
    g["even_w_out"] = jnp.concatenate([mm_tn(ya, dh1b, name="dw_out0_a"), mm_tn(yb, dh1b, name="dw_out0_b")], axis=0)
    d_mixed = mm_nn([dh1b], p["wout_e_t"], name="d_out0")
    dhs, dlg = gate_bwd(d_mixed, 0, hs, 0, proj, COL_LRU_G // D_MODEL, name="lru_out_gate_bwd", o_scan=True)
    dy, dz, g["ssd_norm"] = gnorm_bwd(d_mixed, 1, y, proj, COL_Z // D_MODEL, p["ssd_norm"], name="ssd_norm_bwd")
    dpre, ddt, g["ssd_dt_bias"], g["ssd_a_log"], g["ssd_d"] = ssd_bwd(
        pre, proj, s_in, dy, p["ssd_dt_bias"], p["ssd_a_log"], p["ssd_d"], lp, name="ssd_bwd")
    dxbc, g["ssd_conv_w"], g["ssd_conv_b"] = conv_bwd(proj, COL_XBC, SSD_CONV_DIM, dpre, p["ssd_conv_w"],
                                                      name="ssd_conv_bwd")
    da, db = scan_bwd(a, hs, dhs, lp, name="lru_scan_bwd")
    dlx, g["lru_w_a"], g["lru_w_x"], g["lru_b_a"], g["lru_b_x"], g["lru_lambda"] = gates_bwd(
        lx, rowmask, da, db, p["lru_w_a"], p["lru_b_a"], p["lru_w_x"], p["lru_b_x"], p["lru_lambda"],
        name="lru_gates_bwd")
    dlrux, g["lru_conv_w"], g["lru_conv_b"] = conv_bwd(proj, COL_LRU_X, D_MODEL, dlx, p["lru_conv_w"],
                                                       name="lru_conv_bwd")
    segs0 = [dlrux, dlg, dz, dxbc, ddt]
    du0 = mm_nn(segs0, p["win_e_t"], name="d_proj0")
    g["even_w_in"] = jnp.concatenate([mm_tn(u0, t, name=f"dw_proj0_{k}") for k, t in enumerate(segs0)],
                                     axis=1)[:, :EVEN_IN]
    dh0, _, g["even_norm"] = rmsnorm_bwd(du0, h0, p["even_norm"], dh1, name="norm0_bwd")
    g["meta"] = meta_grad(dh0, lp, name="meta_grad")
    grad_x = dh0.reshape(bsz, lp, d)[:, LEAD:]
    return loss, grad_x, g


WEIGHTS = ("meta", "even_norm", "even_w_in", "lru_conv_w", "lru_conv_b", "lru_w_a", "lru_b_a", "lru_w_x", "lru_b_x",
           "lru_lambda", "ssd_conv_w", "ssd_conv_b", "ssd_dt_bias", "ssd_a_log", "ssd_d", "ssd_norm", "even_w_out",
           "odd_norm", "odd_w_in", "odd_w_out", "final_norm")
BIG = ("even_w_in", "even_w_out", "odd_w_in", "odd_w_out", "lru_w_a", "lru_w_x")
SHARDED_SMALL = {"meta": 256, "lru_conv_w": 256, "ssd_conv_w": 384, "odd_norm": 256}
SMALL_SHAPES = {"meta": (16, 1024), "even_norm": (1, 1024), "lru_conv_w": (4, 1024), "lru_conv_b": (1, 1024),
                "lru_b_a": (1, 1024), "lru_b_x": (1, 1024), "lru_lambda": (1, 1024), "ssd_conv_w": (4, 1536),
                "ssd_conv_b": (1, 1536), "ssd_dt_bias": (1, 16), "ssd_a_log": (1, 16), "ssd_d": (1, 16),
                "ssd_norm": (1, 1024), "odd_norm": (1, 1024), "final_norm": (1, 1024)}
PACK_UNIT = 1024


def _pack(parts):
    flat = []
    for part in parts:
        v = part.reshape(-1)
        flat.append(jnp.pad(v, (0, -v.shape[0] % PACK_UNIT)))
    return jnp.concatenate(flat).reshape(-1, 128)


def _unpack(buf, shapes):
    v = buf.reshape(-1)
    out, off = [], 0
    for shp in shapes:
        n = 1
        for s_ in shp:
            n *= s_
        out.append(v[off:off + n].reshape(shp))
        off += n + (-n % PACK_UNIT)
    return out


def _chip_cols(a4):
    return jnp.transpose(a4, (1, 0, 2)).reshape(a4.shape[1], -1)


def _to_chip_cols(a, cols):
    return jnp.transpose(a.reshape(a.shape[0], 4, cols), (1, 0, 2))


def kernel(x, meta, even_norm, even_w_in, lru_conv_w, lru_conv_b, lru_w_a, lru_b_a, lru_w_x, lru_b_x, lru_lambda, ssd_conv_w, ssd_conv_b, ssd_dt_bias, ssd_a_log, ssd_d, ssd_norm, even_w_out, odd_norm, odd_w_in, odd_w_out, final_norm, loss_target, m_meta, m_even_norm, m_even_w_in, m_lru_conv_w, m_lru_conv_b, m_lru_w_a, m_lru_b_a, m_lru_w_x, m_lru_b_x, m_lru_lambda, m_ssd_conv_w, m_ssd_conv_b, m_ssd_dt_bias, m_ssd_a_log, m_ssd_d, m_ssd_norm, m_even_w_out, m_odd_norm, m_odd_w_in, m_odd_w_out, m_final_norm, v_meta, v_even_norm, v_even_w_in, v_lru_conv_w, v_lru_conv_b, v_lru_w_a, v_lru_b_a, v_lru_w_x, v_lru_b_x, v_lru_lambda, v_ssd_conv_w, v_ssd_conv_b, v_ssd_dt_bias, v_ssd_a_log, v_ssd_d, v_ssd_norm, v_even_w_out, v_odd_norm, v_odd_w_in, v_odd_w_out, v_final_norm):
    given = dict(locals())
    w = {n: given[n] for n in WEIGHTS}
    mom = {n: given["m_" + n] for n in WEIGHTS}
    var = {n: given["v_" + n] for n in WEIGHTS}
    chip = 2 * lax.axis_index("x") + lax.axis_index("y")

    big_local = {"even_w_in": even_w_in[0], "even_w_out": even_w_out[0], "odd_w_in": odd_w_in[0],
                 "odd_w_out": odd_w_out[0], "lru_w_a": lru_w_a[0].reshape(256, 256),
                 "lru_w_x": lru_w_x[0].reshape(256, 256)}
    sharded_local = [meta, lru_conv_w[0], ssd_conv_w[0], odd_norm]
    gathered = allgather_chips([big_local[n].astype(BF16) for n in BIG] + [_pack(sharded_local)], name="gather_weights")
    gb = dict(zip(BIG, gathered[:-1]))
    per_chip = [_unpack(gathered[-1][k], [a.shape for a in sharded_local]) for k in range(4)]
    full_small = [jnp.concatenate([per_chip[k][j] for k in range(4)], axis=-1) for j in range(len(sharded_local))]

    def lru_full(a4):
        return jnp.transpose(a4.reshape(4, LRU_BLOCKS, 64, LRU_BLOCK), (1, 0, 2, 3)).reshape(LRU_BLOCKS, LRU_BLOCK, LRU_BLOCK)

    p = {"meta": full_small[0], "lru_conv_w": full_small[1], "ssd_conv_w": full_small[2], "odd_norm": full_small[3],
         "even_norm": even_norm, "lru_conv_b": lru_conv_b, "lru_b_a": lru_b_a, "lru_b_x": lru_b_x,
         "lru_lambda": lru_lambda, "ssd_conv_b": ssd_conv_b, "ssd_dt_bias": ssd_dt_bias, "ssd_a_log": ssd_a_log,
         "ssd_d": ssd_d, "ssd_norm": ssd_norm, "final_norm": final_norm.reshape(1, D_MODEL)}
    p["win_e"] = jnp.pad(_chip_cols(gb["even_w_in"]), ((0, 0), (0, EVEN_IN_P - EVEN_IN)))
    p["wout_e"] = gb["even_w_out"].reshape(2 * D_MODEL, D_MODEL)
    p["win_o"] = _chip_cols(gb["odd_w_in"])
    p["wout_o"] = gb["odd_w_out"].reshape(D_MODEL, D_MODEL)
    for n in ("win_e", "wout_e", "win_o", "wout_o"):
        p[n + "_t"] = p[n].T
    p["lru_w_a"] = lru_full(gb["lru_w_a"])
    p["lru_w_x"] = lru_full(gb["lru_w_x"])

    loss_part, grad_x, g = _local_step(p, x, loss_target)

    def lru_slabs(a):
        return jnp.transpose(a.reshape(LRU_BLOCKS, 4, 64, LRU_BLOCK), (1, 0, 2, 3)).reshape(4, 256, LRU_BLOCK)

    def by_half(t):
        return jnp.transpose(t.reshape(4, 2, t.shape[1] // 2, t.shape[2]), (1, 0, 2, 3)).astype(BF16)

    slabs = [_to_chip_cols(g["even_w_in"], EVEN_IN // 4), g["even_w_out"].reshape(4, 512, D_MODEL),
             _to_chip_cols(g["odd_w_in"], D_MODEL), g["odd_w_out"].reshape(4, 256, D_MODEL),
             lru_slabs(g["lru_w_a"]), lru_slabs(g["lru_w_x"])]
    small_names = list(SMALL_SHAPES)
    small_part = _pack([loss_part[0:1, 0:1]] + [g[n] for n in small_names])
    halves = split_cores([by_half(t) for t in slabs], name="split_cores")
    pair_sums = [pair_add(halves[2 * k], halves[2 * k + 1], name=f"pair_add_{n}") for k, n in enumerate(BIG)]
    *recv, small_all = exchange_chips(pair_sums, small_part, name="exchange_grads")
    half_sums = [sum_lead(r, name=f"sum_chips_{n}") for n, r in zip(BIG, recv)]
    full_sums = join_cores(half_sums, name="join_cores")
    small_sum = sum_lead(small_all, name="sum_small")
    small_g = dict(zip(["loss"] + small_names, _unpack(small_sum, [(1, 1)] + [SMALL_SHAPES[n] for n in small_names])))
    loss = small_g["loss"].reshape(())

    grads, delta, new_m, new_v = {}, {}, {}, {}
    for n, total in zip(BIG, full_sums):
        shp = w[n].shape
        two_d = lambda t: t.reshape(total.shape)
        res = adamw(two_d(w[n]), [total], two_d(mom[n]), two_d(var[n]), name=f"adamw_{n}")
        grads[n], delta[n], new_m[n], new_v[n] = (r.reshape(shp) for r in res)
    local_g = []
    for n in small_names:
        gn = small_g[n]
        if n in SHARDED_SMALL:
            gn = lax.dynamic_slice_in_dim(gn, chip * SHARDED_SMALL[n], SHARDED_SMALL[n], axis=1)
        local_g.append(gn)
    res = adamw(_pack([w[n] for n in small_names]), [_pack(local_g)], _pack([mom[n] for n in small_names]),
                _pack([var[n] for n in small_names]), name="adamw_small")
    shapes = [w[n].shape for n in small_names]
    for out, r in zip((grads, delta, new_m, new_v), res):
        out.update(dict(zip(small_names, _unpack(r, shapes))))
    return (loss, grad_x, *[grads[n] for n in WEIGHTS], *[delta[n] for n in WEIGHTS], *[new_m[n] for n in WEIGHTS],
            *[new_v[n] for n in WEIGHTS])
```

```python
import functools

import jax
import jax.numpy as jnp
from jax import lax
from jax.experimental import pallas as pl
from jax.experimental.pallas import tpu as pltpu

F32 = jnp.float32
BF16 = jnp.bfloat16

D_MODEL = 1024
N_META = 16
LEAD = 128
PAD = LEAD - N_META
EPS = 1e-6
CONV_W = 4
LRU_BLOCKS = 4
LRU_BLOCK = 256
RG_LRU_C = 8.0
SSD_HEADS = 16
SSD_HEAD_DIM = 64
SSD_GROUPS = 2
SSD_HPG = 8
SSD_STATE = 128
CHUNK = 128
SSD_CONV_DIM = 1536
EVEN_IN = 4624
EVEN_IN_P = 4736
COL_LRU_X, COL_LRU_G, COL_Z, COL_XBC, COL_DT = 0, 1024, 2048, 3072, 4608
SB_HEADS = 16
SB_HEAD_DIM = 64
SB_BLOCK = 128
ADAM_LR, ADAM_B1, ADAM_B2, ADAM_EPS, ADAM_WD, ADAM_STEP = 0.001, 0.9, 0.999, 1e-08, 0.01, 10
VMEM_LIMIT_V7X = 56 * 1024 * 1024
MESH = pl.DeviceIdType.MESH
S = jax.ShapeDtypeStruct


def _tile(n, prefs):
    for p in prefs:
        if n % p == 0:
            return p
    raise ValueError(f"no tile of {prefs} divides {n}")


def _call(body, *, name, out_shape, grid=(), in_specs=None, out_specs=None, scratch=(), sem=None):
    kw = {}
    if in_specs is not None:
        kw["in_specs"] = in_specs
    if out_specs is not None:
        kw["out_specs"] = out_specs
    return pl.pallas_call(
        body, out_shape=out_shape, grid=grid, scratch_shapes=tuple(scratch), name=name,
        compiler_params=pltpu.CompilerParams(dimension_semantics=sem, vmem_limit_bytes=VMEM_LIMIT_V7X), **kw)


def _sigmoid(x):
    return 0.5 * (jnp.tanh(0.5 * x) + 1.0)


def _silu(x):
    return x * _sigmoid(x)


def _softplus(x):
    return jnp.maximum(x, 0.0) + jnp.log(1.0 + jnp.exp(-jnp.abs(x)))


def _dot(a, b):
    return jnp.dot(a, b, preferred_element_type=F32)


def _dot_nt(a, b):
    return lax.dot_general(a, b, (((1,), (1,)), ((), ())), preferred_element_type=F32)


def _dot_tn(a, b):
    return lax.dot_general(a, b, (((0,), (0,)), ((), ())), preferred_element_type=F32)


def mm_nn(a_list, w, *, name, resid=None):
    m = a_list[0].shape[0]
    k_tot, n = w.shape
    ks = [a.shape[1] for a in a_list]
    assert sum(ks) == k_tot
    tm = _tile(m, (256, 128))
    n_a = len(a_list)
    offs = [sum(ks[:i]) for i in range(n_a)]
    n_chunks = [(c0, min(512, n - c0)) for c0 in range(0, n, 512)]

    def body(*refs):
        a_refs, w_ref = refs[:n_a], refs[n_a]
        r_ref = refs[n_a + 1] if resid is not None else None
        o_ref = refs[-1]
        for c0, cw in n_chunks:
            acc = None
            for a_ref, k0, k in zip(a_refs, offs, ks):
                p = _dot(a_ref[...], w_ref[k0:k0 + k, c0:c0 + cw])
                acc = p if acc is None else acc + p
            if r_ref is not None:
                acc = acc + r_ref[:, c0:c0 + cw]
            o_ref[:, c0:c0 + cw] = acc

    in_specs = [pl.BlockSpec((tm, k), lambda i: (i, 0)) for k in ks]
    in_specs.append(pl.BlockSpec((k_tot, n), lambda i: (0, 0)))
    args = list(a_list) + [w]
    if resid is not None:
        in_specs.append(pl.BlockSpec((tm, n), lambda i: (i, 0)))
        args.append(resid)
    return _call(body, name=name, out_shape=S((m, n), F32), grid=(m // tm,), in_specs=in_specs,
                 out_specs=pl.BlockSpec((tm, n), lambda i: (i, 0)), sem=("parallel",))(*args)


def mm_tn(a, g, *, name):
    t, m = a.shape
    n = g.shape[1]
    tk = _tile(t, (512, 256, 128))
    tn = _tile(n, (1024, 512, 256, 128))

    def body(a_ref, g_ref, o_ref):
        @pl.when(pl.program_id(1) == 0)
        def _():
            o_ref[...] = jnp.zeros_like(o_ref)
        o_ref[...] += _dot_tn(a_ref[...], g_ref[...])

    return _call(body, name=name, out_shape=S((m, n), F32), grid=(n // tn, t // tk),
                 in_specs=[pl.BlockSpec((tk, m), lambda j, k: (k, 0)), pl.BlockSpec((tk, tn), lambda j, k: (k, j))],
                 out_specs=pl.BlockSpec((m, tn), lambda j, k: (0, j)), sem=("parallel", "arbitrary"))(a, g)


def rmsnorm_fwd(h, w, *, name):
    m, d = h.shape
    tm = _tile(m, (512, 256, 128))

    def body(h_ref, w_ref, o_ref):
        x = h_ref[...]
        r = lax.rsqrt(jnp.mean(x * x, axis=-1, keepdims=True) + EPS)
        o_ref[...] = (x * r * w_ref[...]).astype(BF16)

    return _call(body, name=name, out_shape=S((m, d), BF16), grid=(m // tm,),
                 in_specs=[pl.BlockSpec((tm, d), lambda i: (i, 0)), pl.BlockSpec((1, d), lambda i: (0, 0))],
                 out_specs=pl.BlockSpec((tm, d), lambda i: (i, 0)), sem=("parallel",))(h, w)


def rmsnorm_bwd(du, h, w, dres, *, name):
    m, d = h.shape
    tm = _tile(m, (256, 128))

    def body(du_ref, h_ref, w_ref, dr_ref, dh_ref, dhb_ref, dw_ref):
        @pl.when(pl.program_id(0) == 0)
        def _():
            dw_ref[...] = jnp.zeros_like(dw_ref)
        x = h_ref[...]
        r = lax.rsqrt(jnp.mean(x * x, axis=-1, keepdims=True) + EPS)
        du_ = du_ref[...]
        g = du_ * w_ref[...]
        c = jnp.mean(g * x, axis=-1, keepdims=True)
        dh = dr_ref[...] + r * g - x * (r * r * r) * c
        dh_ref[...] = dh
        dhb_ref[...] = dh.astype(BF16)
        dw_ref[...] += jnp.sum(du_ * x * r, axis=0, keepdims=True)

    row = pl.BlockSpec((tm, d), lambda i: (i, 0))
    vec = pl.BlockSpec((1, d), lambda i: (0, 0))
    return _call(body, name=name, out_shape=(S((m, d), F32), S((m, d), BF16), S((1, d), F32)), grid=(m // tm,),
                 in_specs=[row, row, vec, row], out_specs=(row, row, vec), sem=("arbitrary",))(du, h, w, dres)


def gate_fwd(o, o_cb, g, g_cb, *, name, o_scan=False):
    m = g.shape[0]
    d = D_MODEL
    tm = _tile(m, (256, 128))

    def body(o_ref, g_ref, y_ref):
        ov = _from_scan_layout(o_ref, 0, d, tm) if o_scan else o_ref[...]
        y_ref[...] = (ov * _silu(g_ref[...])).astype(BF16)

    o_spec = pl.BlockSpec((tm * 8, 128), lambda i: (i, 0)) if o_scan else pl.BlockSpec((tm, d), lambda i: (i, o_cb))
    return _call(body, name=name, out_shape=S((m, d), BF16), grid=(m // tm,),
                 in_specs=[o_spec, pl.BlockSpec((tm, d), lambda i: (i, g_cb))],
                 out_specs=pl.BlockSpec((tm, d), lambda i: (i, 0)), sem=("parallel",))(o, g)


def gate_bwd(dy, dy_cb, o, o_cb, g, g_cb, *, name, o_scan=False):
    m = g.shape[0]
    d = D_MODEL
    tm = _tile(m, (256, 128))

    def body(dy_ref, o_ref, g_ref, do_ref, dg_ref):
        gv = g_ref[...]
        s = _sigmoid(gv)
        dyv = dy_ref[...]
        do = dyv * gv * s
        if o_scan:
            _to_scan_layout(do_ref, do, 0, tm)
            ov = _from_scan_layout(o_ref, 0, d, tm)
        else:
            do_ref[...] = do
            ov = o_ref[...]
        dg_ref[...] = (dyv * ov * (s + gv * s * (1.0 - s))).astype(BF16)

    nat = pl.BlockSpec((tm, d), lambda i: (i, 0))
    scn = pl.BlockSpec((tm * 8, 128), lambda i: (i, 0))
    o_spec = scn if o_scan else pl.BlockSpec((tm, d), lambda i: (i, o_cb))
    do_shape = S((m * 8, 128), F32) if o_scan else S((m, d), F32)
    return _call(body, name=name, out_shape=(do_shape, S((m, d), BF16)), grid=(m // tm,),
                 in_specs=[pl.BlockSpec((tm, d), lambda i: (i, dy_cb)), o_spec, pl.BlockSpec((tm, d), lambda i: (i, g_cb))],
                 out_specs=(scn if o_scan else nat, nat), sem=("parallel",))(dy, o, g)


def _group_mean(x):
    half = x.shape[1] // SSD_GROUPS
    parts = [jnp.broadcast_to(jnp.mean(x[:, k * half:(k + 1) * half], axis=-1, keepdims=True), (x.shape[0], half))
             for k in range(SSD_GROUPS)]
    return jnp.concatenate(parts, axis=1)


def gnorm_fwd(y, z, z_cb, w, *, name):
    m = y.shape[0]
    d = D_MODEL
    tm = _tile(m, (256, 128))

    def body(y_ref, z_ref, w_ref, o_ref):
        g = y_ref[...] * _silu(z_ref[...])
        r = lax.rsqrt(_group_mean(g * g) + EPS)
        o_ref[...] = (g * r * w_ref[...]).astype(BF16)

    return _call(body, name=name, out_shape=S((m, d), BF16), grid=(m // tm,),
                 in_specs=[pl.BlockSpec((tm, d), lambda i: (i, 0)), pl.BlockSpec((tm, d), lambda i: (i, z_cb)),
                           pl.BlockSpec((1, d), lambda i: (0, 0))],
                 out_specs=pl.BlockSpec((tm, d), lambda i: (i, 0)), sem=("parallel",))(y, z, w)


def gnorm_bwd(do, do_cb, y, z, z_cb, w, *, name):
    m = y.shape[0]
    d = D_MODEL
    tm = _tile(m, (256, 128))

    def body(do_ref, y_ref, z_ref, w_ref, dy_ref, dz_ref, dw_ref):
        @pl.when(pl.program_id(0) == 0)
        def _():
            dw_ref[...] = jnp.zeros_like(dw_ref)
        yv, zv, dov = y_ref[...], z_ref[...], do_ref[...]
        s = _sigmoid(zv)
        sz = zv * s
        g = yv * sz
        r = lax.rsqrt(_group_mean(g * g) + EPS)
        dw_ref[...] += jnp.sum(dov * g * r, axis=0, keepdims=True)
        dn = dov * w_ref[...]
        dg = r * dn - g * (r * r * r) * _group_mean(dn * g)
        dy_ref[...] = dg * sz
        dz_ref[...] = (dg * yv * (s + zv * s * (1.0 - s))).astype(BF16)

    row = pl.BlockSpec((tm, d), lambda i: (i, 0))
    vec = pl.BlockSpec((1, d), lambda i: (0, 0))
    return _call(body, name=name, out_shape=(S((m, d), F32), S((m, d), BF16), S((1, d), F32)), grid=(m // tm,),
                 in_specs=[pl.BlockSpec((tm, d), lambda i: (i, do_cb)), row, pl.BlockSpec((tm, d), lambda i: (i, z_cb)), vec],
                 out_specs=(row, row, vec), sem=("arbitrary",))(do, y, z, w)


def loss_head(h, target, w, lp, *, name):
    m, d = h.shape
    bsz = m // lp
    tm = SB_BLOCK
    nblk = lp // tm
    lead_blk = LEAD // tm

    def body(h_ref, t_ref, w_ref, dh_ref, dhb_ref, l_ref, dw_ref):
        i = pl.program_id(1)

        @pl.when(jnp.logical_and(pl.program_id(0) == 0, i == 0))
        def _():
            l_ref[...] = jnp.zeros_like(l_ref)
            dw_ref[...] = jnp.zeros_like(dw_ref)

        @pl.when(i < lead_blk)
        def _():
            dh_ref[...] = jnp.zeros_like(dh_ref)
            dhb_ref[...] = jnp.zeros_like(dhb_ref)

        @pl.when(i >= lead_blk)
        def _():
            x = h_ref[...]
            r = lax.rsqrt(jnp.mean(x * x, axis=-1, keepdims=True) + EPS)
            wv = w_ref[...]
            e = x * r * wv - t_ref[0]
            l_ref[...] += 0.5 * jnp.sum(jnp.mean(e * e, axis=-1, keepdims=True))
            dy = e * (1.0 / d)
            g = dy * wv
            c = jnp.mean(g * x, axis=-1, keepdims=True)
            dh = r * g - x * (r * r * r) * c
            dh_ref[...] = dh
            dhb_ref[...] = dh.astype(BF16)
            dw_ref[...] += jnp.sum(dy * x * r, axis=0, keepdims=True)

    row = pl.BlockSpec((tm, d), lambda b, i: (b * nblk + i, 0))
    return _call(body, name=name, out_shape=(S((m, d), F32), S((m, d), BF16), S((8, 128), F32), S((1, d), F32)),
                 grid=(bsz, nblk),
                 in_specs=[row, pl.BlockSpec((1, tm, d), lambda b, i: (b, jnp.maximum(i - lead_blk, 0), 0)),
                           pl.BlockSpec((1, d), lambda b, i: (0, 0))],
                 out_specs=(row, row, pl.BlockSpec((8, 128), lambda b, i: (0, 0)),
                            pl.BlockSpec((1, d), lambda b, i: (0, 0))),
                 sem=("arbitrary", "arbitrary"))(h, target, w)


def _conv_tiles(m, c):
    return _tile(m, (256, 128)), _tile(c, (512, 256, 128))


def conv_fwd(x, col0, c, w, b, *, name):
    m = x.shape[0]
    tm, tc = _conv_tiles(m, c)
    assert col0 % tc == 0
    cb0 = col0 // tc
    hb = tm // 8

    def body(x_ref, halo_ref, w_ref, b_ref, o_ref, ext):
        ext[0:8, :] = halo_ref[...]
        ext[8:8 + tm, :] = x_ref[...]
        acc = b_ref[...] + w_ref[3:4, :] * ext[8:8 + tm, :]
        for k in range(CONV_W - 1):
            acc = acc + w_ref[k:k + 1, :] * ext[pl.ds(5 + k, tm), :]
        o_ref[...] = acc

    return _call(body, name=name, out_shape=S((m, c), F32), grid=(m // tm, c // tc),
                 in_specs=[pl.BlockSpec((tm, tc), lambda i, j: (i, cb0 + j)),
                           pl.BlockSpec((8, tc), lambda i, j: (jnp.maximum(i * hb - 1, 0), cb0 + j)),
                           pl.BlockSpec((CONV_W, tc), lambda i, j: (0, j)), pl.BlockSpec((1, tc), lambda i, j: (0, j))],
                 out_specs=pl.BlockSpec((tm, tc), lambda i, j: (i, j)), scratch=[pltpu.VMEM((tm + 8, tc), F32)],
                 sem=("parallel", "parallel"))(x, x, w, b)


def conv_bwd(x, col0, c, dy, w, *, name):
    m = x.shape[0]
    tm, tc = _conv_tiles(m, c)
    cb0 = col0 // tc
    n_i = m // tm
    hb = tm // 8

    def body(x_ref, xh_ref, dy_ref, dyn_ref, w_ref, dx_ref, dw_ref, db_ref, ext, dext):
        i = pl.program_id(1)

        @pl.when(i == 0)
        def _():
            dw_ref[...] = jnp.zeros_like(dw_ref)
            db_ref[...] = jnp.zeros_like(db_ref)

        ext[0:8, :] = xh_ref[...]
        ext[8:8 + tm, :] = x_ref[...]
        d_cur = dy_ref[...]
        dext[0:tm, :] = d_cur
        dext[tm:tm + 8, :] = jnp.where(i == n_i - 1, 0.0, dyn_ref[...])
        dx = w_ref[3:4, :] * d_cur
        for k in range(CONV_W - 1):
            dx = dx + w_ref[k:k + 1, :] * dext[pl.ds(3 - k, tm), :]
        dx_ref[...] = dx.astype(BF16)
        for k in range(CONV_W - 1):
            dw_ref[k:k + 1, :] += jnp.sum(ext[pl.ds(5 + k, tm), :] * d_cur, axis=0, keepdims=True)
        dw_ref[3:4, :] += jnp.sum(ext[8:8 + tm, :] * d_cur, axis=0, keepdims=True)
        db_ref[...] += jnp.sum(d_cur, axis=0, keepdims=True)

    return _call(body, name=name, out_shape=(S((m, c), BF16), S((CONV_W, c), F32), S((1, c), F32)),
                 grid=(c // tc, n_i),
                 in_specs=[pl.BlockSpec((tm, tc), lambda j, i: (i, cb0 + j)),
                           pl.BlockSpec((8, tc), lambda j, i: (jnp.maximum(i * hb - 1, 0), cb0 + j)),
                           pl.BlockSpec((tm, tc), lambda j, i: (i, j)),
                           pl.BlockSpec((8, tc), lambda j, i: (jnp.minimum((i + 1) * hb, n_i * hb - 1), j)),
                           pl.BlockSpec((CONV_W, tc), lambda j, i: (0, j))],
                 out_specs=(pl.BlockSpec((tm, tc), lambda j, i: (i, j)), pl.BlockSpec((CONV_W, tc), lambda j, i: (0, j)),
                            pl.BlockSpec((1, tc), lambda j, i: (0, j))),
                 scratch=[pltpu.VMEM((tm + 8, tc), F32), pltpu.VMEM((tm + 8, tc), F32)],
                 sem=("parallel", "arbitrary"))(x, x, dy, dy, w)


def _row_valid(tile_idx, tiles_per_seq, tm):
    pos = lax.rem(tile_idx, tiles_per_seq) * tm + lax.broadcasted_iota(jnp.int32, (tm, 1), 0)
    return (pos >= PAD).astype(F32)


def _neg_expm1(x):
    small = -(x * (1.0 + x * (0.5 + x * (1.0 / 6.0))))
    return jnp.where(x > -0.01, small, 1.0 - jnp.exp(x))


def _to_scan_layout(ref, val, col0, tm):
    for k in range(val.shape[1] // 128):
        ref[pl.ds(col0 // 128 + k, tm, stride=8), :] = val[:, k * 128:(k + 1) * 128]


def _from_scan_layout(ref, col0, width, tm):
    parts = [ref[pl.ds(col0 // 128 + k, tm, stride=8), :] for k in range(width // 128)]
    return parts[0] if len(parts) == 1 else jnp.concatenate(parts, axis=1)


def _gates_core(lx, wa, ba, wx, bx, lam):
    lxb = lx.astype(BF16)
    r = _sigmoid(_dot(lxb, wa) + ba)
    i = _sigmoid(_dot(lxb, wx) + bx)
    sp = _softplus(-lam)
    log_a = (-RG_LRU_C) * r * sp
    a = jnp.exp(log_a)
    mult = jnp.sqrt(_neg_expm1(2.0 * log_a))
    return lxb, r, i, sp, a, mult


def gates_fwd(lx, rowmask, wa, ba, wx, bx, lam, *, name):
    m = lx.shape[0]
    tm = _tile(m, (256, 128))
    cb = LRU_BLOCK

    def body(lx_ref, msk_ref, wa_ref, ba_ref, wx_ref, bx_ref, lam_ref, a_ref, b_ref):
        msk = msk_ref[...]
        for g in range(LRU_BLOCKS):
            ch = slice(g * cb, (g + 1) * cb)
            lxv = lx_ref[:, ch]
            _, _, i, _, a, mult = _gates_core(lxv, wa_ref[g], ba_ref[:, ch], wx_ref[g], bx_ref[:, ch], lam_ref[:, ch])
            _to_scan_layout(a_ref, a, g * cb, tm)
            _to_scan_layout(b_ref, msk * (mult * i * lxv), g * cb, tm)

    tok = pl.BlockSpec((tm, D_MODEL), lambda i: (i, 0))
    msk = pl.BlockSpec((tm, 1), lambda i: (i, 0))
    wsp = pl.BlockSpec((LRU_BLOCKS, cb, cb), lambda i: (0, 0, 0))
    vec = pl.BlockSpec((1, D_MODEL), lambda i: (0, 0))
    scn = pl.BlockSpec((tm * 8, 128), lambda i: (i, 0))
    return _call(body, name=name, out_shape=(S((m * 8, 128), F32),) * 2, grid=(m // tm,),
                 in_specs=[tok, msk, wsp, vec, wsp, vec, vec], out_specs=(scn, scn),
                 sem=("parallel",))(lx, rowmask, wa, ba, wx, bx, lam)


def gates_bwd(lx, rowmask, da, db, wa, ba, wx, bx, lam, *, name):
    m = lx.shape[0]
    tm = _tile(m, (256, 128))
    cb = LRU_BLOCK

    def body(lx_ref, msk_ref, da_ref, db_ref, wa_ref, ba_ref, wx_ref, bx_ref, lam_ref,
             dlx_ref, dwa_ref, dwx_ref, dba_ref, dbx_ref, dlam_ref):
        @pl.when(pl.program_id(0) == 0)
        def _():
            for ref in (dwa_ref, dwx_ref, dba_ref, dbx_ref, dlam_ref):
                ref[...] = jnp.zeros_like(ref)
        msk = msk_ref[...]
        for g in range(LRU_BLOCKS):
            ch = slice(g * cb, (g + 1) * cb)
            lxv = lx_ref[:, ch]
            lamv = lam_ref[:, ch]
            lxb, r, i, sp, a, mult = _gates_core(lxv, wa_ref[g], ba_ref[:, ch], wx_ref[g], bx_ref[:, ch], lamv)
            dbv = msk * _from_scan_layout(db_ref, g * cb, cb, tm)
            d_mult = dbv * (i * lxv)
            d_i = dbv * (mult * lxv)
            d_log_a = _from_scan_layout(da_ref, g * cb, cb, tm) * a - d_mult * (a * a) / mult
            d_pa = (d_log_a * ((-RG_LRU_C) * sp)) * (r * (1.0 - r))
            d_px = d_i * (i * (1.0 - i))
            d_pa16 = d_pa.astype(BF16)
            d_px16 = d_px.astype(BF16)
            dlx_ref[:, ch] = dbv * (mult * i) + _dot_nt(d_pa16, wa_ref[g]) + _dot_nt(d_px16, wx_ref[g])
            dwa_ref[g] += _dot_tn(lxb, d_pa16)
            dwx_ref[g] += _dot_tn(lxb, d_px16)
            dba_ref[:, ch] += jnp.sum(d_pa, axis=0, keepdims=True)
            dbx_ref[:, ch] += jnp.sum(d_px, axis=0, keepdims=True)
            d_sp = jnp.sum(d_log_a * ((-RG_LRU_C) * r), axis=0, keepdims=True)
            dlam_ref[:, ch] += -d_sp * _sigmoid(-lamv)

    tok = pl.BlockSpec((tm, D_MODEL), lambda i: (i, 0))
    msk = pl.BlockSpec((tm, 1), lambda i: (i, 0))
    scn = pl.BlockSpec((tm * 8, 128), lambda i: (i, 0))
    wsp = pl.BlockSpec((LRU_BLOCKS, cb, cb), lambda i: (0, 0, 0))
    vec = pl.BlockSpec((1, D_MODEL), lambda i: (0, 0))
    wshape = S((LRU_BLOCKS, cb, cb), F32)
    vshape = S((1, D_MODEL), F32)
    return _call(body, name=name, out_shape=(S((m, D_MODEL), F32), wshape, wshape, vshape, vshape, vshape),
                 grid=(m // tm,), in_specs=[tok, msk, scn, scn, wsp, vec, wsp, vec, vec],
                 out_specs=(tok, wsp, wsp, vec, vec, vec),
                 sem=("arbitrary",))(lx, rowmask, da, db, wa, ba, wx, bx, lam)


SCAN_TOK = 128


def scan_fwd(a, b, lp, *, name):
    m = a.shape[0] // 8
    bsz = m // lp
    nch = lp // SCAN_TOK
    rows = SCAN_TOK * 8

    def body(a_ref, b_ref, h_ref, carry):
        @pl.when(pl.program_id(1) == 0)
        def _():
            carry[...] = jnp.zeros_like(carry)

        def step(t, h):
            r = pl.ds(pl.multiple_of(t * 8, 8), 8)
            h = a_ref[r, :] * h + b_ref[r, :]
            h_ref[r, :] = h
            return h

        carry[...] = lax.fori_loop(0, SCAN_TOK, step, carry[...], unroll=8)

    blk = pl.BlockSpec((rows, 128), lambda s, c: (s * nch + c, 0))
    return _call(body, name=name, out_shape=S((m * 8, 128), F32), grid=(bsz, nch), in_specs=[blk, blk], out_specs=blk,
                 scratch=[pltpu.VMEM((8, 128), F32)], sem=("parallel", "arbitrary"))(a, b)


def scan_bwd(a, h, dh, lp, *, name):
    m = a.shape[0] // 8
    bsz = m // lp
    nch = lp // SCAN_TOK
    rows = SCAN_TOK * 8

    def body(a_ref, h_ref, hprev_ref, dh_ref, da_ref, db_ref, carry):
        c = pl.program_id(1)

        @pl.when(c == 0)
        def _():
            carry[...] = jnp.zeros_like(carry)

        h_before = jnp.where(c == nch - 1, 0.0, hprev_ref[...])

        def step(k, ag):
            t = SCAN_TOK - 1 - k
            r = pl.ds(pl.multiple_of(t * 8, 8), 8)
            g = dh_ref[r, :] + ag
            db_ref[r, :] = g
            tp = jnp.maximum(t - 1, 0)
            hp = jnp.where(t == 0, h_before, h_ref[pl.ds(pl.multiple_of(tp * 8, 8), 8), :])
            da_ref[r, :] = g * hp
            return a_ref[r, :] * g

        carry[...] = lax.fori_loop(0, SCAN_TOK, step, carry[...], unroll=8)

    blk = pl.BlockSpec((rows, 128), lambda s, c: (s * nch + (nch - 1 - c), 0))
    prev = pl.BlockSpec((8, 128), lambda s, c: (jnp.maximum((s * nch + (nch - 1 - c)) * SCAN_TOK - 1, 0), 0))
    return _call(body, name=name, out_shape=(S((m * 8, 128), F32),) * 2, grid=(bsz, nch),
                 in_specs=[blk, blk, prev, blk], out_specs=(blk, blk), scratch=[pltpu.VMEM((8, 128), F32)],
                 sem=("parallel", "arbitrary"))(a, h, h, dh)


SSD_SEQS = 2
SSD_SEQS_BWD = 1
SSD_SROWS = SSD_HEADS * SSD_HEAD_DIM


def _ssd_chunk(seqs, dtb, alog, dpar, valid):
    row = lax.broadcasted_iota(jnp.int32, (CHUNK, CHUNK), 0)
    col = lax.broadcasted_iota(jnp.int32, (CHUNK, CHUNK), 1)
    tri = row >= col
    neg_a = -jnp.exp(alog)
    dts, acums, acum_ts, b16s, c16s = [], [], [], [], []
    for xs, bs, cs, dtr, ss in seqs:
        dt = _softplus(dtr + dtb) * valid
        acum = jnp.dot(tri.astype(F32), dt * neg_a, precision=lax.Precision.HIGHEST, preferred_element_type=F32)
        dts.append(dt)
        acums.append(acum)
        acum_ts.append(acum.T)
        b16s.append([(_silu(b) * valid).astype(BF16) for b in bs])
        c16s.append([(_silu(c) * valid).astype(BF16) for c in cs])
    cbs = [[_dot_nt(c, b) for c, b in zip(c16, b16)] for c16, b16 in zip(c16s, b16s)]
    idx = [(q, h) for q in range(len(seqs)) for h in range(SSD_HEADS)]
    grp = [h // SSD_HPG for _, h in idx]
    x = [_silu(seqs[q][0][h]) for q, h in idx]
    s_in = [seqs[q][4][h] for q, h in idx]
    ac = [acums[q][:, h:h + 1] for q, h in idx]
    alast = [acums[q][CHUNK - 1:CHUNK, h:h + 1] for q, h in idx]
    xd = [x[n] * dts[q][:, h:h + 1] for n, (q, h) in enumerate(idx)]
    lhs = [(cbs[q][grp[n]] * jnp.exp(jnp.where(tri, ac[n] - acum_ts[q][h:h + 1, :], -1e30))).astype(BF16)
           for n, (q, h) in enumerate(idx)]
    xd16 = [v.astype(BF16) for v in xd]
    xdec16 = [(xd[n] * jnp.exp(alast[n] - ac[n])).astype(BF16) for n in range(len(idx))]
    s16 = [v.astype(BF16) for v in s_in]
    y_diag = [_dot(lhs[n], xd16[n]) for n in range(len(idx))]
    y_off = [_dot_nt(c16s[q][grp[n]], s16[n]) for n, (q, _) in enumerate(idx)]
    st = [_dot_tn(xdec16[n], b16s[q][grp[n]]) for n, (q, _) in enumerate(idx)]
    ys = [y_diag[n] + y_off[n] * jnp.exp(ac[n]) + x[n] * dpar[:, h:h + 1] for n, (_, h) in enumerate(idx)]
    s_new = [jnp.exp(alast[n]) * s_in[n] + st[n] for n in range(len(idx))]
    return [(ys[q * SSD_HEADS:(q + 1) * SSD_HEADS], s_new[q * SSD_HEADS:(q + 1) * SSD_HEADS])
            for q in range(len(seqs))]


def _ssd_load(pre_ref, dt_ref, s_ref, q):
    xs = [pre_ref[:, h * SSD_HEAD_DIM:(h + 1) * SSD_HEAD_DIM] for h in range(SSD_HEADS)]
    b0 = SSD_HEADS * SSD_HEAD_DIM
    bs = [pre_ref[:, b0 + g * SSD_STATE:b0 + (g + 1) * SSD_STATE] for g in range(SSD_GROUPS)]
    c0 = b0 + SSD_GROUPS * SSD_STATE
    cs = [pre_ref[:, c0 + g * SSD_STATE:c0 + (g + 1) * SSD_STATE] for g in range(SSD_GROUPS)]
    r0 = q * SSD_SROWS
    ss = [s_ref[r0 + h * SSD_HEAD_DIM:r0 + (h + 1) * SSD_HEAD_DIM, :] for h in range(SSD_HEADS)]
    return xs, bs, cs, dt_ref[:, 0:SSD_HEADS], ss


def ssd_fwd(pre, proj, dtb, alog, dpar, lp, *, name):
    m = pre.shape[0]
    bsz = m // lp
    nc = lp // CHUNK
    nq = SSD_SEQS
    assert bsz % nq == 0

    def body(pre_ref, dt_ref, dtb_ref, alog_ref, d_ref, y_ref, sin_ref, state):
        c = pl.program_id(1)

        @pl.when(c == 0)
        def _():
            state[...] = jnp.zeros_like(state)

        for q in range(nq):
            sin_ref[q] = state[q * SSD_SROWS:(q + 1) * SSD_SROWS, :]
        seqs = [_ssd_load(pre_ref.at[q], dt_ref.at[q], state, q) for q in range(nq)]
        valid = _row_valid(c, nc, CHUNK)
        res = _ssd_chunk(seqs, dtb_ref[...], alog_ref[...], d_ref[...], valid)
        for q, (ys, s_new) in enumerate(res):
            for h in range(SSD_HEADS):
                y_ref[q, :, h * SSD_HEAD_DIM:(h + 1) * SSD_HEAD_DIM] = ys[h]
                r0 = q * SSD_SROWS + h * SSD_HEAD_DIM
                state[r0:r0 + SSD_HEAD_DIM, :] = s_new[h]

    par = pl.BlockSpec((1, SSD_HEADS), lambda s, c: (0, 0))
    y, s_in = _call(
        body, name=name, out_shape=(S((bsz, lp, D_MODEL), F32), S((bsz, nc * SSD_SROWS, SSD_STATE), F32)),
        grid=(bsz // nq, nc),
        in_specs=[pl.BlockSpec((nq, CHUNK, SSD_CONV_DIM), lambda s, c: (s, c, 0)),
                  pl.BlockSpec((nq, CHUNK, 128), lambda s, c: (s, c, COL_DT // 128)), par, par, par],
        out_specs=(pl.BlockSpec((nq, CHUNK, D_MODEL), lambda s, c: (s, c, 0)),
                   pl.BlockSpec((nq, SSD_SROWS, SSD_STATE), lambda s, c: (s, c, 0))),
        scratch=[pltpu.VMEM((nq * SSD_SROWS, SSD_STATE), F32)],
        sem=("parallel", "arbitrary"))(pre.reshape(bsz, lp, -1), proj.reshape(bsz, lp, -1), dtb, alog, dpar)
    return y.reshape(m, D_MODEL), s_in


def ssd_bwd(pre, proj, s_in, dy, dtb, alog, dpar, lp, *, name):
    m = pre.shape[0]
    bsz = m // lp
    nc = lp // CHUNK
    nq = SSD_SEQS_BWD

    def body(pre_ref, dt_ref, sin_ref, dy_ref, dtb_ref, alog_ref, d_ref,
             dpre_ref, ddt_ref, ddtb_ref, dalog_ref, dd_ref, dstate):
        c = pl.program_id(1)

        @pl.when(jnp.logical_and(pl.program_id(0) == 0, c == 0))
        def _():
            for r in (ddtb_ref, dalog_ref, dd_ref):
                r[...] = jnp.zeros_like(r)

        @pl.when(c == 0)
        def _():
            dstate[...] = jnp.zeros_like(dstate)

        seqs = [_ssd_load(pre_ref.at[q], dt_ref.at[q], sin_ref.at[q], 0) for q in range(nq)]
        valid = _row_valid(nc - 1 - c, nc, CHUNK)
        core = functools.partial(_ssd_chunk, valid=valid)
        _, vjp = jax.vjp(core, seqs, dtb_ref[...], alog_ref[...], d_ref[...])
        cot = []
        for q in range(nq):
            dys = [dy_ref[q, :, h * SSD_HEAD_DIM:(h + 1) * SSD_HEAD_DIM] for h in range(SSD_HEADS)]
            r0 = q * SSD_SROWS
            dsn = [dstate[r0 + h * SSD_HEAD_DIM:r0 + (h + 1) * SSD_HEAD_DIM, :] for h in range(SSD_HEADS)]
            cot.append((dys, dsn))
        dseqs, ddtb, dalog, dd = vjp(cot)
        b0 = SSD_HEADS * SSD_HEAD_DIM
        c0 = b0 + SSD_GROUPS * SSD_STATE
        ddt_ref[...] = jnp.zeros_like(ddt_ref)
        for q, (dxs, dbs, dcs, ddtr, dss) in enumerate(dseqs):
            for h in range(SSD_HEADS):
                dpre_ref[q, :, h * SSD_HEAD_DIM:(h + 1) * SSD_HEAD_DIM] = dxs[h]
                r0 = q * SSD_SROWS + h * SSD_HEAD_DIM
                dstate[r0:r0 + SSD_HEAD_DIM, :] = dss[h]
            for g in range(SSD_GROUPS):
                dpre_ref[q, :, b0 + g * SSD_STATE:b0 + (g + 1) * SSD_STATE] = dbs[g]
                dpre_ref[q, :, c0 + g * SSD_STATE:c0 + (g + 1) * SSD_STATE] = dcs[g]
            ddt_ref[q, :, 0:SSD_HEADS] = ddtr.astype(BF16)
        ddtb_ref[...] += ddtb
        dalog_ref[...] += dalog
        dd_ref[...] += dd

    par = pl.BlockSpec((1, SSD_HEADS), lambda s, c: (0, 0))
    rev = lambda s, c: (s, nc - 1 - c, 0)
    pshape = S((1, SSD_HEADS), F32)
    dpre, ddt, ddtb, dalog, dd = _call(
        body, name=name,
        out_shape=(S((bsz, lp, SSD_CONV_DIM), F32), S((bsz, lp, 128), BF16), pshape, pshape, pshape),
        grid=(bsz // nq, nc),
        in_specs=[pl.BlockSpec((nq, CHUNK, SSD_CONV_DIM), rev),
                  pl.BlockSpec((nq, CHUNK, 128), lambda s, c: (s, nc - 1 - c, COL_DT // 128)),
                  pl.BlockSpec((nq, SSD_SROWS, SSD_STATE), rev), pl.BlockSpec((nq, CHUNK, D_MODEL), rev), par, par, par],
        out_specs=(pl.BlockSpec((nq, CHUNK, SSD_CONV_DIM), rev), pl.BlockSpec((nq, CHUNK, 128), rev), par, par, par),
        scratch=[pltpu.VMEM((nq * SSD_SROWS, SSD_STATE), F32)],
        sem=("arbitrary", "arbitrary"))(pre.reshape(bsz, lp, -1), proj.reshape(bsz, lp, -1), s_in,
                                        dy.reshape(bsz, lp, -1), dtb, alog, dpar)
    return dpre.reshape(m, SSD_CONV_DIM), ddt.reshape(m, 128), ddtb, dalog, dd


SB_KEYS = 256


def _order_mats():
    r = lax.broadcasted_iota(jnp.int32, (SB_KEYS, SB_KEYS), 0)
    c = lax.broadcasted_iota(jnp.int32, (SB_KEYS, SB_KEYS), 1)
    return (r > c).astype(BF16), (r < c).astype(BF16)


def _split_dot(x, mat):
    hi = x.astype(BF16)
    lo = (x - hi.astype(F32)).astype(BF16)
    return _dot(hi, mat) + _dot(lo, mat)


def _sb_tiles(qs_, ks_, blocks_, jt, diff, col, m_later, masked):
    zs = [_dot_nt(q_i, k_t) for q_i, k_t in zip(qs_, ks_)]
    out = []
    for z, i in zip(zs, blocks_):
        if masked:
            valid = jnp.logical_and(diff > jt * SB_KEYS - i * SB_BLOCK, col >= PAD - jt * SB_KEYS)
            lk = jnp.where(valid, -_softplus(z), 0.0)
        else:
            valid, lk = None, -_softplus(z)
        out.append((valid, z, lk))
    sums = [_split_dot(lk, m_later) for _, _, lk in out]
    return [(valid, z, lk, tsum, tsum[:, 0:1] + lk[:, 0:1]) for (valid, z, lk), tsum in zip(out, sums)]


def _sb_iotas():
    row = lax.broadcasted_iota(jnp.int32, (SB_BLOCK, SB_KEYS), 0)
    col = lax.broadcasted_iota(jnp.int32, (SB_BLOCK, SB_KEYS), 1)
    return row - col, col


def _sb_rows(i, size):
    start = i * size
    return pl.ds(start if isinstance(start, int) else pl.multiple_of(start, size), size)


def _sb_fill(dst, src_ref, ln, lp, scale=None):
    v = src_ref[:, ln]
    dst[0:lp, :] = (v if scale is None else v * scale).astype(BF16)
    if dst.shape[0] > lp:
        dst[lp:, :] = jnp.zeros((dst.shape[0] - lp, dst.shape[1]), BF16)


def _sb_schedule(nb, run_blocks):
    def pair(a, _):
        run_blocks([2 * a, 2 * a + 1], a + 1)
        return 0
    if nb >= 2:
        run_blocks([0, 1], 1)
    lax.fori_loop(1, nb // 2, pair, 0)
    if nb % 2:
        run_blocks([nb - 1], (nb + 1) // 2)


def _sb_sweep(ntiles, step, carry):
    carry = step(ntiles - 1, carry, True)
    if isinstance(ntiles, int) and ntiles == 1:
        return carry
    carry = lax.fori_loop(1, ntiles - 1, lambda jj, c: step(ntiles - 1 - jj, c, False), carry)
    return step(0, carry, True)


def attn_fwd(qkvg, lp, *, name):
    m = qkvg.shape[0]
    bsz = m // lp
    nb = lp // SB_BLOCK
    nkt = (nb + 1) // 2
    hd = SB_HEAD_DIM

    def body(q_ref, k_ref, v_ref, o_ref, qs, ks, vs):
        m_later, _ = _order_mats()
        diff, col = _sb_iotas()
        for hh in range(2):
            ln = slice(hh * hd, (hh + 1) * hd)
            _sb_fill(qs.at[hh], q_ref, ln, lp, hd ** -0.5)
            _sb_fill(ks.at[hh], k_ref, ln, lp)
            _sb_fill(vs.at[hh], v_ref, ln, lp)

        def run_blocks(blocks, ntiles):
            rows = [_sb_rows(i, SB_BLOCK) for i in blocks]
            chains = [(b, hh) for b in range(len(blocks)) for hh in range(2)]
            q = [qs[hh, rows[b], :] for b, hh in chains]
            blk_of = [blocks[b] for b, _ in chains]

            def tile_step(jt, carry, masked):
                cols = _sb_rows(jt, SB_KEYS)
                k_t = [ks[hh, cols, :] for hh in range(2)]
                v_t = [vs[hh, cols, :] for hh in range(2)]
                tiles = _sb_tiles(q, [k_t[hh] for _, hh in chains], blk_of, jt, diff, col, m_later, masked)
                ws = []
                for valid, z, lk, tsum, _ in tiles:
                    w = jnp.exp(z + lk + tsum)
                    ws.append((jnp.where(valid, w, 0.0) if masked else w).astype(BF16))
                pvs = [_dot(w, v_t[hh]) for w, (_, hh) in zip(ws, chains)]
                return tuple((acc + jnp.exp(run) * pv, run + tile[4])
                             for (acc, run), pv, tile in zip(carry, pvs, tiles))

            zero = (jnp.zeros((SB_BLOCK, hd), F32), jnp.zeros((SB_BLOCK, 1), F32))
            res = _sb_sweep(ntiles, tile_step, (zero,) * len(chains))
            for n, (b, hh) in enumerate(chains):
                o_ref[rows[b], hh * hd:(hh + 1) * hd] = res[n][0]

        _sb_schedule(nb, run_blocks)

    blk = lambda cb: pl.BlockSpec((lp, 128), lambda s, p: (s, cb * 8 + p))
    return _call(body, name=name, out_shape=S((m, D_MODEL), F32), grid=(bsz, 8),
                 in_specs=[blk(0), blk(1), blk(2)], out_specs=pl.BlockSpec((lp, 128), lambda s, p: (s, p)),
                 scratch=[pltpu.VMEM((2, lp, hd), BF16)] + [pltpu.VMEM((2, nkt * SB_KEYS, hd), BF16)] * 2,
                 sem=("parallel", "parallel"))(qkvg, qkvg, qkvg)


def attn_bwd(qkvg, do, lp, *, name):
    m = qkvg.shape[0]
    bsz = m // lp
    nb = lp // SB_BLOCK
    hd = SB_HEAD_DIM
    scale = hd ** -0.5

    nkt = (nb + 1) // 2

    def body(q_ref, k_ref, v_ref, do_ref, dq_ref, dk_ref, dv_ref, qs, ks, vs, dka, dva, g_keep, s_keep):
        m_later, m_earlier = _order_mats()
        diff, col = _sb_iotas()
        for hh in range(2):
            ln = slice(hh * hd, (hh + 1) * hd)
            _sb_fill(qs.at[hh], q_ref, ln, lp, scale)
            _sb_fill(ks.at[hh], k_ref, ln, lp)
            _sb_fill(vs.at[hh], v_ref, ln, lp)
        dka[...] = jnp.zeros_like(dka)
        dva[...] = jnp.zeros_like(dva)

        def run_blocks(blocks, ntiles):
            rows = [_sb_rows(i, SB_BLOCK) for i in blocks]
            chains = [(b, hh) for b in range(len(blocks)) for hh in range(2)]
            q = [qs[hh, rows[b], :] for b, hh in chains]
            do = [do_ref[rows[b], hh * hd:(hh + 1) * hd] for b, hh in chains]

            blk_of = [blocks[b] for b, _ in chains]
            heads = [hh for _, hh in chains]

            def sweep_left(jt, carry, masked):
                cols = _sb_rows(jt, SB_KEYS)
                k_t = [ks[hh, cols, :] for hh in range(2)]
                v_t = [vs[hh, cols, :] for hh in range(2)]
                tiles = _sb_tiles(q, [k_t[hh] for hh in heads], blk_of, jt, diff, col, m_later, masked)
                do_run = [(d * jnp.exp(run)).astype(BF16) for d, run in zip(do, carry)]
                dws = [_dot_nt(d, v_t[hh]) for d, hh in zip(do_run, heads)]
                ws = []
                for n, ((valid, z, lk, tsum, _), dw) in enumerate(zip(tiles, dws)):
                    sig = jnp.exp(z + lk)
                    if masked:
                        sig = jnp.where(valid, sig, 0.0)
                    w = sig * jnp.exp(tsum)
                    g_keep[n, jt] = dw * w
                    s_keep[n, jt] = sig
                    ws.append(w.astype(BF16))
                dvs = [_dot_tn(w, d) for w, d in zip(ws, do_run)]
                for n, hh in enumerate(heads):
                    dva[hh, cols, :] += dvs[n]
                return tuple(run + tile[4] for run, tile in zip(carry, tiles))

            _sb_sweep(ntiles, sweep_left, (jnp.zeros((SB_BLOCK, 1), F32),) * len(chains))

            def sweep_right(jt, carry):
                cols = _sb_rows(jt, SB_KEYS)
                k_t = [ks[hh, cols, :] for hh in range(2)]
                gmats = [g_keep[n, jt] for n in range(len(chains))]
                gsums = [_split_dot(gmat, m_earlier) for gmat in gmats]
                dzs = [(gmat - s_keep[n, jt] * (gmat + gsum + grun)).astype(BF16)
                       for n, (gmat, gsum, (_, grun)) in enumerate(zip(gmats, gsums, carry))]
                dqs = [_dot(dz, k_t[hh]) for dz, hh in zip(dzs, heads)]
                dks = [_dot_tn(dz, q_n) for dz, q_n in zip(dzs, q)]
                for n, hh in enumerate(heads):
                    dka[hh, cols, :] += dks[n]
                last = slice(SB_KEYS - 1, SB_KEYS)
                return tuple((dq + dqn, grun + gsum[:, last] + gmat[:, last])
                             for (dq, grun), dqn, gsum, gmat in zip(carry, dqs, gsums, gmats))

            zero = (jnp.zeros((SB_BLOCK, hd), F32), jnp.zeros((SB_BLOCK, 1), F32))
            res = lax.fori_loop(0, ntiles, sweep_right, (zero,) * len(chains))
            for n, (b, hh) in enumerate(chains):
                dq_ref[rows[b], hh * hd:(hh + 1) * hd] = (res[n][0] * scale).astype(BF16)

        _sb_schedule(nb, run_blocks)
        for hh in range(2):
            dk_ref[:, hh * hd:(hh + 1) * hd] = dka[hh, 0:lp, :].astype(BF16)
            dv_ref[:, hh * hd:(hh + 1) * hd] = dva[hh, 0:lp, :].astype(BF16)

    blk = lambda cb: pl.BlockSpec((lp, 128), lambda s, p: (s, cb * 8 + p))
    one = pl.BlockSpec((lp, 128), lambda s, p: (s, p))
    keys = nkt * SB_KEYS
    return _call(body, name=name, out_shape=(S((m, D_MODEL), BF16),) * 3, grid=(bsz, 8),
                 in_specs=[blk(0), blk(1), blk(2), one], out_specs=(one, one, one),
                 scratch=[pltpu.VMEM((2, lp, hd), BF16)] + [pltpu.VMEM((2, keys, hd), BF16)] * 2
                 + [pltpu.VMEM((2, keys, hd), F32)] * 2 + [pltpu.VMEM((4, nkt, SB_BLOCK, SB_KEYS), F32)] * 2,
                 sem=("parallel", "parallel"))(qkvg, qkvg, qkvg, do)


def meta_grad(dh, lp, *, name):
    m, d = dh.shape
    bsz = m // lp
    per = lp // N_META

    def body(dh_ref, o_ref):
        @pl.when(pl.program_id(0) == 0)
        def _():
            o_ref[...] = jnp.zeros_like(o_ref)
        o_ref[...] += dh_ref[...]

    return _call(body, name=name, out_shape=S((N_META, d), F32), grid=(bsz,),
                 in_specs=[pl.BlockSpec((N_META, d), lambda b: (b * per + PAD // N_META, 0))],
                 out_specs=pl.BlockSpec((N_META, d), lambda b: (0, 0)), sem=("arbitrary",))(dh)


def sum_lead(arr, *, name):
    n, r, c = arr.shape
    tr = _tile(r, (128, 64, 32, 16, 8))

    def body(a_ref, o_ref):
        acc = a_ref[0].astype(F32)
        for k in range(1, n):
            acc = acc + a_ref[k].astype(F32)
        o_ref[...] = acc

    return _call(body, name=name, out_shape=S((r, c), F32), grid=(r // tr,),
                 in_specs=[pl.BlockSpec((n, tr, c), lambda i: (0, i, 0))],
                 out_specs=pl.BlockSpec((tr, c), lambda i: (i, 0)), sem=("parallel",))(arr)


def adamw(w, g_parts, mom, var, *, name):
    r, c = w.shape
    tr = _tile(r, (128, 64, 32, 16, 8))
    n_g = len(g_parts)
    c1 = 1.0 - ADAM_B1 ** ADAM_STEP
    c2 = 1.0 - ADAM_B2 ** ADAM_STEP

    def body(*refs):
        w_ref, g_refs, m_ref, v_ref = refs[0], refs[1:1 + n_g], refs[1 + n_g], refs[2 + n_g]
        g_out, d_out, m_out, v_out = refs[3 + n_g:]
        g = g_refs[0][...]
        for gr in g_refs[1:]:
            g = g + gr[...]
        mn = ADAM_B1 * m_ref[...] + (1.0 - ADAM_B1) * g
        vn = ADAM_B2 * v_ref[...] + (1.0 - ADAM_B2) * (g * g)
        g_out[...] = g
        m_out[...] = mn
        v_out[...] = vn
        d_out[...] = -ADAM_LR * ((mn / c1) / (jnp.sqrt(vn / c2) + ADAM_EPS) + ADAM_WD * w_ref[...])

    blk = pl.BlockSpec((tr, c), lambda i: (i, 0))
    return _call(body, name=name, out_shape=(S((r, c), F32),) * 4, grid=(r // tr,), in_specs=[blk] * (3 + n_g),
                 out_specs=(blk,) * 4, sem=("parallel",))(w, *g_parts, mom, var)


_ANY = pl.BlockSpec(memory_space=pl.ANY)


def _position():
    return lax.axis_index("x"), lax.axis_index("y"), lax.axis_index("c")


def _other_chips(x, y):
    return [(1 - x, y), (x, 1 - y), (1 - x, 1 - y)]


def _comm_call(body, arrs, out_shapes, n_sem, *, name):
    return pl.pallas_call(
        body, out_shape=tuple(out_shapes), in_specs=[_ANY] * len(arrs), out_specs=tuple([_ANY] * len(out_shapes)),
        scratch_shapes=(pltpu.SemaphoreType.DMA((n_sem,)), pltpu.SemaphoreType.DMA((n_sem,)),
                        pltpu.SemaphoreType.DMA((len(arrs),))),
        name=name)(*arrs)


def allgather_chips(arrs, *, name):
    n = len(arrs)

    def body(*refs):
        ins, outs = refs[:n], refs[n:2 * n]
        send_sems, recv_sems, loc_sems = refs[2 * n:]
        x, y, c = _position()
        me = 2 * x + y
        chips = _other_chips(x, y)

        def half(ref, a, which):
            rows = arrs[a].shape[0] // 2
            return ref.at[pl.ds(which * rows, rows)]

        sent, passed = [], []
        for a in range(n):
            for k, (px, py) in enumerate(chips):
                cp = pltpu.make_async_remote_copy(
                    src_ref=half(ins[a], a, c), dst_ref=half(outs[a].at[me], a, c), send_sem=send_sems.at[6 * a + k],
                    recv_sem=recv_sems.at[6 * a + k], device_id=(px, py, c), device_id_type=MESH)
                cp.start()
                sent.append(cp)
        for a in range(n):
            for k, (px, py) in enumerate(chips):
                sent[3 * a + k].wait_recv()
                landed = half(outs[a].at[2 * px + py], a, c)
                cp = pltpu.make_async_remote_copy(
                    src_ref=landed, dst_ref=landed, send_sem=send_sems.at[6 * a + 3 + k],
                    recv_sem=recv_sems.at[6 * a + 3 + k], device_id=(x, y, 1 - c), device_id_type=MESH)
                cp.start()
                passed.append(cp)
        for cp in sent:
            cp.wait_send()
        for cp in passed:
            cp.wait()

    for a in arrs:
        assert a.shape[0] % 32 == 0
    outs = _comm_call(body, arrs, [S((4,) + a.shape, a.dtype) for a in arrs], 6 * n, name=name)
    chip = 2 * lax.axis_index("x") + lax.axis_index("y")
    return [lax.dynamic_update_index_in_dim(o, a, chip, 0) for o, a in zip(outs, arrs)]


def exchange_chips(arrs, small, *, name):
    n = len(arrs)

    def body(*refs):
        ins, small_in = refs[:n], refs[n]
        outs, small_out = refs[n + 1:2 * n + 1], refs[2 * n + 1]
        send_sems, recv_sems, loc_sems = refs[2 * n + 2:]
        x, y, c = _position()
        me = 2 * x + y
        copies = []
        for a in range(n):
            for k, (px, py) in enumerate(_other_chips(x, y)):
                cp = pltpu.make_async_remote_copy(
                    src_ref=ins[a].at[2 * px + py], dst_ref=outs[a].at[me], send_sem=send_sems.at[3 * a + k],
                    recv_sem=recv_sems.at[3 * a + k], device_id=(px, py, c), device_id_type=MESH)
                cp.start()
                copies.append(cp)
        me8 = 4 * x + 2 * y + c
        k = 3 * n
        for fx in (0, 1):
            for fy in (0, 1):
                for fc in (0, 1):
                    if fx + fy + fc == 0:
                        continue
                    peer = (1 - x if fx else x, 1 - y if fy else y, 1 - c if fc else c)
                    cp = pltpu.make_async_remote_copy(
                        src_ref=small_in, dst_ref=small_out.at[me8], send_sem=send_sems.at[k],
                        recv_sem=recv_sems.at[k], device_id=peer, device_id_type=MESH)
                    cp.start()
                    copies.append(cp)
                    k += 1
        for cp in copies:
            cp.wait()

    outs = [S(a.shape, a.dtype) for a in arrs] + [S((8,) + small.shape, small.dtype)]
    outs = _comm_call(body, list(arrs) + [small], outs, 3 * n + 7, name=name)
    x, y, c = _position()
    chip = 2 * x + y
    res = [lax.dynamic_update_index_in_dim(o, lax.dynamic_index_in_dim(a, chip, 0, keepdims=False), chip, 0)
           for o, a in zip(outs[:-1], arrs)]
    return res + [lax.dynamic_update_index_in_dim(outs[-1], small, 4 * x + 2 * y + c, 0)]


def swap_cores(arrs, *, name):
    n = len(arrs)

    def body(*refs):
        ins, outs = refs[:n], refs[n:2 * n]
        send_sems, recv_sems, _ = refs[2 * n:]
        x, y, c = _position()
        copies = []
        for a in range(n):
            cp = pltpu.make_async_remote_copy(
                src_ref=ins[a], dst_ref=outs[a], send_sem=send_sems.at[a], recv_sem=recv_sems.at[a],
                device_id=(x, y, 1 - c), device_id_type=MESH)
            cp.start()
            copies.append(cp)
        for cp in copies:
            cp.wait()

    return _comm_call(body, arrs, [S(a.shape, a.dtype) for a in arrs], n, name=name)


def pair_add(a, b, *, name):
    k, r, c = a.shape
    tr = _tile(r, (256, 128))

    def body(a_ref, b_ref, o_ref):
        o_ref[...] = (a_ref[...].astype(F32) + b_ref[...].astype(F32)).astype(BF16)

    blk = pl.BlockSpec((1, tr, c), lambda j, i: (j, i, 0))
    return _call(body, name=name, out_shape=S((k, r, c), BF16), grid=(k, r // tr), in_specs=[blk, blk], out_specs=blk,
                 sem=("parallel", "parallel"))(a, b)


def _local_step(p, x, target):
    bsz, seq, d = x.shape
    lp = LEAD + seq
    m = bsz * lp
    h0 = jnp.concatenate([jnp.zeros((bsz, PAD, d), F32), jnp.broadcast_to(p["meta"][None], (bsz, N_META, d)), x],
                         axis=1).reshape(m, d)
    u0 = rmsnorm_fwd(h0, p["even_norm"], name="norm0")
    proj = mm_nn([u0], p["win_e"], name="proj0")
    lx = conv_fwd(proj, COL_LRU_X, D_MODEL, p["lru_conv_w"], p["lru_conv_b"], name="lru_conv")
    rowmask = jnp.tile((jnp.arange(lp) >= PAD).astype(F32), bsz).reshape(m, 1)
    a, b = gates_fwd(lx, rowmask, p["lru_w_a"], p["lru_b_a"], p["lru_w_x"], p["lru_b_x"], p["lru_lambda"],
                     name="lru_gates")
    hs = scan_fwd(a, b, lp, name="lru_scan")
    ya = gate_fwd(hs, 0, proj, COL_LRU_G // D_MODEL, name="lru_out_gate", o_scan=True)
    pre = conv_fwd(proj, COL_XBC, SSD_CONV_DIM, p["ssd_conv_w"], p["ssd_conv_b"], name="ssd_conv")
    y, s_in = ssd_fwd(pre, proj, p["ssd_dt_bias"], p["ssd_a_log"], p["ssd_d"], lp, name="ssd")
    yb = gnorm_fwd(y, proj, COL_Z // D_MODEL, p["ssd_norm"], name="ssd_norm")
    h1 = mm_nn([ya, yb], p["wout_e"], resid=h0, name="out0")
    u1 = rmsnorm_fwd(h1, p["odd_norm"], name="norm1")
    qkvg = mm_nn([u1], p["win_o"], name="proj1")
    o = attn_fwd(qkvg, lp, name="attn")
    og = gate_fwd(o, 0, qkvg, 3, name="attn_gate")
    h2 = mm_nn([og], p["wout_o"], resid=h1, name="out1")
    dh2, dh2b, loss, d_final = loss_head(h2, target, p["final_norm"], lp, name="loss_head")
    g = {"final_norm": d_final}
    g["odd_w_out"] = mm_tn(og, dh2b, name="dw_out1")
    d_og = mm_nn([dh2b], p["wout_o_t"], name="d_out1")
    do, dgate = gate_bwd(d_og, 0, o, 0, qkvg, 3, name="attn_gate_bwd")
    dq, dk, dv = attn_bwd(qkvg, do, lp, name="attn_bwd")
    segs1 = [dq, dk, dv, dgate]
    du1 = mm_nn(segs1, p["win_o_t"], name="d_proj1")
    g["odd_w_in"] = jnp.concatenate([mm_tn(u1, t, name=f"dw_proj1_{k}") for k, t in enumerate(segs1)], axis=1)
    dh1, dh1b, g["odd_norm"] = rmsnorm_bwd(du1, h1, p["odd_norm"], dh2, name="norm1_bwd")
    g["even_w_out"] = jnp.concatenate([mm_tn(ya, dh1b, name="dw_out0_a"), mm_tn(yb, dh1b, name="dw_out0_b")], axis=0)
    d_mixed = mm_nn([dh1b], p["wout_e_t"], name="d_out0")
    dhs, dlg = gate_bwd(d_mixed, 0, hs, 0, proj, COL_LRU_G // D_MODEL, name="lru_out_gate_bwd", o_scan=True)
    dy, dz, g["ssd_norm"] = gnorm_bwd(d_mixed, 1, y, proj, COL_Z // D_MODEL, p["ssd_norm"], name="ssd_norm_bwd")
    dpre, ddt, g["ssd_dt_bias"], g["ssd_a_log"], g["ssd_d"] = ssd_bwd(
        pre, proj, s_in, dy, p["ssd_dt_bias"], p["ssd_a_log"], p["ssd_d"], lp, name="ssd_bwd")
    dxbc, g["ssd_conv_w"], g["ssd_conv_b"] = conv_bwd(proj, COL_XBC, SSD_CONV_DIM, dpre, p["ssd_conv_w"],
                                                      name="ssd_conv_bwd")
    da, db = scan_bwd(a, hs, dhs, lp, name="lru_scan_bwd")
    dlx, g["lru_w_a"], g["lru_w_x"], g["lru_b_a"], g["lru_b_x"], g["lru_lambda"] = gates_bwd(
        lx, rowmask, da, db, p["lru_w_a"], p["lru_b_a"], p["lru_w_x"], p["lru_b_x"], p["lru_lambda"],
        name="lru_gates_bwd")
    dlrux, g["lru_conv_w"], g["lru_conv_b"] = conv_bwd(proj, COL_LRU_X, D_MODEL, dlx, p["lru_conv_w"],
                                                       name="lru_conv_bwd")
    segs0 = [dlrux, dlg, dz, dxbc, ddt]
    du0 = mm_nn(segs0, p["win_e_t"], name="d_proj0")
    g["even_w_in"] = jnp.concatenate([mm_tn(u0, t, name=f"dw_proj0_{k}") for k, t in enumerate(segs0)],
                                     axis=1)[:, :EVEN_IN]
    dh0, _, g["even_norm"] = rmsnorm_bwd(du0, h0, p["even_norm"], dh1, name="norm0_bwd")
    g["meta"] = meta_grad(dh0, lp, name="meta_grad")
    grad_x = dh0.reshape(bsz, lp, d)[:, LEAD:]
    return loss, grad_x, g


WEIGHTS = ("meta", "even_norm", "even_w_in", "lru_conv_w", "lru_conv_b", "lru_w_a", "lru_b_a", "lru_w_x", "lru_b_x",
           "lru_lambda", "ssd_conv_w", "ssd_conv_b", "ssd_dt_bias", "ssd_a_log", "ssd_d", "ssd_norm", "even_w_out",
           "odd_norm", "odd_w_in", "odd_w_out", "final_norm")
BIG = ("even_w_in", "even_w_out", "odd_w_in", "odd_w_out", "lru_w_a", "lru_w_x")
SHARDED_SMALL = {"meta": 256, "lru_conv_w": 256, "ssd_conv_w": 384, "odd_norm": 256}
SMALL_SHAPES = {"meta": (16, 1024), "even_norm": (1, 1024), "lru_conv_w": (4, 1024), "lru_conv_b": (1, 1024),
                "lru_b_a": (1, 1024), "lru_b_x": (1, 1024), "lru_lambda": (1, 1024), "ssd_conv_w": (4, 1536),
                "ssd_conv_b": (1, 1536), "ssd_dt_bias": (1, 16), "ssd_a_log": (1, 16), "ssd_d": (1, 16),
                "ssd_norm": (1, 1024), "odd_norm": (1, 1024), "final_norm": (1, 1024)}
PACK_UNIT = 1024


def _pack(parts):
    flat = []
    for part in parts:
        v = part.reshape(-1)
        flat.append(jnp.pad(v, (0, -v.shape[0] % PACK_UNIT)))
    return jnp.concatenate(flat).reshape(-1, 128)


def _unpack(buf, shapes):
    v = buf.reshape(-1)
    out, off = [], 0
    for shp in shapes:
        n = 1
        for s_ in shp:
            n *= s_
        out.append(v[off:off + n].reshape(shp))
        off += n + (-n % PACK_UNIT)
    return out


def _chip_cols(a4):
    return jnp.transpose(a4, (1, 0, 2)).reshape(a4.shape[1], -1)


def _to_chip_cols(a, cols):
    return jnp.transpose(a.reshape(a.shape[0], 4, cols), (1, 0, 2))


def kernel(x, meta, even_norm, even_w_in, lru_conv_w, lru_conv_b, lru_w_a, lru_b_a, lru_w_x, lru_b_x, lru_lambda, ssd_conv_w, ssd_conv_b, ssd_dt_bias, ssd_a_log, ssd_d, ssd_norm, even_w_out, odd_norm, odd_w_in, odd_w_out, final_norm, loss_target, m_meta, m_even_norm, m_even_w_in, m_lru_conv_w, m_lru_conv_b, m_lru_w_a, m_lru_b_a, m_lru_w_x, m_lru_b_x, m_lru_lambda, m_ssd_conv_w, m_ssd_conv_b, m_ssd_dt_bias, m_ssd_a_log, m_ssd_d, m_ssd_norm, m_even_w_out, m_odd_norm, m_odd_w_in, m_odd_w_out, m_final_norm, v_meta, v_even_norm, v_even_w_in, v_lru_conv_w, v_lru_conv_b, v_lru_w_a, v_lru_b_a, v_lru_w_x, v_lru_b_x, v_lru_lambda, v_ssd_conv_w, v_ssd_conv_b, v_ssd_dt_bias, v_ssd_a_log, v_ssd_d, v_ssd_norm, v_even_w_out, v_odd_norm, v_odd_w_in, v_odd_w_out, v_final_norm):
    given = dict(locals())
    w = {n: given[n] for n in WEIGHTS}
    mom = {n: given["m_" + n] for n in WEIGHTS}
    var = {n: given["v_" + n] for n in WEIGHTS}
    chip = 2 * lax.axis_index("x") + lax.axis_index("y")

    big_local = {"even_w_in": even_w_in[0], "even_w_out": even_w_out[0], "odd_w_in": odd_w_in[0],
                 "odd_w_out": odd_w_out[0], "lru_w_a": lru_w_a[0].reshape(256, 256),
                 "lru_w_x": lru_w_x[0].reshape(256, 256)}
    sharded_local = [meta, lru_conv_w[0], ssd_conv_w[0], odd_norm]
    gathered = allgather_chips([big_local[n].astype(BF16) for n in BIG] + [_pack(sharded_local)], name="gather_weights")
    gb = dict(zip(BIG, gathered[:-1]))
    per_chip = [_unpack(gathered[-1][k], [a.shape for a in sharded_local]) for k in range(4)]
    full_small = [jnp.concatenate([per_chip[k][j] for k in range(4)], axis=-1) for j in range(len(sharded_local))]

    def lru_full(a4):
        return jnp.transpose(a4.reshape(4, LRU_BLOCKS, 64, LRU_BLOCK), (1, 0, 2, 3)).reshape(LRU_BLOCKS, LRU_BLOCK, LRU_BLOCK)

    p = {"meta": full_small[0], "lru_conv_w": full_small[1], "ssd_conv_w": full_small[2], "odd_norm": full_small[3],
         "even_norm": even_norm, "lru_conv_b": lru_conv_b, "lru_b_a": lru_b_a, "lru_b_x": lru_b_x,
         "lru_lambda": lru_lambda, "ssd_conv_b": ssd_conv_b, "ssd_dt_bias": ssd_dt_bias, "ssd_a_log": ssd_a_log,
         "ssd_d": ssd_d, "ssd_norm": ssd_norm, "final_norm": final_norm.reshape(1, D_MODEL)}
    p["win_e"] = jnp.pad(_chip_cols(gb["even_w_in"]), ((0, 0), (0, EVEN_IN_P - EVEN_IN)))
    p["wout_e"] = gb["even_w_out"].reshape(2 * D_MODEL, D_MODEL)
    p["win_o"] = _chip_cols(gb["odd_w_in"])
    p["wout_o"] = gb["odd_w_out"].reshape(D_MODEL, D_MODEL)
    for n in ("win_e", "wout_e", "win_o", "wout_o"):
        p[n + "_t"] = p[n].T
    p["lru_w_a"] = lru_full(gb["lru_w_a"])
    p["lru_w_x"] = lru_full(gb["lru_w_x"])

    loss_part, grad_x, g = _local_step(p, x, loss_target)

    def lru_slabs(a):
        return jnp.transpose(a.reshape(LRU_BLOCKS, 4, 64, LRU_BLOCK), (1, 0, 2, 3)).reshape(4, 256, LRU_BLOCK)

    def by_half(t):
        return jnp.transpose(t.reshape(4, 2, t.shape[1] // 2, t.shape[2]), (1, 0, 2, 3)).astype(BF16)

    slabs = [_to_chip_cols(g["even_w_in"], EVEN_IN // 4), g["even_w_out"].reshape(4, 512, D_MODEL),
             _to_chip_cols(g["odd_w_in"], D_MODEL), g["odd_w_out"].reshape(4, 256, D_MODEL),
             lru_slabs(g["lru_w_a"]), lru_slabs(g["lru_w_x"])]
    small_names = list(SMALL_SHAPES)
    small_part = _pack([loss_part[0:1, 0:1]] + [g[n] for n in small_names])
    core = lax.axis_index("c")
    halves = [by_half(t) for t in slabs]
    mine = [lax.dynamic_index_in_dim(t, core, 0, keepdims=False) for t in halves]
    theirs = swap_cores([lax.dynamic_index_in_dim(t, 1 - core, 0, keepdims=False) for t in halves], name="swap_partials")
    pair_sums = [pair_add(a, b, name=f"pair_add_{n}") for a, b, n in zip(mine, theirs, BIG)]
    *recv, small_all = exchange_chips(pair_sums, small_part, name="exchange_grads")
    half_sums = [sum_lead(r, name=f"sum_chips_{n}") for n, r in zip(BIG, recv)]
    other_half = swap_cores(half_sums, name="swap_halves")
    full_sums = [jnp.where(core == 0, jnp.concatenate([a, b], axis=0), jnp.concatenate([b, a], axis=0))
                 for a, b in zip(half_sums, other_half)]
    small_sum = sum_lead(small_all, name="sum_small")
    small_g = dict(zip(["loss"] + small_names, _unpack(small_sum, [(1, 1)] + [SMALL_SHAPES[n] for n in small_names])))
    loss = small_g["loss"].reshape(())

    grads, delta, new_m, new_v = {}, {}, {}, {}
    for n, total in zip(BIG, full_sums):
        shp = w[n].shape
        two_d = lambda t: t.reshape(total.shape)
        res = adamw(two_d(w[n]), [total], two_d(mom[n]), two_d(var[n]), name=f"adamw_{n}")
        grads[n], delta[n], new_m[n], new_v[n] = (r.reshape(shp) for r in res)
    local_g = []
    for n in small_names:
        gn = small_g[n]
        if n in SHARDED_SMALL:
            gn = lax.dynamic_slice_in_dim(gn, chip * SHARDED_SMALL[n], SHARDED_SMALL[n], axis=1)
        local_g.append(gn)
    res = adamw(_pack([w[n] for n in small_names]), [_pack(local_g)], _pack([mom[n] for n in small_names]),
                _pack([var[n] for n in small_names]), name="adamw_small")
    shapes = [w[n].shape for n in small_names]
    for out, r in zip((grads, delta, new_m, new_v), res):
        out.update(dict(zip(small_names, _unpack(r, shapes))))
    return (loss, grad_x, *[grads[n] for n in WEIGHTS], *[delta[n] for n in WEIGHTS], *[new_m[n] for n in WEIGHTS],
            *[new_v[n] for n in WEIGHTS])
```

```python
import functools

import jax
import jax.numpy as jnp
from jax import lax
from jax.experimental import pallas as pl
from jax.experimental.pallas import tpu as pltpu

F32 = jnp.float32
BF16 = jnp.bfloat16

D_MODEL = 1024
N_META = 16
LEAD = 128
PAD = LEAD - N_META
EPS = 1e-6
CONV_W = 4
LRU_BLOCKS = 4
LRU_BLOCK = 256
RG_LRU_C = 8.0
SSD_HEADS = 16
SSD_HEAD_DIM = 64
SSD_GROUPS = 2
SSD_HPG = 8
SSD_STATE = 128
CHUNK = 128
SSD_CONV_DIM = 1536
EVEN_IN = 4624
EVEN_IN_P = 4736
COL_LRU_X, COL_LRU_G, COL_Z, COL_XBC, COL_DT = 0, 1024, 2048, 3072, 4608
SB_HEADS = 16
SB_HEAD_DIM = 64
SB_BLOCK = 128
ADAM_LR, ADAM_B1, ADAM_B2, ADAM_EPS, ADAM_WD, ADAM_STEP = 0.001, 0.9, 0.999, 1e-08, 0.01, 10
VMEM_LIMIT_V7X = 56 * 1024 * 1024
MESH = pl.DeviceIdType.MESH
S = jax.ShapeDtypeStruct


def _tile(n, prefs):
    for p in prefs:
        if n % p == 0:
            return p
    raise ValueError(f"no tile of {prefs} divides {n}")


def _call(body, *, name, out_shape, grid=(), in_specs=None, out_specs=None, scratch=(), sem=None):
    kw = {}
    if in_specs is not None:
        kw["in_specs"] = in_specs
    if out_specs is not None:
        kw["out_specs"] = out_specs
    return pl.pallas_call(
        body, out_shape=out_shape, grid=grid, scratch_shapes=tuple(scratch), name=name,
        compiler_params=pltpu.CompilerParams(dimension_semantics=sem, vmem_limit_bytes=VMEM_LIMIT_V7X), **kw)


def _sigmoid(x):
    return 0.5 * (jnp.tanh(0.5 * x) + 1.0)


def _silu(x):
    return x * _sigmoid(x)


def _softplus(x):
    return jnp.maximum(x, 0.0) + jnp.log(1.0 + jnp.exp(-jnp.abs(x)))


def _dot(a, b):
    return jnp.dot(a, b, preferred_element_type=F32)


def _dot_nt(a, b):
    return lax.dot_general(a, b, (((1,), (1,)), ((), ())), preferred_element_type=F32)


def _dot_tn(a, b):
    return lax.dot_general(a, b, (((0,), (0,)), ((), ())), preferred_element_type=F32)


def mm_nn(a_list, w, *, name, resid=None):
    m = a_list[0].shape[0]
    k_tot, n = w.shape
    ks = [a.shape[1] for a in a_list]
    assert sum(ks) == k_tot
    tm = _tile(m, (256, 128))
    n_a = len(a_list)
    offs = [sum(ks[:i]) for i in range(n_a)]
    n_chunks = [(c0, min(512, n - c0)) for c0 in range(0, n, 512)]

    def body(*refs):
        a_refs, w_ref = refs[:n_a], refs[n_a]
        r_ref = refs[n_a + 1] if resid is not None else None
        o_ref = refs[-1]
        for c0, cw in n_chunks:
            acc = None
            for a_ref, k0, k in zip(a_refs, offs, ks):
                p = _dot(a_ref[...], w_ref[k0:k0 + k, c0:c0 + cw])
                acc = p if acc is None else acc + p
            if r_ref is not None:
                acc = acc + r_ref[:, c0:c0 + cw]
            o_ref[:, c0:c0 + cw] = acc

    in_specs = [pl.BlockSpec((tm, k), lambda i: (i, 0)) for k in ks]
    in_specs.append(pl.BlockSpec((k_tot, n), lambda i: (0, 0)))
    args = list(a_list) + [w]
    if resid is not None:
        in_specs.append(pl.BlockSpec((tm, n), lambda i: (i, 0)))
        args.append(resid)
    return _call(body, name=name, out_shape=S((m, n), F32), grid=(m // tm,), in_specs=in_specs,
                 out_specs=pl.BlockSpec((tm, n), lambda i: (i, 0)), sem=("parallel",))(*args)


def mm_tn(a, g, *, name):
    t, m = a.shape
    n = g.shape[1]
    tk = _tile(t, (512, 256, 128))
    tn = _tile(n, (1024, 512, 256, 128))

    def body(a_ref, g_ref, o_ref):
        @pl.when(pl.program_id(1) == 0)
        def _():
            o_ref[...] = jnp.zeros_like(o_ref)
        o_ref[...] += _dot_tn(a_ref[...], g_ref[...])

    return _call(body, name=name, out_shape=S((m, n), F32), grid=(n // tn, t // tk),
                 in_specs=[pl.BlockSpec((tk, m), lambda j, k: (k, 0)), pl.BlockSpec((tk, tn), lambda j, k: (k, j))],
                 out_specs=pl.BlockSpec((m, tn), lambda j, k: (0, j)), sem=("parallel", "arbitrary"))(a, g)


def rmsnorm_fwd(h, w, *, name):
    m, d = h.shape
    tm = _tile(m, (512, 256, 128))

    def body(h_ref, w_ref, o_ref):
        x = h_ref[...]
        r = lax.rsqrt(jnp.mean(x * x, axis=-1, keepdims=True) + EPS)
        o_ref[...] = (x * r * w_ref[...]).astype(BF16)

    return _call(body, name=name, out_shape=S((m, d), BF16), grid=(m // tm,),
                 in_specs=[pl.BlockSpec((tm, d), lambda i: (i, 0)), pl.BlockSpec((1, d), lambda i: (0, 0))],
                 out_specs=pl.BlockSpec((tm, d), lambda i: (i, 0)), sem=("parallel",))(h, w)


def rmsnorm_bwd(du, h, w, dres, *, name):
    m, d = h.shape
    tm = _tile(m, (256, 128))

    def body(du_ref, h_ref, w_ref, dr_ref, dh_ref, dhb_ref, dw_ref):
        @pl.when(pl.program_id(0) == 0)
        def _():
            dw_ref[...] = jnp.zeros_like(dw_ref)
        x = h_ref[...]
        r = lax.rsqrt(jnp.mean(x * x, axis=-1, keepdims=True) + EPS)
        du_ = du_ref[...]
        g = du_ * w_ref[...]
        c = jnp.mean(g * x, axis=-1, keepdims=True)
        dh = dr_ref[...] + r * g - x * (r * r * r) * c
        dh_ref[...] = dh
        dhb_ref[...] = dh.astype(BF16)
        dw_ref[...] += jnp.sum(du_ * x * r, axis=0, keepdims=True)

    row = pl.BlockSpec((tm, d), lambda i: (i, 0))
    vec = pl.BlockSpec((1, d), lambda i: (0, 0))
    return _call(body, name=name, out_shape=(S((m, d), F32), S((m, d), BF16), S((1, d), F32)), grid=(m // tm,),
                 in_specs=[row, row, vec, row], out_specs=(row, row, vec), sem=("arbitrary",))(du, h, w, dres)


def gate_fwd(o, o_cb, g, g_cb, *, name, o_scan=False):
    m = g.shape[0]
    d = D_MODEL
    tm = _tile(m, (256, 128))

    def body(o_ref, g_ref, y_ref):
        ov = _from_scan_layout(o_ref, 0, d, tm) if o_scan else o_ref[...]
        y_ref[...] = (ov * _silu(g_ref[...])).astype(BF16)

    o_spec = pl.BlockSpec((tm * 8, 128), lambda i: (i, 0)) if o_scan else pl.BlockSpec((tm, d), lambda i: (i, o_cb))
    return _call(body, name=name, out_shape=S((m, d), BF16), grid=(m // tm,),
                 in_specs=[o_spec, pl.BlockSpec((tm, d), lambda i: (i, g_cb))],
                 out_specs=pl.BlockSpec((tm, d), lambda i: (i, 0)), sem=("parallel",))(o, g)


def gate_bwd(dy, dy_cb, o, o_cb, g, g_cb, *, name, o_scan=False):
    m = g.shape[0]
    d = D_MODEL
    tm = _tile(m, (256, 128))

    def body(dy_ref, o_ref, g_ref, do_ref, dg_ref):
        gv = g_ref[...]
        s = _sigmoid(gv)
        dyv = dy_ref[...]
        do = dyv * gv * s
        if o_scan:
            _to_scan_layout(do_ref, do, 0, tm)
            ov = _from_scan_layout(o_ref, 0, d, tm)
        else:
            do_ref[...] = do
            ov = o_ref[...]
        dg_ref[...] = (dyv * ov * (s + gv * s * (1.0 - s))).astype(BF16)

    nat = pl.BlockSpec((tm, d), lambda i: (i, 0))
    scn = pl.BlockSpec((tm * 8, 128), lambda i: (i, 0))
    o_spec = scn if o_scan else pl.BlockSpec((tm, d), lambda i: (i, o_cb))
    do_shape = S((m * 8, 128), F32) if o_scan else S((m, d), F32)
    return _call(body, name=name, out_shape=(do_shape, S((m, d), BF16)), grid=(m // tm,),
                 in_specs=[pl.BlockSpec((tm, d), lambda i: (i, dy_cb)), o_spec, pl.BlockSpec((tm, d), lambda i: (i, g_cb))],
                 out_specs=(scn if o_scan else nat, nat), sem=("parallel",))(dy, o, g)


def _group_mean(x):
    half = x.shape[1] // SSD_GROUPS
    parts = [jnp.broadcast_to(jnp.mean(x[:, k * half:(k + 1) * half], axis=-1, keepdims=True), (x.shape[0], half))
             for k in range(SSD_GROUPS)]
    return jnp.concatenate(parts, axis=1)


def gnorm_fwd(y, z, z_cb, w, *, name):
    m = y.shape[0]
    d = D_MODEL
    tm = _tile(m, (256, 128))

    def body(y_ref, z_ref, w_ref, o_ref):
        g = y_ref[...] * _silu(z_ref[...])
        r = lax.rsqrt(_group_mean(g * g) + EPS)
        o_ref[...] = (g * r * w_ref[...]).astype(BF16)

    return _call(body, name=name, out_shape=S((m, d), BF16), grid=(m // tm,),
                 in_specs=[pl.BlockSpec((tm, d), lambda i: (i, 0)), pl.BlockSpec((tm, d), lambda i: (i, z_cb)),
                           pl.BlockSpec((1, d), lambda i: (0, 0))],
                 out_specs=pl.BlockSpec((tm, d), lambda i: (i, 0)), sem=("parallel",))(y, z, w)


def gnorm_bwd(do, do_cb, y, z, z_cb, w, *, name):
    m = y.shape[0]
    d = D_MODEL
    tm = _tile(m, (256, 128))

    def body(do_ref, y_ref, z_ref, w_ref, dy_ref, dz_ref, dw_ref):
        @pl.when(pl.program_id(0) == 0)
        def _():
            dw_ref[...] = jnp.zeros_like(dw_ref)
        yv, zv, dov = y_ref[...], z_ref[...], do_ref[...]
        s = _sigmoid(zv)
        sz = zv * s
        g = yv * sz
        r = lax.rsqrt(_group_mean(g * g) + EPS)
        dw_ref[...] += jnp.sum(dov * g * r, axis=0, keepdims=True)
        dn = dov * w_ref[...]
        dg = r * dn - g * (r * r * r) * _group_mean(dn * g)
        dy_ref[...] = dg * sz
        dz_ref[...] = (dg * yv * (s + zv * s * (1.0 - s))).astype(BF16)

    row = pl.BlockSpec((tm, d), lambda i: (i, 0))
    vec = pl.BlockSpec((1, d), lambda i: (0, 0))
    return _call(body, name=name, out_shape=(S((m, d), F32), S((m, d), BF16), S((1, d), F32)), grid=(m // tm,),
                 in_specs=[pl.BlockSpec((tm, d), lambda i: (i, do_cb)), row, pl.BlockSpec((tm, d), lambda i: (i, z_cb)), vec],
                 out_specs=(row, row, vec), sem=("arbitrary",))(do, y, z, w)


def loss_head(h, target, w, lp, *, name):
    m, d = h.shape
    bsz = m // lp
    tm = SB_BLOCK
    nblk = lp // tm
    lead_blk = LEAD // tm

    def body(h_ref, t_ref, w_ref, dh_ref, dhb_ref, l_ref, dw_ref):
        i = pl.program_id(1)

        @pl.when(jnp.logical_and(pl.program_id(0) == 0, i == 0))
        def _():
            l_ref[...] = jnp.zeros_like(l_ref)
            dw_ref[...] = jnp.zeros_like(dw_ref)

        @pl.when(i < lead_blk)
        def _():
            dh_ref[...] = jnp.zeros_like(dh_ref)
            dhb_ref[...] = jnp.zeros_like(dhb_ref)

        @pl.when(i >= lead_blk)
        def _():
            x = h_ref[...]
            r = lax.rsqrt(jnp.mean(x * x, axis=-1, keepdims=True) + EPS)
            wv = w_ref[...]
            e = x * r * wv - t_ref[0]
            l_ref[...] += 0.5 * jnp.sum(jnp.mean(e * e, axis=-1, keepdims=True))
            dy = e * (1.0 / d)
            g = dy * wv
            c = jnp.mean(g * x, axis=-1, keepdims=True)
            dh = r * g - x * (r * r * r) * c
            dh_ref[...] = dh
            dhb_ref[...] = dh.astype(BF16)
            dw_ref[...] += jnp.sum(dy * x * r, axis=0, keepdims=True)

    row = pl.BlockSpec((tm, d), lambda b, i: (b * nblk + i, 0))
    return _call(body, name=name, out_shape=(S((m, d), F32), S((m, d), BF16), S((8, 128), F32), S((1, d), F32)),
                 grid=(bsz, nblk),
                 in_specs=[row, pl.BlockSpec((1, tm, d), lambda b, i: (b, jnp.maximum(i - lead_blk, 0), 0)),
                           pl.BlockSpec((1, d), lambda b, i: (0, 0))],
                 out_specs=(row, row, pl.BlockSpec((8, 128), lambda b, i: (0, 0)),
                            pl.BlockSpec((1, d), lambda b, i: (0, 0))),
                 sem=("arbitrary", "arbitrary"))(h, target, w)


def _conv_tiles(m, c):
    return _tile(m, (256, 128)), _tile(c, (512, 256, 128))


def _shift_down(a, first):
    row = lax.broadcasted_iota(jnp.int32, a.shape, 0)
    return jnp.where(row == 0, first, pltpu.roll(a, 1, 0))


def _shift_up(a, last):
    row = lax.broadcasted_iota(jnp.int32, a.shape, 0)
    return jnp.where(row == a.shape[0] - 1, last, pltpu.roll(a, a.shape[0] - 1, 0))


def conv_fwd(x, col0, c, w, b, *, name):
    m = x.shape[0]
    tm, tc = _conv_tiles(m, c)
    assert col0 % tc == 0
    cb0 = col0 // tc
    hb = tm // 8

    def body(x_ref, halo_ref, w_ref, b_ref, o_ref):
        xv = x_ref[...]
        w = [w_ref[k:k + 1, :] for k in range(CONV_W)]
        before = [halo_ref[8 - d:9 - d, :] for d in (1, 2, 3)]
        acc = w[0] * xv
        first = w[0] * before[0]
        for k in (1, 2):
            acc = w[k] * xv + _shift_down(acc, first)
            first = sum(w[k - d] * before[d] for d in range(k + 1))
        o_ref[...] = b_ref[...] + w[3] * xv + _shift_down(acc, first)

    return _call(body, name=name, out_shape=S((m, c), F32), grid=(m // tm, c // tc),
                 in_specs=[pl.BlockSpec((tm, tc), lambda i, j: (i, cb0 + j)),
                           pl.BlockSpec((8, tc), lambda i, j: (jnp.maximum(i * hb - 1, 0), cb0 + j)),
                           pl.BlockSpec((CONV_W, tc), lambda i, j: (0, j)), pl.BlockSpec((1, tc), lambda i, j: (0, j))],
                 out_specs=pl.BlockSpec((tm, tc), lambda i, j: (i, j)),
                 sem=("parallel", "parallel"))(x, x, w, b)


def conv_bwd(x, col0, c, dy, w, *, name):
    m = x.shape[0]
    tm, tc = _conv_tiles(m, c)
    cb0 = col0 // tc
    n_i = m // tm
    hb = tm // 8

    def body(x_ref, xh_ref, dy_ref, dyn_ref, w_ref, dx_ref, dw_ref, db_ref):
        i = pl.program_id(1)

        @pl.when(i == 0)
        def _():
            dw_ref[...] = jnp.zeros_like(dw_ref)
            db_ref[...] = jnp.zeros_like(db_ref)

        w = [w_ref[k:k + 1, :] for k in range(CONV_W)]
        d_cur = dy_ref[...]
        after = [jnp.where(i == n_i - 1, 0.0, dyn_ref[d:d + 1, :]) for d in range(3)]
        acc = w[0] * d_cur
        last = w[0] * after[0]
        for k in (1, 2):
            acc = w[k] * d_cur + _shift_up(acc, last)
            last = sum(w[k - d] * after[d] for d in range(k + 1))
        dx_ref[...] = (w[3] * d_cur + _shift_up(acc, last)).astype(BF16)
        xs = x_ref[...]
        dw_ref[3:4, :] += jnp.sum(xs * d_cur, axis=0, keepdims=True)
        for d in (1, 2, 3):
            xs = _shift_down(xs, xh_ref[8 - d:9 - d, :])
            dw_ref[3 - d:4 - d, :] += jnp.sum(xs * d_cur, axis=0, keepdims=True)
        db_ref[...] += jnp.sum(d_cur, axis=0, keepdims=True)

    return _call(body, name=name, out_shape=(S((m, c), BF16), S((CONV_W, c), F32), S((1, c), F32)),
                 grid=(c // tc, n_i),
                 in_specs=[pl.BlockSpec((tm, tc), lambda j, i: (i, cb0 + j)),
                           pl.BlockSpec((8, tc), lambda j, i: (jnp.maximum(i * hb - 1, 0), cb0 + j)),
                           pl.BlockSpec((tm, tc), lambda j, i: (i, j)),
                           pl.BlockSpec((8, tc), lambda j, i: (jnp.minimum((i + 1) * hb, n_i * hb - 1), j)),
                           pl.BlockSpec((CONV_W, tc), lambda j, i: (0, j))],
                 out_specs=(pl.BlockSpec((tm, tc), lambda j, i: (i, j)), pl.BlockSpec((CONV_W, tc), lambda j, i: (0, j)),
                            pl.BlockSpec((1, tc), lambda j, i: (0, j))),
                 sem=("parallel", "arbitrary"))(x, x, dy, dy, w)


def _row_valid(tile_idx, tiles_per_seq, tm):
    pos = lax.rem(tile_idx, tiles_per_seq) * tm + lax.broadcasted_iota(jnp.int32, (tm, 1), 0)
    return (pos >= PAD).astype(F32)


def _neg_expm1(x):
    small = -(x * (1.0 + x * (0.5 + x * (1.0 / 6.0))))
    return jnp.where(x > -0.01, small, 1.0 - jnp.exp(x))


def _to_scan_layout(ref, val, col0, tm):
    for k in range(val.shape[1] // 128):
        ref[pl.ds(col0 // 128 + k, tm, stride=8), :] = val[:, k * 128:(k + 1) * 128]


def _from_scan_layout(ref, col0, width, tm):
    parts = [ref[pl.ds(col0 // 128 + k, tm, stride=8), :] for k in range(width // 128)]
    return parts[0] if len(parts) == 1 else jnp.concatenate(parts, axis=1)


def _gates_core(lx, wa, ba, wx, bx, lam):
    lxb = lx.astype(BF16)
    r = _sigmoid(_dot(lxb, wa) + ba)
    i = _sigmoid(_dot(lxb, wx) + bx)
    sp = _softplus(-lam)
    log_a = (-RG_LRU_C) * r * sp
    a = jnp.exp(log_a)
    mult = jnp.sqrt(_neg_expm1(2.0 * log_a))
    return lxb, r, i, sp, a, mult


def gates_fwd(lx, rowmask, wa, ba, wx, bx, lam, *, name):
    m = lx.shape[0]
    tm = _tile(m, (256, 128))
    cb = LRU_BLOCK

    def body(lx_ref, msk_ref, wa_ref, ba_ref, wx_ref, bx_ref, lam_ref, a_ref, b_ref):
        msk = msk_ref[...]
        for g in range(LRU_BLOCKS):
            ch = slice(g * cb, (g + 1) * cb)
            lxv = lx_ref[:, ch]
            _, _, i, _, a, mult = _gates_core(lxv, wa_ref[g], ba_ref[:, ch], wx_ref[g], bx_ref[:, ch], lam_ref[:, ch])
            _to_scan_layout(a_ref, a, g * cb, tm)
            _to_scan_layout(b_ref, msk * (mult * i * lxv), g * cb, tm)

    tok = pl.BlockSpec((tm, D_MODEL), lambda i: (i, 0))
    msk = pl.BlockSpec((tm, 1), lambda i: (i, 0))
    wsp = pl.BlockSpec((LRU_BLOCKS, cb, cb), lambda i: (0, 0, 0))
    vec = pl.BlockSpec((1, D_MODEL), lambda i: (0, 0))
    scn = pl.BlockSpec((tm * 8, 128), lambda i: (i, 0))
    return _call(body, name=name, out_shape=(S((m * 8, 128), F32),) * 2, grid=(m // tm,),
                 in_specs=[tok, msk, wsp, vec, wsp, vec, vec], out_specs=(scn, scn),
                 sem=("parallel",))(lx, rowmask, wa, ba, wx, bx, lam)


def gates_bwd(lx, rowmask, da, db, wa, ba, wx, bx, lam, *, name):
    m = lx.shape[0]
    tm = _tile(m, (256, 128))
    cb = LRU_BLOCK

    def body(lx_ref, msk_ref, da_ref, db_ref, wa_ref, ba_ref, wx_ref, bx_ref, lam_ref,
             dlx_ref, dwa_ref, dwx_ref, dba_ref, dbx_ref, dlam_ref):
        @pl.when(pl.program_id(0) == 0)
        def _():
            for ref in (dwa_ref, dwx_ref, dba_ref, dbx_ref, dlam_ref):
                ref[...] = jnp.zeros_like(ref)
        msk = msk_ref[...]
        for g in range(LRU_BLOCKS):
            ch = slice(g * cb, (g + 1) * cb)
            lxv = lx_ref[:, ch]
            lamv = lam_ref[:, ch]
            lxb, r, i, sp, a, mult = _gates_core(lxv, wa_ref[g], ba_ref[:, ch], wx_ref[g], bx_ref[:, ch], lamv)
            dbv = msk * _from_scan_layout(db_ref, g * cb, cb, tm)
            d_mult = dbv * (i * lxv)
            d_i = dbv * (mult * lxv)
            d_log_a = _from_scan_layout(da_ref, g * cb, cb, tm) * a - d_mult * (a * a) / mult
            d_pa = (d_log_a * ((-RG_LRU_C) * sp)) * (r * (1.0 - r))
            d_px = d_i * (i * (1.0 - i))
            d_pa16 = d_pa.astype(BF16)
            d_px16 = d_px.astype(BF16)
            dlx_ref[:, ch] = dbv * (mult * i) + _dot_nt(d_pa16, wa_ref[g]) + _dot_nt(d_px16, wx_ref[g])
            dwa_ref[g] += _dot_tn(lxb, d_pa16)
            dwx_ref[g] += _dot_tn(lxb, d_px16)
            dba_ref[:, ch] += jnp.sum(d_pa, axis=0, keepdims=True)
            dbx_ref[:, ch] += jnp.sum(d_px, axis=0, keepdims=True)
            d_sp = jnp.sum(d_log_a * ((-RG_LRU_C) * r), axis=0, keepdims=True)
            dlam_ref[:, ch] += -d_sp * _sigmoid(-lamv)

    tok = pl.BlockSpec((tm, D_MODEL), lambda i: (i, 0))
    msk = pl.BlockSpec((tm, 1), lambda i: (i, 0))
    scn = pl.BlockSpec((tm * 8, 128), lambda i: (i, 0))
    wsp = pl.BlockSpec((LRU_BLOCKS, cb, cb), lambda i: (0, 0, 0))
    vec = pl.BlockSpec((1, D_MODEL), lambda i: (0, 0))
    wshape = S((LRU_BLOCKS, cb, cb), F32)
    vshape = S((1, D_MODEL), F32)
    return _call(body, name=name, out_shape=(S((m, D_MODEL), F32), wshape, wshape, vshape, vshape, vshape),
                 grid=(m // tm,), in_specs=[tok, msk, scn, scn, wsp, vec, wsp, vec, vec],
                 out_specs=(tok, wsp, wsp, vec, vec, vec),
                 sem=("arbitrary",))(lx, rowmask, da, db, wa, ba, wx, bx, lam)


SCAN_TOK = 128


def scan_fwd(a, b, lp, *, name):
    m = a.shape[0] // 8
    bsz = m // lp
    nch = lp // SCAN_TOK
    rows = SCAN_TOK * 8

    def body(a_ref, b_ref, h_ref, carry):
        @pl.when(pl.program_id(0) == 0)
        def _():
            carry[...] = jnp.zeros_like(carry)

        def step(t, hs):
            r = pl.ds(pl.multiple_of(t * 8, 8), 8)
            out = []
            for s, h in enumerate(hs):
                h = a_ref[s, r, :] * h + b_ref[s, r, :]
                h_ref[s, r, :] = h
                out.append(h)
            return tuple(out)

        hs = lax.fori_loop(0, SCAN_TOK, step, tuple(carry[s] for s in range(bsz)), unroll=4)
        for s in range(bsz):
            carry[s] = hs[s]

    blk = pl.BlockSpec((bsz, rows, 128), lambda c: (0, c, 0))
    shape3 = (bsz, lp * 8, 128)
    out = _call(body, name=name, out_shape=S(shape3, F32), grid=(nch,), in_specs=[blk, blk], out_specs=blk,
                scratch=[pltpu.VMEM((bsz, 8, 128), F32)], sem=("arbitrary",))(a.reshape(shape3), b.reshape(shape3))
    return out.reshape(m * 8, 128)


def scan_bwd(a, h, dh, lp, *, name):
    m = a.shape[0] // 8
    bsz = m // lp
    nch = lp // SCAN_TOK
    rows = SCAN_TOK * 8

    def body(a_ref, h_ref, hprev_ref, dh_ref, da_ref, db_ref, carry):
        c = pl.program_id(0)

        @pl.when(c == 0)
        def _():
            carry[...] = jnp.zeros_like(carry)

        h_before = [jnp.where(c == nch - 1, 0.0, hprev_ref[s]) for s in range(bsz)]

        def step(k, ags):
            t = SCAN_TOK - 1 - k
            r = pl.ds(pl.multiple_of(t * 8, 8), 8)
            rp = pl.ds(pl.multiple_of(jnp.maximum(t - 1, 0) * 8, 8), 8)
            out = []
            for s, ag in enumerate(ags):
                g = dh_ref[s, r, :] + ag
                db_ref[s, r, :] = g
                da_ref[s, r, :] = g * jnp.where(t == 0, h_before[s], h_ref[s, rp, :])
                out.append(a_ref[s, r, :] * g)
            return tuple(out)

        ags = lax.fori_loop(0, SCAN_TOK, step, tuple(carry[s] for s in range(bsz)), unroll=4)
        for s in range(bsz):
            carry[s] = ags[s]

    blk = pl.BlockSpec((bsz, rows, 128), lambda c: (0, nch - 1 - c, 0))
    prev = pl.BlockSpec((bsz, 8, 128), lambda c: (0, jnp.maximum((nch - 1 - c) * SCAN_TOK - 1, 0), 0))
    shape3 = (bsz, lp * 8, 128)
    a3, h3, dh3 = (v.reshape(shape3) for v in (a, h, dh))
    da, db = _call(body, name=name, out_shape=(S(shape3, F32),) * 2, grid=(nch,),
                   in_specs=[blk, blk, prev, blk], out_specs=(blk, blk), scratch=[pltpu.VMEM((bsz, 8, 128), F32)],
                   sem=("arbitrary",))(a3, h3, h3, dh3)
    return da.reshape(m * 8, 128), db.reshape(m * 8, 128)


SSD_SEQS = 2
SSD_SEQS_BWD = 1
SSD_SROWS = SSD_HEADS * SSD_HEAD_DIM


def _ssd_chunk(seqs, dtb, alog, dpar, valid):
    row = lax.broadcasted_iota(jnp.int32, (CHUNK, CHUNK), 0)
    col = lax.broadcasted_iota(jnp.int32, (CHUNK, CHUNK), 1)
    tri = row >= col
    neg_a = -jnp.exp(alog)
    dts, acums, acum_ts, b16s, c16s = [], [], [], [], []
    for xs, bs, cs, dtr, ss in seqs:
        dt = _softplus(dtr + dtb) * valid
        acum = jnp.dot(tri.astype(F32), dt * neg_a, precision=lax.Precision.HIGHEST, preferred_element_type=F32)
        dts.append(dt)
        acums.append(acum)
        acum_ts.append(acum.T)
        b16s.append([(_silu(b) * valid).astype(BF16) for b in bs])
        c16s.append([(_silu(c) * valid).astype(BF16) for c in cs])
    cbs = [[_dot_nt(c, b) for c, b in zip(c16, b16)] for c16, b16 in zip(c16s, b16s)]
    idx = [(q, h) for q in range(len(seqs)) for h in range(SSD_HEADS)]
    grp = [h // SSD_HPG for _, h in idx]
    x = [_silu(seqs[q][0][h]) for q, h in idx]
    s_in = [seqs[q][4][h] for q, h in idx]
    ac = [acums[q][:, h:h + 1] for q, h in idx]
    alast = [acums[q][CHUNK - 1:CHUNK, h:h + 1] for q, h in idx]
    xd = [x[n] * dts[q][:, h:h + 1] for n, (q, h) in enumerate(idx)]
    lhs = [(cbs[q][grp[n]] * jnp.exp(jnp.where(tri, ac[n] - acum_ts[q][h:h + 1, :], -1e30))).astype(BF16)
           for n, (q, h) in enumerate(idx)]
    xd16 = [v.astype(BF16) for v in xd]
    xdec16 = [(xd[n] * jnp.exp(alast[n] - ac[n])).astype(BF16) for n in range(len(idx))]
    s16 = [v.astype(BF16) for v in s_in]
    y_diag = [_dot(lhs[n], xd16[n]) for n in range(len(idx))]
    y_off = [_dot_nt(c16s[q][grp[n]], s16[n]) for n, (q, _) in enumerate(idx)]
    st = [_dot_tn(xdec16[n], b16s[q][grp[n]]) for n, (q, _) in enumerate(idx)]
    ys = [y_diag[n] + y_off[n] * jnp.exp(ac[n]) + x[n] * dpar[:, h:h + 1] for n, (_, h) in enumerate(idx)]
    s_new = [jnp.exp(alast[n]) * s_in[n] + st[n] for n in range(len(idx))]
    return [(ys[q * SSD_HEADS:(q + 1) * SSD_HEADS], s_new[q * SSD_HEADS:(q + 1) * SSD_HEADS])
            for q in range(len(seqs))]


def _ssd_load(pre_ref, dt_ref, s_ref, q):
    xs = [pre_ref[:, h * SSD_HEAD_DIM:(h + 1) * SSD_HEAD_DIM] for h in range(SSD_HEADS)]
    b0 = SSD_HEADS * SSD_HEAD_DIM
    bs = [pre_ref[:, b0 + g * SSD_STATE:b0 + (g + 1) * SSD_STATE] for g in range(SSD_GROUPS)]
    c0 = b0 + SSD_GROUPS * SSD_STATE
    cs = [pre_ref[:, c0 + g * SSD_STATE:c0 + (g + 1) * SSD_STATE] for g in range(SSD_GROUPS)]
    r0 = q * SSD_SROWS
    ss = [s_ref[r0 + h * SSD_HEAD_DIM:r0 + (h + 1) * SSD_HEAD_DIM, :] for h in range(SSD_HEADS)]
    return xs, bs, cs, dt_ref[:, 0:SSD_HEADS], ss


def ssd_fwd(pre, proj, dtb, alog, dpar, lp, *, name):
    m = pre.shape[0]
    bsz = m // lp
    nc = lp // CHUNK
    nq = SSD_SEQS
    assert bsz % nq == 0

    def body(pre_ref, dt_ref, dtb_ref, alog_ref, d_ref, y_ref, sin_ref, state):
        c = pl.program_id(1)

        @pl.when(c == 0)
        def _():
            state[...] = jnp.zeros_like(state)

        for q in range(nq):
            sin_ref[q] = state[q * SSD_SROWS:(q + 1) * SSD_SROWS, :]
        seqs = [_ssd_load(pre_ref.at[q], dt_ref.at[q], state, q) for q in range(nq)]
        valid = _row_valid(c, nc, CHUNK)
        res = _ssd_chunk(seqs, dtb_ref[...], alog_ref[...], d_ref[...], valid)
        for q, (ys, s_new) in enumerate(res):
            for h in range(SSD_HEADS):
                y_ref[q, :, h * SSD_HEAD_DIM:(h + 1) * SSD_HEAD_DIM] = ys[h]
                r0 = q * SSD_SROWS + h * SSD_HEAD_DIM
                state[r0:r0 + SSD_HEAD_DIM, :] = s_new[h]

    par = pl.BlockSpec((1, SSD_HEADS), lambda s, c: (0, 0))
    y, s_in = _call(
        body, name=name, out_shape=(S((bsz, lp, D_MODEL), F32), S((bsz, nc * SSD_SROWS, SSD_STATE), F32)),
        grid=(bsz // nq, nc),
        in_specs=[pl.BlockSpec((nq, CHUNK, SSD_CONV_DIM), lambda s, c: (s, c, 0)),
                  pl.BlockSpec((nq, CHUNK, 128), lambda s, c: (s, c, COL_DT // 128)), par, par, par],
        out_specs=(pl.BlockSpec((nq, CHUNK, D_MODEL), lambda s, c: (s, c, 0)),
                   pl.BlockSpec((nq, SSD_SROWS, SSD_STATE), lambda s, c: (s, c, 0))),
        scratch=[pltpu.VMEM((nq * SSD_SROWS, SSD_STATE), F32)],
        sem=("parallel", "arbitrary"))(pre.reshape(bsz, lp, -1), proj.reshape(bsz, lp, -1), dtb, alog, dpar)
    return y.reshape(m, D_MODEL), s_in


def ssd_bwd(pre, proj, s_in, dy, dtb, alog, dpar, lp, *, name):
    m = pre.shape[0]
    bsz = m // lp
    nc = lp // CHUNK
    nq = SSD_SEQS_BWD

    def body(pre_ref, dt_ref, sin_ref, dy_ref, dtb_ref, alog_ref, d_ref,
             dpre_ref, ddt_ref, ddtb_ref, dalog_ref, dd_ref, dstate):
        c = pl.program_id(1)

        @pl.when(jnp.logical_and(pl.program_id(0) == 0, c == 0))
        def _():
            for r in (ddtb_ref, dalog_ref, dd_ref):
                r[...] = jnp.zeros_like(r)

        @pl.when(c == 0)
        def _():
            dstate[...] = jnp.zeros_like(dstate)

        seqs = [_ssd_load(pre_ref.at[q], dt_ref.at[q], sin_ref.at[q], 0) for q in range(nq)]
        valid = _row_valid(nc - 1 - c, nc, CHUNK)
        core = functools.partial(_ssd_chunk, valid=valid)
        _, vjp = jax.vjp(core, seqs, dtb_ref[...], alog_ref[...], d_ref[...])
        cot = []
        for q in range(nq):
            dys = [dy_ref[q, :, h * SSD_HEAD_DIM:(h + 1) * SSD_HEAD_DIM] for h in range(SSD_HEADS)]
            r0 = q * SSD_SROWS
            dsn = [dstate[r0 + h * SSD_HEAD_DIM:r0 + (h + 1) * SSD_HEAD_DIM, :] for h in range(SSD_HEADS)]
            cot.append((dys, dsn))
        dseqs, ddtb, dalog, dd = vjp(cot)
        b0 = SSD_HEADS * SSD_HEAD_DIM
        c0 = b0 + SSD_GROUPS * SSD_STATE
        ddt_ref[...] = jnp.zeros_like(ddt_ref)
        for q, (dxs, dbs, dcs, ddtr, dss) in enumerate(dseqs):
            for h in range(SSD_HEADS):
                dpre_ref[q, :, h * SSD_HEAD_DIM:(h + 1) * SSD_HEAD_DIM] = dxs[h]
                r0 = q * SSD_SROWS + h * SSD_HEAD_DIM
                dstate[r0:r0 + SSD_HEAD_DIM, :] = dss[h]
            for g in range(SSD_GROUPS):
                dpre_ref[q, :, b0 + g * SSD_STATE:b0 + (g + 1) * SSD_STATE] = dbs[g]
                dpre_ref[q, :, c0 + g * SSD_STATE:c0 + (g + 1) * SSD_STATE] = dcs[g]
            ddt_ref[q, :, 0:SSD_HEADS] = ddtr.astype(BF16)
        ddtb_ref[...] += ddtb
        dalog_ref[...] += dalog
        dd_ref[...] += dd

    par = pl.BlockSpec((1, SSD_HEADS), lambda s, c: (0, 0))
    rev = lambda s, c: (s, nc - 1 - c, 0)
    pshape = S((1, SSD_HEADS), F32)
    dpre, ddt, ddtb, dalog, dd = _call(
        body, name=name,
        out_shape=(S((bsz, lp, SSD_CONV_DIM), F32), S((bsz, lp, 128), BF16), pshape, pshape, pshape),
        grid=(bsz // nq, nc),
        in_specs=[pl.BlockSpec((nq, CHUNK, SSD_CONV_DIM), rev),
                  pl.BlockSpec((nq, CHUNK, 128), lambda s, c: (s, nc - 1 - c, COL_DT // 128)),
                  pl.BlockSpec((nq, SSD_SROWS, SSD_STATE), rev), pl.BlockSpec((nq, CHUNK, D_MODEL), rev), par, par, par],
        out_specs=(pl.BlockSpec((nq, CHUNK, SSD_CONV_DIM), rev), pl.BlockSpec((nq, CHUNK, 128), rev), par, par, par),
        scratch=[pltpu.VMEM((nq * SSD_SROWS, SSD_STATE), F32)],
        sem=("arbitrary", "arbitrary"))(pre.reshape(bsz, lp, -1), proj.reshape(bsz, lp, -1), s_in,
                                        dy.reshape(bsz, lp, -1), dtb, alog, dpar)
    return dpre.reshape(m, SSD_CONV_DIM), ddt.reshape(m, 128), ddtb, dalog, dd


SB_KEYS = 256


def _order_mats():
    r = lax.broadcasted_iota(jnp.int32, (SB_KEYS, SB_KEYS), 0)
    c = lax.broadcasted_iota(jnp.int32, (SB_KEYS, SB_KEYS), 1)
    return (r > c).astype(BF16), (r < c).astype(BF16)


def _split_dot(x, mat):
    hi = x.astype(BF16)
    lo = (x - hi.astype(F32)).astype(BF16)
    return _dot(hi, mat) + _dot(lo, mat)


def _sb_tiles(qs_, ks_, blocks_, jt, diff, col, m_later, masked):
    zs = [_dot_nt(q_i, k_t) for q_i, k_t in zip(qs_, ks_)]
    out = []
    for z, i in zip(zs, blocks_):
        if masked:
            valid = jnp.logical_and(diff > jt * SB_KEYS - i * SB_BLOCK, col >= PAD - jt * SB_KEYS)
            lk = jnp.where(valid, -_softplus(z), 0.0)
        else:
            valid, lk = None, -_softplus(z)
        out.append((valid, z, lk))
    sums = [_split_dot(lk, m_later) for _, _, lk in out]
    return [(valid, z, lk, tsum, tsum[:, 0:1] + lk[:, 0:1]) for (valid, z, lk), tsum in zip(out, sums)]


def _sb_iotas():
    row = lax.broadcasted_iota(jnp.int32, (SB_BLOCK, SB_KEYS), 0)
    col = lax.broadcasted_iota(jnp.int32, (SB_BLOCK, SB_KEYS), 1)
    return row - col, col


def _sb_rows(i, size):
    start = i * size
    return pl.ds(start if isinstance(start, int) else pl.multiple_of(start, size), size)


def _sb_fill(dst, src_ref, ln, lp, scale=None):
    v = src_ref[:, ln]
    dst[0:lp, :] = (v if scale is None else v * scale).astype(BF16)
    if dst.shape[0] > lp:
        dst[lp:, :] = jnp.zeros((dst.shape[0] - lp, dst.shape[1]), BF16)


def _sb_schedule(nb, run_blocks):
    def pair(a, _):
        run_blocks([2 * a, 2 * a + 1], a + 1)
        return 0
    if nb >= 2:
        run_blocks([0, 1], 1)
    lax.fori_loop(1, nb // 2, pair, 0)
    if nb % 2:
        run_blocks([nb - 1], (nb + 1) // 2)


def _sb_sweep(ntiles, step, carry):
    carry = step(ntiles - 1, carry, True)
    if isinstance(ntiles, int) and ntiles == 1:
        return carry
    carry = lax.fori_loop(1, ntiles - 1, lambda jj, c: step(ntiles - 1 - jj, c, False), carry)
    return step(0, carry, True)


def attn_fwd(qkvg, lp, *, name):
    m = qkvg.shape[0]
    bsz = m // lp
    nb = lp // SB_BLOCK
    nkt = (nb + 1) // 2
    hd = SB_HEAD_DIM

    def body(q_ref, k_ref, v_ref, o_ref, qs, ks, vs):
        m_later, _ = _order_mats()
        diff, col = _sb_iotas()
        for hh in range(2):
            ln = slice(hh * hd, (hh + 1) * hd)
            _sb_fill(qs.at[hh], q_ref, ln, lp, hd ** -0.5)
            _sb_fill(ks.at[hh], k_ref, ln, lp)
            _sb_fill(vs.at[hh], v_ref, ln, lp)

        def run_blocks(blocks, ntiles):
            rows = [_sb_rows(i, SB_BLOCK) for i in blocks]
            chains = [(b, hh) for b in range(len(blocks)) for hh in range(2)]
            q = [qs[hh, rows[b], :] for b, hh in chains]
            blk_of = [blocks[b] for b, _ in chains]

            def tile_step(jt, carry, masked):
                cols = _sb_rows(jt, SB_KEYS)
                k_t = [ks[hh, cols, :] for hh in range(2)]
                v_t = [vs[hh, cols, :] for hh in range(2)]
                tiles = _sb_tiles(q, [k_t[hh] for _, hh in chains], blk_of, jt, diff, col, m_later, masked)
                ws = []
                for valid, z, lk, tsum, _ in tiles:
                    w = jnp.exp(z + lk + tsum)
                    ws.append((jnp.where(valid, w, 0.0) if masked else w).astype(BF16))
                pvs = [_dot(w, v_t[hh]) for w, (_, hh) in zip(ws, chains)]
                return tuple((acc + jnp.exp(run) * pv, run + tile[4])
                             for (acc, run), pv, tile in zip(carry, pvs, tiles))

            zero = (jnp.zeros((SB_BLOCK, hd), F32), jnp.zeros((SB_BLOCK, 1), F32))
            res = _sb_sweep(ntiles, tile_step, (zero,) * len(chains))
            for n, (b, hh) in enumerate(chains):
                o_ref[rows[b], hh * hd:(hh + 1) * hd] = res[n][0]

        _sb_schedule(nb, run_blocks)

    blk = lambda cb: pl.BlockSpec((lp, 128), lambda s, p: (s, cb * 8 + p))
    return _call(body, name=name, out_shape=S((m, D_MODEL), F32), grid=(bsz, 8),
                 in_specs=[blk(0), blk(1), blk(2)], out_specs=pl.BlockSpec((lp, 128), lambda s, p: (s, p)),
                 scratch=[pltpu.VMEM((2, lp, hd), BF16)] + [pltpu.VMEM((2, nkt * SB_KEYS, hd), BF16)] * 2,
                 sem=("parallel", "parallel"))(qkvg, qkvg, qkvg)


def attn_bwd(qkvg, do, lp, *, name):
    m = qkvg.shape[0]
    bsz = m // lp
    nb = lp // SB_BLOCK
    hd = SB_HEAD_DIM
    scale = hd ** -0.5

    nkt = (nb + 1) // 2

    def body(q_ref, k_ref, v_ref, do_ref, dq_ref, dk_ref, dv_ref, qs, ks, vs, dka, dva, g_keep, s_keep):
        m_later, m_earlier = _order_mats()
        diff, col = _sb_iotas()
        for hh in range(2):
            ln = slice(hh * hd, (hh + 1) * hd)
            _sb_fill(qs.at[hh], q_ref, ln, lp, scale)
            _sb_fill(ks.at[hh], k_ref, ln, lp)
            _sb_fill(vs.at[hh], v_ref, ln, lp)
        dka[...] = jnp.zeros_like(dka)
        dva[...] = jnp.zeros_like(dva)

        def run_blocks(blocks, ntiles):
            rows = [_sb_rows(i, SB_BLOCK) for i in blocks]
            chains = [(b, hh) for b in range(len(blocks)) for hh in range(2)]
            q = [qs[hh, rows[b], :] for b, hh in chains]
            do = [do_ref[rows[b], hh * hd:(hh + 1) * hd] for b, hh in chains]

            blk_of = [blocks[b] for b, _ in chains]
            heads = [hh for _, hh in chains]

            def sweep_left(jt, carry, masked):
                cols = _sb_rows(jt, SB_KEYS)
                k_t = [ks[hh, cols, :] for hh in range(2)]
                v_t = [vs[hh, cols, :] for hh in range(2)]
                tiles = _sb_tiles(q, [k_t[hh] for hh in heads], blk_of, jt, diff, col, m_later, masked)
                do_run = [(d * jnp.exp(run)).astype(BF16) for d, run in zip(do, carry)]
                dws = [_dot_nt(d, v_t[hh]) for d, hh in zip(do_run, heads)]
                ws = []
                for n, ((valid, z, lk, tsum, _), dw) in enumerate(zip(tiles, dws)):
                    sig = jnp.exp(z + lk)
                    if masked:
                        sig = jnp.where(valid, sig, 0.0)
                    w = sig * jnp.exp(tsum)
                    g_keep[n, jt] = dw * w
                    s_keep[n, jt] = sig
                    ws.append(w.astype(BF16))
                dvs = [_dot_tn(w, d) for w, d in zip(ws, do_run)]
                for n, hh in enumerate(heads):
                    dva[hh, cols, :] += dvs[n]
                return tuple(run + tile[4] for run, tile in zip(carry, tiles))

            _sb_sweep(ntiles, sweep_left, (jnp.zeros((SB_BLOCK, 1), F32),) * len(chains))

            def sweep_right(jt, carry):
                cols = _sb_rows(jt, SB_KEYS)
                k_t = [ks[hh, cols, :] for hh in range(2)]
                gmats = [g_keep[n, jt] for n in range(len(chains))]
                gsums = [_split_dot(gmat, m_earlier) for gmat in gmats]
                dzs = [(gmat - s_keep[n, jt] * (gmat + gsum + grun)).astype(BF16)
                       for n, (gmat, gsum, (_, grun)) in enumerate(zip(gmats, gsums, carry))]
                dqs = [_dot(dz, k_t[hh]) for dz, hh in zip(dzs, heads)]
                dks = [_dot_tn(dz, q_n) for dz, q_n in zip(dzs, q)]
                for n, hh in enumerate(heads):
                    dka[hh, cols, :] += dks[n]
                last = slice(SB_KEYS - 1, SB_KEYS)
                return tuple((dq + dqn, grun + gsum[:, last] + gmat[:, last])
                             for (dq, grun), dqn, gsum, gmat in zip(carry, dqs, gsums, gmats))

            zero = (jnp.zeros((SB_BLOCK, hd), F32), jnp.zeros((SB_BLOCK, 1), F32))
            res = lax.fori_loop(0, ntiles, sweep_right, (zero,) * len(chains))
            for n, (b, hh) in enumerate(chains):
                dq_ref[rows[b], hh * hd:(hh + 1) * hd] = (res[n][0] * scale).astype(BF16)

        _sb_schedule(nb, run_blocks)
        for hh in range(2):
            dk_ref[:, hh * hd:(hh + 1) * hd] = dka[hh, 0:lp, :].astype(BF16)
            dv_ref[:, hh * hd:(hh + 1) * hd] = dva[hh, 0:lp, :].astype(BF16)

    blk = lambda cb: pl.BlockSpec((lp, 128), lambda s, p: (s, cb * 8 + p))
    one = pl.BlockSpec((lp, 128), lambda s, p: (s, p))
    keys = nkt * SB_KEYS
    return _call(body, name=name, out_shape=(S((m, D_MODEL), BF16),) * 3, grid=(bsz, 8),
                 in_specs=[blk(0), blk(1), blk(2), one], out_specs=(one, one, one),
                 scratch=[pltpu.VMEM((2, lp, hd), BF16)] + [pltpu.VMEM((2, keys, hd), BF16)] * 2
                 + [pltpu.VMEM((2, keys, hd), F32)] * 2 + [pltpu.VMEM((4, nkt, SB_BLOCK, SB_KEYS), F32)] * 2,
                 sem=("parallel", "parallel"))(qkvg, qkvg, qkvg, do)


def meta_grad(dh, lp, *, name):
    m, d = dh.shape
    bsz = m // lp
    per = lp // N_META

    def body(dh_ref, o_ref):
        @pl.when(pl.program_id(0) == 0)
        def _():
            o_ref[...] = jnp.zeros_like(o_ref)
        o_ref[...] += dh_ref[...]

    return _call(body, name=name, out_shape=S((N_META, d), F32), grid=(bsz,),
                 in_specs=[pl.BlockSpec((N_META, d), lambda b: (b * per + PAD // N_META, 0))],
                 out_specs=pl.BlockSpec((N_META, d), lambda b: (0, 0)), sem=("arbitrary",))(dh)


def sum_lead(arr, *, name):
    n, r, c = arr.shape
    tr = _tile(r, (128, 64, 32, 16, 8))

    def body(a_ref, o_ref):
        acc = a_ref[0].astype(F32)
        for k in range(1, n):
            acc = acc + a_ref[k].astype(F32)
        o_ref[...] = acc

    return _call(body, name=name, out_shape=S((r, c), F32), grid=(r // tr,),
                 in_specs=[pl.BlockSpec((n, tr, c), lambda i: (0, i, 0))],
                 out_specs=pl.BlockSpec((tr, c), lambda i: (i, 0)), sem=("parallel",))(arr)


def adamw(w, g_parts, mom, var, *, name):
    r, c = w.shape
    tr = _tile(r, (128, 64, 32, 16, 8))
    n_g = len(g_parts)
    c1 = 1.0 - ADAM_B1 ** ADAM_STEP
    c2 = 1.0 - ADAM_B2 ** ADAM_STEP

    def body(*refs):
        w_ref, g_refs, m_ref, v_ref = refs[0], refs[1:1 + n_g], refs[1 + n_g], refs[2 + n_g]
        g_out, d_out, m_out, v_out = refs[3 + n_g:]
        g = g_refs[0][...]
        for gr in g_refs[1:]:
            g = g + gr[...]
        mn = ADAM_B1 * m_ref[...] + (1.0 - ADAM_B1) * g
        vn = ADAM_B2 * v_ref[...] + (1.0 - ADAM_B2) * (g * g)
        g_out[...] = g
        m_out[...] = mn
        v_out[...] = vn
        d_out[...] = -ADAM_LR * ((mn / c1) / (jnp.sqrt(vn / c2) + ADAM_EPS) + ADAM_WD * w_ref[...])

    blk = pl.BlockSpec((tr, c), lambda i: (i, 0))
    return _call(body, name=name, out_shape=(S((r, c), F32),) * 4, grid=(r // tr,), in_specs=[blk] * (3 + n_g),
                 out_specs=(blk,) * 4, sem=("parallel",))(w, *g_parts, mom, var)


_ANY = pl.BlockSpec(memory_space=pl.ANY)


def _position():
    return lax.axis_index("x"), lax.axis_index("y"), lax.axis_index("c")


def _other_chips(x, y):
    return [(1 - x, y), (x, 1 - y), (1 - x, 1 - y)]


def _comm_call(body, arrs, out_shapes, n_sem, *, name):
    return pl.pallas_call(
        body, out_shape=tuple(out_shapes), in_specs=[_ANY] * len(arrs), out_specs=tuple([_ANY] * len(out_shapes)),
        scratch_shapes=(pltpu.SemaphoreType.DMA((n_sem,)), pltpu.SemaphoreType.DMA((n_sem,)),
                        pltpu.SemaphoreType.DMA((len(arrs),))),
        name=name)(*arrs)


def allgather_chips(arrs, *, name):
    n = len(arrs)

    def body(*refs):
        ins, outs = refs[:n], refs[n:2 * n]
        send_sems, recv_sems, loc_sems = refs[2 * n:]
        x, y, c = _position()
        me = 2 * x + y
        chips = _other_chips(x, y)

        def half(ref, a, which):
            rows = arrs[a].shape[0] // 2
            return ref.at[pl.ds(which * rows, rows)]

        sent, passed = [], []
        for a in range(n):
            for k, (px, py) in enumerate(chips):
                cp = pltpu.make_async_remote_copy(
                    src_ref=half(ins[a], a, c), dst_ref=half(outs[a].at[me], a, c), send_sem=send_sems.at[6 * a + k],
                    recv_sem=recv_sems.at[6 * a + k], device_id=(px, py, c), device_id_type=MESH)
                cp.start()
                sent.append(cp)
        for a in range(n):
            for k, (px, py) in enumerate(chips):
                sent[3 * a + k].wait_recv()
                landed = half(outs[a].at[2 * px + py], a, c)
                cp = pltpu.make_async_remote_copy(
                    src_ref=landed, dst_ref=landed, send_sem=send_sems.at[6 * a + 3 + k],
                    recv_sem=recv_sems.at[6 * a + 3 + k], device_id=(x, y, 1 - c), device_id_type=MESH)
                cp.start()
                passed.append(cp)
        for cp in sent:
            cp.wait_send()
        for cp in passed:
            cp.wait()

    for a in arrs:
        assert a.shape[0] % 32 == 0
    outs = _comm_call(body, arrs, [S((4,) + a.shape, a.dtype) for a in arrs], 6 * n, name=name)
    chip = 2 * lax.axis_index("x") + lax.axis_index("y")
    return [lax.dynamic_update_index_in_dim(o, a, chip, 0) for o, a in zip(outs, arrs)]


def exchange_chips(arrs, small, *, name):
    n = len(arrs)

    def body(*refs):
        ins, small_in = refs[:n], refs[n]
        outs, small_out = refs[n + 1:2 * n + 1], refs[2 * n + 1]
        send_sems, recv_sems, loc_sems = refs[2 * n + 2:]
        x, y, c = _position()
        me = 2 * x + y
        copies = []
        for a in range(n):
            for k, (px, py) in enumerate(_other_chips(x, y)):
                cp = pltpu.make_async_remote_copy(
                    src_ref=ins[a].at[2 * px + py], dst_ref=outs[a].at[me], send_sem=send_sems.at[3 * a + k],
                    recv_sem=recv_sems.at[3 * a + k], device_id=(px, py, c), device_id_type=MESH)
                cp.start()
                copies.append(cp)
        me8 = 4 * x + 2 * y + c
        k = 3 * n
        for fx in (0, 1):
            for fy in (0, 1):
                for fc in (0, 1):
                    if fx + fy + fc == 0:
                        continue
                    peer = (1 - x if fx else x, 1 - y if fy else y, 1 - c if fc else c)
                    cp = pltpu.make_async_remote_copy(
                        src_ref=small_in, dst_ref=small_out.at[me8], send_sem=send_sems.at[k],
                        recv_sem=recv_sems.at[k], device_id=peer, device_id_type=MESH)
                    cp.start()
                    copies.append(cp)
                    k += 1
        for cp in copies:
            cp.wait()

    outs = [S(a.shape, a.dtype) for a in arrs] + [S((8,) + small.shape, small.dtype)]
    outs = _comm_call(body, list(arrs) + [small], outs, 3 * n + 7, name=name)
    x, y, c = _position()
    chip = 2 * x + y
    res = [lax.dynamic_update_index_in_dim(o, lax.dynamic_index_in_dim(a, chip, 0, keepdims=False), chip, 0)
           for o, a in zip(outs[:-1], arrs)]
    return res + [lax.dynamic_update_index_in_dim(outs[-1], small, 4 * x + 2 * y + c, 0)]


def swap_cores(arrs, *, name):
    n = len(arrs)

    def body(*refs):
        ins, outs = refs[:n], refs[n:2 * n]
        send_sems, recv_sems, _ = refs[2 * n:]
        x, y, c = _position()
        copies = []
        for a in range(n):
            cp = pltpu.make_async_remote_copy(
                src_ref=ins[a], dst_ref=outs[a], send_sem=send_sems.at[a], recv_sem=recv_sems.at[a],
                device_id=(x, y, 1 - c), device_id_type=MESH)
            cp.start()
            copies.append(cp)
        for cp in copies:
            cp.wait()

    return _comm_call(body, arrs, [S(a.shape, a.dtype) for a in arrs], n, name=name)


def pair_add(a, b, *, name):
    k, r, c = a.shape
    tr = _tile(r, (256, 128))

    def body(a_ref, b_ref, o_ref):
        o_ref[...] = (a_ref[...].astype(F32) + b_ref[...].astype(F32)).astype(BF16)

    blk = pl.BlockSpec((1, tr, c), lambda j, i: (j, i, 0))
    return _call(body, name=name, out_shape=S((k, r, c), BF16), grid=(k, r // tr), in_specs=[blk, blk], out_specs=blk,
                 sem=("parallel", "parallel"))(a, b)


def _local_step(p, x, target):
    bsz, seq, d = x.shape
    lp = LEAD + seq
    m = bsz * lp
    h0 = jnp.concatenate([jnp.zeros((bsz, PAD, d), F32), jnp.broadcast_to(p["meta"][None], (bsz, N_META, d)), x],
                         axis=1).reshape(m, d)
    u0 = rmsnorm_fwd(h0, p["even_norm"], name="norm0")
    proj = mm_nn([u0], p["win_e"], name="proj0")
    lx = conv_fwd(proj, COL_LRU_X, D_MODEL, p["lru_conv_w"], p["lru_conv_b"], name="lru_conv")
    rowmask = jnp.tile((jnp.arange(lp) >= PAD).astype(F32), bsz).reshape(m, 1)
    a, b = gates_fwd(lx, rowmask, p["lru_w_a"], p["lru_b_a"], p["lru_w_x"], p["lru_b_x"], p["lru_lambda"],
                     name="lru_gates")
    hs = scan_fwd(a, b, lp, name="lru_scan")
    ya = gate_fwd(hs, 0, proj, COL_LRU_G // D_MODEL, name="lru_out_gate", o_scan=True)
    pre = conv_fwd(proj, COL_XBC, SSD_CONV_DIM, p["ssd_conv_w"], p["ssd_conv_b"], name="ssd_conv")
    y, s_in = ssd_fwd(pre, proj, p["ssd_dt_bias"], p["ssd_a_log"], p["ssd_d"], lp, name="ssd")
    yb = gnorm_fwd(y, proj, COL_Z // D_MODEL, p["ssd_norm"], name="ssd_norm")
    h1 = mm_nn([ya, yb], p["wout_e"], resid=h0, name="out0")
    u1 = rmsnorm_fwd(h1, p["odd_norm"], name="norm1")
    qkvg = mm_nn([u1], p["win_o"], name="proj1")
    o = attn_fwd(qkvg, lp, name="attn")
    og = gate_fwd(o, 0, qkvg, 3, name="attn_gate")
    h2 = mm_nn([og], p["wout_o"], resid=h1, name="out1")
    dh2, dh2b, loss, d_final = loss_head(h2, target, p["final_norm"], lp, name="loss_head")
    g = {"final_norm": d_final}
    g["odd_w_out"] = mm_tn(og, dh2b, name="dw_out1")
    d_og = mm_nn([dh2b], p["wout_o_t"], name="d_out1")
    do, dgate = gate_bwd(d_og, 0, o, 0, qkvg, 3, name="attn_gate_bwd")
    dq, dk, dv = attn_bwd(qkvg, do, lp, name="attn_bwd")
    segs1 = [dq, dk, dv, dgate]
    du1 = mm_nn(segs1, p["win_o_t"], name="d_proj1")
    g["odd_w_in"] = jnp.concatenate([mm_tn(u1, t, name=f"dw_proj1_{k}") for k, t in enumerate(segs1)], axis=1)
    dh1, dh1b, g["odd_norm"] = rmsnorm_bwd(du1, h1, p["odd_norm"], dh2, name="norm1_bwd")
    g["even_w_out"] = jnp.concatenate([mm_tn(ya, dh1b, name="dw_out0_a"), mm_tn(yb, dh1b, name="dw_out0_b")], axis=0)
    d_mixed = mm_nn([dh1b], p["wout_e_t"], name="d_out0")
    dhs, dlg = gate_bwd(d_mixed, 0, hs, 0, proj, COL_LRU_G // D_MODEL, name="lru_out_gate_bwd", o_scan=True)
    dy, dz, g["ssd_norm"] = gnorm_bwd(d_mixed, 1, y, proj, COL_Z // D_MODEL, p["ssd_norm"], name="ssd_norm_bwd")
    dpre, ddt, g["ssd_dt_bias"], g["ssd_a_log"], g["ssd_d"] = ssd_bwd(
        pre, proj, s_in, dy, p["ssd_dt_bias"], p["ssd_a_log"], p["ssd_d"], lp, name="ssd_bwd")
    dxbc, g["ssd_conv_w"], g["ssd_conv_b"] = conv_bwd(proj, COL_XBC, SSD_CONV_DIM, dpre, p["ssd_conv_w"],
                                                      name="ssd_conv_bwd")
    da, db = scan_bwd(a, hs, dhs, lp, name="lru_scan_bwd")
    dlx, g["lru_w_a"], g["lru_w_x"], g["lru_b_a"], g["lru_b_x"], g["lru_lambda"] = gates_bwd(
        lx, rowmask, da, db, p["lru_w_a"], p["lru_b_a"], p["lru_w_x"], p["lru_b_x"], p["lru_lambda"],
        name="lru_gates_bwd")
    dlrux, g["lru_conv_w"], g["lru_conv_b"] = conv_bwd(proj, COL_LRU_X, D_MODEL, dlx, p["lru_conv_w"],
                                                       name="lru_conv_bwd")
    segs0 = [dlrux, dlg, dz, dxbc, ddt]
    du0 = mm_nn(segs0, p["win_e_t"], name="d_proj0")
    g["even_w_in"] = jnp.concatenate([mm_tn(u0, t, name=f"dw_proj0_{k}") for k, t in enumerate(segs0)],
                                     axis=1)[:, :EVEN_IN]
    dh0, _, g["even_norm"] = rmsnorm_bwd(du0, h0, p["even_norm"], dh1, name="norm0_bwd")
    g["meta"] = meta_grad(dh0, lp, name="meta_grad")
    grad_x = dh0.reshape(bsz, lp, d)[:, LEAD:]
    return loss, grad_x, g


WEIGHTS = ("meta", "even_norm", "even_w_in", "lru_conv_w", "lru_conv_b", "lru_w_a", "lru_b_a", "lru_w_x", "lru_b_x",
           "lru_lambda", "ssd_conv_w", "ssd_conv_b", "ssd_dt_bias", "ssd_a_log", "ssd_d", "ssd_norm", "even_w_out",
           "odd_norm", "odd_w_in", "odd_w_out", "final_norm")
BIG = ("even_w_in", "even_w_out", "odd_w_in", "odd_w_out", "lru_w_a", "lru_w_x")
SHARDED_SMALL = {"meta": 256, "lru_conv_w": 256, "ssd_conv_w": 384, "odd_norm": 256}
SMALL_SHAPES = {"meta": (16, 1024), "even_norm": (1, 1024), "lru_conv_w": (4, 1024), "lru_conv_b": (1, 1024),
                "lru_b_a": (1, 1024), "lru_b_x": (1, 1024), "lru_lambda": (1, 1024), "ssd_conv_w": (4, 1536),
                "ssd_conv_b": (1, 1536), "ssd_dt_bias": (1, 16), "ssd_a_log": (1, 16), "ssd_d": (1, 16),
                "ssd_norm": (1, 1024), "odd_norm": (1, 1024), "final_norm": (1, 1024)}
PACK_UNIT = 1024


def _pack(parts):
    flat = []
    for part in parts:
        v = part.reshape(-1)
        flat.append(jnp.pad(v, (0, -v.shape[0] % PACK_UNIT)))
    return jnp.concatenate(flat).reshape(-1, 128)


def _unpack(buf, shapes):
    v = buf.reshape(-1)
    out, off = [], 0
    for shp in shapes:
        n = 1
        for s_ in shp:
            n *= s_
        out.append(v[off:off + n].reshape(shp))
        off += n + (-n % PACK_UNIT)
    return out


def _chip_cols(a4):
    return jnp.transpose(a4, (1, 0, 2)).reshape(a4.shape[1], -1)


def _to_chip_cols(a, cols):
    return jnp.transpose(a.reshape(a.shape[0], 4, cols), (1, 0, 2))


def kernel(x, meta, even_norm, even_w_in, lru_conv_w, lru_conv_b, lru_w_a, lru_b_a, lru_w_x, lru_b_x, lru_lambda, ssd_conv_w, ssd_conv_b, ssd_dt_bias, ssd_a_log, ssd_d, ssd_norm, even_w_out, odd_norm, odd_w_in, odd_w_out, final_norm, loss_target, m_meta, m_even_norm, m_even_w_in, m_lru_conv_w, m_lru_conv_b, m_lru_w_a, m_lru_b_a, m_lru_w_x, m_lru_b_x, m_lru_lambda, m_ssd_conv_w, m_ssd_conv_b, m_ssd_dt_bias, m_ssd_a_log, m_ssd_d, m_ssd_norm, m_even_w_out, m_odd_norm, m_odd_w_in, m_odd_w_out, m_final_norm, v_meta, v_even_norm, v_even_w_in, v_lru_conv_w, v_lru_conv_b, v_lru_w_a, v_lru_b_a, v_lru_w_x, v_lru_b_x, v_lru_lambda, v_ssd_conv_w, v_ssd_conv_b, v_ssd_dt_bias, v_ssd_a_log, v_ssd_d, v_ssd_norm, v_even_w_out, v_odd_norm, v_odd_w_in, v_odd_w_out, v_final_norm):
    given = dict(locals())
    w = {n: given[n] for n in WEIGHTS}
    mom = {n: given["m_" + n] for n in WEIGHTS}
    var = {n: given["v_" + n] for n in WEIGHTS}
    chip = 2 * lax.axis_index("x") + lax.axis_index("y")

    big_local = {"even_w_in": even_w_in[0], "even_w_out": even_w_out[0], "odd_w_in": odd_w_in[0],
                 "odd_w_out": odd_w_out[0], "lru_w_a": lru_w_a[0].reshape(256, 256),
                 "lru_w_x": lru_w_x[0].reshape(256, 256)}
    sharded_local = [meta, lru_conv_w[0], ssd_conv_w[0], odd_norm]
    gathered = allgather_chips([big_local[n].astype(BF16) for n in BIG] + [_pack(sharded_local)], name="gather_weights")
    gb = dict(zip(BIG, gathered[:-1]))
    per_chip = [_unpack(gathered[-1][k], [a.shape for a in sharded_local]) for k in range(4)]
    full_small = [jnp.concatenate([per_chip[k][j] for k in range(4)], axis=-1) for j in range(len(sharded_local))]

    def lru_full(a4):
        return jnp.transpose(a4.reshape(4, LRU_BLOCKS, 64, LRU_BLOCK), (1, 0, 2, 3)).reshape(LRU_BLOCKS, LRU_BLOCK, LRU_BLOCK)

    p = {"meta": full_small[0], "lru_conv_w": full_small[1], "ssd_conv_w": full_small[2], "odd_norm": full_small[3],
         "even_norm": even_norm, "lru_conv_b": lru_conv_b, "lru_b_a": lru_b_a, "lru_b_x": lru_b_x,
         "lru_lambda": lru_lambda, "ssd_conv_b": ssd_conv_b, "ssd_dt_bias": ssd_dt_bias, "ssd_a_log": ssd_a_log,
         "ssd_d": ssd_d, "ssd_norm": ssd_norm, "final_norm": final_norm.reshape(1, D_MODEL)}
    p["win_e"] = jnp.pad(_chip_cols(gb["even_w_in"]), ((0, 0), (0, EVEN_IN_P - EVEN_IN)))
    p["wout_e"] = gb["even_w_out"].reshape(2 * D_MODEL, D_MODEL)
    p["win_o"] = _chip_cols(gb["odd_w_in"])
    p["wout_o"] = gb["odd_w_out"].reshape(D_MODEL, D_MODEL)
    for n in ("win_e", "wout_e", "win_o", "wout_o"):
        p[n + "_t"] = p[n].T
    p["lru_w_a"] = lru_full(gb["lru_w_a"])
    p["lru_w_x"] = lru_full(gb["lru_w_x"])

    loss_part, grad_x, g = _local_step(p, x, loss_target)

    def lru_slabs(a):
        return jnp.transpose(a.reshape(LRU_BLOCKS, 4, 64, LRU_BLOCK), (1, 0, 2, 3)).reshape(4, 256, LRU_BLOCK)

    def by_half(t):
        return jnp.transpose(t.reshape(4, 2, t.shape[1] // 2, t.shape[2]), (1, 0, 2, 3)).astype(BF16)

    slabs = [_to_chip_cols(g["even_w_in"], EVEN_IN // 4), g["even_w_out"].reshape(4, 512, D_MODEL),
             _to_chip_cols(g["odd_w_in"], D_MODEL), g["odd_w_out"].reshape(4, 256, D_MODEL),
             lru_slabs(g["lru_w_a"]), lru_slabs(g["lru_w_x"])]
    small_names = list(SMALL_SHAPES)
    small_part = _pack([loss_part[0:1, 0:1]] + [g[n] for n in small_names])
    core = lax.axis_index("c")
    halves = [by_half(t) for t in slabs]
    mine = [lax.dynamic_index_in_dim(t, core, 0, keepdims=False) for t in halves]
    theirs = swap_cores([lax.dynamic_index_in_dim(t, 1 - core, 0, keepdims=False) for t in halves], name="swap_partials")
    pair_sums = [pair_add(a, b, name=f"pair_add_{n}") for a, b, n in zip(mine, theirs, BIG)]
    *recv, small_all = exchange_chips(pair_sums, small_part, name="exchange_grads")
    half_sums = [sum_lead(r, name=f"sum_chips_{n}") for n, r in zip(BIG, recv)]
    other_half = swap_cores(half_sums, name="swap_halves")
    full_sums = [jnp.where(core == 0, jnp.concatenate([a, b], axis=0), jnp.concatenate([b, a], axis=0))
                 for a, b in zip(half_sums, other_half)]
    small_sum = sum_lead(small_all, name="sum_small")
    small_g = dict(zip(["loss"] + small_names, _unpack(small_sum, [(1, 1)] + [SMALL_SHAPES[n] for n in small_names])))
    loss = small_g["loss"].reshape(())

    grads, delta, new_m, new_v = {}, {}, {}, {}
    for n, total in zip(BIG, full_sums):
        shp = w[n].shape
        two_d = lambda t: t.reshape(total.shape)
        res = adamw(two_d(w[n]), [total], two_d(mom[n]), two_d(var[n]), name=f"adamw_{n}")
        grads[n], delta[n], new_m[n], new_v[n] = (r.reshape(shp) for r in res)
    local_g = []
    for n in small_names:
        gn = small_g[n]
        if n in SHARDED_SMALL:
            gn = lax.dynamic_slice_in_dim(gn, chip * SHARDED_SMALL[n], SHARDED_SMALL[n], axis=1)
        local_g.append(gn)
    res = adamw(_pack([w[n] for n in small_names]), [_pack(local_g)], _pack([mom[n] for n in small_names]),
                _pack([var[n] for n in small_names]), name="adamw_small")
    shapes = [w[n].shape for n in small_names]
    for out, r in zip((grads, delta, new_m, new_v), res):
        out.update(dict(zip(small_names, _unpack(r, shapes))))
    return (loss, grad_x, *[grads[n] for n in WEIGHTS], *[delta[n] for n in WEIGHTS], *[new_m[n] for n in WEIGHTS],
            *[new_v[n] for n in WEIGHTS])
```

```python
import functools

import jax
import jax.numpy as jnp
from jax import lax
from jax.experimental import pallas as pl
from jax.experimental.pallas import tpu as pltpu

F32 = jnp.float32
BF16 = jnp.bfloat16

D_MODEL = 1024
N_META = 16
LEAD = 128
PAD = LEAD - N_META
EPS = 1e-6
CONV_W = 4
LRU_BLOCKS = 4
LRU_BLOCK = 256
RG_LRU_C = 8.0
SSD_HEADS = 16
SSD_HEAD_DIM = 64
SSD_GROUPS = 2
SSD_HPG = 8
SSD_STATE = 128
CHUNK = 128
SSD_CONV_DIM = 1536
EVEN_IN = 4624
EVEN_IN_P = 4736
COL_LRU_X, COL_LRU_G, COL_Z, COL_XBC, COL_DT = 0, 1024, 2048, 3072, 4608
SB_HEADS = 16
SB_HEAD_DIM = 64
SB_BLOCK = 128
ADAM_LR, ADAM_B1, ADAM_B2, ADAM_EPS, ADAM_WD, ADAM_STEP = 0.001, 0.9, 0.999, 1e-08, 0.01, 10
VMEM_LIMIT_V7X = 56 * 1024 * 1024
MESH = pl.DeviceIdType.MESH
S = jax.ShapeDtypeStruct


def _tile(n, prefs):
    for p in prefs:
        if n % p == 0:
            return p
    raise ValueError(f"no tile of {prefs} divides {n}")


def _call(body, *, name, out_shape, grid=(), in_specs=None, out_specs=None, scratch=(), sem=None):
    kw = {}
    if in_specs is not None:
        kw["in_specs"] = in_specs
    if out_specs is not None:
        kw["out_specs"] = out_specs
    return pl.pallas_call(
        body, out_shape=out_shape, grid=grid, scratch_shapes=tuple(scratch), name=name,
        compiler_params=pltpu.CompilerParams(dimension_semantics=sem, vmem_limit_bytes=VMEM_LIMIT_V7X), **kw)


def _sigmoid(x):
    return 0.5 * (jnp.tanh(0.5 * x) + 1.0)


def _silu(x):
    return x * _sigmoid(x)


def _softplus(x):
    return jnp.maximum(x, 0.0) + jnp.log(1.0 + jnp.exp(-jnp.abs(x)))


def _dot(a, b):
    return jnp.dot(a, b, preferred_element_type=F32)


def _dot_nt(a, b):
    return lax.dot_general(a, b, (((1,), (1,)), ((), ())), preferred_element_type=F32)


def _dot_tn(a, b):
    return lax.dot_general(a, b, (((0,), (0,)), ((), ())), preferred_element_type=F32)


def mm_nn(a_list, w, *, name, resid=None):
    m = a_list[0].shape[0]
    k_tot, n = w.shape
    ks = [a.shape[1] for a in a_list]
    assert sum(ks) == k_tot
    tm = _tile(m, (256, 128))
    n_a = len(a_list)
    offs = [sum(ks[:i]) for i in range(n_a)]
    n_chunks = [(c0, min(512, n - c0)) for c0 in range(0, n, 512)]

    def body(*refs):
        a_refs, w_ref = refs[:n_a], refs[n_a]
        r_ref = refs[n_a + 1] if resid is not None else None
        o_ref = refs[-1]
        for c0, cw in n_chunks:
            acc = None
            for a_ref, k0, k in zip(a_refs, offs, ks):
                p = _dot(a_ref[...], w_ref[k0:k0 + k, c0:c0 + cw])
                acc = p if acc is None else acc + p
            if r_ref is not None:
                acc = acc + r_ref[:, c0:c0 + cw]
            o_ref[:, c0:c0 + cw] = acc

    in_specs = [pl.BlockSpec((tm, k), lambda i: (i, 0)) for k in ks]
    in_specs.append(pl.BlockSpec((k_tot, n), lambda i: (0, 0)))
    args = list(a_list) + [w]
    if resid is not None:
        in_specs.append(pl.BlockSpec((tm, n), lambda i: (i, 0)))
        args.append(resid)
    return _call(body, name=name, out_shape=S((m, n), F32), grid=(m // tm,), in_specs=in_specs,
                 out_specs=pl.BlockSpec((tm, n), lambda i: (i, 0)), sem=("parallel",))(*args)


def mm_tn(a, g, *, name):
    t, m = a.shape
    n = g.shape[1]
    tk = _tile(t, (512, 256, 128))
    tn = _tile(n, (1024, 512, 256, 128))
    nk = t // tk

    def body(a_ref, g_ref, o_ref, acc):
        k = pl.program_id(1)

        @pl.when(k == 0)
        def _():
            acc[...] = jnp.zeros_like(acc)
        acc[...] += _dot_tn(a_ref[...], g_ref[...])

        @pl.when(k == nk - 1)
        def _():
            o_ref[...] = acc[...].astype(BF16)

    return _call(body, name=name, out_shape=S((m, n), BF16), grid=(n // tn, nk),
                 in_specs=[pl.BlockSpec((tk, m), lambda j, k: (k, 0)), pl.BlockSpec((tk, tn), lambda j, k: (k, j))],
                 out_specs=pl.BlockSpec((m, tn), lambda j, k: (0, j)), scratch=[pltpu.VMEM((m, tn), F32)],
                 sem=("parallel", "arbitrary"))(a, g)


def rmsnorm_fwd(h, w, *, name):
    m, d = h.shape
    tm = _tile(m, (512, 256, 128))

    def body(h_ref, w_ref, o_ref):
        x = h_ref[...]
        r = lax.rsqrt(jnp.mean(x * x, axis=-1, keepdims=True) + EPS)
        o_ref[...] = (x * r * w_ref[...]).astype(BF16)

    return _call(body, name=name, out_shape=S((m, d), BF16), grid=(m // tm,),
                 in_specs=[pl.BlockSpec((tm, d), lambda i: (i, 0)), pl.BlockSpec((1, d), lambda i: (0, 0))],
                 out_specs=pl.BlockSpec((tm, d), lambda i: (i, 0)), sem=("parallel",))(h, w)


def rmsnorm_bwd(du, h, w, dres, *, name):
    m, d = h.shape
    tm = _tile(m, (256, 128))

    def body(du_ref, h_ref, w_ref, dr_ref, dh_ref, dhb_ref, dw_ref):
        @pl.when(pl.program_id(0) == 0)
        def _():
            dw_ref[...] = jnp.zeros_like(dw_ref)
        x = h_ref[...]
        r = lax.rsqrt(jnp.mean(x * x, axis=-1, keepdims=True) + EPS)
        du_ = du_ref[...]
        g = du_ * w_ref[...]
        c = jnp.mean(g * x, axis=-1, keepdims=True)
        dh = dr_ref[...] + r * g - x * (r * r * r) * c
        dh_ref[...] = dh
        dhb_ref[...] = dh.astype(BF16)
        dw_ref[...] += jnp.sum(du_ * x * r, axis=0, keepdims=True)

    row = pl.BlockSpec((tm, d), lambda i: (i, 0))
    vec = pl.BlockSpec((1, d), lambda i: (0, 0))
    return _call(body, name=name, out_shape=(S((m, d), F32), S((m, d), BF16), S((1, d), F32)), grid=(m // tm,),
                 in_specs=[row, row, vec, row], out_specs=(row, row, vec), sem=("arbitrary",))(du, h, w, dres)


def gate_fwd(o, o_cb, g, g_cb, *, name, o_scan=False):
    m = g.shape[0]
    d = D_MODEL
    tm = _tile(m, (256, 128))

    def body(o_ref, g_ref, y_ref):
        ov = _from_scan_layout(o_ref, 0, d, tm) if o_scan else o_ref[...]
        y_ref[...] = (ov * _silu(g_ref[...])).astype(BF16)

    o_spec = pl.BlockSpec((tm * 8, 128), lambda i: (i, 0)) if o_scan else pl.BlockSpec((tm, d), lambda i: (i, o_cb))
    return _call(body, name=name, out_shape=S((m, d), BF16), grid=(m // tm,),
                 in_specs=[o_spec, pl.BlockSpec((tm, d), lambda i: (i, g_cb))],
                 out_specs=pl.BlockSpec((tm, d), lambda i: (i, 0)), sem=("parallel",))(o, g)


def gate_bwd(dy, dy_cb, o, o_cb, g, g_cb, *, name, o_scan=False):
    m = g.shape[0]
    d = D_MODEL
    tm = _tile(m, (256, 128))

    def body(dy_ref, o_ref, g_ref, do_ref, dg_ref):
        gv = g_ref[...]
        s = _sigmoid(gv)
        dyv = dy_ref[...]
        do = dyv * gv * s
        if o_scan:
            _to_scan_layout(do_ref, do, 0, tm)
            ov = _from_scan_layout(o_ref, 0, d, tm)
        else:
            do_ref[...] = do
            ov = o_ref[...]
        dg_ref[...] = (dyv * ov * (s + gv * s * (1.0 - s))).astype(BF16)

    nat = pl.BlockSpec((tm, d), lambda i: (i, 0))
    scn = pl.BlockSpec((tm * 8, 128), lambda i: (i, 0))
    o_spec = scn if o_scan else pl.BlockSpec((tm, d), lambda i: (i, o_cb))
    do_shape = S((m * 8, 128), F32) if o_scan else S((m, d), F32)
    return _call(body, name=name, out_shape=(do_shape, S((m, d), BF16)), grid=(m // tm,),
                 in_specs=[pl.BlockSpec((tm, d), lambda i: (i, dy_cb)), o_spec, pl.BlockSpec((tm, d), lambda i: (i, g_cb))],
                 out_specs=(scn if o_scan else nat, nat), sem=("parallel",))(dy, o, g)


def _group_mean(x):
    half = x.shape[1] // SSD_GROUPS
    parts = [jnp.broadcast_to(jnp.mean(x[:, k * half:(k + 1) * half], axis=-1, keepdims=True), (x.shape[0], half))
             for k in range(SSD_GROUPS)]
    return jnp.concatenate(parts, axis=1)


def gnorm_fwd(y, z, z_cb, w, *, name):
    m = y.shape[0]
    d = D_MODEL
    tm = _tile(m, (256, 128))

    def body(y_ref, z_ref, w_ref, o_ref):
        g = y_ref[...] * _silu(z_ref[...])
        r = lax.rsqrt(_group_mean(g * g) + EPS)
        o_ref[...] = (g * r * w_ref[...]).astype(BF16)

    return _call(body, name=name, out_shape=S((m, d), BF16), grid=(m // tm,),
                 in_specs=[pl.BlockSpec((tm, d), lambda i: (i, 0)), pl.BlockSpec((tm, d), lambda i: (i, z_cb)),
                           pl.BlockSpec((1, d), lambda i: (0, 0))],
                 out_specs=pl.BlockSpec((tm, d), lambda i: (i, 0)), sem=("parallel",))(y, z, w)


def gnorm_bwd(do, do_cb, y, z, z_cb, w, *, name):
    m = y.shape[0]
    d = D_MODEL
    tm = _tile(m, (256, 128))

    def body(do_ref, y_ref, z_ref, w_ref, dy_ref, dz_ref, dw_ref):
        @pl.when(pl.program_id(0) == 0)
        def _():
            dw_ref[...] = jnp.zeros_like(dw_ref)
        yv, zv, dov = y_ref[...], z_ref[...], do_ref[...]
        s = _sigmoid(zv)
        sz = zv * s
        g = yv * sz
        r = lax.rsqrt(_group_mean(g * g) + EPS)
        dw_ref[...] += jnp.sum(dov * g * r, axis=0, keepdims=True)
        dn = dov * w_ref[...]
        dg = r * dn - g * (r * r * r) * _group_mean(dn * g)
        dy_ref[...] = dg * sz
        dz_ref[...] = (dg * yv * (s + zv * s * (1.0 - s))).astype(BF16)

    row = pl.BlockSpec((tm, d), lambda i: (i, 0))
    vec = pl.BlockSpec((1, d), lambda i: (0, 0))
    return _call(body, name=name, out_shape=(S((m, d), F32), S((m, d), BF16), S((1, d), F32)), grid=(m // tm,),
                 in_specs=[pl.BlockSpec((tm, d), lambda i: (i, do_cb)), row, pl.BlockSpec((tm, d), lambda i: (i, z_cb)), vec],
                 out_specs=(row, row, vec), sem=("arbitrary",))(do, y, z, w)


def loss_head(h, target, w, lp, *, name):
    m, d = h.shape
    bsz = m // lp
    tm = SB_BLOCK
    nblk = lp // tm
    lead_blk = LEAD // tm

    def body(h_ref, t_ref, w_ref, dh_ref, dhb_ref, l_ref, dw_ref):
        i = pl.program_id(1)

        @pl.when(jnp.logical_and(pl.program_id(0) == 0, i == 0))
        def _():
            l_ref[...] = jnp.zeros_like(l_ref)
            dw_ref[...] = jnp.zeros_like(dw_ref)

        @pl.when(i < lead_blk)
        def _():
            dh_ref[...] = jnp.zeros_like(dh_ref)
            dhb_ref[...] = jnp.zeros_like(dhb_ref)

        @pl.when(i >= lead_blk)
        def _():
            x = h_ref[...]
            r = lax.rsqrt(jnp.mean(x * x, axis=-1, keepdims=True) + EPS)
            wv = w_ref[...]
            e = x * r * wv - t_ref[0]
            l_ref[...] += 0.5 * jnp.sum(jnp.mean(e * e, axis=-1, keepdims=True))
            dy = e * (1.0 / d)
            g = dy * wv
            c = jnp.mean(g * x, axis=-1, keepdims=True)
            dh = r * g - x * (r * r * r) * c
            dh_ref[...] = dh
            dhb_ref[...] = dh.astype(BF16)
            dw_ref[...] += jnp.sum(dy * x * r, axis=0, keepdims=True)

    row = pl.BlockSpec((tm, d), lambda b, i: (b * nblk + i, 0))
    return _call(body, name=name, out_shape=(S((m, d), F32), S((m, d), BF16), S((8, 128), F32), S((1, d), F32)),
                 grid=(bsz, nblk),
                 in_specs=[row, pl.BlockSpec((1, tm, d), lambda b, i: (b, jnp.maximum(i - lead_blk, 0), 0)),
                           pl.BlockSpec((1, d), lambda b, i: (0, 0))],
                 out_specs=(row, row, pl.BlockSpec((8, 128), lambda b, i: (0, 0)),
                            pl.BlockSpec((1, d), lambda b, i: (0, 0))),
                 sem=("arbitrary", "arbitrary"))(h, target, w)


def _conv_tiles(m, c):
    return _tile(m, (256, 128)), _tile(c, (512, 256, 128))


def _shift_down(a, first):
    row = lax.broadcasted_iota(jnp.int32, a.shape, 0)
    return jnp.where(row == 0, first, pltpu.roll(a, 1, 0))


def _shift_up(a, last):
    row = lax.broadcasted_iota(jnp.int32, a.shape, 0)
    return jnp.where(row == a.shape[0] - 1, last, pltpu.roll(a, a.shape[0] - 1, 0))


def conv_fwd(x, col0, c, w, b, *, name):
    m = x.shape[0]
    tm, tc = _conv_tiles(m, c)
    assert col0 % tc == 0
    cb0 = col0 // tc
    hb = tm // 8

    def body(x_ref, halo_ref, w_ref, b_ref, o_ref):
        xv = x_ref[...]
        w = [w_ref[k:k + 1, :] for k in range(CONV_W)]
        before = [halo_ref[8 - d:9 - d, :] for d in (1, 2, 3)]
        acc = w[0] * xv
        first = w[0] * before[0]
        for k in (1, 2):
            acc = w[k] * xv + _shift_down(acc, first)
            first = sum(w[k - d] * before[d] for d in range(k + 1))
        o_ref[...] = b_ref[...] + w[3] * xv + _shift_down(acc, first)

    return _call(body, name=name, out_shape=S((m, c), F32), grid=(m // tm, c // tc),
                 in_specs=[pl.BlockSpec((tm, tc), lambda i, j: (i, cb0 + j)),
                           pl.BlockSpec((8, tc), lambda i, j: (jnp.maximum(i * hb - 1, 0), cb0 + j)),
                           pl.BlockSpec((CONV_W, tc), lambda i, j: (0, j)), pl.BlockSpec((1, tc), lambda i, j: (0, j))],
                 out_specs=pl.BlockSpec((tm, tc), lambda i, j: (i, j)),
                 sem=("parallel", "parallel"))(x, x, w, b)


def conv_bwd(x, col0, c, dy, w, *, name):
    m = x.shape[0]
    tm, tc = _conv_tiles(m, c)
    cb0 = col0 // tc
    n_i = m // tm
    hb = tm // 8

    def body(x_ref, xh_ref, dy_ref, dyn_ref, w_ref, dx_ref, dw_ref, db_ref):
        i = pl.program_id(1)

        @pl.when(i == 0)
        def _():
            dw_ref[...] = jnp.zeros_like(dw_ref)
            db_ref[...] = jnp.zeros_like(db_ref)

        w = [w_ref[k:k + 1, :] for k in range(CONV_W)]
        d_cur = dy_ref[...]
        after = [jnp.where(i == n_i - 1, 0.0, dyn_ref[d:d + 1, :]) for d in range(3)]
        acc = w[0] * d_cur
        last = w[0] * after[0]
        for k in (1, 2):
            acc = w[k] * d_cur + _shift_up(acc, last)
            last = sum(w[k - d] * after[d] for d in range(k + 1))
        dx_ref[...] = (w[3] * d_cur + _shift_up(acc, last)).astype(BF16)
        xs = x_ref[...]
        dw_ref[3:4, :] += jnp.sum(xs * d_cur, axis=0, keepdims=True)
        for d in (1, 2, 3):
            xs = _shift_down(xs, xh_ref[8 - d:9 - d, :])
            dw_ref[3 - d:4 - d, :] += jnp.sum(xs * d_cur, axis=0, keepdims=True)
        db_ref[...] += jnp.sum(d_cur, axis=0, keepdims=True)

    return _call(body, name=name, out_shape=(S((m, c), BF16), S((CONV_W, c), F32), S((1, c), F32)),
                 grid=(c // tc, n_i),
                 in_specs=[pl.BlockSpec((tm, tc), lambda j, i: (i, cb0 + j)),
                           pl.BlockSpec((8, tc), lambda j, i: (jnp.maximum(i * hb - 1, 0), cb0 + j)),
                           pl.BlockSpec((tm, tc), lambda j, i: (i, j)),
                           pl.BlockSpec((8, tc), lambda j, i: (jnp.minimum((i + 1) * hb, n_i * hb - 1), j)),
                           pl.BlockSpec((CONV_W, tc), lambda j, i: (0, j))],
                 out_specs=(pl.BlockSpec((tm, tc), lambda j, i: (i, j)), pl.BlockSpec((CONV_W, tc), lambda j, i: (0, j)),
                            pl.BlockSpec((1, tc), lambda j, i: (0, j))),
                 sem=("parallel", "arbitrary"))(x, x, dy, dy, w)


def _row_valid(tile_idx, tiles_per_seq, tm):
    pos = lax.rem(tile_idx, tiles_per_seq) * tm + lax.broadcasted_iota(jnp.int32, (tm, 1), 0)
    return (pos >= PAD).astype(F32)


def _neg_expm1(x):
    small = -(x * (1.0 + x * (0.5 + x * (1.0 / 6.0))))
    return jnp.where(x > -0.01, small, 1.0 - jnp.exp(x))


def _to_scan_layout(ref, val, col0, tm):
    for k in range(val.shape[1] // 128):
        ref[pl.ds(col0 // 128 + k, tm, stride=8), :] = val[:, k * 128:(k + 1) * 128]


def _from_scan_layout(ref, col0, width, tm):
    parts = [ref[pl.ds(col0 // 128 + k, tm, stride=8), :] for k in range(width // 128)]
    return parts[0] if len(parts) == 1 else jnp.concatenate(parts, axis=1)


def _gates_core(lx, wa, ba, wx, bx, lam):
    lxb = lx.astype(BF16)
    r = _sigmoid(_dot(lxb, wa) + ba)
    i = _sigmoid(_dot(lxb, wx) + bx)
    sp = _softplus(-lam)
    log_a = (-RG_LRU_C) * r * sp
    a = jnp.exp(log_a)
    mult = jnp.sqrt(_neg_expm1(2.0 * log_a))
    return lxb, r, i, sp, a, mult


def gates_fwd(lx, rowmask, wa, ba, wx, bx, lam, *, name):
    m = lx.shape[0]
    tm = _tile(m, (256, 128))
    cb = LRU_BLOCK

    def body(lx_ref, msk_ref, wa_ref, ba_ref, wx_ref, bx_ref, lam_ref, a_ref, b_ref):
        msk = msk_ref[...]
        for g in range(LRU_BLOCKS):
            ch = slice(g * cb, (g + 1) * cb)
            lxv = lx_ref[:, ch]
            _, _, i, _, a, mult = _gates_core(lxv, wa_ref[g], ba_ref[:, ch], wx_ref[g], bx_ref[:, ch], lam_ref[:, ch])
            _to_scan_layout(a_ref, a, g * cb, tm)
            _to_scan_layout(b_ref, msk * (mult * i * lxv), g * cb, tm)

    tok = pl.BlockSpec((tm, D_MODEL), lambda i: (i, 0))
    msk = pl.BlockSpec((tm, 1), lambda i: (i, 0))
    wsp = pl.BlockSpec((LRU_BLOCKS, cb, cb), lambda i: (0, 0, 0))
    vec = pl.BlockSpec((1, D_MODEL), lambda i: (0, 0))
    scn = pl.BlockSpec((tm * 8, 128), lambda i: (i, 0))
    return _call(body, name=name, out_shape=(S((m * 8, 128), F32),) * 2, grid=(m // tm,),
                 in_specs=[tok, msk, wsp, vec, wsp, vec, vec], out_specs=(scn, scn),
                 sem=("parallel",))(lx, rowmask, wa, ba, wx, bx, lam)


def gates_bwd(lx, rowmask, da, db, wa, ba, wx, bx, lam, *, name):
    m = lx.shape[0]
    tm = _tile(m, (256, 128))
    cb = LRU_BLOCK

    def body(lx_ref, msk_ref, da_ref, db_ref, wa_ref, ba_ref, wx_ref, bx_ref, lam_ref,
             dlx_ref, dwa_ref, dwx_ref, dba_ref, dbx_ref, dlam_ref):
        @pl.when(pl.program_id(0) == 0)
        def _():
            for ref in (dwa_ref, dwx_ref, dba_ref, dbx_ref, dlam_ref):
                ref[...] = jnp.zeros_like(ref)
        msk = msk_ref[...]
        for g in range(LRU_BLOCKS):
            ch = slice(g * cb, (g + 1) * cb)
            lxv = lx_ref[:, ch]
            lamv = lam_ref[:, ch]
            lxb, r, i, sp, a, mult = _gates_core(lxv, wa_ref[g], ba_ref[:, ch], wx_ref[g], bx_ref[:, ch], lamv)
            dbv = msk * _from_scan_layout(db_ref, g * cb, cb, tm)
            d_mult = dbv * (i * lxv)
            d_i = dbv * (mult * lxv)
            d_log_a = _from_scan_layout(da_ref, g * cb, cb, tm) * a - d_mult * (a * a) / mult
            d_pa = (d_log_a * ((-RG_LRU_C) * sp)) * (r * (1.0 - r))
            d_px = d_i * (i * (1.0 - i))
            d_pa16 = d_pa.astype(BF16)
            d_px16 = d_px.astype(BF16)
            dlx_ref[:, ch] = dbv * (mult * i) + _dot_nt(d_pa16, wa_ref[g]) + _dot_nt(d_px16, wx_ref[g])
            dwa_ref[g] += _dot_tn(lxb, d_pa16)
            dwx_ref[g] += _dot_tn(lxb, d_px16)
            dba_ref[:, ch] += jnp.sum(d_pa, axis=0, keepdims=True)
            dbx_ref[:, ch] += jnp.sum(d_px, axis=0, keepdims=True)
            d_sp = jnp.sum(d_log_a * ((-RG_LRU_C) * r), axis=0, keepdims=True)
            dlam_ref[:, ch] += -d_sp * _sigmoid(-lamv)

    tok = pl.BlockSpec((tm, D_MODEL), lambda i: (i, 0))
    msk = pl.BlockSpec((tm, 1), lambda i: (i, 0))
    scn = pl.BlockSpec((tm * 8, 128), lambda i: (i, 0))
    wsp = pl.BlockSpec((LRU_BLOCKS, cb, cb), lambda i: (0, 0, 0))
    vec = pl.BlockSpec((1, D_MODEL), lambda i: (0, 0))
    wshape = S((LRU_BLOCKS, cb, cb), F32)
    vshape = S((1, D_MODEL), F32)
    return _call(body, name=name, out_shape=(S((m, D_MODEL), F32), wshape, wshape, vshape, vshape, vshape),
                 grid=(m // tm,), in_specs=[tok, msk, scn, scn, wsp, vec, wsp, vec, vec],
                 out_specs=(tok, wsp, wsp, vec, vec, vec),
                 sem=("arbitrary",))(lx, rowmask, da, db, wa, ba, wx, bx, lam)


SCAN_TOK = 128


def scan_fwd(a, b, lp, *, name):
    m = a.shape[0] // 8
    bsz = m // lp
    nch = lp // SCAN_TOK
    rows = SCAN_TOK * 8

    def body(a_ref, b_ref, h_ref, carry):
        @pl.when(pl.program_id(0) == 0)
        def _():
            carry[...] = jnp.zeros_like(carry)

        def step(t, hs):
            r = pl.ds(pl.multiple_of(t * 8, 8), 8)
            out = []
            for s, h in enumerate(hs):
                h = a_ref[s, r, :] * h + b_ref[s, r, :]
                h_ref[s, r, :] = h
                out.append(h)
            return tuple(out)

        hs = lax.fori_loop(0, SCAN_TOK, step, tuple(carry[s] for s in range(bsz)), unroll=4)
        for s in range(bsz):
            carry[s] = hs[s]

    blk = pl.BlockSpec((bsz, rows, 128), lambda c: (0, c, 0))
    shape3 = (bsz, lp * 8, 128)
    out = _call(body, name=name, out_shape=S(shape3, F32), grid=(nch,), in_specs=[blk, blk], out_specs=blk,
                scratch=[pltpu.VMEM((bsz, 8, 128), F32)], sem=("arbitrary",))(a.reshape(shape3), b.reshape(shape3))
    return out.reshape(m * 8, 128)


def scan_bwd(a, h, dh, lp, *, name):
    m = a.shape[0] // 8
    bsz = m // lp
    nch = lp // SCAN_TOK
    rows = SCAN_TOK * 8

    def body(a_ref, h_ref, hprev_ref, dh_ref, da_ref, db_ref, carry):
        c = pl.program_id(0)

        @pl.when(c == 0)
        def _():
            carry[...] = jnp.zeros_like(carry)

        h_before = [jnp.where(c == nch - 1, 0.0, hprev_ref[s]) for s in range(bsz)]

        def step(k, ags):
            t = SCAN_TOK - 1 - k
            r = pl.ds(pl.multiple_of(t * 8, 8), 8)
            rp = pl.ds(pl.multiple_of(jnp.maximum(t - 1, 0) * 8, 8), 8)
            out = []
            for s, ag in enumerate(ags):
                g = dh_ref[s, r, :] + ag
                db_ref[s, r, :] = g
                da_ref[s, r, :] = g * jnp.where(t == 0, h_before[s], h_ref[s, rp, :])
                out.append(a_ref[s, r, :] * g)
            return tuple(out)

        ags = lax.fori_loop(0, SCAN_TOK, step, tuple(carry[s] for s in range(bsz)), unroll=4)
        for s in range(bsz):
            carry[s] = ags[s]

    blk = pl.BlockSpec((bsz, rows, 128), lambda c: (0, nch - 1 - c, 0))
    prev = pl.BlockSpec((bsz, 8, 128), lambda c: (0, jnp.maximum((nch - 1 - c) * SCAN_TOK - 1, 0), 0))
    shape3 = (bsz, lp * 8, 128)
    a3, h3, dh3 = (v.reshape(shape3) for v in (a, h, dh))
    da, db = _call(body, name=name, out_shape=(S(shape3, F32),) * 2, grid=(nch,),
                   in_specs=[blk, blk, prev, blk], out_specs=(blk, blk), scratch=[pltpu.VMEM((bsz, 8, 128), F32)],
                   sem=("arbitrary",))(a3, h3, h3, dh3)
    return da.reshape(m * 8, 128), db.reshape(m * 8, 128)


SSD_SEQS = 2
SSD_SEQS_BWD = 1
SSD_SROWS = SSD_HEADS * SSD_HEAD_DIM


def _ssd_chunk(seqs, dtb, alog, dpar, valid):
    row = lax.broadcasted_iota(jnp.int32, (CHUNK, CHUNK), 0)
    col = lax.broadcasted_iota(jnp.int32, (CHUNK, CHUNK), 1)
    tri = row >= col
    neg_a = -jnp.exp(alog)
    dts, acums, acum_ts, b16s, c16s = [], [], [], [], []
    for xs, bs, cs, dtr, ss in seqs:
        dt = _softplus(dtr + dtb) * valid
        acum = jnp.dot(tri.astype(F32), dt * neg_a, precision=lax.Precision.HIGHEST, preferred_element_type=F32)
        dts.append(dt)
        acums.append(acum)
        acum_ts.append(acum.T)
        b16s.append([(_silu(b) * valid).astype(BF16) for b in bs])
        c16s.append([(_silu(c) * valid).astype(BF16) for c in cs])
    cbs = [[_dot_nt(c, b) for c, b in zip(c16, b16)] for c16, b16 in zip(c16s, b16s)]
    idx = [(q, h) for q in range(len(seqs)) for h in range(SSD_HEADS)]
    grp = [h // SSD_HPG for _, h in idx]
    x = [_silu(seqs[q][0][h]) for q, h in idx]
    s_in = [seqs[q][4][h] for q, h in idx]
    ac = [acums[q][:, h:h + 1] for q, h in idx]
    alast = [acums[q][CHUNK - 1:CHUNK, h:h + 1] for q, h in idx]
    xd = [x[n] * dts[q][:, h:h + 1] for n, (q, h) in enumerate(idx)]
    lhs = [(cbs[q][grp[n]] * jnp.exp(jnp.where(tri, ac[n] - acum_ts[q][h:h + 1, :], -1e30))).astype(BF16)
           for n, (q, h) in enumerate(idx)]
    xd16 = [v.astype(BF16) for v in xd]
    xdec16 = [(xd[n] * jnp.exp(alast[n] - ac[n])).astype(BF16) for n in range(len(idx))]
    s16 = [v.astype(BF16) for v in s_in]
    y_diag = [_dot(lhs[n], xd16[n]) for n in range(len(idx))]
    y_off = [_dot_nt(c16s[q][grp[n]], s16[n]) for n, (q, _) in enumerate(idx)]
    st = [_dot_tn(xdec16[n], b16s[q][grp[n]]) for n, (q, _) in enumerate(idx)]
    ys = [y_diag[n] + y_off[n] * jnp.exp(ac[n]) + x[n] * dpar[:, h:h + 1] for n, (_, h) in enumerate(idx)]
    s_new = [jnp.exp(alast[n]) * s_in[n] + st[n] for n in range(len(idx))]
    return [(ys[q * SSD_HEADS:(q + 1) * SSD_HEADS], s_new[q * SSD_HEADS:(q + 1) * SSD_HEADS])
            for q in range(len(seqs))]


def _ssd_load(pre_ref, dt_ref, s_ref, q):
    xs = [pre_ref[:, h * SSD_HEAD_DIM:(h + 1) * SSD_HEAD_DIM] for h in range(SSD_HEADS)]
    b0 = SSD_HEADS * SSD_HEAD_DIM
    bs = [pre_ref[:, b0 + g * SSD_STATE:b0 + (g + 1) * SSD_STATE] for g in range(SSD_GROUPS)]
    c0 = b0 + SSD_GROUPS * SSD_STATE
    cs = [pre_ref[:, c0 + g * SSD_STATE:c0 + (g + 1) * SSD_STATE] for g in range(SSD_GROUPS)]
    r0 = q * SSD_SROWS
    ss = [s_ref[r0 + h * SSD_HEAD_DIM:r0 + (h + 1) * SSD_HEAD_DIM, :] for h in range(SSD_HEADS)]
    return xs, bs, cs, dt_ref[:, 0:SSD_HEADS], ss


def ssd_fwd(pre, proj, dtb, alog, dpar, lp, *, name):
    m = pre.shape[0]
    bsz = m // lp
    nc = lp // CHUNK
    nq = SSD_SEQS
    assert bsz % nq == 0

    def body(pre_ref, dt_ref, dtb_ref, alog_ref, d_ref, y_ref, sin_ref, state):
        c = pl.program_id(1)

        @pl.when(c == 0)
        def _():
            state[...] = jnp.zeros_like(state)

        for q in range(nq):
            sin_ref[q] = state[q * SSD_SROWS:(q + 1) * SSD_SROWS, :]
        seqs = [_ssd_load(pre_ref.at[q], dt_ref.at[q], state, q) for q in range(nq)]
        valid = _row_valid(c, nc, CHUNK)
        res = _ssd_chunk(seqs, dtb_ref[...], alog_ref[...], d_ref[...], valid)
        for q, (ys, s_new) in enumerate(res):
            for h in range(SSD_HEADS):
                y_ref[q, :, h * SSD_HEAD_DIM:(h + 1) * SSD_HEAD_DIM] = ys[h]
                r0 = q * SSD_SROWS + h * SSD_HEAD_DIM
                state[r0:r0 + SSD_HEAD_DIM, :] = s_new[h]

    par = pl.BlockSpec((1, SSD_HEADS), lambda s, c: (0, 0))
    y, s_in = _call(
        body, name=name, out_shape=(S((bsz, lp, D_MODEL), F32), S((bsz, nc * SSD_SROWS, SSD_STATE), F32)),
        grid=(bsz // nq, nc),
        in_specs=[pl.BlockSpec((nq, CHUNK, SSD_CONV_DIM), lambda s, c: (s, c, 0)),
                  pl.BlockSpec((nq, CHUNK, 128), lambda s, c: (s, c, COL_DT // 128)), par, par, par],
        out_specs=(pl.BlockSpec((nq, CHUNK, D_MODEL), lambda s, c: (s, c, 0)),
                   pl.BlockSpec((nq, SSD_SROWS, SSD_STATE), lambda s, c: (s, c, 0))),
        scratch=[pltpu.VMEM((nq * SSD_SROWS, SSD_STATE), F32)],
        sem=("parallel", "arbitrary"))(pre.reshape(bsz, lp, -1), proj.reshape(bsz, lp, -1), dtb, alog, dpar)
    return y.reshape(m, D_MODEL), s_in


def ssd_bwd(pre, proj, s_in, dy, dtb, alog, dpar, lp, *, name):
    m = pre.shape[0]
    bsz = m // lp
    nc = lp // CHUNK
    nq = SSD_SEQS_BWD

    def body(pre_ref, dt_ref, sin_ref, dy_ref, dtb_ref, alog_ref, d_ref,
             dpre_ref, ddt_ref, ddtb_ref, dalog_ref, dd_ref, dstate):
        c = pl.program_id(1)

        @pl.when(jnp.logical_and(pl.program_id(0) == 0, c == 0))
        def _():
            for r in (ddtb_ref, dalog_ref, dd_ref):
                r[...] = jnp.zeros_like(r)

        @pl.when(c == 0)
        def _():
            dstate[...] = jnp.zeros_like(dstate)

        seqs = [_ssd_load(pre_ref.at[q], dt_ref.at[q], sin_ref.at[q], 0) for q in range(nq)]
        valid = _row_valid(nc - 1 - c, nc, CHUNK)
        core = functools.partial(_ssd_chunk, valid=valid)
        _, vjp = jax.vjp(core, seqs, dtb_ref[...], alog_ref[...], d_ref[...])
        cot = []
        for q in range(nq):
            dys = [dy_ref[q, :, h * SSD_HEAD_DIM:(h + 1) * SSD_HEAD_DIM] for h in range(SSD_HEADS)]
            r0 = q * SSD_SROWS
            dsn = [dstate[r0 + h * SSD_HEAD_DIM:r0 + (h + 1) * SSD_HEAD_DIM, :] for h in range(SSD_HEADS)]
            cot.append((dys, dsn))
        dseqs, ddtb, dalog, dd = vjp(cot)
        b0 = SSD_HEADS * SSD_HEAD_DIM
        c0 = b0 + SSD_GROUPS * SSD_STATE
        ddt_ref[...] = jnp.zeros_like(ddt_ref)
        for q, (dxs, dbs, dcs, ddtr, dss) in enumerate(dseqs):
            for h in range(SSD_HEADS):
                dpre_ref[q, :, h * SSD_HEAD_DIM:(h + 1) * SSD_HEAD_DIM] = dxs[h]
                r0 = q * SSD_SROWS + h * SSD_HEAD_DIM
                dstate[r0:r0 + SSD_HEAD_DIM, :] = dss[h]
            for g in range(SSD_GROUPS):
                dpre_ref[q, :, b0 + g * SSD_STATE:b0 + (g + 1) * SSD_STATE] = dbs[g]
                dpre_ref[q, :, c0 + g * SSD_STATE:c0 + (g + 1) * SSD_STATE] = dcs[g]
            ddt_ref[q, :, 0:SSD_HEADS] = ddtr.astype(BF16)
        ddtb_ref[...] += ddtb
        dalog_ref[...] += dalog
        dd_ref[...] += dd

    par = pl.BlockSpec((1, SSD_HEADS), lambda s, c: (0, 0))
    rev = lambda s, c: (s, nc - 1 - c, 0)
    pshape = S((1, SSD_HEADS), F32)
    dpre, ddt, ddtb, dalog, dd = _call(
        body, name=name,
        out_shape=(S((bsz, lp, SSD_CONV_DIM), F32), S((bsz, lp, 128), BF16), pshape, pshape, pshape),
        grid=(bsz // nq, nc),
        in_specs=[pl.BlockSpec((nq, CHUNK, SSD_CONV_DIM), rev),
                  pl.BlockSpec((nq, CHUNK, 128), lambda s, c: (s, nc - 1 - c, COL_DT // 128)),
                  pl.BlockSpec((nq, SSD_SROWS, SSD_STATE), rev), pl.BlockSpec((nq, CHUNK, D_MODEL), rev), par, par, par],
        out_specs=(pl.BlockSpec((nq, CHUNK, SSD_CONV_DIM), rev), pl.BlockSpec((nq, CHUNK, 128), rev), par, par, par),
        scratch=[pltpu.VMEM((nq * SSD_SROWS, SSD_STATE), F32)],
        sem=("arbitrary", "arbitrary"))(pre.reshape(bsz, lp, -1), proj.reshape(bsz, lp, -1), s_in,
                                        dy.reshape(bsz, lp, -1), dtb, alog, dpar)
    return dpre.reshape(m, SSD_CONV_DIM), ddt.reshape(m, 128), ddtb, dalog, dd


SB_KEYS = 256


def _order_mats():
    r = lax.broadcasted_iota(jnp.int32, (SB_KEYS, SB_KEYS), 0)
    c = lax.broadcasted_iota(jnp.int32, (SB_KEYS, SB_KEYS), 1)
    return (r > c).astype(BF16), (r < c).astype(BF16)


def _split_dot(x, mat):
    hi = x.astype(BF16)
    lo = (x - hi.astype(F32)).astype(BF16)
    return _dot(hi, mat) + _dot(lo, mat)


def _sb_tiles(qs_, ks_, blocks_, jts, diff, col, m_later, masked):
    zs = [_dot_nt(q_i, k_t) for q_i, k_t in zip(qs_, ks_)]
    out = []
    for z, i, jt in zip(zs, blocks_, jts):
        if masked:
            valid = jnp.logical_and(diff > jt * SB_KEYS - i * SB_BLOCK, col >= PAD - jt * SB_KEYS)
            lk = jnp.where(valid, -_softplus(z), 0.0)
        else:
            valid, lk = None, -_softplus(z)
        out.append((valid, z, lk))
    sums = [_split_dot(lk, m_later) for _, _, lk in out]
    return [(valid, z, lk, tsum, tsum[:, 0:1] + lk[:, 0:1]) for (valid, z, lk), tsum in zip(out, sums)]


def _sb_iotas():
    row = lax.broadcasted_iota(jnp.int32, (SB_BLOCK, SB_KEYS), 0)
    col = lax.broadcasted_iota(jnp.int32, (SB_BLOCK, SB_KEYS), 1)
    return row - col, col


def _sb_rows(i, size):
    start = i * size
    return pl.ds(start if isinstance(start, int) else pl.multiple_of(start, size), size)


def _sb_fill(dst, src_ref, ln, lp, scale=None):
    v = src_ref[:, ln]
    dst[0:lp, :] = (v if scale is None else v * scale).astype(BF16)
    if dst.shape[0] > lp:
        dst[lp:, :] = jnp.zeros((dst.shape[0] - lp, dst.shape[1]), BF16)


def _sb_schedule(nb, run_blocks):
    def pair(a, _):
        run_blocks([2 * a, 2 * a + 1], a + 1, 1)
        return 0
    if nb >= 2:
        run_blocks([0, 1], 1, 1)
    lax.fori_loop(1, nb // 2, pair, 0)
    if nb % 2:
        run_blocks([nb - 1], (nb + 1) // 2, 2)


def _sb_sweep(nsteps, step, carry):
    carry = step(0, carry, True)
    if isinstance(nsteps, int) and nsteps == 1:
        return carry
    carry = lax.fori_loop(1, nsteps - 1, lambda s, c: step(s, c, False), carry)
    return step(nsteps - 1, carry, True)


def attn_fwd(qkvg, lp, *, name):
    m = qkvg.shape[0]
    bsz = m // lp
    nb = lp // SB_BLOCK
    nkt = (nb + 1) // 2
    hd = SB_HEAD_DIM

    def body(q_ref, k_ref, v_ref, o_ref, qs, ks, vs):
        m_later, _ = _order_mats()
        diff, col = _sb_iotas()
        for hh in range(2):
            ln = slice(hh * hd, (hh + 1) * hd)
            _sb_fill(qs.at[hh], q_ref, ln, lp, hd ** -0.5)
            _sb_fill(ks.at[hh], k_ref, ln, lp)
            _sb_fill(vs.at[hh], v_ref, ln, lp)

        def run_blocks(blocks, ntiles, stride):
            rows = [_sb_rows(i, SB_BLOCK) for i in blocks]
            groups = [(b, hh) for b in range(len(blocks)) for hh in range(2)]
            chains = [(g, sub) for g in range(len(groups)) for sub in range(stride)]
            q = [qs[groups[g][1], rows[groups[g][0]], :] for g, _ in chains]
            blk_of = [blocks[groups[g][0]] for g, _ in chains]
            heads = [groups[g][1] for g, _ in chains]

            def tile_step(s, carry, masked):
                jts = [ntiles - 1 - stride * s - sub for _, sub in chains]
                cols = [_sb_rows(jt if stride == 1 else jnp.maximum(jt, 0), SB_KEYS) for jt in jts]
                tiles = _sb_tiles(q, [ks[hh, c, :] for hh, c in zip(heads, cols)], blk_of, jts, diff, col, m_later, masked)
                ws = []
                for valid, z, lk, tsum, _ in tiles:
                    w = jnp.exp(z + lk + tsum)
                    ws.append((jnp.where(valid, w, 0.0) if masked else w).astype(BF16))
                pvs = [_dot(w, vs[hh, c, :]) for w, hh, c in zip(ws, heads, cols)]
                out = []
                for g, (acc, run) in enumerate(carry):
                    for n, (gn, _) in enumerate(chains):
                        if gn == g:
                            acc = acc + jnp.exp(run) * pvs[n]
                            run = run + tiles[n][4]
                    out.append((acc, run))
                return tuple(out)

            zero = (jnp.zeros((SB_BLOCK, hd), F32), jnp.zeros((SB_BLOCK, 1), F32))
            nsteps = ntiles if stride == 1 else (ntiles + stride - 1) // stride
            res = _sb_sweep(nsteps, tile_step, (zero,) * len(groups))
            for g, (b, hh) in enumerate(groups):
                o_ref[rows[b], hh * hd:(hh + 1) * hd] = res[g][0]

        _sb_schedule(nb, run_blocks)

    blk = lambda cb: pl.BlockSpec((lp, 128), lambda s, p: (s, cb * 8 + p))
    return _call(body, name=name, out_shape=S((m, D_MODEL), F32), grid=(bsz, 8),
                 in_specs=[blk(0), blk(1), blk(2)], out_specs=pl.BlockSpec((lp, 128), lambda s, p: (s, p)),
                 scratch=[pltpu.VMEM((2, lp, hd), BF16)] + [pltpu.VMEM((2, nkt * SB_KEYS, hd), BF16)] * 2,
                 sem=("parallel", "parallel"))(qkvg, qkvg, qkvg)


def attn_bwd(qkvg, do, lp, *, name):
    m = qkvg.shape[0]
    bsz = m // lp
    nb = lp // SB_BLOCK
    hd = SB_HEAD_DIM
    scale = hd ** -0.5

    nkt = (nb + 1) // 2

    def body(q_ref, k_ref, v_ref, do_ref, dq_ref, dk_ref, dv_ref, qs, ks, vs, dka, dva, g_keep, s_keep):
        m_later, m_earlier = _order_mats()
        diff, col = _sb_iotas()
        for hh in range(2):
            ln = slice(hh * hd, (hh + 1) * hd)
            _sb_fill(qs.at[hh], q_ref, ln, lp, scale)
            _sb_fill(ks.at[hh], k_ref, ln, lp)
            _sb_fill(vs.at[hh], v_ref, ln, lp)
        dka[...] = jnp.zeros_like(dka)
        dva[...] = jnp.zeros_like(dva)

        def run_blocks(blocks, ntiles, stride):
            rows = [_sb_rows(i, SB_BLOCK) for i in blocks]
            groups = [(b, hh) for b in range(len(blocks)) for hh in range(2)]
            chains = [(g, sub) for g in range(len(groups)) for sub in range(stride)]
            q = [qs[groups[g][1], rows[groups[g][0]], :] for g, _ in chains]
            do = [do_ref[rows[b], hh * hd:(hh + 1) * hd] for b, hh in groups]
            blk_of = [blocks[groups[g][0]] for g, _ in chains]
            heads = [groups[g][1] for g, _ in chains]
            nsteps = ntiles if stride == 1 else (ntiles + stride - 1) // stride

            def place(jt):
                if stride == 1:
                    return jt, _sb_rows(jt, SB_KEYS)
                inside = jnp.logical_and(jt >= 0, jt < ntiles)
                return jnp.where(inside, jt, nkt), _sb_rows(jnp.clip(jt, 0, ntiles - 1), SB_KEYS)

            def sweep_left(s, carry, masked):
                jts = [ntiles - 1 - stride * s - sub for _, sub in chains]
                slots, cols = zip(*[place(jt) for jt in jts])
                tiles = _sb_tiles(q, [ks[hh, c, :] for hh, c in zip(heads, cols)], blk_of, jts, diff, col, m_later, masked)
                runs, out = [None] * len(chains), []
                for g, run in enumerate(carry):
                    for n, (gn, _) in enumerate(chains):
                        if gn == g:
                            runs[n] = run
                            run = run + tiles[n][4]
                    out.append(run)
                do_run = [(do[g] * jnp.exp(run)).astype(BF16) for (g, _), run in zip(chains, runs)]
                dws = [_dot_nt(d, vs[hh, c, :]) for d, hh, c in zip(do_run, heads, cols)]
                ws = []
                for (g, _), slot, (valid, z, lk, tsum, _), dw in zip(chains, slots, tiles, dws):
                    sig = jnp.exp(z + lk)
                    if masked:
                        sig = jnp.where(valid, sig, 0.0)
                    w = sig * jnp.exp(tsum)
                    g_keep[g, slot] = dw * w
                    s_keep[g, slot] = sig
                    ws.append(w.astype(BF16))
                dvs = [_dot_tn(w, d) for w, d in zip(ws, do_run)]
                for hh, c, dv in zip(heads, cols, dvs):
                    dva[hh, c, :] += dv
                return tuple(out)

            _sb_sweep(nsteps, sweep_left, (jnp.zeros((SB_BLOCK, 1), F32),) * len(groups))

            def sweep_right(s, carry):
                jts = [stride * s + sub for _, sub in chains]
                slots, cols = zip(*[place(jt) for jt in jts])
                gmats = [g_keep[g, slot] for (g, _), slot in zip(chains, slots)]
                gsums = [_split_dot(gmat, m_earlier) for gmat in gmats]
                last = slice(SB_KEYS - 1, SB_KEYS)
                gruns, out_grun = [None] * len(chains), []
                for g, (_, grun) in enumerate(carry):
                    for n, (gn, _) in enumerate(chains):
                        if gn == g:
                            gruns[n] = grun
                            grun = grun + gsums[n][:, last] + gmats[n][:, last]
                    out_grun.append(grun)
                dzs = [(gmat - s_keep[g, slot] * (gmat + gsum + grun)).astype(BF16)
                       for (g, _), slot, gmat, gsum, grun in zip(chains, slots, gmats, gsums, gruns)]
                dqs = [_dot(dz, ks[hh, c, :]) for dz, hh, c in zip(dzs, heads, cols)]
                dks = [_dot_tn(dz, q_n) for dz, q_n in zip(dzs, q)]
                for hh, c, dk in zip(heads, cols, dks):
                    dka[hh, c, :] += dk
                out = []
                for g, (dq, _) in enumerate(carry):
                    for n, (gn, _) in enumerate(chains):
                        if gn == g:
                            dq = dq + dqs[n]
                    out.append((dq, out_grun[g]))
                return tuple(out)

            zero = (jnp.zeros((SB_BLOCK, hd), F32), jnp.zeros((SB_BLOCK, 1), F32))
            res = lax.fori_loop(0, nsteps, sweep_right, (zero,) * len(groups))
            for g, (b, hh) in enumerate(groups):
                dq_ref[rows[b], hh * hd:(hh + 1) * hd] = (res[g][0] * scale).astype(BF16)

        _sb_schedule(nb, run_blocks)
        for hh in range(2):
            dk_ref[:, hh * hd:(hh + 1) * hd] = dka[hh, 0:lp, :].astype(BF16)
            dv_ref[:, hh * hd:(hh + 1) * hd] = dva[hh, 0:lp, :].astype(BF16)

    blk = lambda cb: pl.BlockSpec((lp, 128), lambda s, p: (s, cb * 8 + p))
    one = pl.BlockSpec((lp, 128), lambda s, p: (s, p))
    keys = nkt * SB_KEYS
    return _call(body, name=name, out_shape=(S((m, D_MODEL), BF16),) * 3, grid=(bsz, 8),
                 in_specs=[blk(0), blk(1), blk(2), one], out_specs=(one, one, one),
                 scratch=[pltpu.VMEM((2, lp, hd), BF16)] + [pltpu.VMEM((2, keys, hd), BF16)] * 2
                 + [pltpu.VMEM((2, keys, hd), F32)] * 2 + [pltpu.VMEM((4, nkt + 1, SB_BLOCK, SB_KEYS), F32)] * 2,
                 sem=("parallel", "parallel"))(qkvg, qkvg, qkvg, do)


def meta_grad(dh, lp, *, name):
    m, d = dh.shape
    bsz = m // lp
    per = lp // N_META

    def body(dh_ref, o_ref):
        @pl.when(pl.program_id(0) == 0)
        def _():
            o_ref[...] = jnp.zeros_like(o_ref)
        o_ref[...] += dh_ref[...]

    return _call(body, name=name, out_shape=S((N_META, d), F32), grid=(bsz,),
                 in_specs=[pl.BlockSpec((N_META, d), lambda b: (b * per + PAD // N_META, 0))],
                 out_specs=pl.BlockSpec((N_META, d), lambda b: (0, 0)), sem=("arbitrary",))(dh)


def sum_lead(arr, *, name):
    n, r, c = arr.shape
    tr = _tile(r, (128, 64, 32, 16, 8))

    def body(a_ref, o_ref):
        acc = a_ref[0].astype(F32)
        for k in range(1, n):
            acc = acc + a_ref[k].astype(F32)
        o_ref[...] = acc

    return _call(body, name=name, out_shape=S((r, c), F32), grid=(r // tr,),
                 in_specs=[pl.BlockSpec((n, tr, c), lambda i: (0, i, 0))],
                 out_specs=pl.BlockSpec((tr, c), lambda i: (i, 0)), sem=("parallel",))(arr)


def adamw(w, g_parts, mom, var, *, name):
    r, c = w.shape
    tr = _tile(r, (128, 64, 32, 16, 8))
    n_g = len(g_parts)
    c1 = 1.0 - ADAM_B1 ** ADAM_STEP
    c2 = 1.0 - ADAM_B2 ** ADAM_STEP

    def body(*refs):
        w_ref, g_refs, m_ref, v_ref = refs[0], refs[1:1 + n_g], refs[1 + n_g], refs[2 + n_g]
        g_out, d_out, m_out, v_out = refs[3 + n_g:]
        g = g_refs[0][...]
        for gr in g_refs[1:]:
            g = g + gr[...]
        mn = ADAM_B1 * m_ref[...] + (1.0 - ADAM_B1) * g
        vn = ADAM_B2 * v_ref[...] + (1.0 - ADAM_B2) * (g * g)
        g_out[...] = g
        m_out[...] = mn
        v_out[...] = vn
        d_out[...] = -ADAM_LR * ((mn / c1) / (jnp.sqrt(vn / c2) + ADAM_EPS) + ADAM_WD * w_ref[...])

    blk = pl.BlockSpec((tr, c), lambda i: (i, 0))
    return _call(body, name=name, out_shape=(S((r, c), F32),) * 4, grid=(r // tr,), in_specs=[blk] * (3 + n_g),
                 out_specs=(blk,) * 4, sem=("parallel",))(w, *g_parts, mom, var)


_ANY = pl.BlockSpec(memory_space=pl.ANY)


def _position():
    return lax.axis_index("x"), lax.axis_index("y"), lax.axis_index("c")


def _other_chips(x, y):
    return [(1 - x, y), (x, 1 - y), (1 - x, 1 - y)]


def _comm_call(body, arrs, out_shapes, n_sem, *, name):
    return pl.pallas_call(
        body, out_shape=tuple(out_shapes), in_specs=[_ANY] * len(arrs), out_specs=tuple([_ANY] * len(out_shapes)),
        scratch_shapes=(pltpu.SemaphoreType.DMA((n_sem,)), pltpu.SemaphoreType.DMA((n_sem,)),
                        pltpu.SemaphoreType.DMA((len(arrs),))),
        name=name)(*arrs)


def allgather_chips(arrs, *, name):
    n = len(arrs)

    def body(*refs):
        ins, outs = refs[:n], refs[n:2 * n]
        send_sems, recv_sems, loc_sems = refs[2 * n:]
        x, y, c = _position()
        me = 2 * x + y
        chips = _other_chips(x, y)

        def half(ref, a, which):
            rows = arrs[a].shape[0] // 2
            return ref.at[pl.ds(which * rows, rows)]

        sent, passed = [], []
        for a in range(n):
            for k, (px, py) in enumerate(chips):
                cp = pltpu.make_async_remote_copy(
                    src_ref=half(ins[a], a, c), dst_ref=half(outs[a].at[me], a, c), send_sem=send_sems.at[6 * a + k],
                    recv_sem=recv_sems.at[6 * a + k], device_id=(px, py, c), device_id_type=MESH)
                cp.start()
                sent.append(cp)
        for a in range(n):
            for k, (px, py) in enumerate(chips):
                sent[3 * a + k].wait_recv()
                landed = half(outs[a].at[2 * px + py], a, c)
                cp = pltpu.make_async_remote_copy(
                    src_ref=landed, dst_ref=landed, send_sem=send_sems.at[6 * a + 3 + k],
                    recv_sem=recv_sems.at[6 * a + 3 + k], device_id=(x, y, 1 - c), device_id_type=MESH)
                cp.start()
                passed.append(cp)
        for cp in sent:
            cp.wait_send()
        for cp in passed:
            cp.wait()

    for a in arrs:
        assert a.shape[0] % 32 == 0
    outs = _comm_call(body, arrs, [S((4,) + a.shape, a.dtype) for a in arrs], 6 * n, name=name)
    chip = 2 * lax.axis_index("x") + lax.axis_index("y")
    return [lax.dynamic_update_index_in_dim(o, a, chip, 0) for o, a in zip(outs, arrs)]


def exchange_chips(arrs, small, *, name):
    n = len(arrs)

    def body(*refs):
        ins, small_in = refs[:n], refs[n]
        outs, small_out = refs[n + 1:2 * n + 1], refs[2 * n + 1]
        send_sems, recv_sems, loc_sems = refs[2 * n + 2:]
        x, y, c = _position()
        me = 2 * x + y
        copies = []
        for a in range(n):
            for k, (px, py) in enumerate(_other_chips(x, y)):
                cp = pltpu.make_async_remote_copy(
                    src_ref=ins[a].at[2 * px + py], dst_ref=outs[a].at[me], send_sem=send_sems.at[3 * a + k],
                    recv_sem=recv_sems.at[3 * a + k], device_id=(px, py, c), device_id_type=MESH)
                cp.start()
                copies.append(cp)
        me8 = 4 * x + 2 * y + c
        k = 3 * n
        for fx in (0, 1):
            for fy in (0, 1):
                for fc in (0, 1):
                    if fx + fy + fc == 0:
                        continue
                    peer = (1 - x if fx else x, 1 - y if fy else y, 1 - c if fc else c)
                    cp = pltpu.make_async_remote_copy(
                        src_ref=small_in, dst_ref=small_out.at[me8], send_sem=send_sems.at[k],
                        recv_sem=recv_sems.at[k], device_id=peer, device_id_type=MESH)
                    cp.start()
                    copies.append(cp)
                    k += 1
        for cp in copies:
            cp.wait()

    outs = [S(a.shape, a.dtype) for a in arrs] + [S((8,) + small.shape, small.dtype)]
    outs = _comm_call(body, list(arrs) + [small], outs, 3 * n + 7, name=name)
    x, y, c = _position()
    chip = 2 * x + y
    res = [lax.dynamic_update_index_in_dim(o, lax.dynamic_index_in_dim(a, chip, 0, keepdims=False), chip, 0)
           for o, a in zip(outs[:-1], arrs)]
    return res + [lax.dynamic_update_index_in_dim(outs[-1], small, 4 * x + 2 * y + c, 0)]


def swap_cores(arrs, *, name):
    n = len(arrs)

    def body(*refs):
        ins, outs = refs[:n], refs[n:2 * n]
        send_sems, recv_sems, _ = refs[2 * n:]
        x, y, c = _position()
        copies = []
        for a in range(n):
            cp = pltpu.make_async_remote_copy(
                src_ref=ins[a], dst_ref=outs[a], send_sem=send_sems.at[a], recv_sem=recv_sems.at[a],
                device_id=(x, y, 1 - c), device_id_type=MESH)
            cp.start()
            copies.append(cp)
        for cp in copies:
            cp.wait()

    return _comm_call(body, arrs, [S(a.shape, a.dtype) for a in arrs], n, name=name)


def pair_add(a, b, *, name):
    k, r, c = a.shape
    tr = _tile(r, (256, 128))

    def body(a_ref, b_ref, o_ref):
        o_ref[...] = (a_ref[...].astype(F32) + b_ref[...].astype(F32)).astype(BF16)

    blk = pl.BlockSpec((1, tr, c), lambda j, i: (j, i, 0))
    return _call(body, name=name, out_shape=S((k, r, c), BF16), grid=(k, r // tr), in_specs=[blk, blk], out_specs=blk,
                 sem=("parallel", "parallel"))(a, b)


def _local_step(p, x, target):
    bsz, seq, d = x.shape
    lp = LEAD + seq
    m = bsz * lp
    h0 = jnp.concatenate([jnp.zeros((bsz, PAD, d), F32), jnp.broadcast_to(p["meta"][None], (bsz, N_META, d)), x],
                         axis=1).reshape(m, d)
    u0 = rmsnorm_fwd(h0, p["even_norm"], name="norm0")
    proj = mm_nn([u0], p["win_e"], name="proj0")
    lx = conv_fwd(proj, COL_LRU_X, D_MODEL, p["lru_conv_w"], p["lru_conv_b"], name="lru_conv")
    rowmask = jnp.tile((jnp.arange(lp) >= PAD).astype(F32), bsz).reshape(m, 1)
    a, b = gates_fwd(lx, rowmask, p["lru_w_a"], p["lru_b_a"], p["lru_w_x"], p["lru_b_x"], p["lru_lambda"],
                     name="lru_gates")
    hs = scan_fwd(a, b, lp, name="lru_scan")
    ya = gate_fwd(hs, 0, proj, COL_LRU_G // D_MODEL, name="lru_out_gate", o_scan=True)
    pre = conv_fwd(proj, COL_XBC, SSD_CONV_DIM, p["ssd_conv_w"], p["ssd_conv_b"], name="ssd_conv")
    y, s_in = ssd_fwd(pre, proj, p["ssd_dt_bias"], p["ssd_a_log"], p["ssd_d"], lp, name="ssd")
    yb = gnorm_fwd(y, proj, COL_Z // D_MODEL, p["ssd_norm"], name="ssd_norm")
    h1 = mm_nn([ya, yb], p["wout_e"], resid=h0, name="out0")
    u1 = rmsnorm_fwd(h1, p["odd_norm"], name="norm1")
    qkvg = mm_nn([u1], p["win_o"], name="proj1")
    o = attn_fwd(qkvg, lp, name="attn")
    og = gate_fwd(o, 0, qkvg, 3, name="attn_gate")
    h2 = mm_nn([og], p["wout_o"], resid=h1, name="out1")
    dh2, dh2b, loss, d_final = loss_head(h2, target, p["final_norm"], lp, name="loss_head")
    g = {"final_norm": d_final}
    g["odd_w_out"] = mm_tn(og, dh2b, name="dw_out1")
    d_og = mm_nn([dh2b], p["wout_o_t"], name="d_out1")
    do, dgate = gate_bwd(d_og, 0, o, 0, qkvg, 3, name="attn_gate_bwd")
    dq, dk, dv = attn_bwd(qkvg, do, lp, name="attn_bwd")
    segs1 = [dq, dk, dv, dgate]
    du1 = mm_nn(segs1, p["win_o_t"], name="d_proj1")
    g["odd_w_in"] = jnp.concatenate([mm_tn(u1, t, name=f"dw_proj1_{k}") for k, t in enumerate(segs1)], axis=1)
    dh1, dh1b, g["odd_norm"] = rmsnorm_bwd(du1, h1, p["odd_norm"], dh2, name="norm1_bwd")
    g["even_w_out"] = jnp.concatenate([mm_tn(ya, dh1b, name="dw_out0_a"), mm_tn(yb, dh1b, name="dw_out0_b")], axis=0)
    d_mixed = mm_nn([dh1b], p["wout_e_t"], name="d_out0")
    dhs, dlg = gate_bwd(d_mixed, 0, hs, 0, proj, COL_LRU_G // D_MODEL, name="lru_out_gate_bwd", o_scan=True)
    dy, dz, g["ssd_norm"] = gnorm_bwd(d_mixed, 1, y, proj, COL_Z // D_MODEL, p["ssd_norm"], name="ssd_norm_bwd")
    dpre, ddt, g["ssd_dt_bias"], g["ssd_a_log"], g["ssd_d"] = ssd_bwd(
        pre, proj, s_in, dy, p["ssd_dt_bias"], p["ssd_a_log"], p["ssd_d"], lp, name="ssd_bwd")
    dxbc, g["ssd_conv_w"], g["ssd_conv_b"] = conv_bwd(proj, COL_XBC, SSD_CONV_DIM, dpre, p["ssd_conv_w"],
                                                      name="ssd_conv_bwd")
    da, db = scan_bwd(a, hs, dhs, lp, name="lru_scan_bwd")
    dlx, g["lru_w_a"], g["lru_w_x"], g["lru_b_a"], g["lru_b_x"], g["lru_lambda"] = gates_bwd(
        lx, rowmask, da, db, p["lru_w_a"], p["lru_b_a"], p["lru_w_x"], p["lru_b_x"], p["lru_lambda"],
        name="lru_gates_bwd")
    dlrux, g["lru_conv_w"], g["lru_conv_b"] = conv_bwd(proj, COL_LRU_X, D_MODEL, dlx, p["lru_conv_w"],
                                                       name="lru_conv_bwd")
    segs0 = [dlrux, dlg, dz, dxbc, ddt]
    du0 = mm_nn(segs0, p["win_e_t"], name="d_proj0")
    g["even_w_in"] = jnp.concatenate([mm_tn(u0, t, name=f"dw_proj0_{k}") for k, t in enumerate(segs0)],
                                     axis=1)[:, :EVEN_IN]
    dh0, _, g["even_norm"] = rmsnorm_bwd(du0, h0, p["even_norm"], dh1, name="norm0_bwd")
    g["meta"] = meta_grad(dh0, lp, name="meta_grad")
    grad_x = dh0.reshape(bsz, lp, d)[:, LEAD:]
    return loss, grad_x, g


WEIGHTS = ("meta", "even_norm", "even_w_in", "lru_conv_w", "lru_conv_b", "lru_w_a", "lru_b_a", "lru_w_x", "lru_b_x",
           "lru_lambda", "ssd_conv_w", "ssd_conv_b", "ssd_dt_bias", "ssd_a_log", "ssd_d", "ssd_norm", "even_w_out",
           "odd_norm", "odd_w_in", "odd_w_out", "final_norm")
BIG = ("even_w_in", "even_w_out", "odd_w_in", "odd_w_out", "lru_w_a", "lru_w_x")
SHARDED_SMALL = {"meta": 256, "lru_conv_w": 256, "ssd_conv_w": 384, "odd_norm": 256}
SMALL_SHAPES = {"meta": (16, 1024), "even_norm": (1, 1024), "lru_conv_w": (4, 1024), "lru_conv_b": (1, 1024),
                "lru_b_a": (1, 1024), "lru_b_x": (1, 1024), "lru_lambda": (1, 1024), "ssd_conv_w": (4, 1536),
                "ssd_conv_b": (1, 1536), "ssd_dt_bias": (1, 16), "ssd_a_log": (1, 16), "ssd_d": (1, 16),
                "ssd_norm": (1, 1024), "odd_norm": (1, 1024), "final_norm": (1, 1024)}
PACK_UNIT = 1024


def _pack(parts):
    flat = []
    for part in parts:
        v = part.reshape(-1)
        flat.append(jnp.pad(v, (0, -v.shape[0] % PACK_UNIT)))
    return jnp.concatenate(flat).reshape(-1, 128)


def _unpack(buf, shapes):
    v = buf.reshape(-1)
    out, off = [], 0
    for shp in shapes:
        n = 1
        for s_ in shp:
            n *= s_
        out.append(v[off:off + n].reshape(shp))
        off += n + (-n % PACK_UNIT)
    return out


def _chip_cols(a4):
    return jnp.transpose(a4, (1, 0, 2)).reshape(a4.shape[1], -1)


def _to_chip_cols(a, cols):
    return jnp.transpose(a.reshape(a.shape[0], 4, cols), (1, 0, 2))


def kernel(x, meta, even_norm, even_w_in, lru_conv_w, lru_conv_b, lru_w_a, lru_b_a, lru_w_x, lru_b_x, lru_lambda, ssd_conv_w, ssd_conv_b, ssd_dt_bias, ssd_a_log, ssd_d, ssd_norm, even_w_out, odd_norm, odd_w_in, odd_w_out, final_norm, loss_target, m_meta, m_even_norm, m_even_w_in, m_lru_conv_w, m_lru_conv_b, m_lru_w_a, m_lru_b_a, m_lru_w_x, m_lru_b_x, m_lru_lambda, m_ssd_conv_w, m_ssd_conv_b, m_ssd_dt_bias, m_ssd_a_log, m_ssd_d, m_ssd_norm, m_even_w_out, m_odd_norm, m_odd_w_in, m_odd_w_out, m_final_norm, v_meta, v_even_norm, v_even_w_in, v_lru_conv_w, v_lru_conv_b, v_lru_w_a, v_lru_b_a, v_lru_w_x, v_lru_b_x, v_lru_lambda, v_ssd_conv_w, v_ssd_conv_b, v_ssd_dt_bias, v_ssd_a_log, v_ssd_d, v_ssd_norm, v_even_w_out, v_odd_norm, v_odd_w_in, v_odd_w_out, v_final_norm):
    given = dict(locals())
    w = {n: given[n] for n in WEIGHTS}
    mom = {n: given["m_" + n] for n in WEIGHTS}
    var = {n: given["v_" + n] for n in WEIGHTS}
    chip = 2 * lax.axis_index("x") + lax.axis_index("y")

    big_local = {"even_w_in": even_w_in[0], "even_w_out": even_w_out[0], "odd_w_in": odd_w_in[0],
                 "odd_w_out": odd_w_out[0], "lru_w_a": lru_w_a[0].reshape(256, 256),
                 "lru_w_x": lru_w_x[0].reshape(256, 256)}
    sharded_local = [meta, lru_conv_w[0], ssd_conv_w[0], odd_norm]
    gathered = allgather_chips([big_local[n].astype(BF16) for n in BIG] + [_pack(sharded_local)], name="gather_weights")
    gb = dict(zip(BIG, gathered[:-1]))
    per_chip = [_unpack(gathered[-1][k], [a.shape for a in sharded_local]) for k in range(4)]
    full_small = [jnp.concatenate([per_chip[k][j] for k in range(4)], axis=-1) for j in range(len(sharded_local))]

    def lru_full(a4):
        return jnp.transpose(a4.reshape(4, LRU_BLOCKS, 64, LRU_BLOCK), (1, 0, 2, 3)).reshape(LRU_BLOCKS, LRU_BLOCK, LRU_BLOCK)

    p = {"meta": full_small[0], "lru_conv_w": full_small[1], "ssd_conv_w": full_small[2], "odd_norm": full_small[3],
         "even_norm": even_norm, "lru_conv_b": lru_conv_b, "lru_b_a": lru_b_a, "lru_b_x": lru_b_x,
         "lru_lambda": lru_lambda, "ssd_conv_b": ssd_conv_b, "ssd_dt_bias": ssd_dt_bias, "ssd_a_log": ssd_a_log,
         "ssd_d": ssd_d, "ssd_norm": ssd_norm, "final_norm": final_norm.reshape(1, D_MODEL)}
    p["win_e"] = jnp.pad(_chip_cols(gb["even_w_in"]), ((0, 0), (0, EVEN_IN_P - EVEN_IN)))
    p["wout_e"] = gb["even_w_out"].reshape(2 * D_MODEL, D_MODEL)
    p["win_o"] = _chip_cols(gb["odd_w_in"])
    p["wout_o"] = gb["odd_w_out"].reshape(D_MODEL, D_MODEL)
    for n in ("win_e", "wout_e", "win_o", "wout_o"):
        p[n + "_t"] = p[n].T
    p["lru_w_a"] = lru_full(gb["lru_w_a"])
    p["lru_w_x"] = lru_full(gb["lru_w_x"])

    loss_part, grad_x, g = _local_step(p, x, loss_target)

    def lru_slabs(a):
        return jnp.transpose(a.reshape(LRU_BLOCKS, 4, 64, LRU_BLOCK), (1, 0, 2, 3)).reshape(4, 256, LRU_BLOCK)

    def by_half(t):
        return jnp.transpose(t.reshape(4, 2, t.shape[1] // 2, t.shape[2]), (1, 0, 2, 3)).astype(BF16)

    slabs = [_to_chip_cols(g["even_w_in"], EVEN_IN // 4), g["even_w_out"].reshape(4, 512, D_MODEL),
             _to_chip_cols(g["odd_w_in"], D_MODEL), g["odd_w_out"].reshape(4, 256, D_MODEL),
             lru_slabs(g["lru_w_a"]), lru_slabs(g["lru_w_x"])]
    small_names = list(SMALL_SHAPES)
    small_part = _pack([loss_part[0:1, 0:1]] + [g[n] for n in small_names])
    core = lax.axis_index("c")
    halves = [by_half(t) for t in slabs]
    mine = [lax.dynamic_index_in_dim(t, core, 0, keepdims=False) for t in halves]
    theirs = swap_cores([lax.dynamic_index_in_dim(t, 1 - core, 0, keepdims=False) for t in halves], name="swap_partials")
    pair_sums = [pair_add(a, b, name=f"pair_add_{n}") for a, b, n in zip(mine, theirs, BIG)]
    *recv, small_all = exchange_chips(pair_sums, small_part, name="exchange_grads")
    half_sums = [sum_lead(r, name=f"sum_chips_{n}") for n, r in zip(BIG, recv)]
    other_half = swap_cores(half_sums, name="swap_halves")
    full_sums = [jnp.where(core == 0, jnp.concatenate([a, b], axis=0), jnp.concatenate([b, a], axis=0))
                 for a, b in zip(half_sums, other_half)]
    small_sum = sum_lead(small_all, name="sum_small")
    small_g = dict(zip(["loss"] + small_names, _unpack(small_sum, [(1, 1)] + [SMALL_SHAPES[n] for n in small_names])))
    loss = small_g["loss"].reshape(())

    grads, delta, new_m, new_v = {}, {}, {}, {}
    for n, total in zip(BIG, full_sums):
        shp = w[n].shape
        two_d = lambda t: t.reshape(total.shape)
        res = adamw(two_d(w[n]), [total], two_d(mom[n]), two_d(var[n]), name=f"adamw_{n}")
        grads[n], delta[n], new_m[n], new_v[n] = (r.reshape(shp) for r in res)
    local_g = []
    for n in small_names:
        gn = small_g[n]
        if n in SHARDED_SMALL:
            gn = lax.dynamic_slice_in_dim(gn, chip * SHARDED_SMALL[n], SHARDED_SMALL[n], axis=1)
        local_g.append(gn)
    res = adamw(_pack([w[n] for n in small_names]), [_pack(local_g)], _pack([mom[n] for n in small_names]),
                _pack([var[n] for n in small_names]), name="adamw_small")
    shapes = [w[n].shape for n in small_names]
    for out, r in zip((grads, delta, new_m, new_v), res):
        out.update(dict(zip(small_names, _unpack(r, shapes))))
    return (loss, grad_x, *[grads[n] for n in WEIGHTS], *[delta[n] for n in WEIGHTS], *[new_m[n] for n in WEIGHTS],
            *[new_v[n] for n in WEIGHTS])
```

```python
import functools

import jax
import jax.numpy as jnp
from jax import lax
from jax.experimental import pallas as pl
from jax.experimental.pallas import tpu as pltpu

F32 = jnp.float32
BF16 = jnp.bfloat16

D_MODEL = 1024
N_META = 16
LEAD = 128
PAD = LEAD - N_META
EPS = 1e-6
CONV_W = 4
LRU_BLOCKS = 4
LRU_BLOCK = 256
RG_LRU_C = 8.0
SSD_HEADS = 16
SSD_HEAD_DIM = 64
SSD_GROUPS = 2
SSD_HPG = 8
SSD_STATE = 128
CHUNK = 128
SSD_CONV_DIM = 1536
EVEN_IN = 4624
EVEN_IN_P = 4736
COL_LRU_X, COL_LRU_G, COL_Z, COL_XBC, COL_DT = 0, 1024, 2048, 3072, 4608
SB_HEADS = 16
SB_HEAD_DIM = 64
SB_BLOCK = 128
ADAM_LR, ADAM_B1, ADAM_B2, ADAM_EPS, ADAM_WD, ADAM_STEP = 0.001, 0.9, 0.999, 1e-08, 0.01, 10
VMEM_LIMIT_V7X = 56 * 1024 * 1024
MESH = pl.DeviceIdType.MESH
S = jax.ShapeDtypeStruct


def _tile(n, prefs):
    for p in prefs:
        if n % p == 0:
            return p
    raise ValueError(f"no tile of {prefs} divides {n}")


def _call(body, *, name, out_shape, grid=(), in_specs=None, out_specs=None, scratch=(), sem=None):
    kw = {}
    if in_specs is not None:
        kw["in_specs"] = in_specs
    if out_specs is not None:
        kw["out_specs"] = out_specs
    return pl.pallas_call(
        body, out_shape=out_shape, grid=grid, scratch_shapes=tuple(scratch), name=name,
        compiler_params=pltpu.CompilerParams(dimension_semantics=sem, vmem_limit_bytes=VMEM_LIMIT_V7X), **kw)


def _sigmoid(x):
    return 0.5 * (jnp.tanh(0.5 * x) + 1.0)


def _silu(x):
    return x * _sigmoid(x)


def _softplus(x):
    return jnp.maximum(x, 0.0) + jnp.log(1.0 + jnp.exp(-jnp.abs(x)))


def _dot(a, b):
    return jnp.dot(a, b, preferred_element_type=F32)


def _dot_nt(a, b):
    return lax.dot_general(a, b, (((1,), (1,)), ((), ())), preferred_element_type=F32)


def _dot_tn(a, b):
    return lax.dot_general(a, b, (((0,), (0,)), ((), ())), preferred_element_type=F32)


def mm_nn(a_list, w, *, name, resid=None):
    m = a_list[0].shape[0]
    k_tot, n = w.shape
    ks = [a.shape[1] for a in a_list]
    assert sum(ks) == k_tot
    tm = _tile(m, (256, 128))
    n_a = len(a_list)
    offs = [sum(ks[:i]) for i in range(n_a)]
    n_chunks = [(c0, min(512, n - c0)) for c0 in range(0, n, 512)]

    def body(*refs):
        a_refs, w_ref = refs[:n_a], refs[n_a]
        r_ref = refs[n_a + 1] if resid is not None else None
        o_ref = refs[-1]
        for c0, cw in n_chunks:
            acc = None
            for a_ref, k0, k in zip(a_refs, offs, ks):
                p = _dot(a_ref[...], w_ref[k0:k0 + k, c0:c0 + cw])
                acc = p if acc is None else acc + p
            if r_ref is not None:
                acc = acc + r_ref[:, c0:c0 + cw]
            o_ref[:, c0:c0 + cw] = acc

    in_specs = [pl.BlockSpec((tm, k), lambda i: (i, 0)) for k in ks]
    in_specs.append(pl.BlockSpec((k_tot, n), lambda i: (0, 0)))
    args = list(a_list) + [w]
    if resid is not None:
        in_specs.append(pl.BlockSpec((tm, n), lambda i: (i, 0)))
        args.append(resid)
    return _call(body, name=name, out_shape=S((m, n), F32), grid=(m // tm,), in_specs=in_specs,
                 out_specs=pl.BlockSpec((tm, n), lambda i: (i, 0)), sem=("parallel",))(*args)


def mm_tn(a, g, *, name):
    t, m = a.shape
    n = g.shape[1]
    tk = _tile(t, (512, 256, 128))
    tn = _tile(n, (1024, 512, 256, 128))
    nk = t // tk

    def body(a_ref, g_ref, o_ref, acc):
        k = pl.program_id(1)

        @pl.when(k == 0)
        def _():
            acc[...] = jnp.zeros_like(acc)
        acc[...] += _dot_tn(a_ref[...], g_ref[...])

        @pl.when(k == nk - 1)
        def _():
            o_ref[...] = acc[...].astype(BF16)

    return _call(body, name=name, out_shape=S((m, n), BF16), grid=(n // tn, nk),
                 in_specs=[pl.BlockSpec((tk, m), lambda j, k: (k, 0)), pl.BlockSpec((tk, tn), lambda j, k: (k, j))],
                 out_specs=pl.BlockSpec((m, tn), lambda j, k: (0, j)), scratch=[pltpu.VMEM((m, tn), F32)],
                 sem=("parallel", "arbitrary"))(a, g)


def rmsnorm_fwd(h, w, *, name):
    m, d = h.shape
    tm = _tile(m, (512, 256, 128))

    def body(h_ref, w_ref, o_ref):
        x = h_ref[...]
        r = lax.rsqrt(jnp.mean(x * x, axis=-1, keepdims=True) + EPS)
        o_ref[...] = (x * r * w_ref[...]).astype(BF16)

    return _call(body, name=name, out_shape=S((m, d), BF16), grid=(m // tm,),
                 in_specs=[pl.BlockSpec((tm, d), lambda i: (i, 0)), pl.BlockSpec((1, d), lambda i: (0, 0))],
                 out_specs=pl.BlockSpec((tm, d), lambda i: (i, 0)), sem=("parallel",))(h, w)


def rmsnorm_bwd(du, h, w, dres, *, name):
    m, d = h.shape
    tm = _tile(m, (256, 128))

    def body(du_ref, h_ref, w_ref, dr_ref, dh_ref, dhb_ref, dw_ref):
        @pl.when(pl.program_id(0) == 0)
        def _():
            dw_ref[...] = jnp.zeros_like(dw_ref)
        x = h_ref[...]
        r = lax.rsqrt(jnp.mean(x * x, axis=-1, keepdims=True) + EPS)
        du_ = du_ref[...]
        g = du_ * w_ref[...]
        c = jnp.mean(g * x, axis=-1, keepdims=True)
        dh = dr_ref[...] + r * g - x * (r * r * r) * c
        dh_ref[...] = dh
        dhb_ref[...] = dh.astype(BF16)
        dw_ref[...] += jnp.sum(du_ * x * r, axis=0, keepdims=True)

    row = pl.BlockSpec((tm, d), lambda i: (i, 0))
    vec = pl.BlockSpec((1, d), lambda i: (0, 0))
    return _call(body, name=name, out_shape=(S((m, d), F32), S((m, d), BF16), S((1, d), F32)), grid=(m // tm,),
                 in_specs=[row, row, vec, row], out_specs=(row, row, vec), sem=("arbitrary",))(du, h, w, dres)


def gate_fwd(o, o_cb, g, g_cb, *, name, o_scan=False):
    m = g.shape[0]
    d = D_MODEL
    tm = _tile(m, (256, 128))

    def body(o_ref, g_ref, y_ref):
        ov = _from_scan_layout(o_ref, 0, d, tm) if o_scan else o_ref[...]
        y_ref[...] = (ov * _silu(g_ref[...])).astype(BF16)

    o_spec = pl.BlockSpec((tm * 8, 128), lambda i: (i, 0)) if o_scan else pl.BlockSpec((tm, d), lambda i: (i, o_cb))
    return _call(body, name=name, out_shape=S((m, d), BF16), grid=(m // tm,),
                 in_specs=[o_spec, pl.BlockSpec((tm, d), lambda i: (i, g_cb))],
                 out_specs=pl.BlockSpec((tm, d), lambda i: (i, 0)), sem=("parallel",))(o, g)


def gate_bwd(dy, dy_cb, o, o_cb, g, g_cb, *, name, o_scan=False):
    m = g.shape[0]
    d = D_MODEL
    tm = _tile(m, (256, 128))

    def body(dy_ref, o_ref, g_ref, do_ref, dg_ref):
        gv = g_ref[...]
        s = _sigmoid(gv)
        dyv = dy_ref[...]
        do = dyv * gv * s
        if o_scan:
            _to_scan_layout(do_ref, do, 0, tm)
            ov = _from_scan_layout(o_ref, 0, d, tm)
        else:
            do_ref[...] = do
            ov = o_ref[...]
        dg_ref[...] = (dyv * ov * (s + gv * s * (1.0 - s))).astype(BF16)

    nat = pl.BlockSpec((tm, d), lambda i: (i, 0))
    scn = pl.BlockSpec((tm * 8, 128), lambda i: (i, 0))
    o_spec = scn if o_scan else pl.BlockSpec((tm, d), lambda i: (i, o_cb))
    do_shape = S((m * 8, 128), F32) if o_scan else S((m, d), F32)
    return _call(body, name=name, out_shape=(do_shape, S((m, d), BF16)), grid=(m // tm,),
                 in_specs=[pl.BlockSpec((tm, d), lambda i: (i, dy_cb)), o_spec, pl.BlockSpec((tm, d), lambda i: (i, g_cb))],
                 out_specs=(scn if o_scan else nat, nat), sem=("parallel",))(dy, o, g)


def _group_mean(x):
    half = x.shape[1] // SSD_GROUPS
    parts = [jnp.broadcast_to(jnp.mean(x[:, k * half:(k + 1) * half], axis=-1, keepdims=True), (x.shape[0], half))
             for k in range(SSD_GROUPS)]
    return jnp.concatenate(parts, axis=1)


def gnorm_fwd(y, z, z_cb, w, *, name):
    m = y.shape[0]
    d = D_MODEL
    tm = _tile(m, (256, 128))

    def body(y_ref, z_ref, w_ref, o_ref):
        g = y_ref[...] * _silu(z_ref[...])
        r = lax.rsqrt(_group_mean(g * g) + EPS)
        o_ref[...] = (g * r * w_ref[...]).astype(BF16)

    return _call(body, name=name, out_shape=S((m, d), BF16), grid=(m // tm,),
                 in_specs=[pl.BlockSpec((tm, d), lambda i: (i, 0)), pl.BlockSpec((tm, d), lambda i: (i, z_cb)),
                           pl.BlockSpec((1, d), lambda i: (0, 0))],
                 out_specs=pl.BlockSpec((tm, d), lambda i: (i, 0)), sem=("parallel",))(y, z, w)


def gnorm_bwd(do, do_cb, y, z, z_cb, w, *, name):
    m = y.shape[0]
    d = D_MODEL
    tm = _tile(m, (256, 128))

    def body(do_ref, y_ref, z_ref, w_ref, dy_ref, dz_ref, dw_ref):
        @pl.when(pl.program_id(0) == 0)
        def _():
            dw_ref[...] = jnp.zeros_like(dw_ref)
        yv, zv, dov = y_ref[...], z_ref[...], do_ref[...]
        s = _sigmoid(zv)
        sz = zv * s
        g = yv * sz
        r = lax.rsqrt(_group_mean(g * g) + EPS)
        dw_ref[...] += jnp.sum(dov * g * r, axis=0, keepdims=True)
        dn = dov * w_ref[...]
        dg = r * dn - g * (r * r * r) * _group_mean(dn * g)
        dy_ref[...] = dg * sz
        dz_ref[...] = (dg * yv * (s + zv * s * (1.0 - s))).astype(BF16)

    row = pl.BlockSpec((tm, d), lambda i: (i, 0))
    vec = pl.BlockSpec((1, d), lambda i: (0, 0))
    return _call(body, name=name, out_shape=(S((m, d), F32), S((m, d), BF16), S((1, d), F32)), grid=(m // tm,),
                 in_specs=[pl.BlockSpec((tm, d), lambda i: (i, do_cb)), row, pl.BlockSpec((tm, d), lambda i: (i, z_cb)), vec],
                 out_specs=(row, row, vec), sem=("arbitrary",))(do, y, z, w)


def loss_head(h, target, w, lp, *, name):
    m, d = h.shape
    bsz = m // lp
    tm = SB_BLOCK
    nblk = lp // tm
    lead_blk = LEAD // tm

    def body(h_ref, t_ref, w_ref, dh_ref, dhb_ref, l_ref, dw_ref):
        i = pl.program_id(1)

        @pl.when(jnp.logical_and(pl.program_id(0) == 0, i == 0))
        def _():
            l_ref[...] = jnp.zeros_like(l_ref)
            dw_ref[...] = jnp.zeros_like(dw_ref)

        @pl.when(i < lead_blk)
        def _():
            dh_ref[...] = jnp.zeros_like(dh_ref)
            dhb_ref[...] = jnp.zeros_like(dhb_ref)

        @pl.when(i >= lead_blk)
        def _():
            x = h_ref[...]
            r = lax.rsqrt(jnp.mean(x * x, axis=-1, keepdims=True) + EPS)
            wv = w_ref[...]
            e = x * r * wv - t_ref[0]
            l_ref[...] += 0.5 * jnp.sum(jnp.mean(e * e, axis=-1, keepdims=True))
            dy = e * (1.0 / d)
            g = dy * wv
            c = jnp.mean(g * x, axis=-1, keepdims=True)
            dh = r * g - x * (r * r * r) * c
            dh_ref[...] = dh
            dhb_ref[...] = dh.astype(BF16)
            dw_ref[...] += jnp.sum(dy * x * r, axis=0, keepdims=True)

    row = pl.BlockSpec((tm, d), lambda b, i: (b * nblk + i, 0))
    return _call(body, name=name, out_shape=(S((m, d), F32), S((m, d), BF16), S((8, 128), F32), S((1, d), F32)),
                 grid=(bsz, nblk),
                 in_specs=[row, pl.BlockSpec((1, tm, d), lambda b, i: (b, jnp.maximum(i - lead_blk, 0), 0)),
                           pl.BlockSpec((1, d), lambda b, i: (0, 0))],
                 out_specs=(row, row, pl.BlockSpec((8, 128), lambda b, i: (0, 0)),
                            pl.BlockSpec((1, d), lambda b, i: (0, 0))),
                 sem=("arbitrary", "arbitrary"))(h, target, w)


def _conv_tiles(m, c):
    return _tile(m, (256, 128)), _tile(c, (512, 256, 128))


def _shift_down(a, first):
    row = lax.broadcasted_iota(jnp.int32, a.shape, 0)
    return jnp.where(row == 0, first, pltpu.roll(a, 1, 0))


def _shift_up(a, last):
    row = lax.broadcasted_iota(jnp.int32, a.shape, 0)
    return jnp.where(row == a.shape[0] - 1, last, pltpu.roll(a, a.shape[0] - 1, 0))


def conv_fwd(x, col0, c, w, b, *, name):
    m = x.shape[0]
    tm, tc = _conv_tiles(m, c)
    assert col0 % tc == 0
    cb0 = col0 // tc
    hb = tm // 8

    def body(x_ref, halo_ref, w_ref, b_ref, o_ref):
        xv = x_ref[...]
        w = [w_ref[k:k + 1, :] for k in range(CONV_W)]
        before = [halo_ref[8 - d:9 - d, :] for d in (1, 2, 3)]
        acc = w[0] * xv
        first = w[0] * before[0]
        for k in (1, 2):
            acc = w[k] * xv + _shift_down(acc, first)
            first = sum(w[k - d] * before[d] for d in range(k + 1))
        o_ref[...] = b_ref[...] + w[3] * xv + _shift_down(acc, first)

    return _call(body, name=name, out_shape=S((m, c), F32), grid=(m // tm, c // tc),
                 in_specs=[pl.BlockSpec((tm, tc), lambda i, j: (i, cb0 + j)),
                           pl.BlockSpec((8, tc), lambda i, j: (jnp.maximum(i * hb - 1, 0), cb0 + j)),
                           pl.BlockSpec((CONV_W, tc), lambda i, j: (0, j)), pl.BlockSpec((1, tc), lambda i, j: (0, j))],
                 out_specs=pl.BlockSpec((tm, tc), lambda i, j: (i, j)),
                 sem=("parallel", "parallel"))(x, x, w, b)


def conv_bwd(x, col0, c, dy, w, *, name):
    m = x.shape[0]
    tm, tc = _conv_tiles(m, c)
    cb0 = col0 // tc
    n_i = m // tm
    hb = tm // 8

    def body(x_ref, xh_ref, dy_ref, dyn_ref, w_ref, dx_ref, dw_ref, db_ref):
        i = pl.program_id(1)

        @pl.when(i == 0)
        def _():
            dw_ref[...] = jnp.zeros_like(dw_ref)
            db_ref[...] = jnp.zeros_like(db_ref)

        w = [w_ref[k:k + 1, :] for k in range(CONV_W)]
        d_cur = dy_ref[...]
        after = [jnp.where(i == n_i - 1, 0.0, dyn_ref[d:d + 1, :]) for d in range(3)]
        acc = w[0] * d_cur
        last = w[0] * after[0]
        for k in (1, 2):
            acc = w[k] * d_cur + _shift_up(acc, last)
            last = sum(w[k - d] * after[d] for d in range(k + 1))
        dx_ref[...] = (w[3] * d_cur + _shift_up(acc, last)).astype(BF16)
        xs = x_ref[...]
        dw_ref[3:4, :] += jnp.sum(xs * d_cur, axis=0, keepdims=True)
        for d in (1, 2, 3):
            xs = _shift_down(xs, xh_ref[8 - d:9 - d, :])
            dw_ref[3 - d:4 - d, :] += jnp.sum(xs * d_cur, axis=0, keepdims=True)
        db_ref[...] += jnp.sum(d_cur, axis=0, keepdims=True)

    return _call(body, name=name, out_shape=(S((m, c), BF16), S((CONV_W, c), F32), S((1, c), F32)),
                 grid=(c // tc, n_i),
                 in_specs=[pl.BlockSpec((tm, tc), lambda j, i: (i, cb0 + j)),
                           pl.BlockSpec((8, tc), lambda j, i: (jnp.maximum(i * hb - 1, 0), cb0 + j)),
                           pl.BlockSpec((tm, tc), lambda j, i: (i, j)),
                           pl.BlockSpec((8, tc), lambda j, i: (jnp.minimum((i + 1) * hb, n_i * hb - 1), j)),
                           pl.BlockSpec((CONV_W, tc), lambda j, i: (0, j))],
                 out_specs=(pl.BlockSpec((tm, tc), lambda j, i: (i, j)), pl.BlockSpec((CONV_W, tc), lambda j, i: (0, j)),
                            pl.BlockSpec((1, tc), lambda j, i: (0, j))),
                 sem=("parallel", "arbitrary"))(x, x, dy, dy, w)


def _row_valid(tile_idx, tiles_per_seq, tm):
    pos = lax.rem(tile_idx, tiles_per_seq) * tm + lax.broadcasted_iota(jnp.int32, (tm, 1), 0)
    return (pos >= PAD).astype(F32)


def _neg_expm1(x):
    small = -(x * (1.0 + x * (0.5 + x * (1.0 / 6.0))))
    return jnp.where(x > -0.01, small, 1.0 - jnp.exp(x))


def _to_scan_layout(ref, val, col0, tm):
    for k in range(val.shape[1] // 128):
        ref[pl.ds(col0 // 128 + k, tm, stride=8), :] = val[:, k * 128:(k + 1) * 128]


def _from_scan_layout(ref, col0, width, tm):
    parts = [ref[pl.ds(col0 // 128 + k, tm, stride=8), :] for k in range(width // 128)]
    return parts[0] if len(parts) == 1 else jnp.concatenate(parts, axis=1)


def _gates_core(lx, wa, ba, wx, bx, lam):
    lxb = lx.astype(BF16)
    r = _sigmoid(_dot(lxb, wa) + ba)
    i = _sigmoid(_dot(lxb, wx) + bx)
    sp = _softplus(-lam)
    log_a = (-RG_LRU_C) * r * sp
    a = jnp.exp(log_a)
    mult = jnp.sqrt(_neg_expm1(2.0 * log_a))
    return lxb, r, i, sp, a, mult


def gates_fwd(lx, rowmask, wa, ba, wx, bx, lam, *, name):
    m = lx.shape[0]
    tm = _tile(m, (256, 128))
    cb = LRU_BLOCK

    def body(lx_ref, msk_ref, wa_ref, ba_ref, wx_ref, bx_ref, lam_ref, a_ref, b_ref):
        msk = msk_ref[...]
        for g in range(LRU_BLOCKS):
            ch = slice(g * cb, (g + 1) * cb)
            lxv = lx_ref[:, ch]
            _, _, i, _, a, mult = _gates_core(lxv, wa_ref[g], ba_ref[:, ch], wx_ref[g], bx_ref[:, ch], lam_ref[:, ch])
            _to_scan_layout(a_ref, a, g * cb, tm)
            _to_scan_layout(b_ref, msk * (mult * i * lxv), g * cb, tm)

    tok = pl.BlockSpec((tm, D_MODEL), lambda i: (i, 0))
    msk = pl.BlockSpec((tm, 1), lambda i: (i, 0))
    wsp = pl.BlockSpec((LRU_BLOCKS, cb, cb), lambda i: (0, 0, 0))
    vec = pl.BlockSpec((1, D_MODEL), lambda i: (0, 0))
    scn = pl.BlockSpec((tm * 8, 128), lambda i: (i, 0))
    return _call(body, name=name, out_shape=(S((m * 8, 128), F32),) * 2, grid=(m // tm,),
                 in_specs=[tok, msk, wsp, vec, wsp, vec, vec], out_specs=(scn, scn),
                 sem=("parallel",))(lx, rowmask, wa, ba, wx, bx, lam)


def gates_bwd(lx, rowmask, da, db, wa, ba, wx, bx, lam, *, name):
    m = lx.shape[0]
    tm = _tile(m, (256, 128))
    cb = LRU_BLOCK

    def body(lx_ref, msk_ref, da_ref, db_ref, wa_ref, ba_ref, wx_ref, bx_ref, lam_ref,
             dlx_ref, dwa_ref, dwx_ref, dba_ref, dbx_ref, dlam_ref):
        @pl.when(pl.program_id(0) == 0)
        def _():
            for ref in (dwa_ref, dwx_ref, dba_ref, dbx_ref, dlam_ref):
                ref[...] = jnp.zeros_like(ref)
        msk = msk_ref[...]
        for g in range(LRU_BLOCKS):
            ch = slice(g * cb, (g + 1) * cb)
            lxv = lx_ref[:, ch]
            lamv = lam_ref[:, ch]
            lxb, r, i, sp, a, mult = _gates_core(lxv, wa_ref[g], ba_ref[:, ch], wx_ref[g], bx_ref[:, ch], lamv)
            dbv = msk * _from_scan_layout(db_ref, g * cb, cb, tm)
            d_mult = dbv * (i * lxv)
            d_i = dbv * (mult * lxv)
            d_log_a = _from_scan_layout(da_ref, g * cb, cb, tm) * a - d_mult * (a * a) / mult
            d_pa = (d_log_a * ((-RG_LRU_C) * sp)) * (r * (1.0 - r))
            d_px = d_i * (i * (1.0 - i))
            d_pa16 = d_pa.astype(BF16)
            d_px16 = d_px.astype(BF16)
            dlx_ref[:, ch] = dbv * (mult * i) + _dot_nt(d_pa16, wa_ref[g]) + _dot_nt(d_px16, wx_ref[g])
            dwa_ref[g] += _dot_tn(lxb, d_pa16)
            dwx_ref[g] += _dot_tn(lxb, d_px16)
            dba_ref[:, ch] += jnp.sum(d_pa, axis=0, keepdims=True)
            dbx_ref[:, ch] += jnp.sum(d_px, axis=0, keepdims=True)
            d_sp = jnp.sum(d_log_a * ((-RG_LRU_C) * r), axis=0, keepdims=True)
            dlam_ref[:, ch] += -d_sp * _sigmoid(-lamv)

    tok = pl.BlockSpec((tm, D_MODEL), lambda i: (i, 0))
    msk = pl.BlockSpec((tm, 1), lambda i: (i, 0))
    scn = pl.BlockSpec((tm * 8, 128), lambda i: (i, 0))
    wsp = pl.BlockSpec((LRU_BLOCKS, cb, cb), lambda i: (0, 0, 0))
    vec = pl.BlockSpec((1, D_MODEL), lambda i: (0, 0))
    wshape = S((LRU_BLOCKS, cb, cb), F32)
    vshape = S((1, D_MODEL), F32)
    return _call(body, name=name, out_shape=(S((m, D_MODEL), F32), wshape, wshape, vshape, vshape, vshape),
                 grid=(m // tm,), in_specs=[tok, msk, scn, scn, wsp, vec, wsp, vec, vec],
                 out_specs=(tok, wsp, wsp, vec, vec, vec),
                 sem=("arbitrary",))(lx, rowmask, da, db, wa, ba, wx, bx, lam)


SCAN_TOK = 128


def scan_fwd(a, b, lp, *, name):
    m = a.shape[0] // 8
    bsz = m // lp
    nch = lp // SCAN_TOK
    rows = SCAN_TOK * 8

    def body(a_ref, b_ref, h_ref, carry):
        @pl.when(pl.program_id(0) == 0)
        def _():
            carry[...] = jnp.zeros_like(carry)

        def step(t, hs):
            r = pl.ds(pl.multiple_of(t * 8, 8), 8)
            out = []
            for s, h in enumerate(hs):
                h = a_ref[s, r, :] * h + b_ref[s, r, :]
                h_ref[s, r, :] = h
                out.append(h)
            return tuple(out)

        hs = lax.fori_loop(0, SCAN_TOK, step, tuple(carry[s] for s in range(bsz)), unroll=4)
        for s in range(bsz):
            carry[s] = hs[s]

    blk = pl.BlockSpec((bsz, rows, 128), lambda c: (0, c, 0))
    shape3 = (bsz, lp * 8, 128)
    out = _call(body, name=name, out_shape=S(shape3, F32), grid=(nch,), in_specs=[blk, blk], out_specs=blk,
                scratch=[pltpu.VMEM((bsz, 8, 128), F32)], sem=("arbitrary",))(a.reshape(shape3), b.reshape(shape3))
    return out.reshape(m * 8, 128)


def scan_bwd(a, h, dh, lp, *, name):
    m = a.shape[0] // 8
    bsz = m // lp
    nch = lp // SCAN_TOK
    rows = SCAN_TOK * 8

    def body(a_ref, h_ref, hprev_ref, dh_ref, da_ref, db_ref, carry):
        c = pl.program_id(0)

        @pl.when(c == 0)
        def _():
            carry[...] = jnp.zeros_like(carry)

        h_before = [jnp.where(c == nch - 1, 0.0, hprev_ref[s]) for s in range(bsz)]

        def step(k, ags):
            t = SCAN_TOK - 1 - k
            r = pl.ds(pl.multiple_of(t * 8, 8), 8)
            rp = pl.ds(pl.multiple_of(jnp.maximum(t - 1, 0) * 8, 8), 8)
            out = []
            for s, ag in enumerate(ags):
                g = dh_ref[s, r, :] + ag
                db_ref[s, r, :] = g
                da_ref[s, r, :] = g * jnp.where(t == 0, h_before[s], h_ref[s, rp, :])
                out.append(a_ref[s, r, :] * g)
            return tuple(out)

        ags = lax.fori_loop(0, SCAN_TOK, step, tuple(carry[s] for s in range(bsz)), unroll=4)
        for s in range(bsz):
            carry[s] = ags[s]

    blk = pl.BlockSpec((bsz, rows, 128), lambda c: (0, nch - 1 - c, 0))
    prev = pl.BlockSpec((bsz, 8, 128), lambda c: (0, jnp.maximum((nch - 1 - c) * SCAN_TOK - 1, 0), 0))
    shape3 = (bsz, lp * 8, 128)
    a3, h3, dh3 = (v.reshape(shape3) for v in (a, h, dh))
    da, db = _call(body, name=name, out_shape=(S(shape3, F32),) * 2, grid=(nch,),
                   in_specs=[blk, blk, prev, blk], out_specs=(blk, blk), scratch=[pltpu.VMEM((bsz, 8, 128), F32)],
                   sem=("arbitrary",))(a3, h3, h3, dh3)
    return da.reshape(m * 8, 128), db.reshape(m * 8, 128)


SSD_SEQS = 2
SSD_SEQS_BWD = 1
SSD_SROWS = SSD_HEADS * SSD_HEAD_DIM


def _ssd_chunk(seqs, dtb, alog, dpar, valid):
    row = lax.broadcasted_iota(jnp.int32, (CHUNK, CHUNK), 0)
    col = lax.broadcasted_iota(jnp.int32, (CHUNK, CHUNK), 1)
    tri = row >= col
    neg_a = -jnp.exp(alog)
    dts, acums, acum_ts, b16s, c16s = [], [], [], [], []
    for xs, bs, cs, dtr, ss in seqs:
        dt = _softplus(dtr + dtb) * valid
        acum = jnp.dot(tri.astype(F32), dt * neg_a, precision=lax.Precision.HIGHEST, preferred_element_type=F32)
        dts.append(dt)
        acums.append(acum)
        acum_ts.append(acum.T)
        b16s.append([(_silu(b) * valid).astype(BF16) for b in bs])
        c16s.append([(_silu(c) * valid).astype(BF16) for c in cs])
    cbs = [[_dot_nt(c, b) for c, b in zip(c16, b16)] for c16, b16 in zip(c16s, b16s)]
    idx = [(q, h) for q in range(len(seqs)) for h in range(SSD_HEADS)]
    grp = [h // SSD_HPG for _, h in idx]
    x = [_silu(seqs[q][0][h]) for q, h in idx]
    s_in = [seqs[q][4][h] for q, h in idx]
    ac = [acums[q][:, h:h + 1] for q, h in idx]
    alast = [acums[q][CHUNK - 1:CHUNK, h:h + 1] for q, h in idx]
    xd = [x[n] * dts[q][:, h:h + 1] for n, (q, h) in enumerate(idx)]
    lhs = [(cbs[q][grp[n]] * jnp.exp(jnp.where(tri, ac[n] - acum_ts[q][h:h + 1, :], -1e30))).astype(BF16)
           for n, (q, h) in enumerate(idx)]
    xd16 = [v.astype(BF16) for v in xd]
    xdec16 = [(xd[n] * jnp.exp(alast[n] - ac[n])).astype(BF16) for n in range(len(idx))]
    s16 = [v.astype(BF16) for v in s_in]
    y_diag = [_dot(lhs[n], xd16[n]) for n in range(len(idx))]
    y_off = [_dot_nt(c16s[q][grp[n]], s16[n]) for n, (q, _) in enumerate(idx)]
    st = [_dot_tn(xdec16[n], b16s[q][grp[n]]) for n, (q, _) in enumerate(idx)]
    ys = [y_diag[n] + y_off[n] * jnp.exp(ac[n]) + x[n] * dpar[:, h:h + 1] for n, (_, h) in enumerate(idx)]
    s_new = [jnp.exp(alast[n]) * s_in[n] + st[n] for n in range(len(idx))]
    return [(ys[q * SSD_HEADS:(q + 1) * SSD_HEADS], s_new[q * SSD_HEADS:(q + 1) * SSD_HEADS])
            for q in range(len(seqs))]


def _ssd_load(pre_ref, dt_ref, s_ref, q):
    xs = [pre_ref[:, h * SSD_HEAD_DIM:(h + 1) * SSD_HEAD_DIM] for h in range(SSD_HEADS)]
    b0 = SSD_HEADS * SSD_HEAD_DIM
    bs = [pre_ref[:, b0 + g * SSD_STATE:b0 + (g + 1) * SSD_STATE] for g in range(SSD_GROUPS)]
    c0 = b0 + SSD_GROUPS * SSD_STATE
    cs = [pre_ref[:, c0 + g * SSD_STATE:c0 + (g + 1) * SSD_STATE] for g in range(SSD_GROUPS)]
    r0 = q * SSD_SROWS
    ss = [s_ref[r0 + h * SSD_HEAD_DIM:r0 + (h + 1) * SSD_HEAD_DIM, :] for h in range(SSD_HEADS)]
    return xs, bs, cs, dt_ref[:, 0:SSD_HEADS], ss


def ssd_fwd(pre, proj, dtb, alog, dpar, lp, *, name):
    m = pre.shape[0]
    bsz = m // lp
    nc = lp // CHUNK
    nq = SSD_SEQS
    assert bsz % nq == 0

    def body(pre_ref, dt_ref, dtb_ref, alog_ref, d_ref, y_ref, sin_ref, state):
        c = pl.program_id(1)

        @pl.when(c == 0)
        def _():
            state[...] = jnp.zeros_like(state)

        for q in range(nq):
            sin_ref[q] = state[q * SSD_SROWS:(q + 1) * SSD_SROWS, :]
        seqs = [_ssd_load(pre_ref.at[q], dt_ref.at[q], state, q) for q in range(nq)]
        valid = _row_valid(c, nc, CHUNK)
        res = _ssd_chunk(seqs, dtb_ref[...], alog_ref[...], d_ref[...], valid)
        for q, (ys, s_new) in enumerate(res):
            for h in range(SSD_HEADS):
                y_ref[q, :, h * SSD_HEAD_DIM:(h + 1) * SSD_HEAD_DIM] = ys[h]
                r0 = q * SSD_SROWS + h * SSD_HEAD_DIM
                state[r0:r0 + SSD_HEAD_DIM, :] = s_new[h]

    par = pl.BlockSpec((1, SSD_HEADS), lambda s, c: (0, 0))
    y, s_in = _call(
        body, name=name, out_shape=(S((bsz, lp, D_MODEL), F32), S((bsz, nc * SSD_SROWS, SSD_STATE), F32)),
        grid=(bsz // nq, nc),
        in_specs=[pl.BlockSpec((nq, CHUNK, SSD_CONV_DIM), lambda s, c: (s, c, 0)),
                  pl.BlockSpec((nq, CHUNK, 128), lambda s, c: (s, c, COL_DT // 128)), par, par, par],
        out_specs=(pl.BlockSpec((nq, CHUNK, D_MODEL), lambda s, c: (s, c, 0)),
                   pl.BlockSpec((nq, SSD_SROWS, SSD_STATE), lambda s, c: (s, c, 0))),
        scratch=[pltpu.VMEM((nq * SSD_SROWS, SSD_STATE), F32)],
        sem=("parallel", "arbitrary"))(pre.reshape(bsz, lp, -1), proj.reshape(bsz, lp, -1), dtb, alog, dpar)
    return y.reshape(m, D_MODEL), s_in


def ssd_bwd(pre, proj, s_in, dy, dtb, alog, dpar, lp, *, name):
    m = pre.shape[0]
    bsz = m // lp
    nc = lp // CHUNK
    nq = SSD_SEQS_BWD

    def body(pre_ref, dt_ref, sin_ref, dy_ref, dtb_ref, alog_ref, d_ref,
             dpre_ref, ddt_ref, ddtb_ref, dalog_ref, dd_ref, dstate):
        c = pl.program_id(1)

        @pl.when(jnp.logical_and(pl.program_id(0) == 0, c == 0))
        def _():
            for r in (ddtb_ref, dalog_ref, dd_ref):
                r[...] = jnp.zeros_like(r)

        @pl.when(c == 0)
        def _():
            dstate[...] = jnp.zeros_like(dstate)

        seqs = [_ssd_load(pre_ref.at[q], dt_ref.at[q], sin_ref.at[q], 0) for q in range(nq)]
        valid = _row_valid(nc - 1 - c, nc, CHUNK)
        core = functools.partial(_ssd_chunk, valid=valid)
        _, vjp = jax.vjp(core, seqs, dtb_ref[...], alog_ref[...], d_ref[...])
        cot = []
        for q in range(nq):
            dys = [dy_ref[q, :, h * SSD_HEAD_DIM:(h + 1) * SSD_HEAD_DIM] for h in range(SSD_HEADS)]
            r0 = q * SSD_SROWS
            dsn = [dstate[r0 + h * SSD_HEAD_DIM:r0 + (h + 1) * SSD_HEAD_DIM, :] for h in range(SSD_HEADS)]
            cot.append((dys, dsn))
        dseqs, ddtb, dalog, dd = vjp(cot)
        b0 = SSD_HEADS * SSD_HEAD_DIM
        c0 = b0 + SSD_GROUPS * SSD_STATE
        ddt_ref[...] = jnp.zeros_like(ddt_ref)
        for q, (dxs, dbs, dcs, ddtr, dss) in enumerate(dseqs):
            for h in range(SSD_HEADS):
                dpre_ref[q, :, h * SSD_HEAD_DIM:(h + 1) * SSD_HEAD_DIM] = dxs[h]
                r0 = q * SSD_SROWS + h * SSD_HEAD_DIM
                dstate[r0:r0 + SSD_HEAD_DIM, :] = dss[h]
            for g in range(SSD_GROUPS):
                dpre_ref[q, :, b0 + g * SSD_STATE:b0 + (g + 1) * SSD_STATE] = dbs[g]
                dpre_ref[q, :, c0 + g * SSD_STATE:c0 + (g + 1) * SSD_STATE] = dcs[g]
            ddt_ref[q, :, 0:SSD_HEADS] = ddtr.astype(BF16)
        ddtb_ref[...] += ddtb
        dalog_ref[...] += dalog
        dd_ref[...] += dd

    par = pl.BlockSpec((1, SSD_HEADS), lambda s, c: (0, 0))
    rev = lambda s, c: (s, nc - 1 - c, 0)
    pshape = S((1, SSD_HEADS), F32)
    dpre, ddt, ddtb, dalog, dd = _call(
        body, name=name,
        out_shape=(S((bsz, lp, SSD_CONV_DIM), F32), S((bsz, lp, 128), BF16), pshape, pshape, pshape),
        grid=(bsz // nq, nc),
        in_specs=[pl.BlockSpec((nq, CHUNK, SSD_CONV_DIM), rev),
                  pl.BlockSpec((nq, CHUNK, 128), lambda s, c: (s, nc - 1 - c, COL_DT // 128)),
                  pl.BlockSpec((nq, SSD_SROWS, SSD_STATE), rev), pl.BlockSpec((nq, CHUNK, D_MODEL), rev), par, par, par],
        out_specs=(pl.BlockSpec((nq, CHUNK, SSD_CONV_DIM), rev), pl.BlockSpec((nq, CHUNK, 128), rev), par, par, par),
        scratch=[pltpu.VMEM((nq * SSD_SROWS, SSD_STATE), F32)],
        sem=("arbitrary", "arbitrary"))(pre.reshape(bsz, lp, -1), proj.reshape(bsz, lp, -1), s_in,
                                        dy.reshape(bsz, lp, -1), dtb, alog, dpar)
    return dpre.reshape(m, SSD_CONV_DIM), ddt.reshape(m, 128), ddtb, dalog, dd


SB_KEYS = 256


def _order_mats():
    r = lax.broadcasted_iota(jnp.int32, (SB_KEYS, SB_KEYS), 0)
    c = lax.broadcasted_iota(jnp.int32, (SB_KEYS, SB_KEYS), 1)
    return (r > c).astype(BF16), (r < c).astype(BF16)


def _split_dot(x, mat):
    hi = x.astype(BF16)
    lo = (x - hi.astype(F32)).astype(BF16)
    return _dot(hi, mat) + _dot(lo, mat)


def _sb_tiles(qs_, ks_, blocks_, jts, diff, col, m_later, masked):
    zs = [_dot_nt(q_i, k_t) for q_i, k_t in zip(qs_, ks_)]
    out = []
    for z, i, jt in zip(zs, blocks_, jts):
        if masked:
            valid = jnp.logical_and(diff > jt * SB_KEYS - i * SB_BLOCK, col >= PAD - jt * SB_KEYS)
            lk = jnp.where(valid, -_softplus(z), 0.0)
        else:
            valid, lk = None, -_softplus(z)
        out.append((valid, z, lk))
    sums = [_split_dot(lk, m_later) for _, _, lk in out]
    return [(valid, z, lk, tsum, tsum[:, 0:1] + lk[:, 0:1]) for (valid, z, lk), tsum in zip(out, sums)]


def _sb_iotas():
    row = lax.broadcasted_iota(jnp.int32, (SB_BLOCK, SB_KEYS), 0)
    col = lax.broadcasted_iota(jnp.int32, (SB_BLOCK, SB_KEYS), 1)
    return row - col, col


def _sb_rows(i, size):
    start = i * size
    return pl.ds(start if isinstance(start, int) else pl.multiple_of(start, size), size)


def _sb_fill(dst, src_ref, ln, lp, scale=None):
    v = src_ref[:, ln]
    dst[0:lp, :] = (v if scale is None else v * scale).astype(BF16)
    if dst.shape[0] > lp:
        dst[lp:, :] = jnp.zeros((dst.shape[0] - lp, dst.shape[1]), BF16)


def _sb_schedule(nb, run_blocks):
    def pair(a, _):
        run_blocks([2 * a, 2 * a + 1], a + 1, 2)
        return 0
    for a in range(min(2, nb // 2)):
        run_blocks([2 * a, 2 * a + 1], a + 1, 2)
    lax.fori_loop(2, nb // 2, pair, 0)
    if nb % 2:
        run_blocks([nb - 1], (nb + 1) // 2, 2)


def _sb_steps(ntiles, stride):
    if isinstance(ntiles, int):
        return (ntiles + stride - 1) // stride
    return lax.div(ntiles + stride - 1, stride)


def _sb_sweep(nsteps, step, carry):
    carry = step(0, carry, True)
    if isinstance(nsteps, int) and nsteps == 1:
        return carry
    carry = lax.fori_loop(1, nsteps - 1, lambda s, c: step(s, c, False), carry)
    return step(nsteps - 1, carry, True)


def attn_fwd(qkvg, lp, *, name):
    m = qkvg.shape[0]
    bsz = m // lp
    nb = lp // SB_BLOCK
    nkt = (nb + 1) // 2
    hd = SB_HEAD_DIM

    def body(q_ref, k_ref, v_ref, o_ref, qs, ks, vs):
        m_later, _ = _order_mats()
        diff, col = _sb_iotas()
        for hh in range(2):
            ln = slice(hh * hd, (hh + 1) * hd)
            _sb_fill(qs.at[hh], q_ref, ln, lp, hd ** -0.5)
            _sb_fill(ks.at[hh], k_ref, ln, lp)
            _sb_fill(vs.at[hh], v_ref, ln, lp)

        def run_blocks(blocks, ntiles, stride):
            rows = [_sb_rows(i, SB_BLOCK) for i in blocks]
            groups = [(b, hh) for b in range(len(blocks)) for hh in range(2)]
            chains = [(g, sub) for g in range(len(groups)) for sub in range(stride)]
            q = [qs[groups[g][1], rows[groups[g][0]], :] for g, _ in chains]
            blk_of = [blocks[groups[g][0]] for g, _ in chains]
            heads = [groups[g][1] for g, _ in chains]

            def tile_step(s, carry, masked):
                jts = [ntiles - 1 - stride * s - sub for _, sub in chains]
                cols = [_sb_rows(jt if stride == 1 else jnp.maximum(jt, 0), SB_KEYS) for jt in jts]
                tiles = _sb_tiles(q, [ks[hh, c, :] for hh, c in zip(heads, cols)], blk_of, jts, diff, col, m_later, masked)
                ws = []
                for valid, z, lk, tsum, _ in tiles:
                    w = jnp.exp(z + lk + tsum)
                    ws.append((jnp.where(valid, w, 0.0) if masked else w).astype(BF16))
                pvs = [_dot(w, vs[hh, c, :]) for w, hh, c in zip(ws, heads, cols)]
                out = []
                for g, (acc, run) in enumerate(carry):
                    for n, (gn, _) in enumerate(chains):
                        if gn == g:
                            acc = acc + jnp.exp(run) * pvs[n]
                            run = run + tiles[n][4]
                    out.append((acc, run))
                return tuple(out)

            zero = (jnp.zeros((SB_BLOCK, hd), F32), jnp.zeros((SB_BLOCK, 1), F32))
            nsteps = _sb_steps(ntiles, stride)
            res = _sb_sweep(nsteps, tile_step, (zero,) * len(groups))
            for g, (b, hh) in enumerate(groups):
                o_ref[rows[b], hh * hd:(hh + 1) * hd] = res[g][0]

        _sb_schedule(nb, run_blocks)

    blk = lambda cb: pl.BlockSpec((lp, 128), lambda s, p: (s, cb * 8 + p))
    return _call(body, name=name, out_shape=S((m, D_MODEL), F32), grid=(bsz, 8),
                 in_specs=[blk(0), blk(1), blk(2)], out_specs=pl.BlockSpec((lp, 128), lambda s, p: (s, p)),
                 scratch=[pltpu.VMEM((2, lp, hd), BF16)] + [pltpu.VMEM((2, nkt * SB_KEYS, hd), BF16)] * 2,
                 sem=("parallel", "parallel"))(qkvg, qkvg, qkvg)


def attn_bwd(qkvg, do, lp, *, name):
    m = qkvg.shape[0]
    bsz = m // lp
    nb = lp // SB_BLOCK
    hd = SB_HEAD_DIM
    scale = hd ** -0.5

    nkt = (nb + 1) // 2

    def body(q_ref, k_ref, v_ref, do_ref, dq_ref, dk_ref, dv_ref, qs, ks, vs, dka, dva, g_keep, s_keep):
        m_later, m_earlier = _order_mats()
        diff, col = _sb_iotas()
        for hh in range(2):
            ln = slice(hh * hd, (hh + 1) * hd)
            _sb_fill(qs.at[hh], q_ref, ln, lp, scale)
            _sb_fill(ks.at[hh], k_ref, ln, lp)
            _sb_fill(vs.at[hh], v_ref, ln, lp)
        dka[...] = jnp.zeros_like(dka)
        dva[...] = jnp.zeros_like(dva)

        def run_blocks(blocks, ntiles, stride):
            rows = [_sb_rows(i, SB_BLOCK) for i in blocks]
            groups = [(b, hh) for b in range(len(blocks)) for hh in range(2)]
            chains = [(g, sub) for g in range(len(groups)) for sub in range(stride)]
            q = [qs[groups[g][1], rows[groups[g][0]], :] for g, _ in chains]
            do = [do_ref[rows[b], hh * hd:(hh + 1) * hd] for b, hh in groups]
            blk_of = [blocks[groups[g][0]] for g, _ in chains]
            heads = [groups[g][1] for g, _ in chains]
            nsteps = _sb_steps(ntiles, stride)

            def place(jt):
                if stride == 1:
                    return jt, _sb_rows(jt, SB_KEYS)
                inside = jnp.logical_and(jt >= 0, jt < ntiles)
                return jnp.where(inside, jt, nkt), _sb_rows(jnp.clip(jt, 0, ntiles - 1), SB_KEYS)

            def sweep_left(s, carry, masked):
                jts = [ntiles - 1 - stride * s - sub for _, sub in chains]
                slots, cols = zip(*[place(jt) for jt in jts])
                tiles = _sb_tiles(q, [ks[hh, c, :] for hh, c in zip(heads, cols)], blk_of, jts, diff, col, m_later, masked)
                runs, out = [None] * len(chains), []
                for g, run in enumerate(carry):
                    for n, (gn, _) in enumerate(chains):
                        if gn == g:
                            runs[n] = run
                            run = run + tiles[n][4]
                    out.append(run)
                do_run = [(do[g] * jnp.exp(run)).astype(BF16) for (g, _), run in zip(chains, runs)]
                dws = [_dot_nt(d, vs[hh, c, :]) for d, hh, c in zip(do_run, heads, cols)]
                ws = []
                for (g, _), slot, (valid, z, lk, tsum, _), dw in zip(chains, slots, tiles, dws):
                    sig = jnp.exp(z + lk)
                    if masked:
                        sig = jnp.where(valid, sig, 0.0)
                    w = sig * jnp.exp(tsum)
                    g_keep[g, slot] = dw * w
                    s_keep[g, slot] = sig
                    ws.append(w.astype(BF16))
                dvs = [_dot_tn(w, d) for w, d in zip(ws, do_run)]
                for hh, c, dv in zip(heads, cols, dvs):
                    dva[hh, c, :] += dv
                return tuple(out)

            _sb_sweep(nsteps, sweep_left, (jnp.zeros((SB_BLOCK, 1), F32),) * len(groups))

            def sweep_right(s, carry):
                jts = [stride * s + sub for _, sub in chains]
                slots, cols = zip(*[place(jt) for jt in jts])
                gmats = [g_keep[g, slot] for (g, _), slot in zip(chains, slots)]
                gsums = [_split_dot(gmat, m_earlier) for gmat in gmats]
                last = slice(SB_KEYS - 1, SB_KEYS)
                gruns, out_grun = [None] * len(chains), []
                for g, (_, grun) in enumerate(carry):
                    for n, (gn, _) in enumerate(chains):
                        if gn == g:
                            gruns[n] = grun
                            grun = grun + gsums[n][:, last] + gmats[n][:, last]
                    out_grun.append(grun)
                dzs = [(gmat - s_keep[g, slot] * (gmat + gsum + grun)).astype(BF16)
                       for (g, _), slot, gmat, gsum, grun in zip(chains, slots, gmats, gsums, gruns)]
                dqs = [_dot(dz, ks[hh, c, :]) for dz, hh, c in zip(dzs, heads, cols)]
                dks = [_dot_tn(dz, q_n) for dz, q_n in zip(dzs, q)]
                for hh, c, dk in zip(heads, cols, dks):
                    dka[hh, c, :] += dk
                out = []
                for g, (dq, _) in enumerate(carry):
                    for n, (gn, _) in enumerate(chains):
                        if gn == g:
                            dq = dq + dqs[n]
                    out.append((dq, out_grun[g]))
                return tuple(out)

            zero = (jnp.zeros((SB_BLOCK, hd), F32), jnp.zeros((SB_BLOCK, 1), F32))
            res = lax.fori_loop(0, nsteps, sweep_right, (zero,) * len(groups))
            for g, (b, hh) in enumerate(groups):
                dq_ref[rows[b], hh * hd:(hh + 1) * hd] = (res[g][0] * scale).astype(BF16)

        _sb_schedule(nb, run_blocks)
        for hh in range(2):
            dk_ref[:, hh * hd:(hh + 1) * hd] = dka[hh, 0:lp, :].astype(BF16)
            dv_ref[:, hh * hd:(hh + 1) * hd] = dva[hh, 0:lp, :].astype(BF16)

    blk = lambda cb: pl.BlockSpec((lp, 128), lambda s, p: (s, cb * 8 + p))
    one = pl.BlockSpec((lp, 128), lambda s, p: (s, p))
    keys = nkt * SB_KEYS
    return _call(body, name=name, out_shape=(S((m, D_MODEL), BF16),) * 3, grid=(bsz, 8),
                 in_specs=[blk(0), blk(1), blk(2), one], out_specs=(one, one, one),
                 scratch=[pltpu.VMEM((2, lp, hd), BF16)] + [pltpu.VMEM((2, keys, hd), BF16)] * 2
                 + [pltpu.VMEM((2, keys, hd), F32)] * 2 + [pltpu.VMEM((4, nkt + 1, SB_BLOCK, SB_KEYS), F32)] * 2,
                 sem=("parallel", "parallel"))(qkvg, qkvg, qkvg, do)


def meta_grad(dh, lp, *, name):
    m, d = dh.shape
    bsz = m // lp
    per = lp // N_META

    def body(dh_ref, o_ref):
        @pl.when(pl.program_id(0) == 0)
        def _():
            o_ref[...] = jnp.zeros_like(o_ref)
        o_ref[...] += dh_ref[...]

    return _call(body, name=name, out_shape=S((N_META, d), F32), grid=(bsz,),
                 in_specs=[pl.BlockSpec((N_META, d), lambda b: (b * per + PAD // N_META, 0))],
                 out_specs=pl.BlockSpec((N_META, d), lambda b: (0, 0)), sem=("arbitrary",))(dh)


def sum_lead(arr, *, name):
    n, r, c = arr.shape
    tr = _tile(r, (128, 64, 32, 16, 8))

    def body(a_ref, o_ref):
        acc = a_ref[0].astype(F32)
        for k in range(1, n):
            acc = acc + a_ref[k].astype(F32)
        o_ref[...] = acc

    return _call(body, name=name, out_shape=S((r, c), F32), grid=(r // tr,),
                 in_specs=[pl.BlockSpec((n, tr, c), lambda i: (0, i, 0))],
                 out_specs=pl.BlockSpec((tr, c), lambda i: (i, 0)), sem=("parallel",))(arr)


def adamw(w, g_parts, mom, var, *, name):
    r, c = w.shape
    tr = _tile(r, (128, 64, 32, 16, 8))
    n_g = len(g_parts)
    c1 = 1.0 - ADAM_B1 ** ADAM_STEP
    c2 = 1.0 - ADAM_B2 ** ADAM_STEP

    def body(*refs):
        w_ref, g_refs, m_ref, v_ref = refs[0], refs[1:1 + n_g], refs[1 + n_g], refs[2 + n_g]
        g_out, d_out, m_out, v_out = refs[3 + n_g:]
        g = g_refs[0][...]
        for gr in g_refs[1:]:
            g = g + gr[...]
        mn = ADAM_B1 * m_ref[...] + (1.0 - ADAM_B1) * g
        vn = ADAM_B2 * v_ref[...] + (1.0 - ADAM_B2) * (g * g)
        g_out[...] = g
        m_out[...] = mn
        v_out[...] = vn
        d_out[...] = -ADAM_LR * ((mn / c1) / (jnp.sqrt(vn / c2) + ADAM_EPS) + ADAM_WD * w_ref[...])

    blk = pl.BlockSpec((tr, c), lambda i: (i, 0))
    return _call(body, name=name, out_shape=(S((r, c), F32),) * 4, grid=(r // tr,), in_specs=[blk] * (3 + n_g),
                 out_specs=(blk,) * 4, sem=("parallel",))(w, *g_parts, mom, var)


_ANY = pl.BlockSpec(memory_space=pl.ANY)


def _position():
    return lax.axis_index("x"), lax.axis_index("y"), lax.axis_index("c")


def _other_chips(x, y):
    return [(1 - x, y), (x, 1 - y), (1 - x, 1 - y)]


def _comm_call(body, arrs, out_shapes, n_sem, *, name):
    return pl.pallas_call(
        body, out_shape=tuple(out_shapes), in_specs=[_ANY] * len(arrs), out_specs=tuple([_ANY] * len(out_shapes)),
        scratch_shapes=(pltpu.SemaphoreType.DMA((n_sem,)), pltpu.SemaphoreType.DMA((n_sem,)),
                        pltpu.SemaphoreType.DMA((len(arrs),))),
        name=name)(*arrs)


def allgather_chips(arrs, *, name):
    n = len(arrs)

    def body(*refs):
        ins, outs = refs[:n], refs[n:2 * n]
        send_sems, recv_sems, loc_sems = refs[2 * n:]
        x, y, c = _position()
        me = 2 * x + y
        chips = _other_chips(x, y)

        def half(ref, a, which):
            rows = arrs[a].shape[0] // 2
            return ref.at[pl.ds(which * rows, rows)]

        sent, passed = [], []
        for a in range(n):
            for k, (px, py) in enumerate(chips):
                cp = pltpu.make_async_remote_copy(
                    src_ref=half(ins[a], a, c), dst_ref=half(outs[a].at[me], a, c), send_sem=send_sems.at[6 * a + k],
                    recv_sem=recv_sems.at[6 * a + k], device_id=(px, py, c), device_id_type=MESH)
                cp.start()
                sent.append(cp)
        for a in range(n):
            for k, (px, py) in enumerate(chips):
                sent[3 * a + k].wait_recv()
                landed = half(outs[a].at[2 * px + py], a, c)
                cp = pltpu.make_async_remote_copy(
                    src_ref=landed, dst_ref=landed, send_sem=send_sems.at[6 * a + 3 + k],
                    recv_sem=recv_sems.at[6 * a + 3 + k], device_id=(x, y, 1 - c), device_id_type=MESH)
                cp.start()
                passed.append(cp)
        for cp in sent:
            cp.wait_send()
        for cp in passed:
            cp.wait()

    for a in arrs:
        assert a.shape[0] % 32 == 0
    outs = _comm_call(body, arrs, [S((4,) + a.shape, a.dtype) for a in arrs], 6 * n, name=name)
    chip = 2 * lax.axis_index("x") + lax.axis_index("y")
    return [lax.dynamic_update_index_in_dim(o, a, chip, 0) for o, a in zip(outs, arrs)]


def exchange_chips(arrs, small, *, name):
    n = len(arrs)

    def body(*refs):
        ins, small_in = refs[:n], refs[n]
        outs, small_out = refs[n + 1:2 * n + 1], refs[2 * n + 1]
        send_sems, recv_sems, loc_sems = refs[2 * n + 2:]
        x, y, c = _position()
        me = 2 * x + y
        copies = []
        for a in range(n):
            for k, (px, py) in enumerate(_other_chips(x, y)):
                cp = pltpu.make_async_remote_copy(
                    src_ref=ins[a].at[2 * px + py], dst_ref=outs[a].at[me], send_sem=send_sems.at[3 * a + k],
                    recv_sem=recv_sems.at[3 * a + k], device_id=(px, py, c), device_id_type=MESH)
                cp.start()
                copies.append(cp)
        me8 = 4 * x + 2 * y + c
        k = 3 * n
        for fx in (0, 1):
            for fy in (0, 1):
                for fc in (0, 1):
                    if fx + fy + fc == 0:
                        continue
                    peer = (1 - x if fx else x, 1 - y if fy else y, 1 - c if fc else c)
                    cp = pltpu.make_async_remote_copy(
                        src_ref=small_in, dst_ref=small_out.at[me8], send_sem=send_sems.at[k],
                        recv_sem=recv_sems.at[k], device_id=peer, device_id_type=MESH)
                    cp.start()
                    copies.append(cp)
                    k += 1
        for cp in copies:
            cp.wait()

    outs = [S(a.shape, a.dtype) for a in arrs] + [S((8,) + small.shape, small.dtype)]
    outs = _comm_call(body, list(arrs) + [small], outs, 3 * n + 7, name=name)
    x, y, c = _position()
    chip = 2 * x + y
    res = [lax.dynamic_update_index_in_dim(o, lax.dynamic_index_in_dim(a, chip, 0, keepdims=False), chip, 0)
           for o, a in zip(outs[:-1], arrs)]
    return res + [lax.dynamic_update_index_in_dim(outs[-1], small, 4 * x + 2 * y + c, 0)]


def swap_cores(arrs, *, name):
    n = len(arrs)

    def body(*refs):
        ins, outs = refs[:n], refs[n:2 * n]
        send_sems, recv_sems, _ = refs[2 * n:]
        x, y, c = _position()
        copies = []
        for a in range(n):
            cp = pltpu.make_async_remote_copy(
                src_ref=ins[a], dst_ref=outs[a], send_sem=send_sems.at[a], recv_sem=recv_sems.at[a],
                device_id=(x, y, 1 - c), device_id_type=MESH)
            cp.start()
            copies.append(cp)
        for cp in copies:
            cp.wait()

    return _comm_call(body, arrs, [S(a.shape, a.dtype) for a in arrs], n, name=name)


def pair_add(a, b, *, name):
    k, r, c = a.shape
    tr = _tile(r, (256, 128))

    def body(a_ref, b_ref, o_ref):
        o_ref[...] = (a_ref[...].astype(F32) + b_ref[...].astype(F32)).astype(BF16)

    blk = pl.BlockSpec((1, tr, c), lambda j, i: (j, i, 0))
    return _call(body, name=name, out_shape=S((k, r, c), BF16), grid=(k, r // tr), in_specs=[blk, blk], out_specs=blk,
                 sem=("parallel", "parallel"))(a, b)


def _local_step(p, x, target):
    bsz, seq, d = x.shape
    lp = LEAD + seq
    m = bsz * lp
    h0 = jnp.concatenate([jnp.zeros((bsz, PAD, d), F32), jnp.broadcast_to(p["meta"][None], (bsz, N_META, d)), x],
                         axis=1).reshape(m, d)
    u0 = rmsnorm_fwd(h0, p["even_norm"], name="norm0")
    proj = mm_nn([u0], p["win_e"], name="proj0")
    lx = conv_fwd(proj, COL_LRU_X, D_MODEL, p["lru_conv_w"], p["lru_conv_b"], name="lru_conv")
    rowmask = jnp.tile((jnp.arange(lp) >= PAD).astype(F32), bsz).reshape(m, 1)
    a, b = gates_fwd(lx, rowmask, p["lru_w_a"], p["lru_b_a"], p["lru_w_x"], p["lru_b_x"], p["lru_lambda"],
                     name="lru_gates")
    hs = scan_fwd(a, b, lp, name="lru_scan")
    ya = gate_fwd(hs, 0, proj, COL_LRU_G // D_MODEL, name="lru_out_gate", o_scan=True)
    pre = conv_fwd(proj, COL_XBC, SSD_CONV_DIM, p["ssd_conv_w"], p["ssd_conv_b"], name="ssd_conv")
    y, s_in = ssd_fwd(pre, proj, p["ssd_dt_bias"], p["ssd_a_log"], p["ssd_d"], lp, name="ssd")
    yb = gnorm_fwd(y, proj, COL_Z // D_MODEL, p["ssd_norm"], name="ssd_norm")
    h1 = mm_nn([ya, yb], p["wout_e"], resid=h0, name="out0")
    u1 = rmsnorm_fwd(h1, p["odd_norm"], name="norm1")
    qkvg = mm_nn([u1], p["win_o"], name="proj1")
    o = attn_fwd(qkvg, lp, name="attn")
    og = gate_fwd(o, 0, qkvg, 3, name="attn_gate")
    h2 = mm_nn([og], p["wout_o"], resid=h1, name="out1")
    dh2, dh2b, loss, d_final = loss_head(h2, target, p["final_norm"], lp, name="loss_head")
    g = {"final_norm": d_final}
    g["odd_w_out"] = mm_tn(og, dh2b, name="dw_out1")
    d_og = mm_nn([dh2b], p["wout_o_t"], name="d_out1")
    do, dgate = gate_bwd(d_og, 0, o, 0, qkvg, 3, name="attn_gate_bwd")
    dq, dk, dv = attn_bwd(qkvg, do, lp, name="attn_bwd")
    segs1 = [dq, dk, dv, dgate]
    du1 = mm_nn(segs1, p["win_o_t"], name="d_proj1")
    g["odd_w_in"] = jnp.concatenate([mm_tn(u1, t, name=f"dw_proj1_{k}") for k, t in enumerate(segs1)], axis=1)
    dh1, dh1b, g["odd_norm"] = rmsnorm_bwd(du1, h1, p["odd_norm"], dh2, name="norm1_bwd")
    g["even_w_out"] = jnp.concatenate([mm_tn(ya, dh1b, name="dw_out0_a"), mm_tn(yb, dh1b, name="dw_out0_b")], axis=0)
    d_mixed = mm_nn([dh1b], p["wout_e_t"], name="d_out0")
    dhs, dlg = gate_bwd(d_mixed, 0, hs, 0, proj, COL_LRU_G // D_MODEL, name="lru_out_gate_bwd", o_scan=True)
    dy, dz, g["ssd_norm"] = gnorm_bwd(d_mixed, 1, y, proj, COL_Z // D_MODEL, p["ssd_norm"], name="ssd_norm_bwd")
    dpre, ddt, g["ssd_dt_bias"], g["ssd_a_log"], g["ssd_d"] = ssd_bwd(
        pre, proj, s_in, dy, p["ssd_dt_bias"], p["ssd_a_log"], p["ssd_d"], lp, name="ssd_bwd")
    dxbc, g["ssd_conv_w"], g["ssd_conv_b"] = conv_bwd(proj, COL_XBC, SSD_CONV_DIM, dpre, p["ssd_conv_w"],
                                                      name="ssd_conv_bwd")
    da, db = scan_bwd(a, hs, dhs, lp, name="lru_scan_bwd")
    dlx, g["lru_w_a"], g["lru_w_x"], g["lru_b_a"], g["lru_b_x"], g["lru_lambda"] = gates_bwd(
        lx, rowmask, da, db, p["lru_w_a"], p["lru_b_a"], p["lru_w_x"], p["lru_b_x"], p["lru_lambda"],
        name="lru_gates_bwd")
    dlrux, g["lru_conv_w"], g["lru_conv_b"] = conv_bwd(proj, COL_LRU_X, D_MODEL, dlx, p["lru_conv_w"],
                                                       name="lru_conv_bwd")
    segs0 = [dlrux, dlg, dz, dxbc, ddt]
    du0 = mm_nn(segs0, p["win_e_t"], name="d_proj0")
    g["even_w_in"] = jnp.concatenate([mm_tn(u0, t, name=f"dw_proj0_{k}") for k, t in enumerate(segs0)],
                                     axis=1)[:, :EVEN_IN]
    dh0, _, g["even_norm"] = rmsnorm_bwd(du0, h0, p["even_norm"], dh1, name="norm0_bwd")
    g["meta"] = meta_grad(dh0, lp, name="meta_grad")
    grad_x = dh0.reshape(bsz, lp, d)[:, LEAD:]
    return loss, grad_x, g


WEIGHTS = ("meta", "even_norm", "even_w_in", "lru_conv_w", "lru_conv_b", "lru_w_a", "lru_b_a", "lru_w_x", "lru_b_x",
           "lru_lambda", "ssd_conv_w", "ssd_conv_b", "ssd_dt_bias", "ssd_a_log", "ssd_d", "ssd_norm", "even_w_out",
           "odd_norm", "odd_w_in", "odd_w_out", "final_norm")
BIG = ("even_w_in", "even_w_out", "odd_w_in", "odd_w_out", "lru_w_a", "lru_w_x")
SHARDED_SMALL = {"meta": 256, "lru_conv_w": 256, "ssd_conv_w": 384, "odd_norm": 256}
SMALL_SHAPES = {"meta": (16, 1024), "even_norm": (1, 1024), "lru_conv_w": (4, 1024), "lru_conv_b": (1, 1024),
                "lru_b_a": (1, 1024), "lru_b_x": (1, 1024), "lru_lambda": (1, 1024), "ssd_conv_w": (4, 1536),
                "ssd_conv_b": (1, 1536), "ssd_dt_bias": (1, 16), "ssd_a_log": (1, 16), "ssd_d": (1, 16),
                "ssd_norm": (1, 1024), "odd_norm": (1, 1024), "final_norm": (1, 1024)}
PACK_UNIT = 1024


def _pack(parts):
    flat = []
    for part in parts:
        v = part.reshape(-1)
        flat.append(jnp.pad(v, (0, -v.shape[0] % PACK_UNIT)))
    return jnp.concatenate(flat).reshape(-1, 128)


def _unpack(buf, shapes):
    v = buf.reshape(-1)
    out, off = [], 0
    for shp in shapes:
        n = 1
        for s_ in shp:
            n *= s_
        out.append(v[off:off + n].reshape(shp))
        off += n + (-n % PACK_UNIT)
    return out


def _chip_cols(a4):
    return jnp.transpose(a4, (1, 0, 2)).reshape(a4.shape[1], -1)


def _to_chip_cols(a, cols):
    return jnp.transpose(a.reshape(a.shape[0], 4, cols), (1, 0, 2))


def kernel(x, meta, even_norm, even_w_in, lru_conv_w, lru_conv_b, lru_w_a, lru_b_a, lru_w_x, lru_b_x, lru_lambda, ssd_conv_w, ssd_conv_b, ssd_dt_bias, ssd_a_log, ssd_d, ssd_norm, even_w_out, odd_norm, odd_w_in, odd_w_out, final_norm, loss_target, m_meta, m_even_norm, m_even_w_in, m_lru_conv_w, m_lru_conv_b, m_lru_w_a, m_lru_b_a, m_lru_w_x, m_lru_b_x, m_lru_lambda, m_ssd_conv_w, m_ssd_conv_b, m_ssd_dt_bias, m_ssd_a_log, m_ssd_d, m_ssd_norm, m_even_w_out, m_odd_norm, m_odd_w_in, m_odd_w_out, m_final_norm, v_meta, v_even_norm, v_even_w_in, v_lru_conv_w, v_lru_conv_b, v_lru_w_a, v_lru_b_a, v_lru_w_x, v_lru_b_x, v_lru_lambda, v_ssd_conv_w, v_ssd_conv_b, v_ssd_dt_bias, v_ssd_a_log, v_ssd_d, v_ssd_norm, v_even_w_out, v_odd_norm, v_odd_w_in, v_odd_w_out, v_final_norm):
    given = dict(locals())
    w = {n: given[n] for n in WEIGHTS}
    mom = {n: given["m_" + n] for n in WEIGHTS}
    var = {n: given["v_" + n] for n in WEIGHTS}
    chip = 2 * lax.axis_index("x") + lax.axis_index("y")

    big_local = {"even_w_in": even_w_in[0], "even_w_out": even_w_out[0], "odd_w_in": odd_w_in[0],
                 "odd_w_out": odd_w_out[0], "lru_w_a": lru_w_a[0].reshape(256, 256),
                 "lru_w_x": lru_w_x[0].reshape(256, 256)}
    sharded_local = [meta, lru_conv_w[0], ssd_conv_w[0], odd_norm]
    gathered = allgather_chips([big_local[n].astype(BF16) for n in BIG] + [_pack(sharded_local)], name="gather_weights")
    gb = dict(zip(BIG, gathered[:-1]))
    per_chip = [_unpack(gathered[-1][k], [a.shape for a in sharded_local]) for k in range(4)]
    full_small = [jnp.concatenate([per_chip[k][j] for k in range(4)], axis=-1) for j in range(len(sharded_local))]

    def lru_full(a4):
        return jnp.transpose(a4.reshape(4, LRU_BLOCKS, 64, LRU_BLOCK), (1, 0, 2, 3)).reshape(LRU_BLOCKS, LRU_BLOCK, LRU_BLOCK)

    p = {"meta": full_small[0], "lru_conv_w": full_small[1], "ssd_conv_w": full_small[2], "odd_norm": full_small[3],
         "even_norm": even_norm, "lru_conv_b": lru_conv_b, "lru_b_a": lru_b_a, "lru_b_x": lru_b_x,
         "lru_lambda": lru_lambda, "ssd_conv_b": ssd_conv_b, "ssd_dt_bias": ssd_dt_bias, "ssd_a_log": ssd_a_log,
         "ssd_d": ssd_d, "ssd_norm": ssd_norm, "final_norm": final_norm.reshape(1, D_MODEL)}
    p["win_e"] = jnp.pad(_chip_cols(gb["even_w_in"]), ((0, 0), (0, EVEN_IN_P - EVEN_IN)))
    p["wout_e"] = gb["even_w_out"].reshape(2 * D_MODEL, D_MODEL)
    p["win_o"] = _chip_cols(gb["odd_w_in"])
    p["wout_o"] = gb["odd_w_out"].reshape(D_MODEL, D_MODEL)
    for n in ("win_e", "wout_e", "win_o", "wout_o"):
        p[n + "_t"] = p[n].T
    p["lru_w_a"] = lru_full(gb["lru_w_a"])
    p["lru_w_x"] = lru_full(gb["lru_w_x"])

    loss_part, grad_x, g = _local_step(p, x, loss_target)

    def lru_slabs(a):
        return jnp.transpose(a.reshape(LRU_BLOCKS, 4, 64, LRU_BLOCK), (1, 0, 2, 3)).reshape(4, 256, LRU_BLOCK)

    def by_half(t):
        return jnp.transpose(t.reshape(4, 2, t.shape[1] // 2, t.shape[2]), (1, 0, 2, 3)).astype(BF16)

    slabs = [_to_chip_cols(g["even_w_in"], EVEN_IN // 4), g["even_w_out"].reshape(4, 512, D_MODEL),
             _to_chip_cols(g["odd_w_in"], D_MODEL), g["odd_w_out"].reshape(4, 256, D_MODEL),
             lru_slabs(g["lru_w_a"]), lru_slabs(g["lru_w_x"])]
    small_names = list(SMALL_SHAPES)
    small_part = _pack([loss_part[0:1, 0:1]] + [g[n] for n in small_names])
    core = lax.axis_index("c")
    halves = [by_half(t) for t in slabs]
    mine = [lax.dynamic_index_in_dim(t, core, 0, keepdims=False) for t in halves]
    theirs = swap_cores([lax.dynamic_index_in_dim(t, 1 - core, 0, keepdims=False) for t in halves], name="swap_partials")
    pair_sums = [pair_add(a, b, name=f"pair_add_{n}") for a, b, n in zip(mine, theirs, BIG)]
    *recv, small_all = exchange_chips(pair_sums, small_part, name="exchange_grads")
    half_sums = [sum_lead(r, name=f"sum_chips_{n}") for n, r in zip(BIG, recv)]
    other_half = swap_cores(half_sums, name="swap_halves")
    full_sums = [jnp.where(core == 0, jnp.concatenate([a, b], axis=0), jnp.concatenate([b, a], axis=0))
                 for a, b in zip(half_sums, other_half)]
    small_sum = sum_lead(small_all, name="sum_small")
    small_g = dict(zip(["loss"] + small_names, _unpack(small_sum, [(1, 1)] + [SMALL_SHAPES[n] for n in small_names])))
    loss = small_g["loss"].reshape(())

    grads, delta, new_m, new_v = {}, {}, {}, {}
    for n, total in zip(BIG, full_sums):
        shp = w[n].shape
        two_d = lambda t: t.reshape(total.shape)
        res = adamw(two_d(w[n]), [total], two_d(mom[n]), two_d(var[n]), name=f"adamw_{n}")
        grads[n], delta[n], new_m[n], new_v[n] = (r.reshape(shp) for r in res)
    local_g = []
    for n in small_names:
        gn = small_g[n]
        if n in SHARDED_SMALL:
            gn = lax.dynamic_slice_in_dim(gn, chip * SHARDED_SMALL[n], SHARDED_SMALL[n], axis=1)
        local_g.append(gn)
    res = adamw(_pack([w[n] for n in small_names]), [_pack(local_g)], _pack([mom[n] for n in small_names]),
                _pack([var[n] for n in small_names]), name="adamw_small")
    shapes = [w[n].shape for n in small_names]
    for out, r in zip((grads, delta, new_m, new_v), res):
        out.update(dict(zip(small_names, _unpack(r, shapes))))
    return (loss, grad_x, *[grads[n] for n in WEIGHTS], *[delta[n] for n in WEIGHTS], *[new_m[n] for n in WEIGHTS],
            *[new_v[n] for n in WEIGHTS])
```

```python
import functools

import jax
import jax.numpy as jnp
from jax import lax
from jax.experimental import pallas as pl
from jax.experimental.pallas import tpu as pltpu

F32 = jnp.float32
BF16 = jnp.bfloat16

D_MODEL = 1024
N_META = 16
LEAD = 128
PAD = LEAD - N_META
EPS = 1e-6
CONV_W = 4
LRU_BLOCKS = 4
LRU_BLOCK = 256
RG_LRU_C = 8.0
SSD_HEADS = 16
SSD_HEAD_DIM = 64
SSD_GROUPS = 2
SSD_HPG = 8
SSD_STATE = 128
CHUNK = 128
SSD_CONV_DIM = 1536
EVEN_IN = 4624
EVEN_IN_P = 4736
COL_LRU_X, COL_LRU_G, COL_Z, COL_XBC, COL_DT = 0, 1024, 2048, 3072, 4608
SB_HEADS = 16
SB_HEAD_DIM = 64
SB_BLOCK = 128
ADAM_LR, ADAM_B1, ADAM_B2, ADAM_EPS, ADAM_WD, ADAM_STEP = 0.001, 0.9, 0.999, 1e-08, 0.01, 10
VMEM_LIMIT_V7X = 56 * 1024 * 1024
MESH = pl.DeviceIdType.MESH
S = jax.ShapeDtypeStruct


def _tile(n, prefs):
    for p in prefs:
        if n % p == 0:
            return p
    raise ValueError(f"no tile of {prefs} divides {n}")


def _call(body, *, name, out_shape, grid=(), in_specs=None, out_specs=None, scratch=(), sem=None):
    kw = {}
    if in_specs is not None:
        kw["in_specs"] = in_specs
    if out_specs is not None:
        kw["out_specs"] = out_specs
    return pl.pallas_call(
        body, out_shape=out_shape, grid=grid, scratch_shapes=tuple(scratch), name=name,
        compiler_params=pltpu.CompilerParams(dimension_semantics=sem, vmem_limit_bytes=VMEM_LIMIT_V7X), **kw)


def _sigmoid(x):
    return 0.5 * (jnp.tanh(0.5 * x) + 1.0)


def _silu(x):
    return x * _sigmoid(x)


def _softplus(x):
    return jnp.maximum(x, 0.0) + jnp.log(1.0 + jnp.exp(-jnp.abs(x)))


def _dot(a, b):
    return jnp.dot(a, b, preferred_element_type=F32)


def _dot_nt(a, b):
    return lax.dot_general(a, b, (((1,), (1,)), ((), ())), preferred_element_type=F32)


def _dot_tn(a, b):
    return lax.dot_general(a, b, (((0,), (0,)), ((), ())), preferred_element_type=F32)


def mm_nn(a_list, w, *, name, resid=None):
    m = a_list[0].shape[0]
    k_tot, n = w.shape
    ks = [a.shape[1] for a in a_list]
    assert sum(ks) == k_tot
    tm = _tile(m, (256, 128))
    n_a = len(a_list)
    offs = [sum(ks[:i]) for i in range(n_a)]
    n_chunks = [(c0, min(512, n - c0)) for c0 in range(0, n, 512)]

    def body(*refs):
        a_refs, w_ref = refs[:n_a], refs[n_a]
        r_ref = refs[n_a + 1] if resid is not None else None
        o_ref = refs[-1]
        for c0, cw in n_chunks:
            acc = None
            for a_ref, k0, k in zip(a_refs, offs, ks):
                p = _dot(a_ref[...], w_ref[k0:k0 + k, c0:c0 + cw])
                acc = p if acc is None else acc + p
            if r_ref is not None:
                acc = acc + r_ref[:, c0:c0 + cw]
            o_ref[:, c0:c0 + cw] = acc

    in_specs = [pl.BlockSpec((tm, k), lambda i: (i, 0)) for k in ks]
    in_specs.append(pl.BlockSpec((k_tot, n), lambda i: (0, 0)))
    args = list(a_list) + [w]
    if resid is not None:
        in_specs.append(pl.BlockSpec((tm, n), lambda i: (i, 0)))
        args.append(resid)
    return _call(body, name=name, out_shape=S((m, n), F32), grid=(m // tm,), in_specs=in_specs,
                 out_specs=pl.BlockSpec((tm, n), lambda i: (i, 0)), sem=("parallel",))(*args)


def mm_tn(a, g, *, name):
    t, m = a.shape
    n = g.shape[1]
    tk = _tile(t, (2176, 1024, 512, 256, 128))
    tn = _tile(n, (1024, 512, 256, 128))
    nk = t // tk

    def body(a_ref, g_ref, o_ref, acc):
        k = pl.program_id(1)

        @pl.when(k == 0)
        def _():
            acc[...] = jnp.zeros_like(acc)
        acc[...] += _dot_tn(a_ref[...], g_ref[...])

        @pl.when(k == nk - 1)
        def _():
            o_ref[...] = acc[...].astype(BF16)

    return _call(body, name=name, out_shape=S((m, n), BF16), grid=(n // tn, nk),
                 in_specs=[pl.BlockSpec((tk, m), lambda j, k: (k, 0)), pl.BlockSpec((tk, tn), lambda j, k: (k, j))],
                 out_specs=pl.BlockSpec((m, tn), lambda j, k: (0, j)), scratch=[pltpu.VMEM((m, tn), F32)],
                 sem=("parallel", "arbitrary"))(a, g)


def rmsnorm_fwd(h, w, *, name):
    m, d = h.shape
    tm = _tile(m, (512, 256, 128))

    def body(h_ref, w_ref, o_ref):
        x = h_ref[...]
        r = lax.rsqrt(jnp.mean(x * x, axis=-1, keepdims=True) + EPS)
        o_ref[...] = (x * r * w_ref[...]).astype(BF16)

    return _call(body, name=name, out_shape=S((m, d), BF16), grid=(m // tm,),
                 in_specs=[pl.BlockSpec((tm, d), lambda i: (i, 0)), pl.BlockSpec((1, d), lambda i: (0, 0))],
                 out_specs=pl.BlockSpec((tm, d), lambda i: (i, 0)), sem=("parallel",))(h, w)


def rmsnorm_bwd(du, h, w, dres, *, name):
    m, d = h.shape
    tm = _tile(m, (256, 128))

    def body(du_ref, h_ref, w_ref, dr_ref, dh_ref, dhb_ref, dw_ref):
        @pl.when(pl.program_id(0) == 0)
        def _():
            dw_ref[...] = jnp.zeros_like(dw_ref)
        x = h_ref[...]
        r = lax.rsqrt(jnp.mean(x * x, axis=-1, keepdims=True) + EPS)
        du_ = du_ref[...]
        g = du_ * w_ref[...]
        c = jnp.mean(g * x, axis=-1, keepdims=True)
        dh = dr_ref[...] + r * g - x * (r * r * r) * c
        dh_ref[...] = dh
        dhb_ref[...] = dh.astype(BF16)
        dw_ref[...] += jnp.sum(du_ * x * r, axis=0, keepdims=True)

    row = pl.BlockSpec((tm, d), lambda i: (i, 0))
    vec = pl.BlockSpec((1, d), lambda i: (0, 0))
    return _call(body, name=name, out_shape=(S((m, d), F32), S((m, d), BF16), S((1, d), F32)), grid=(m // tm,),
                 in_specs=[row, row, vec, row], out_specs=(row, row, vec), sem=("arbitrary",))(du, h, w, dres)


def gate_fwd(o, o_cb, g, g_cb, *, name, o_scan=False):
    m = g.shape[0]
    d = D_MODEL
    tm = _tile(m, (256, 128))

    def body(o_ref, g_ref, y_ref):
        ov = _from_scan_layout(o_ref, 0, d, tm) if o_scan else o_ref[...]
        y_ref[...] = (ov * _silu(g_ref[...])).astype(BF16)

    o_spec = pl.BlockSpec((tm * 8, 128), lambda i: (i, 0)) if o_scan else pl.BlockSpec((tm, d), lambda i: (i, o_cb))
    return _call(body, name=name, out_shape=S((m, d), BF16), grid=(m // tm,),
                 in_specs=[o_spec, pl.BlockSpec((tm, d), lambda i: (i, g_cb))],
                 out_specs=pl.BlockSpec((tm, d), lambda i: (i, 0)), sem=("parallel",))(o, g)


def gate_bwd(dy, dy_cb, o, o_cb, g, g_cb, *, name, o_scan=False):
    m = g.shape[0]
    d = D_MODEL
    tm = _tile(m, (256, 128))

    def body(dy_ref, o_ref, g_ref, do_ref, dg_ref):
        gv = g_ref[...]
        s = _sigmoid(gv)
        dyv = dy_ref[...]
        do = dyv * gv * s
        if o_scan:
            _to_scan_layout(do_ref, do, 0, tm)
            ov = _from_scan_layout(o_ref, 0, d, tm)
        else:
            do_ref[...] = do
            ov = o_ref[...]
        dg_ref[...] = (dyv * ov * (s + gv * s * (1.0 - s))).astype(BF16)

    nat = pl.BlockSpec((tm, d), lambda i: (i, 0))
    scn = pl.BlockSpec((tm * 8, 128), lambda i: (i, 0))
    o_spec = scn if o_scan else pl.BlockSpec((tm, d), lambda i: (i, o_cb))
    do_shape = S((m * 8, 128), F32) if o_scan else S((m, d), F32)
    return _call(body, name=name, out_shape=(do_shape, S((m, d), BF16)), grid=(m // tm,),
                 in_specs=[pl.BlockSpec((tm, d), lambda i: (i, dy_cb)), o_spec, pl.BlockSpec((tm, d), lambda i: (i, g_cb))],
                 out_specs=(scn if o_scan else nat, nat), sem=("parallel",))(dy, o, g)


def _group_mean(x):
    half = x.shape[1] // SSD_GROUPS
    parts = [jnp.broadcast_to(jnp.mean(x[:, k * half:(k + 1) * half], axis=-1, keepdims=True), (x.shape[0], half))
             for k in range(SSD_GROUPS)]
    return jnp.concatenate(parts, axis=1)


def gnorm_fwd(y, z, z_cb, w, *, name):
    m = y.shape[0]
    d = D_MODEL
    tm = _tile(m, (256, 128))

    def body(y_ref, z_ref, w_ref, o_ref):
        g = y_ref[...] * _silu(z_ref[...])
        r = lax.rsqrt(_group_mean(g * g) + EPS)
        o_ref[...] = (g * r * w_ref[...]).astype(BF16)

    return _call(body, name=name, out_shape=S((m, d), BF16), grid=(m // tm,),
                 in_specs=[pl.BlockSpec((tm, d), lambda i: (i, 0)), pl.BlockSpec((tm, d), lambda i: (i, z_cb)),
                           pl.BlockSpec((1, d), lambda i: (0, 0))],
                 out_specs=pl.BlockSpec((tm, d), lambda i: (i, 0)), sem=("parallel",))(y, z, w)


def gnorm_bwd(do, do_cb, y, z, z_cb, w, *, name):
    m = y.shape[0]
    d = D_MODEL
    tm = _tile(m, (256, 128))

    def body(do_ref, y_ref, z_ref, w_ref, dy_ref, dz_ref, dw_ref):
        @pl.when(pl.program_id(0) == 0)
        def _():
            dw_ref[...] = jnp.zeros_like(dw_ref)
        yv, zv, dov = y_ref[...], z_ref[...], do_ref[...]
        s = _sigmoid(zv)
        sz = zv * s
        g = yv * sz
        r = lax.rsqrt(_group_mean(g * g) + EPS)
        dw_ref[...] += jnp.sum(dov * g * r, axis=0, keepdims=True)
        dn = dov * w_ref[...]
        dg = r * dn - g * (r * r * r) * _group_mean(dn * g)
        dy_ref[...] = dg * sz
        dz_ref[...] = (dg * yv * (s + zv * s * (1.0 - s))).astype(BF16)

    row = pl.BlockSpec((tm, d), lambda i: (i, 0))
    vec = pl.BlockSpec((1, d), lambda i: (0, 0))
    return _call(body, name=name, out_shape=(S((m, d), F32), S((m, d), BF16), S((1, d), F32)), grid=(m // tm,),
                 in_specs=[pl.BlockSpec((tm, d), lambda i: (i, do_cb)), row, pl.BlockSpec((tm, d), lambda i: (i, z_cb)), vec],
                 out_specs=(row, row, vec), sem=("arbitrary",))(do, y, z, w)


def loss_head(h, target, w, lp, *, name):
    m, d = h.shape
    bsz = m // lp
    tm = SB_BLOCK
    nblk = lp // tm
    lead_blk = LEAD // tm

    def body(h_ref, t_ref, w_ref, dh_ref, dhb_ref, l_ref, dw_ref):
        i = pl.program_id(1)

        @pl.when(jnp.logical_and(pl.program_id(0) == 0, i == 0))
        def _():
            l_ref[...] = jnp.zeros_like(l_ref)
            dw_ref[...] = jnp.zeros_like(dw_ref)

        @pl.when(i < lead_blk)
        def _():
            dh_ref[...] = jnp.zeros_like(dh_ref)
            dhb_ref[...] = jnp.zeros_like(dhb_ref)

        @pl.when(i >= lead_blk)
        def _():
            x = h_ref[...]
            r = lax.rsqrt(jnp.mean(x * x, axis=-1, keepdims=True) + EPS)
            wv = w_ref[...]
            e = x * r * wv - t_ref[0]
            l_ref[...] += 0.5 * jnp.sum(jnp.mean(e * e, axis=-1, keepdims=True))
            dy = e * (1.0 / d)
            g = dy * wv
            c = jnp.mean(g * x, axis=-1, keepdims=True)
            dh = r * g - x * (r * r * r) * c
            dh_ref[...] = dh
            dhb_ref[...] = dh.astype(BF16)
            dw_ref[...] += jnp.sum(dy * x * r, axis=0, keepdims=True)

    row = pl.BlockSpec((tm, d), lambda b, i: (b * nblk + i, 0))
    return _call(body, name=name, out_shape=(S((m, d), F32), S((m, d), BF16), S((8, 128), F32), S((1, d), F32)),
                 grid=(bsz, nblk),
                 in_specs=[row, pl.BlockSpec((1, tm, d), lambda b, i: (b, jnp.maximum(i - lead_blk, 0), 0)),
                           pl.BlockSpec((1, d), lambda b, i: (0, 0))],
                 out_specs=(row, row, pl.BlockSpec((8, 128), lambda b, i: (0, 0)),
                            pl.BlockSpec((1, d), lambda b, i: (0, 0))),
                 sem=("arbitrary", "arbitrary"))(h, target, w)


def _conv_tiles(m, c):
    return _tile(m, (256, 128)), _tile(c, (512, 256, 128))


def _shift_down(a, first):
    row = lax.broadcasted_iota(jnp.int32, a.shape, 0)
    return jnp.where(row == 0, first, pltpu.roll(a, 1, 0))


def _shift_up(a, last):
    row = lax.broadcasted_iota(jnp.int32, a.shape, 0)
    return jnp.where(row == a.shape[0] - 1, last, pltpu.roll(a, a.shape[0] - 1, 0))


def conv_fwd(x, col0, c, w, b, *, name):
    m = x.shape[0]
    tm, tc = _conv_tiles(m, c)
    assert col0 % tc == 0
    cb0 = col0 // tc
    hb = tm // 8

    def body(x_ref, halo_ref, w_ref, b_ref, o_ref):
        xv = x_ref[...]
        w = [w_ref[k:k + 1, :] for k in range(CONV_W)]
        before = [halo_ref[8 - d:9 - d, :] for d in (1, 2, 3)]
        acc = w[0] * xv
        first = w[0] * before[0]
        for k in (1, 2):
            acc = w[k] * xv + _shift_down(acc, first)
            first = sum(w[k - d] * before[d] for d in range(k + 1))
        o_ref[...] = b_ref[...] + w[3] * xv + _shift_down(acc, first)

    return _call(body, name=name, out_shape=S((m, c), F32), grid=(m // tm, c // tc),
                 in_specs=[pl.BlockSpec((tm, tc), lambda i, j: (i, cb0 + j)),
                           pl.BlockSpec((8, tc), lambda i, j: (jnp.maximum(i * hb - 1, 0), cb0 + j)),
                           pl.BlockSpec((CONV_W, tc), lambda i, j: (0, j)), pl.BlockSpec((1, tc), lambda i, j: (0, j))],
                 out_specs=pl.BlockSpec((tm, tc), lambda i, j: (i, j)),
                 sem=("parallel", "parallel"))(x, x, w, b)


def conv_bwd(x, col0, c, dy, w, *, name):
    m = x.shape[0]
    tm, tc = _conv_tiles(m, c)
    cb0 = col0 // tc
    n_i = m // tm
    hb = tm // 8

    def body(x_ref, xh_ref, dy_ref, dyn_ref, w_ref, dx_ref, dw_ref, db_ref):
        i = pl.program_id(1)

        @pl.when(i == 0)
        def _():
            dw_ref[...] = jnp.zeros_like(dw_ref)
            db_ref[...] = jnp.zeros_like(db_ref)

        w = [w_ref[k:k + 1, :] for k in range(CONV_W)]
        d_cur = dy_ref[...]
        after = [jnp.where(i == n_i - 1, 0.0, dyn_ref[d:d + 1, :]) for d in range(3)]
        acc = w[0] * d_cur
        last = w[0] * after[0]
        for k in (1, 2):
            acc = w[k] * d_cur + _shift_up(acc, last)
            last = sum(w[k - d] * after[d] for d in range(k + 1))
        dx_ref[...] = (w[3] * d_cur + _shift_up(acc, last)).astype(BF16)
        xs = x_ref[...]
        dw_ref[3:4, :] += jnp.sum(xs * d_cur, axis=0, keepdims=True)
        for d in (1, 2, 3):
            xs = _shift_down(xs, xh_ref[8 - d:9 - d, :])
            dw_ref[3 - d:4 - d, :] += jnp.sum(xs * d_cur, axis=0, keepdims=True)
        db_ref[...] += jnp.sum(d_cur, axis=0, keepdims=True)

    return _call(body, name=name, out_shape=(S((m, c), BF16), S((CONV_W, c), F32), S((1, c), F32)),
                 grid=(c // tc, n_i),
                 in_specs=[pl.BlockSpec((tm, tc), lambda j, i: (i, cb0 + j)),
                           pl.BlockSpec((8, tc), lambda j, i: (jnp.maximum(i * hb - 1, 0), cb0 + j)),
                           pl.BlockSpec((tm, tc), lambda j, i: (i, j)),
                           pl.BlockSpec((8, tc), lambda j, i: (jnp.minimum((i + 1) * hb, n_i * hb - 1), j)),
                           pl.BlockSpec((CONV_W, tc), lambda j, i: (0, j))],
                 out_specs=(pl.BlockSpec((tm, tc), lambda j, i: (i, j)), pl.BlockSpec((CONV_W, tc), lambda j, i: (0, j)),
                            pl.BlockSpec((1, tc), lambda j, i: (0, j))),
                 sem=("parallel", "arbitrary"))(x, x, dy, dy, w)


def _row_valid(tile_idx, tiles_per_seq, tm):
    pos = lax.rem(tile_idx, tiles_per_seq) * tm + lax.broadcasted_iota(jnp.int32, (tm, 1), 0)
    return (pos >= PAD).astype(F32)


def _neg_expm1(x):
    small = -(x * (1.0 + x * (0.5 + x * (1.0 / 6.0))))
    return jnp.where(x > -0.01, small, 1.0 - jnp.exp(x))


def _to_scan_layout(ref, val, col0, tm):
    for k in range(val.shape[1] // 128):
        ref[pl.ds(col0 // 128 + k, tm, stride=8), :] = val[:, k * 128:(k + 1) * 128]


def _from_scan_layout(ref, col0, width, tm):
    parts = [ref[pl.ds(col0 // 128 + k, tm, stride=8), :] for k in range(width // 128)]
    return parts[0] if len(parts) == 1 else jnp.concatenate(parts, axis=1)


def _gates_core(lx, wa, ba, wx, bx, lam):
    lxb = lx.astype(BF16)
    r = _sigmoid(_dot(lxb, wa) + ba)
    i = _sigmoid(_dot(lxb, wx) + bx)
    sp = _softplus(-lam)
    log_a = (-RG_LRU_C) * r * sp
    a = jnp.exp(log_a)
    mult = jnp.sqrt(_neg_expm1(2.0 * log_a))
    return lxb, r, i, sp, a, mult


def gates_fwd(lx, rowmask, wa, ba, wx, bx, lam, *, name):
    m = lx.shape[0]
    tm = _tile(m, (256, 128))
    cb = LRU_BLOCK

    def body(lx_ref, msk_ref, wa_ref, ba_ref, wx_ref, bx_ref, lam_ref, a_ref, b_ref):
        msk = msk_ref[...]
        for g in range(LRU_BLOCKS):
            ch = slice(g * cb, (g + 1) * cb)
            lxv = lx_ref[:, ch]
            _, _, i, _, a, mult = _gates_core(lxv, wa_ref[g], ba_ref[:, ch], wx_ref[g], bx_ref[:, ch], lam_ref[:, ch])
            _to_scan_layout(a_ref, a, g * cb, tm)
            _to_scan_layout(b_ref, msk * (mult * i * lxv), g * cb, tm)

    tok = pl.BlockSpec((tm, D_MODEL), lambda i: (i, 0))
    msk = pl.BlockSpec((tm, 1), lambda i: (i, 0))
    wsp = pl.BlockSpec((LRU_BLOCKS, cb, cb), lambda i: (0, 0, 0))
    vec = pl.BlockSpec((1, D_MODEL), lambda i: (0, 0))
    scn = pl.BlockSpec((tm * 8, 128), lambda i: (i, 0))
    return _call(body, name=name, out_shape=(S((m * 8, 128), F32),) * 2, grid=(m // tm,),
                 in_specs=[tok, msk, wsp, vec, wsp, vec, vec], out_specs=(scn, scn),
                 sem=("parallel",))(lx, rowmask, wa, ba, wx, bx, lam)


def gates_bwd(lx, rowmask, da, db, wa, ba, wx, bx, lam, *, name):
    m = lx.shape[0]
    tm = _tile(m, (256, 128))
    cb = LRU_BLOCK

    def body(lx_ref, msk_ref, da_ref, db_ref, wa_ref, ba_ref, wx_ref, bx_ref, lam_ref,
             dlx_ref, dwa_ref, dwx_ref, dba_ref, dbx_ref, dlam_ref):
        @pl.when(pl.program_id(0) == 0)
        def _():
            for ref in (dwa_ref, dwx_ref, dba_ref, dbx_ref, dlam_ref):
                ref[...] = jnp.zeros_like(ref)
        msk = msk_ref[...]
        for g in range(LRU_BLOCKS):
            ch = slice(g * cb, (g + 1) * cb)
            lxv = lx_ref[:, ch]
            lamv = lam_ref[:, ch]
            lxb, r, i, sp, a, mult = _gates_core(lxv, wa_ref[g], ba_ref[:, ch], wx_ref[g], bx_ref[:, ch], lamv)
            dbv = msk * _from_scan_layout(db_ref, g * cb, cb, tm)
            d_mult = dbv * (i * lxv)
            d_i = dbv * (mult * lxv)
            d_log_a = _from_scan_layout(da_ref, g * cb, cb, tm) * a - d_mult * (a * a) / mult
            d_pa = (d_log_a * ((-RG_LRU_C) * sp)) * (r * (1.0 - r))
            d_px = d_i * (i * (1.0 - i))
            d_pa16 = d_pa.astype(BF16)
            d_px16 = d_px.astype(BF16)
            dlx_ref[:, ch] = dbv * (mult * i) + _dot_nt(d_pa16, wa_ref[g]) + _dot_nt(d_px16, wx_ref[g])
            dwa_ref[g] += _dot_tn(lxb, d_pa16)
            dwx_ref[g] += _dot_tn(lxb, d_px16)
            dba_ref[:, ch] += jnp.sum(d_pa, axis=0, keepdims=True)
            dbx_ref[:, ch] += jnp.sum(d_px, axis=0, keepdims=True)
            d_sp = jnp.sum(d_log_a * ((-RG_LRU_C) * r), axis=0, keepdims=True)
            dlam_ref[:, ch] += -d_sp * _sigmoid(-lamv)

    tok = pl.BlockSpec((tm, D_MODEL), lambda i: (i, 0))
    msk = pl.BlockSpec((tm, 1), lambda i: (i, 0))
    scn = pl.BlockSpec((tm * 8, 128), lambda i: (i, 0))
    wsp = pl.BlockSpec((LRU_BLOCKS, cb, cb), lambda i: (0, 0, 0))
    vec = pl.BlockSpec((1, D_MODEL), lambda i: (0, 0))
    wshape = S((LRU_BLOCKS, cb, cb), F32)
    vshape = S((1, D_MODEL), F32)
    return _call(body, name=name, out_shape=(S((m, D_MODEL), F32), wshape, wshape, vshape, vshape, vshape),
                 grid=(m // tm,), in_specs=[tok, msk, scn, scn, wsp, vec, wsp, vec, vec],
                 out_specs=(tok, wsp, wsp, vec, vec, vec),
                 sem=("arbitrary",))(lx, rowmask, da, db, wa, ba, wx, bx, lam)


SCAN_TOK = 128


def scan_fwd(a, b, lp, *, name):
    m = a.shape[0] // 8
    bsz = m // lp
    nch = lp // SCAN_TOK
    rows = SCAN_TOK * 8

    def body(a_ref, b_ref, h_ref, carry):
        @pl.when(pl.program_id(0) == 0)
        def _():
            carry[...] = jnp.zeros_like(carry)

        def step(t, hs):
            r = pl.ds(pl.multiple_of(t * 8, 8), 8)
            out = []
            for s, h in enumerate(hs):
                h = a_ref[s, r, :] * h + b_ref[s, r, :]
                h_ref[s, r, :] = h
                out.append(h)
            return tuple(out)

        hs = lax.fori_loop(0, SCAN_TOK, step, tuple(carry[s] for s in range(bsz)), unroll=4)
        for s in range(bsz):
            carry[s] = hs[s]

    blk = pl.BlockSpec((bsz, rows, 128), lambda c: (0, c, 0))
    shape3 = (bsz, lp * 8, 128)
    out = _call(body, name=name, out_shape=S(shape3, F32), grid=(nch,), in_specs=[blk, blk], out_specs=blk,
                scratch=[pltpu.VMEM((bsz, 8, 128), F32)], sem=("arbitrary",))(a.reshape(shape3), b.reshape(shape3))
    return out.reshape(m * 8, 128)


def scan_bwd(a, h, dh, lp, *, name):
    m = a.shape[0] // 8
    bsz = m // lp
    nch = lp // SCAN_TOK
    rows = SCAN_TOK * 8

    def body(a_ref, h_ref, hprev_ref, dh_ref, da_ref, db_ref, carry):
        c = pl.program_id(0)

        @pl.when(c == 0)
        def _():
            carry[...] = jnp.zeros_like(carry)

        h_before = [jnp.where(c == nch - 1, 0.0, hprev_ref[s]) for s in range(bsz)]

        def step(k, ags):
            t = SCAN_TOK - 1 - k
            r = pl.ds(pl.multiple_of(t * 8, 8), 8)
            rp = pl.ds(pl.multiple_of(jnp.maximum(t - 1, 0) * 8, 8), 8)
            out = []
            for s, ag in enumerate(ags):
                g = dh_ref[s, r, :] + ag
                db_ref[s, r, :] = g
                da_ref[s, r, :] = g * jnp.where(t == 0, h_before[s], h_ref[s, rp, :])
                out.append(a_ref[s, r, :] * g)
            return tuple(out)

        ags = lax.fori_loop(0, SCAN_TOK, step, tuple(carry[s] for s in range(bsz)), unroll=4)
        for s in range(bsz):
            carry[s] = ags[s]

    blk = pl.BlockSpec((bsz, rows, 128), lambda c: (0, nch - 1 - c, 0))
    prev = pl.BlockSpec((bsz, 8, 128), lambda c: (0, jnp.maximum((nch - 1 - c) * SCAN_TOK - 1, 0), 0))
    shape3 = (bsz, lp * 8, 128)
    a3, h3, dh3 = (v.reshape(shape3) for v in (a, h, dh))
    da, db = _call(body, name=name, out_shape=(S(shape3, F32),) * 2, grid=(nch,),
                   in_specs=[blk, blk, prev, blk], out_specs=(blk, blk), scratch=[pltpu.VMEM((bsz, 8, 128), F32)],
                   sem=("arbitrary",))(a3, h3, h3, dh3)
    return da.reshape(m * 8, 128), db.reshape(m * 8, 128)


SSD_SEQS = 2
SSD_SEQS_BWD = 1
SSD_SROWS = SSD_HEADS * SSD_HEAD_DIM


def _ssd_chunk(seqs, dtb, alog, dpar, valid):
    row = lax.broadcasted_iota(jnp.int32, (CHUNK, CHUNK), 0)
    col = lax.broadcasted_iota(jnp.int32, (CHUNK, CHUNK), 1)
    tri = row >= col
    neg_a = -jnp.exp(alog)
    dts, acums, acum_ts, b16s, c16s = [], [], [], [], []
    for xs, bs, cs, dtr, ss in seqs:
        dt = _softplus(dtr + dtb) * valid
        acum = jnp.dot(tri.astype(F32), dt * neg_a, precision=lax.Precision.HIGHEST, preferred_element_type=F32)
        dts.append(dt)
        acums.append(acum)
        acum_ts.append(acum.T)
        b16s.append([(_silu(b) * valid).astype(BF16) for b in bs])
        c16s.append([(_silu(c) * valid).astype(BF16) for c in cs])
    cbs = [[_dot_nt(c, b) for c, b in zip(c16, b16)] for c16, b16 in zip(c16s, b16s)]
    idx = [(q, h) for q in range(len(seqs)) for h in range(SSD_HEADS)]
    grp = [h // SSD_HPG for _, h in idx]
    x = [_silu(seqs[q][0][h]) for q, h in idx]
    s_in = [seqs[q][4][h] for q, h in idx]
    ac = [acums[q][:, h:h + 1] for q, h in idx]
    alast = [acums[q][CHUNK - 1:CHUNK, h:h + 1] for q, h in idx]
    xd = [x[n] * dts[q][:, h:h + 1] for n, (q, h) in enumerate(idx)]
    lhs = [(cbs[q][grp[n]] * jnp.exp(jnp.where(tri, ac[n] - acum_ts[q][h:h + 1, :], -1e30))).astype(BF16)
           for n, (q, h) in enumerate(idx)]
    xd16 = [v.astype(BF16) for v in xd]
    xdec16 = [(xd[n] * jnp.exp(alast[n] - ac[n])).astype(BF16) for n in range(len(idx))]
    s16 = [v.astype(BF16) for v in s_in]
    y_diag = [_dot(lhs[n], xd16[n]) for n in range(len(idx))]
    y_off = [_dot_nt(c16s[q][grp[n]], s16[n]) for n, (q, _) in enumerate(idx)]
    st = [_dot_tn(xdec16[n], b16s[q][grp[n]]) for n, (q, _) in enumerate(idx)]
    ys = [y_diag[n] + y_off[n] * jnp.exp(ac[n]) + x[n] * dpar[:, h:h + 1] for n, (_, h) in enumerate(idx)]
    s_new = [jnp.exp(alast[n]) * s_in[n] + st[n] for n in range(len(idx))]
    return [(ys[q * SSD_HEADS:(q + 1) * SSD_HEADS], s_new[q * SSD_HEADS:(q + 1) * SSD_HEADS])
            for q in range(len(seqs))]


def _ssd_load(pre_ref, dt_ref, s_ref, q):
    xs = [pre_ref[:, h * SSD_HEAD_DIM:(h + 1) * SSD_HEAD_DIM] for h in range(SSD_HEADS)]
    b0 = SSD_HEADS * SSD_HEAD_DIM
    bs = [pre_ref[:, b0 + g * SSD_STATE:b0 + (g + 1) * SSD_STATE] for g in range(SSD_GROUPS)]
    c0 = b0 + SSD_GROUPS * SSD_STATE
    cs = [pre_ref[:, c0 + g * SSD_STATE:c0 + (g + 1) * SSD_STATE] for g in range(SSD_GROUPS)]
    r0 = q * SSD_SROWS
    ss = [s_ref[r0 + h * SSD_HEAD_DIM:r0 + (h + 1) * SSD_HEAD_DIM, :] for h in range(SSD_HEADS)]
    return xs, bs, cs, dt_ref[:, 0:SSD_HEADS], ss


def ssd_fwd(pre, proj, dtb, alog, dpar, lp, *, name):
    m = pre.shape[0]
    bsz = m // lp
    nc = lp // CHUNK
    nq = SSD_SEQS
    assert bsz % nq == 0

    def body(pre_ref, dt_ref, dtb_ref, alog_ref, d_ref, y_ref, sin_ref, state):
        c = pl.program_id(1)

        @pl.when(c == 0)
        def _():
            state[...] = jnp.zeros_like(state)

        for q in range(nq):
            sin_ref[q] = state[q * SSD_SROWS:(q + 1) * SSD_SROWS, :]
        seqs = [_ssd_load(pre_ref.at[q], dt_ref.at[q], state, q) for q in range(nq)]
        valid = _row_valid(c, nc, CHUNK)
        res = _ssd_chunk(seqs, dtb_ref[...], alog_ref[...], d_ref[...], valid)
        for q, (ys, s_new) in enumerate(res):
            for h in range(SSD_HEADS):
                y_ref[q, :, h * SSD_HEAD_DIM:(h + 1) * SSD_HEAD_DIM] = ys[h]
                r0 = q * SSD_SROWS + h * SSD_HEAD_DIM
                state[r0:r0 + SSD_HEAD_DIM, :] = s_new[h]

    par = pl.BlockSpec((1, SSD_HEADS), lambda s, c: (0, 0))
    y, s_in = _call(
        body, name=name, out_shape=(S((bsz, lp, D_MODEL), F32), S((bsz, nc * SSD_SROWS, SSD_STATE), F32)),
        grid=(bsz // nq, nc),
        in_specs=[pl.BlockSpec((nq, CHUNK, SSD_CONV_DIM), lambda s, c: (s, c, 0)),
                  pl.BlockSpec((nq, CHUNK, 128), lambda s, c: (s, c, COL_DT // 128)), par, par, par],
        out_specs=(pl.BlockSpec((nq, CHUNK, D_MODEL), lambda s, c: (s, c, 0)),
                   pl.BlockSpec((nq, SSD_SROWS, SSD_STATE), lambda s, c: (s, c, 0))),
        scratch=[pltpu.VMEM((nq * SSD_SROWS, SSD_STATE), F32)],
        sem=("parallel", "arbitrary"))(pre.reshape(bsz, lp, -1), proj.reshape(bsz, lp, -1), dtb, alog, dpar)
    return y.reshape(m, D_MODEL), s_in


def ssd_bwd(pre, proj, s_in, dy, dtb, alog, dpar, lp, *, name):
    m = pre.shape[0]
    bsz = m // lp
    nc = lp // CHUNK
    nq = SSD_SEQS_BWD

    def body(pre_ref, dt_ref, sin_ref, dy_ref, dtb_ref, alog_ref, d_ref,
             dpre_ref, ddt_ref, ddtb_ref, dalog_ref, dd_ref, dstate):
        c = pl.program_id(1)

        @pl.when(jnp.logical_and(pl.program_id(0) == 0, c == 0))
        def _():
            for r in (ddtb_ref, dalog_ref, dd_ref):
                r[...] = jnp.zeros_like(r)

        @pl.when(c == 0)
        def _():
            dstate[...] = jnp.zeros_like(dstate)

        seqs = [_ssd_load(pre_ref.at[q], dt_ref.at[q], sin_ref.at[q], 0) for q in range(nq)]
        valid = _row_valid(nc - 1 - c, nc, CHUNK)
        core = functools.partial(_ssd_chunk, valid=valid)
        _, vjp = jax.vjp(core, seqs, dtb_ref[...], alog_ref[...], d_ref[...])
        cot = []
        for q in range(nq):
            dys = [dy_ref[q, :, h * SSD_HEAD_DIM:(h + 1) * SSD_HEAD_DIM] for h in range(SSD_HEADS)]
            r0 = q * SSD_SROWS
            dsn = [dstate[r0 + h * SSD_HEAD_DIM:r0 + (h + 1) * SSD_HEAD_DIM, :] for h in range(SSD_HEADS)]
            cot.append((dys, dsn))
        dseqs, ddtb, dalog, dd = vjp(cot)
        b0 = SSD_HEADS * SSD_HEAD_DIM
        c0 = b0 + SSD_GROUPS * SSD_STATE
        ddt_ref[...] = jnp.zeros_like(ddt_ref)
        for q, (dxs, dbs, dcs, ddtr, dss) in enumerate(dseqs):
            for h in range(SSD_HEADS):
                dpre_ref[q, :, h * SSD_HEAD_DIM:(h + 1) * SSD_HEAD_DIM] = dxs[h]
                r0 = q * SSD_SROWS + h * SSD_HEAD_DIM
                dstate[r0:r0 + SSD_HEAD_DIM, :] = dss[h]
            for g in range(SSD_GROUPS):
                dpre_ref[q, :, b0 + g * SSD_STATE:b0 + (g + 1) * SSD_STATE] = dbs[g]
                dpre_ref[q, :, c0 + g * SSD_STATE:c0 + (g + 1) * SSD_STATE] = dcs[g]
            ddt_ref[q, :, 0:SSD_HEADS] = ddtr.astype(BF16)
        ddtb_ref[...] += ddtb
        dalog_ref[...] += dalog
        dd_ref[...] += dd

    par = pl.BlockSpec((1, SSD_HEADS), lambda s, c: (0, 0))
    rev = lambda s, c: (s, nc - 1 - c, 0)
    pshape = S((1, SSD_HEADS), F32)
    dpre, ddt, ddtb, dalog, dd = _call(
        body, name=name,
        out_shape=(S((bsz, lp, SSD_CONV_DIM), F32), S((bsz, lp, 128), BF16), pshape, pshape, pshape),
        grid=(bsz // nq, nc),
        in_specs=[pl.BlockSpec((nq, CHUNK, SSD_CONV_DIM), rev),
                  pl.BlockSpec((nq, CHUNK, 128), lambda s, c: (s, nc - 1 - c, COL_DT // 128)),
                  pl.BlockSpec((nq, SSD_SROWS, SSD_STATE), rev), pl.BlockSpec((nq, CHUNK, D_MODEL), rev), par, par, par],
        out_specs=(pl.BlockSpec((nq, CHUNK, SSD_CONV_DIM), rev), pl.BlockSpec((nq, CHUNK, 128), rev), par, par, par),
        scratch=[pltpu.VMEM((nq * SSD_SROWS, SSD_STATE), F32)],
        sem=("arbitrary", "arbitrary"))(pre.reshape(bsz, lp, -1), proj.reshape(bsz, lp, -1), s_in,
                                        dy.reshape(bsz, lp, -1), dtb, alog, dpar)
    return dpre.reshape(m, SSD_CONV_DIM), ddt.reshape(m, 128), ddtb, dalog, dd


SB_KEYS = 256


def _order_mats():
    r = lax.broadcasted_iota(jnp.int32, (SB_KEYS, SB_KEYS), 0)
    c = lax.broadcasted_iota(jnp.int32, (SB_KEYS, SB_KEYS), 1)
    return (r > c).astype(BF16), (r < c).astype(BF16)


def _split_dot(x, mat):
    hi = x.astype(BF16)
    lo = (x - hi.astype(F32)).astype(BF16)
    return _dot(hi, mat) + _dot(lo, mat)


def _sb_tiles(qs_, ks_, blocks_, jts, diff, col, m_later, masked):
    zs = [_dot_nt(q_i, k_t) for q_i, k_t in zip(qs_, ks_)]
    out = []
    for z, i, jt in zip(zs, blocks_, jts):
        if masked:
            valid = jnp.logical_and(diff > jt * SB_KEYS - i * SB_BLOCK, col >= PAD - jt * SB_KEYS)
            lk = jnp.where(valid, -_softplus(z), 0.0)
        else:
            valid, lk = None, -_softplus(z)
        out.append((valid, z, lk))
    sums = [_split_dot(lk, m_later) for _, _, lk in out]
    return [(valid, z, lk, tsum, tsum[:, 0:1] + lk[:, 0:1]) for (valid, z, lk), tsum in zip(out, sums)]


def _sb_iotas():
    row = lax.broadcasted_iota(jnp.int32, (SB_BLOCK, SB_KEYS), 0)
    col = lax.broadcasted_iota(jnp.int32, (SB_BLOCK, SB_KEYS), 1)
    return row - col, col


def _sb_rows(i, size):
    start = i * size
    return pl.ds(start if isinstance(start, int) else pl.multiple_of(start, size), size)


def _sb_fill(dst, src_ref, ln, lp, scale=None):
    v = src_ref[:, ln]
    dst[0:lp, :] = (v if scale is None else v * scale).astype(BF16)
    if dst.shape[0] > lp:
        dst[lp:, :] = jnp.zeros((dst.shape[0] - lp, dst.shape[1]), BF16)


def _sb_schedule(nb, run_blocks):
    def pair(a, _):
        run_blocks([2 * a, 2 * a + 1], a + 1, 2)
        return 0
    for a in range(min(2, nb // 2)):
        run_blocks([2 * a, 2 * a + 1], a + 1, 2)
    lax.fori_loop(2, nb // 2, pair, 0)
    if nb % 2:
        run_blocks([nb - 1], (nb + 1) // 2, 2)


def _sb_steps(ntiles, stride):
    if isinstance(ntiles, int):
        return (ntiles + stride - 1) // stride
    return lax.div(ntiles + stride - 1, stride)


def _sb_sweep(nsteps, step, carry):
    carry = step(0, carry, True)
    if isinstance(nsteps, int) and nsteps == 1:
        return carry
    carry = lax.fori_loop(1, nsteps - 1, lambda s, c: step(s, c, False), carry)
    return step(nsteps - 1, carry, True)


def attn_fwd(qkvg, lp, *, name):
    m = qkvg.shape[0]
    bsz = m // lp
    nb = lp // SB_BLOCK
    nkt = (nb + 1) // 2
    hd = SB_HEAD_DIM

    def body(q_ref, k_ref, v_ref, o_ref, qs, ks, vs):
        m_later, _ = _order_mats()
        diff, col = _sb_iotas()
        for hh in range(2):
            ln = slice(hh * hd, (hh + 1) * hd)
            _sb_fill(qs.at[hh], q_ref, ln, lp, hd ** -0.5)
            _sb_fill(ks.at[hh], k_ref, ln, lp)
            _sb_fill(vs.at[hh], v_ref, ln, lp)

        def run_blocks(blocks, ntiles, stride):
            rows = [_sb_rows(i, SB_BLOCK) for i in blocks]
            groups = [(b, hh) for b in range(len(blocks)) for hh in range(2)]
            chains = [(g, sub) for g in range(len(groups)) for sub in range(stride)]
            q = [qs[groups[g][1], rows[groups[g][0]], :] for g, _ in chains]
            blk_of = [blocks[groups[g][0]] for g, _ in chains]
            heads = [groups[g][1] for g, _ in chains]

            def tile_step(s, carry, masked):
                jts = [ntiles - 1 - stride * s - sub for _, sub in chains]
                cols = [_sb_rows(jt if stride == 1 else jnp.maximum(jt, 0), SB_KEYS) for jt in jts]
                tiles = _sb_tiles(q, [ks[hh, c, :] for hh, c in zip(heads, cols)], blk_of, jts, diff, col, m_later, masked)
                ws = []
                for valid, z, lk, tsum, _ in tiles:
                    w = jnp.exp(z + lk + tsum)
                    ws.append((jnp.where(valid, w, 0.0) if masked else w).astype(BF16))
                pvs = [_dot(w, vs[hh, c, :]) for w, hh, c in zip(ws, heads, cols)]
                out = []
                for g, (acc, run) in enumerate(carry):
                    for n, (gn, _) in enumerate(chains):
                        if gn == g:
                            acc = acc + jnp.exp(run) * pvs[n]
                            run = run + tiles[n][4]
                    out.append((acc, run))
                return tuple(out)

            zero = (jnp.zeros((SB_BLOCK, hd), F32), jnp.zeros((SB_BLOCK, 1), F32))
            nsteps = _sb_steps(ntiles, stride)
            res = _sb_sweep(nsteps, tile_step, (zero,) * len(groups))
            for g, (b, hh) in enumerate(groups):
                o_ref[rows[b], hh * hd:(hh + 1) * hd] = res[g][0]

        _sb_schedule(nb, run_blocks)

    blk = lambda cb: pl.BlockSpec((lp, 128), lambda s, p: (s, cb * 8 + p))
    return _call(body, name=name, out_shape=S((m, D_MODEL), F32), grid=(bsz, 8),
                 in_specs=[blk(0), blk(1), blk(2)], out_specs=pl.BlockSpec((lp, 128), lambda s, p: (s, p)),
                 scratch=[pltpu.VMEM((2, lp, hd), BF16)] + [pltpu.VMEM((2, nkt * SB_KEYS, hd), BF16)] * 2,
                 sem=("parallel", "parallel"))(qkvg, qkvg, qkvg)


def attn_bwd(qkvg, do, lp, *, name):
    m = qkvg.shape[0]
    bsz = m // lp
    nb = lp // SB_BLOCK
    hd = SB_HEAD_DIM
    scale = hd ** -0.5

    nkt = (nb + 1) // 2

    def body(q_ref, k_ref, v_ref, do_ref, dq_ref, dk_ref, dv_ref, qs, ks, vs, dka, dva, g_keep, s_keep):
        m_later, m_earlier = _order_mats()
        diff, col = _sb_iotas()
        for hh in range(2):
            ln = slice(hh * hd, (hh + 1) * hd)
            _sb_fill(qs.at[hh], q_ref, ln, lp, scale)
            _sb_fill(ks.at[hh], k_ref, ln, lp)
            _sb_fill(vs.at[hh], v_ref, ln, lp)
        dka[...] = jnp.zeros_like(dka)
        dva[...] = jnp.zeros_like(dva)

        def run_blocks(blocks, ntiles, stride):
            rows = [_sb_rows(i, SB_BLOCK) for i in blocks]
            groups = [(b, hh) for b in range(len(blocks)) for hh in range(2)]
            chains = [(g, sub) for g in range(len(groups)) for sub in range(stride)]
            q = [qs[groups[g][1], rows[groups[g][0]], :] for g, _ in chains]
            do = [do_ref[rows[b], hh * hd:(hh + 1) * hd] for b, hh in groups]
            blk_of = [blocks[groups[g][0]] for g, _ in chains]
            heads = [groups[g][1] for g, _ in chains]
            nsteps = _sb_steps(ntiles, stride)

            def place(jt):
                if stride == 1:
                    return jt, _sb_rows(jt, SB_KEYS)
                inside = jnp.logical_and(jt >= 0, jt < ntiles)
                return jnp.where(inside, jt, nkt), _sb_rows(jnp.clip(jt, 0, ntiles - 1), SB_KEYS)

            def sweep_left(s, carry, masked):
                jts = [ntiles - 1 - stride * s - sub for _, sub in chains]
                slots, cols = zip(*[place(jt) for jt in jts])
                tiles = _sb_tiles(q, [ks[hh, c, :] for hh, c in zip(heads, cols)], blk_of, jts, diff, col, m_later, masked)
                runs, out = [None] * len(chains), []
                for g, run in enumerate(carry):
                    for n, (gn, _) in enumerate(chains):
                        if gn == g:
                            runs[n] = run
                            run = run + tiles[n][4]
                    out.append(run)
                do_run = [(do[g] * jnp.exp(run)).astype(BF16) for (g, _), run in zip(chains, runs)]
                dws = [_dot_nt(d, vs[hh, c, :]) for d, hh, c in zip(do_run, heads, cols)]
                ws = []
                for (g, _), slot, (valid, z, lk, tsum, _), dw in zip(chains, slots, tiles, dws):
                    sig = jnp.exp(z + lk)
                    if masked:
                        sig = jnp.where(valid, sig, 0.0)
                    w = sig * jnp.exp(tsum)
                    g_keep[g, slot] = dw * w
                    s_keep[g, slot] = sig
                    ws.append(w.astype(BF16))
                dvs = [_dot_tn(w, d) for w, d in zip(ws, do_run)]
                for hh, c, dv in zip(heads, cols, dvs):
                    dva[hh, c, :] += dv
                return tuple(out)

            _sb_sweep(nsteps, sweep_left, (jnp.zeros((SB_BLOCK, 1), F32),) * len(groups))

            def sweep_right(s, carry):
                jts = [stride * s + sub for _, sub in chains]
                slots, cols = zip(*[place(jt) for jt in jts])
                gmats = [g_keep[g, slot] for (g, _), slot in zip(chains, slots)]
                gsums = [_split_dot(gmat, m_earlier) for gmat in gmats]
                last = slice(SB_KEYS - 1, SB_KEYS)
                gruns, out_grun = [None] * len(chains), []
                for g, (_, grun) in enumerate(carry):
                    for n, (gn, _) in enumerate(chains):
                        if gn == g:
                            gruns[n] = grun
                            grun = grun + gsums[n][:, last] + gmats[n][:, last]
                    out_grun.append(grun)
                dzs = [(gmat - s_keep[g, slot] * (gmat + gsum + grun)).astype(BF16)
                       for (g, _), slot, gmat, gsum, grun in zip(chains, slots, gmats, gsums, gruns)]
                dqs = [_dot(dz, ks[hh, c, :]) for dz, hh, c in zip(dzs, heads, cols)]
                dks = [_dot_tn(dz, q_n) for dz, q_n in zip(dzs, q)]
                for hh, c, dk in zip(heads, cols, dks):
                    dka[hh, c, :] += dk
                out = []
                for g, (dq, _) in enumerate(carry):
                    for n, (gn, _) in enumerate(chains):
                        if gn == g:
                            dq = dq + dqs[n]
                    out.append((dq, out_grun[g]))
                return tuple(out)

            zero = (jnp.zeros((SB_BLOCK, hd), F32), jnp.zeros((SB_BLOCK, 1), F32))
            res = lax.fori_loop(0, nsteps, sweep_right, (zero,) * len(groups))
            for g, (b, hh) in enumerate(groups):
                dq_ref[rows[b], hh * hd:(hh + 1) * hd] = (res[g][0] * scale).astype(BF16)

        _sb_schedule(nb, run_blocks)
        for hh in range(2):
            dk_ref[:, hh * hd:(hh + 1) * hd] = dka[hh, 0:lp, :].astype(BF16)
            dv_ref[:, hh * hd:(hh + 1) * hd] = dva[hh, 0:lp, :].astype(BF16)

    blk = lambda cb: pl.BlockSpec((lp, 128), lambda s, p: (s, cb * 8 + p))
    one = pl.BlockSpec((lp, 128), lambda s, p: (s, p))
    keys = nkt * SB_KEYS
    return _call(body, name=name, out_shape=(S((m, D_MODEL), BF16),) * 3, grid=(bsz, 8),
                 in_specs=[blk(0), blk(1), blk(2), one], out_specs=(one, one, one),
                 scratch=[pltpu.VMEM((2, lp, hd), BF16)] + [pltpu.VMEM((2, keys, hd), BF16)] * 2
                 + [pltpu.VMEM((2, keys, hd), F32)] * 2 + [pltpu.VMEM((4, nkt + 1, SB_BLOCK, SB_KEYS), F32)] * 2,
                 sem=("parallel", "parallel"))(qkvg, qkvg, qkvg, do)


def meta_grad(dh, lp, *, name):
    m, d = dh.shape
    bsz = m // lp
    per = lp // N_META

    def body(dh_ref, o_ref):
        @pl.when(pl.program_id(0) == 0)
        def _():
            o_ref[...] = jnp.zeros_like(o_ref)
        o_ref[...] += dh_ref[...]

    return _call(body, name=name, out_shape=S((N_META, d), F32), grid=(bsz,),
                 in_specs=[pl.BlockSpec((N_META, d), lambda b: (b * per + PAD // N_META, 0))],
                 out_specs=pl.BlockSpec((N_META, d), lambda b: (0, 0)), sem=("arbitrary",))(dh)


def sum_lead(arr, *, name):
    n, r, c = arr.shape
    tr = _tile(r, (128, 64, 32, 16, 8))

    def body(a_ref, o_ref):
        acc = a_ref[0].astype(F32)
        for k in range(1, n):
            acc = acc + a_ref[k].astype(F32)
        o_ref[...] = acc

    return _call(body, name=name, out_shape=S((r, c), F32), grid=(r // tr,),
                 in_specs=[pl.BlockSpec((n, tr, c), lambda i: (0, i, 0))],
                 out_specs=pl.BlockSpec((tr, c), lambda i: (i, 0)), sem=("parallel",))(arr)


def adamw(w, g_parts, mom, var, *, name):
    r, c = w.shape
    tr = _tile(r, (128, 64, 32, 16, 8))
    n_g = len(g_parts)
    c1 = 1.0 - ADAM_B1 ** ADAM_STEP
    c2 = 1.0 - ADAM_B2 ** ADAM_STEP

    def body(*refs):
        w_ref, g_refs, m_ref, v_ref = refs[0], refs[1:1 + n_g], refs[1 + n_g], refs[2 + n_g]
        g_out, d_out, m_out, v_out = refs[3 + n_g:]
        g = g_refs[0][...]
        for gr in g_refs[1:]:
            g = g + gr[...]
        mn = ADAM_B1 * m_ref[...] + (1.0 - ADAM_B1) * g
        vn = ADAM_B2 * v_ref[...] + (1.0 - ADAM_B2) * (g * g)
        g_out[...] = g
        m_out[...] = mn
        v_out[...] = vn
        d_out[...] = -ADAM_LR * ((mn / c1) / (jnp.sqrt(vn / c2) + ADAM_EPS) + ADAM_WD * w_ref[...])

    blk = pl.BlockSpec((tr, c), lambda i: (i, 0))
    return _call(body, name=name, out_shape=(S((r, c), F32),) * 4, grid=(r // tr,), in_specs=[blk] * (3 + n_g),
                 out_specs=(blk,) * 4, sem=("parallel",))(w, *g_parts, mom, var)


_ANY = pl.BlockSpec(memory_space=pl.ANY)


def _position():
    return lax.axis_index("x"), lax.axis_index("y"), lax.axis_index("c")


def _other_chips(x, y):
    return [(1 - x, y), (x, 1 - y), (1 - x, 1 - y)]


def _comm_call(body, arrs, out_shapes, n_sem, *, name):
    return pl.pallas_call(
        body, out_shape=tuple(out_shapes), in_specs=[_ANY] * len(arrs), out_specs=tuple([_ANY] * len(out_shapes)),
        scratch_shapes=(pltpu.SemaphoreType.DMA((n_sem,)), pltpu.SemaphoreType.DMA((n_sem,)),
                        pltpu.SemaphoreType.DMA((len(arrs),))),
        name=name)(*arrs)


def allgather_chips(arrs, *, name):
    n = len(arrs)

    def body(*refs):
        ins, outs = refs[:n], refs[n:2 * n]
        send_sems, recv_sems, loc_sems = refs[2 * n:]
        x, y, c = _position()
        me = 2 * x + y
        chips = _other_chips(x, y)

        def half(ref, a, which):
            rows = arrs[a].shape[0] // 2
            return ref.at[pl.ds(which * rows, rows)]

        sent, passed = [], []
        for a in range(n):
            for k, (px, py) in enumerate(chips):
                cp = pltpu.make_async_remote_copy(
                    src_ref=half(ins[a], a, c), dst_ref=half(outs[a].at[me], a, c), send_sem=send_sems.at[6 * a + k],
                    recv_sem=recv_sems.at[6 * a + k], device_id=(px, py, c), device_id_type=MESH)
                cp.start()
                sent.append(cp)
        for a in range(n):
            for k, (px, py) in enumerate(chips):
                sent[3 * a + k].wait_recv()
                landed = half(outs[a].at[2 * px + py], a, c)
                cp = pltpu.make_async_remote_copy(
                    src_ref=landed, dst_ref=landed, send_sem=send_sems.at[6 * a + 3 + k],
                    recv_sem=recv_sems.at[6 * a + 3 + k], device_id=(x, y, 1 - c), device_id_type=MESH)
                cp.start()
                passed.append(cp)
        for cp in sent:
            cp.wait_send()
        for cp in passed:
            cp.wait()

    for a in arrs:
        assert a.shape[0] % 32 == 0
    outs = _comm_call(body, arrs, [S((4,) + a.shape, a.dtype) for a in arrs], 6 * n, name=name)
    chip = 2 * lax.axis_index("x") + lax.axis_index("y")
    return [lax.dynamic_update_index_in_dim(o, a, chip, 0) for o, a in zip(outs, arrs)]


def exchange_chips(arrs, small, *, name):
    n = len(arrs)

    def body(*refs):
        ins, small_in = refs[:n], refs[n]
        outs, small_out = refs[n + 1:2 * n + 1], refs[2 * n + 1]
        send_sems, recv_sems, loc_sems = refs[2 * n + 2:]
        x, y, c = _position()
        me = 2 * x + y
        copies = []
        for a in range(n):
            for k, (px, py) in enumerate(_other_chips(x, y)):
                cp = pltpu.make_async_remote_copy(
                    src_ref=ins[a].at[2 * px + py], dst_ref=outs[a].at[me], send_sem=send_sems.at[3 * a + k],
                    recv_sem=recv_sems.at[3 * a + k], device_id=(px, py, c), device_id_type=MESH)
                cp.start()
                copies.append(cp)
        me8 = 4 * x + 2 * y + c
        k = 3 * n
        for fx in (0, 1):
            for fy in (0, 1):
                for fc in (0, 1):
                    if fx + fy + fc == 0:
                        continue
                    peer = (1 - x if fx else x, 1 - y if fy else y, 1 - c if fc else c)
                    cp = pltpu.make_async_remote_copy(
                        src_ref=small_in, dst_ref=small_out.at[me8], send_sem=send_sems.at[k],
                        recv_sem=recv_sems.at[k], device_id=peer, device_id_type=MESH)
                    cp.start()
                    copies.append(cp)
                    k += 1
        for cp in copies:
            cp.wait()

    outs = [S(a.shape, a.dtype) for a in arrs] + [S((8,) + small.shape, small.dtype)]
    outs = _comm_call(body, list(arrs) + [small], outs, 3 * n + 7, name=name)
    x, y, c = _position()
    chip = 2 * x + y
    res = [lax.dynamic_update_index_in_dim(o, lax.dynamic_index_in_dim(a, chip, 0, keepdims=False), chip, 0)
           for o, a in zip(outs[:-1], arrs)]
    return res + [lax.dynamic_update_index_in_dim(outs[-1], small, 4 * x + 2 * y + c, 0)]


def swap_cores(arrs, *, name):
    n = len(arrs)

    def body(*refs):
        ins, outs = refs[:n], refs[n:2 * n]
        send_sems, recv_sems, _ = refs[2 * n:]
        x, y, c = _position()
        copies = []
        for a in range(n):
            cp = pltpu.make_async_remote_copy(
                src_ref=ins[a], dst_ref=outs[a], send_sem=send_sems.at[a], recv_sem=recv_sems.at[a],
                device_id=(x, y, 1 - c), device_id_type=MESH)
            cp.start()
            copies.append(cp)
        for cp in copies:
            cp.wait()

    return _comm_call(body, arrs, [S(a.shape, a.dtype) for a in arrs], n, name=name)


def pair_add(a, b, *, name):
    k, r, c = a.shape
    tr = _tile(r, (256, 128))

    def body(a_ref, b_ref, o_ref):
        o_ref[...] = (a_ref[...].astype(F32) + b_ref[...].astype(F32)).astype(BF16)

    blk = pl.BlockSpec((1, tr, c), lambda j, i: (j, i, 0))
    return _call(body, name=name, out_shape=S((k, r, c), BF16), grid=(k, r // tr), in_specs=[blk, blk], out_specs=blk,
                 sem=("parallel", "parallel"))(a, b)


def _local_step(p, x, target):
    bsz, seq, d = x.shape
    lp = LEAD + seq
    m = bsz * lp
    h0 = jnp.concatenate([jnp.zeros((bsz, PAD, d), F32), jnp.broadcast_to(p["meta"][None], (bsz, N_META, d)), x],
                         axis=1).reshape(m, d)
    u0 = rmsnorm_fwd(h0, p["even_norm"], name="norm0")
    proj = mm_nn([u0], p["win_e"], name="proj0")
    lx = conv_fwd(proj, COL_LRU_X, D_MODEL, p["lru_conv_w"], p["lru_conv_b"], name="lru_conv")
    rowmask = jnp.tile((jnp.arange(lp) >= PAD).astype(F32), bsz).reshape(m, 1)
    a, b = gates_fwd(lx, rowmask, p["lru_w_a"], p["lru_b_a"], p["lru_w_x"], p["lru_b_x"], p["lru_lambda"],
                     name="lru_gates")
    hs = scan_fwd(a, b, lp, name="lru_scan")
    ya = gate_fwd(hs, 0, proj, COL_LRU_G // D_MODEL, name="lru_out_gate", o_scan=True)
    pre = conv_fwd(proj, COL_XBC, SSD_CONV_DIM, p["ssd_conv_w"], p["ssd_conv_b"], name="ssd_conv")
    y, s_in = ssd_fwd(pre, proj, p["ssd_dt_bias"], p["ssd_a_log"], p["ssd_d"], lp, name="ssd")
    yb = gnorm_fwd(y, proj, COL_Z // D_MODEL, p["ssd_norm"], name="ssd_norm")
    h1 = mm_nn([ya, yb], p["wout_e"], resid=h0, name="out0")
    u1 = rmsnorm_fwd(h1, p["odd_norm"], name="norm1")
    qkvg = mm_nn([u1], p["win_o"], name="proj1")
    o = attn_fwd(qkvg, lp, name="attn")
    og = gate_fwd(o, 0, qkvg, 3, name="attn_gate")
    h2 = mm_nn([og], p["wout_o"], resid=h1, name="out1")
    dh2, dh2b, loss, d_final = loss_head(h2, target, p["final_norm"], lp, name="loss_head")
    g = {"final_norm": d_final}
    g["odd_w_out"] = mm_tn(og, dh2b, name="dw_out1")
    d_og = mm_nn([dh2b], p["wout_o_t"], name="d_out1")
    do, dgate = gate_bwd(d_og, 0, o, 0, qkvg, 3, name="attn_gate_bwd")
    dq, dk, dv = attn_bwd(qkvg, do, lp, name="attn_bwd")
    segs1 = [dq, dk, dv, dgate]
    du1 = mm_nn(segs1, p["win_o_t"], name="d_proj1")
    g["odd_w_in"] = jnp.concatenate([mm_tn(u1, t, name=f"dw_proj1_{k}") for k, t in enumerate(segs1)], axis=1)
    dh1, dh1b, g["odd_norm"] = rmsnorm_bwd(du1, h1, p["odd_norm"], dh2, name="norm1_bwd")
    g["even_w_out"] = jnp.concatenate([mm_tn(ya, dh1b, name="dw_out0_a"), mm_tn(yb, dh1b, name="dw_out0_b")], axis=0)
    d_mixed = mm_nn([dh1b], p["wout_e_t"], name="d_out0")
    dhs, dlg = gate_bwd(d_mixed, 0, hs, 0, proj, COL_LRU_G // D_MODEL, name="lru_out_gate_bwd", o_scan=True)
    dy, dz, g["ssd_norm"] = gnorm_bwd(d_mixed, 1, y, proj, COL_Z // D_MODEL, p["ssd_norm"], name="ssd_norm_bwd")
    dpre, ddt, g["ssd_dt_bias"], g["ssd_a_log"], g["ssd_d"] = ssd_bwd(
        pre, proj, s_in, dy, p["ssd_dt_bias"], p["ssd_a_log"], p["ssd_d"], lp, name="ssd_bwd")
    dxbc, g["ssd_conv_w"], g["ssd_conv_b"] = conv_bwd(proj, COL_XBC, SSD_CONV_DIM, dpre, p["ssd_conv_w"],
                                                      name="ssd_conv_bwd")
    da, db = scan_bwd(a, hs, dhs, lp, name="lru_scan_bwd")
    dlx, g["lru_w_a"], g["lru_w_x"], g["lru_b_a"], g["lru_b_x"], g["lru_lambda"] = gates_bwd(
        lx, rowmask, da, db, p["lru_w_a"], p["lru_b_a"], p["lru_w_x"], p["lru_b_x"], p["lru_lambda"],
        name="lru_gates_bwd")
    dlrux, g["lru_conv_w"], g["lru_conv_b"] = conv_bwd(proj, COL_LRU_X, D_MODEL, dlx, p["lru_conv_w"],
                                                       name="lru_conv_bwd")
    segs0 = [dlrux, dlg, dz, dxbc, ddt]
    du0 = mm_nn(segs0, p["win_e_t"], name="d_proj0")
    g["even_w_in"] = jnp.concatenate([mm_tn(u0, t, name=f"dw_proj0_{k}") for k, t in enumerate(segs0)],
                                     axis=1)[:, :EVEN_IN]
    dh0, _, g["even_norm"] = rmsnorm_bwd(du0, h0, p["even_norm"], dh1, name="norm0_bwd")
    g["meta"] = meta_grad(dh0, lp, name="meta_grad")
    grad_x = dh0.reshape(bsz, lp, d)[:, LEAD:]
    return loss, grad_x, g


WEIGHTS = ("meta", "even_norm", "even_w_in", "lru_conv_w", "lru_conv_b", "lru_w_a", "lru_b_a", "lru_w_x", "lru_b_x",
           "lru_lambda", "ssd_conv_w", "ssd_conv_b", "ssd_dt_bias", "ssd_a_log", "ssd_d", "ssd_norm", "even_w_out",
           "odd_norm", "odd_w_in", "odd_w_out", "final_norm")
BIG = ("even_w_in", "even_w_out", "odd_w_in", "odd_w_out", "lru_w_a", "lru_w_x")
SHARDED_SMALL = {"meta": 256, "lru_conv_w": 256, "ssd_conv_w": 384, "odd_norm": 256}
SMALL_SHAPES = {"meta": (16, 1024), "even_norm": (1, 1024), "lru_conv_w": (4, 1024), "lru_conv_b": (1, 1024),
                "lru_b_a": (1, 1024), "lru_b_x": (1, 1024), "lru_lambda": (1, 1024), "ssd_conv_w": (4, 1536),
                "ssd_conv_b": (1, 1536), "ssd_dt_bias": (1, 16), "ssd_a_log": (1, 16), "ssd_d": (1, 16),
                "ssd_norm": (1, 1024), "odd_norm": (1, 1024), "final_norm": (1, 1024)}
PACK_UNIT = 1024


def _pack(parts):
    flat = []
    for part in parts:
        v = part.reshape(-1)
        flat.append(jnp.pad(v, (0, -v.shape[0] % PACK_UNIT)))
    return jnp.concatenate(flat).reshape(-1, 128)


def _unpack(buf, shapes):
    v = buf.reshape(-1)
    out, off = [], 0
    for shp in shapes:
        n = 1
        for s_ in shp:
            n *= s_
        out.append(v[off:off + n].reshape(shp))
        off += n + (-n % PACK_UNIT)
    return out


def _chip_cols(a4):
    return jnp.transpose(a4, (1, 0, 2)).reshape(a4.shape[1], -1)


def _to_chip_cols(a, cols):
    return jnp.transpose(a.reshape(a.shape[0], 4, cols), (1, 0, 2))


def kernel(x, meta, even_norm, even_w_in, lru_conv_w, lru_conv_b, lru_w_a, lru_b_a, lru_w_x, lru_b_x, lru_lambda, ssd_conv_w, ssd_conv_b, ssd_dt_bias, ssd_a_log, ssd_d, ssd_norm, even_w_out, odd_norm, odd_w_in, odd_w_out, final_norm, loss_target, m_meta, m_even_norm, m_even_w_in, m_lru_conv_w, m_lru_conv_b, m_lru_w_a, m_lru_b_a, m_lru_w_x, m_lru_b_x, m_lru_lambda, m_ssd_conv_w, m_ssd_conv_b, m_ssd_dt_bias, m_ssd_a_log, m_ssd_d, m_ssd_norm, m_even_w_out, m_odd_norm, m_odd_w_in, m_odd_w_out, m_final_norm, v_meta, v_even_norm, v_even_w_in, v_lru_conv_w, v_lru_conv_b, v_lru_w_a, v_lru_b_a, v_lru_w_x, v_lru_b_x, v_lru_lambda, v_ssd_conv_w, v_ssd_conv_b, v_ssd_dt_bias, v_ssd_a_log, v_ssd_d, v_ssd_norm, v_even_w_out, v_odd_norm, v_odd_w_in, v_odd_w_out, v_final_norm):
    given = dict(locals())
    w = {n: given[n] for n in WEIGHTS}
    mom = {n: given["m_" + n] for n in WEIGHTS}
    var = {n: given["v_" + n] for n in WEIGHTS}
    chip = 2 * lax.axis_index("x") + lax.axis_index("y")

    big_local = {"even_w_in": even_w_in[0], "even_w_out": even_w_out[0], "odd_w_in": odd_w_in[0],
                 "odd_w_out": odd_w_out[0], "lru_w_a": lru_w_a[0].reshape(256, 256),
                 "lru_w_x": lru_w_x[0].reshape(256, 256)}
    sharded_local = [meta, lru_conv_w[0], ssd_conv_w[0], odd_norm]
    gathered = allgather_chips([big_local[n].astype(BF16) for n in BIG] + [_pack(sharded_local)], name="gather_weights")
    gb = dict(zip(BIG, gathered[:-1]))
    per_chip = [_unpack(gathered[-1][k], [a.shape for a in sharded_local]) for k in range(4)]
    full_small = [jnp.concatenate([per_chip[k][j] for k in range(4)], axis=-1) for j in range(len(sharded_local))]

    def lru_full(a4):
        return jnp.transpose(a4.reshape(4, LRU_BLOCKS, 64, LRU_BLOCK), (1, 0, 2, 3)).reshape(LRU_BLOCKS, LRU_BLOCK, LRU_BLOCK)

    p = {"meta": full_small[0], "lru_conv_w": full_small[1], "ssd_conv_w": full_small[2], "odd_norm": full_small[3],
         "even_norm": even_norm, "lru_conv_b": lru_conv_b, "lru_b_a": lru_b_a, "lru_b_x": lru_b_x,
         "lru_lambda": lru_lambda, "ssd_conv_b": ssd_conv_b, "ssd_dt_bias": ssd_dt_bias, "ssd_a_log": ssd_a_log,
         "ssd_d": ssd_d, "ssd_norm": ssd_norm, "final_norm": final_norm.reshape(1, D_MODEL)}
    p["win_e"] = jnp.pad(_chip_cols(gb["even_w_in"]), ((0, 0), (0, EVEN_IN_P - EVEN_IN)))
    p["wout_e"] = gb["even_w_out"].reshape(2 * D_MODEL, D_MODEL)
    p["win_o"] = _chip_cols(gb["odd_w_in"])
    p["wout_o"] = gb["odd_w_out"].reshape(D_MODEL, D_MODEL)
    for n in ("win_e", "wout_e", "win_o", "wout_o"):
        p[n + "_t"] = p[n].T
    p["lru_w_a"] = lru_full(gb["lru_w_a"])
    p["lru_w_x"] = lru_full(gb["lru_w_x"])

    loss_part, grad_x, g = _local_step(p, x, loss_target)

    def lru_slabs(a):
        return jnp.transpose(a.reshape(LRU_BLOCKS, 4, 64, LRU_BLOCK), (1, 0, 2, 3)).reshape(4, 256, LRU_BLOCK)

    def by_half(t):
        return jnp.transpose(t.reshape(4, 2, t.shape[1] // 2, t.shape[2]), (1, 0, 2, 3)).astype(BF16)

    slabs = [_to_chip_cols(g["even_w_in"], EVEN_IN // 4), g["even_w_out"].reshape(4, 512, D_MODEL),
             _to_chip_cols(g["odd_w_in"], D_MODEL), g["odd_w_out"].reshape(4, 256, D_MODEL),
             lru_slabs(g["lru_w_a"]), lru_slabs(g["lru_w_x"])]
    small_names = list(SMALL_SHAPES)
    small_part = _pack([loss_part[0:1, 0:1]] + [g[n] for n in small_names])
    core = lax.axis_index("c")
    halves = [by_half(t) for t in slabs]
    mine = [lax.dynamic_index_in_dim(t, core, 0, keepdims=False) for t in halves]
    theirs = swap_cores([lax.dynamic_index_in_dim(t, 1 - core, 0, keepdims=False) for t in halves], name="swap_partials")
    pair_sums = [pair_add(a, b, name=f"pair_add_{n}") for a, b, n in zip(mine, theirs, BIG)]
    *recv, small_all = exchange_chips(pair_sums, small_part, name="exchange_grads")
    half_sums = [sum_lead(r, name=f"sum_chips_{n}") for n, r in zip(BIG, recv)]
    other_half = swap_cores(half_sums, name="swap_halves")
    full_sums = [jnp.where(core == 0, jnp.concatenate([a, b], axis=0), jnp.concatenate([b, a], axis=0))
                 for a, b in zip(half_sums, other_half)]
    small_sum = sum_lead(small_all, name="sum_small")
    small_g = dict(zip(["loss"] + small_names, _unpack(small_sum, [(1, 1)] + [SMALL_SHAPES[n] for n in small_names])))
    loss = small_g["loss"].reshape(())

    grads, delta, new_m, new_v = {}, {}, {}, {}
    for n, total in zip(BIG, full_sums):
        shp = w[n].shape
        two_d = lambda t: t.reshape(total.shape)
        res = adamw(two_d(w[n]), [total], two_d(mom[n]), two_d(var[n]), name=f"adamw_{n}")
        grads[n], delta[n], new_m[n], new_v[n] = (r.reshape(shp) for r in res)
    local_g = []
    for n in small_names:
        gn = small_g[n]
        if n in SHARDED_SMALL:
            gn = lax.dynamic_slice_in_dim(gn, chip * SHARDED_SMALL[n], SHARDED_SMALL[n], axis=1)
        local_g.append(gn)
    res = adamw(_pack([w[n] for n in small_names]), [_pack(local_g)], _pack([mom[n] for n in small_names]),
                _pack([var[n] for n in small_names]), name="adamw_small")
    shapes = [w[n].shape for n in small_names]
    for out, r in zip((grads, delta, new_m, new_v), res):
        out.update(dict(zip(small_names, _unpack(r, shapes))))
    return (loss, grad_x, *[grads[n] for n in WEIGHTS], *[delta[n] for n in WEIGHTS], *[new_m[n] for n in WEIGHTS],
            *[new_v[n] for n in WEIGHTS])
```
